```python
import jax
import jax.numpy as jnp
from jax import lax
import numpy as np

D_MODEL = 1024
BATCH = 2
SEQ = 8192
DEPTH = 4
DEC_BATCH = 128
DEC_SEQ = 1
PAST_LEN = 2048
PAGE_SIZE = 128

HEAD_DIM = 64
NSA_HEADS = 4
NSA_BLOCK = 64
NSA_N_SEL = 16
NSA_WINDOW = 512
FORCE_BONUS = 1.0e4
CONV_CH = 256
CONV_WIDTH = 31
SGU_CH = 256
SGU_GROUPS = 4
SGU_CHUNK = 128
DSA_HEADS = 4
IDX_HEADS = 4
DSA_TOPK = 256
N_BRANCH = 4
BRANCH_WIDTH = 256
D_FF = 4 * D_MODEL
Q_BLOCK = 128
EPS = 1e-6
NEG = -1e30

SPLIT_WIDTHS = (
    NSA_HEADS * HEAD_DIM,
    6 * HEAD_DIM,
    NSA_HEADS * 3,
    2 * CONV_CH,
    2 * SGU_CH,
    DSA_HEADS * HEAD_DIM,
    2 * HEAD_DIM,
    IDX_HEADS * HEAD_DIM,
    HEAD_DIM,
    IDX_HEADS,
)
D_IN = sum(SPLIT_WIDTHS)

kernel_name = 'hybrid_nsa_conv_sgu_dsa_step'


def rmsnorm(x, g):
    xf = x.astype(jnp.float32)
    y = xf * lax.rsqrt(jnp.mean(xf * xf, axis=-1, keepdims=True) + EPS)
    return (y * g.astype(jnp.float32)).astype(x.dtype)


def layernorm(x, g, b):
    xf = x.astype(jnp.float32)
    mu = jnp.mean(xf, axis=-1, keepdims=True)
    var = jnp.mean(jnp.square(xf - mu), axis=-1, keepdims=True)
    y = (xf - mu) * lax.rsqrt(var + EPS) * g.astype(jnp.float32) + b.astype(jnp.float32)
    return y.astype(x.dtype)


def masked_probs(s, mask):
    s = jnp.where(mask, s, NEG)
    m = jnp.max(s, axis=-1, keepdims=True)
    p = jnp.where(mask, jnp.exp(s - m), 0.0)
    return p / jnp.maximum(jnp.sum(p, axis=-1, keepdims=True), 1e-30)


def gather_rows(src, idx):
    return jax.vmap(lambda s, i: s[i])(src, idx)


def split_proj(p):
    points = np.cumsum(SPLIT_WIDTHS)[:-1].tolist()
    return jnp.split(p, points, axis=-1)


def to_blocks(a):
    b, s = a.shape[:2]
    return a.reshape(b, s // Q_BLOCK, Q_BLOCK, *a.shape[2:]).swapaxes(0, 1)


def from_blocks(a):
    nq, b, qb = a.shape[:3]
    return a.swapaxes(0, 1).reshape(b, nq * qb, *a.shape[3:])


def compress_blocks(k, pos, w1, w2):
    b, L, dh = k.shape
    nb = L // NSA_BLOCK
    kb = k[:, :nb * NSA_BLOCK].reshape(b, nb, NSA_BLOCK, dh) + pos
    return jax.nn.silu(kb.reshape(b, nb, NSA_BLOCK * dh) @ w1) @ w2


def nsa_queries(q, gates, q_pos, kc, vc, k_slc, v_slc, k_win, v_win, win_pos):
    b, t, nh, dh = q.shape
    scale = dh ** -0.5
    nb = kc.shape[1]
    L = k_slc.shape[1]
    s_c = jnp.einsum('bthd,bnd->bthn', q, kc).astype(jnp.float32) * scale
    blk_end = (jnp.arange(nb) + 1) * NSA_BLOCK - 1
    mask_c = (blk_end[None, :] <= q_pos[:, None])[None, :, None, :]
    p_c = masked_probs(s_c, mask_c)
    o_c = jnp.einsum('bthn,bnd->bthd', p_c.astype(vc.dtype), vc)
    nbs = -(-L // NSA_BLOCK)
    imp = jnp.pad(jnp.sum(p_c, axis=2), ((0, 0), (0, 0), (0, nbs - nb)))
    j = jnp.arange(nbs)[None, :]
    cur = (q_pos // NSA_BLOCK)[:, None]
    forced = (j == 0) | (j == cur) | (j == cur - 1)
    score = jnp.where(j <= cur, imp, -1.0) + jnp.where(forced, FORCE_BONUS, 0.0)
    n_sel = min(NSA_N_SEL, nbs)
    _, idx = lax.top_k(score, n_sel)
    pad = nbs * NSA_BLOCK - L
    kb = jnp.pad(k_slc, ((0, 0), (0, pad), (0, 0))).reshape(b, nbs, NSA_BLOCK, dh)
    vb = jnp.pad(v_slc, ((0, 0), (0, pad), (0, 0))).reshape(b, nbs, NSA_BLOCK, dh)
    k_sel = gather_rows(kb, idx).reshape(b, t, n_sel * NSA_BLOCK, dh)
    v_sel = gather_rows(vb, idx).reshape(b, t, n_sel * NSA_BLOCK, dh)
    pos_sel = (idx[..., None] * NSA_BLOCK + jnp.arange(NSA_BLOCK)).reshape(b, t, n_sel * NSA_BLOCK)
    mask_s = (pos_sel <= q_pos[None, :, None])[:, :, None, :]
    s_s = jnp.einsum('bthd,btnd->bthn', q, k_sel).astype(jnp.float32) * scale
    p_s = masked_probs(s_s, mask_s)
    o_s = jnp.einsum('bthn,btnd->bthd', p_s.astype(v_sel.dtype), v_sel)
    dpos = q_pos[:, None] - win_pos[None, :]
    mask_w = ((dpos >= 0) & (dpos < NSA_WINDOW) & (win_pos[None, :] >= 0))[None, :, None, :]
    s_w = jnp.einsum('bthd,bnd->bthn', q, k_win).astype(jnp.float32) * scale
    p_w = masked_probs(s_w, mask_w)
    o_w = jnp.einsum('bthn,bnd->bthd', p_w.astype(v_win.dtype), v_win)
    g = jax.nn.sigmoid(gates)
    o = g[..., 0:1] * o_c + g[..., 1:2] * o_s + g[..., 2:3] * o_w
    return o.reshape(b, t, nh * dh)


def dsa_queries(q, q_idx, w_idx, q_pos, k, v, k_idx):
    b, t, nh, dh = q.shape
    L = k.shape[1]
    n_keep = min(DSA_TOPK, L // 4)
    rel = jax.nn.relu(jnp.einsum('bthd,bsd->bths', q_idx, k_idx).astype(jnp.float32))
    w = w_idx.astype(jnp.float32) * (IDX_HEADS ** -0.5 * dh ** -0.5)
    score = jnp.einsum('bths,bth->bts', rel, w)
    causal = jnp.arange(L)[None, :] <= q_pos[:, None]
    score = jnp.where(causal[None], score, NEG)
    _, idx = lax.top_k(score, n_keep)
    k_sel = gather_rows(k, idx)
    v_sel = gather_rows(v, idx)
    mask = (idx <= q_pos[None, :, None])[:, :, None, :]
    s = jnp.einsum('bthd,btkd->bthk', q, k_sel).astype(jnp.float32) * dh ** -0.5
    p = masked_probs(s, mask)
    o = jnp.einsum('bthk,btkd->bthd', p.astype(v_sel.dtype), v_sel)
    return o.reshape(b, t, nh * dh)


def conformer_conv(glu_in, buf, conv_w, conv_b, ln_g, ln_b):
    a, gate = jnp.split(glu_in, 2, axis=-1)
    hx = jnp.concatenate([buf, a * jax.nn.sigmoid(gate)], axis=1)
    y = lax.conv_general_dilated(hx, conv_w[:, None, :], window_strides=(1,), padding='VALID',
                                 dimension_numbers=('NWC', 'WIO', 'NWC'),
                                 feature_group_count=hx.shape[-1]) + conv_b
    return jax.nn.silu(layernorm(y, ln_g, ln_b)), hx[:, -(CONV_WIDTH - 1):]


def spatial_gating(uv, ln_g, ln_b, w_s, b_s):
    u, v = jnp.split(jax.nn.gelu(uv, approximate=False), 2, axis=-1)
    b, t, c = v.shape
    nc = -(-t // SGU_CHUNK)
    vn = jnp.pad(layernorm(v, ln_g, ln_b), ((0, 0), (0, nc * SGU_CHUNK - t), (0, 0)))
    vn = vn.reshape(b, nc, SGU_CHUNK, SGU_GROUPS, c // SGU_GROUPS)
    w = w_s * jnp.tril(jnp.ones((SGU_CHUNK, SGU_CHUNK), w_s.dtype))
    mixed = jnp.einsum('gij,bcjgd->bcigd', w, vn) + b_s.T[None, None, :, :, None]
    mixed = mixed.reshape(b, nc * SGU_CHUNK, c)[:, :t]
    return u * mixed, v


def merge_branches(h, outs, w_branch, w_gate, w_out):
    o = jnp.stack(outs, axis=2)
    br = jnp.einsum('btkc,kcd->btkd', o, w_branch)
    g = jax.nn.sigmoid(h @ w_gate).reshape(br.shape)
    return jnp.sum(g * br, axis=2) @ w_out


def sq_relu_mlp(h, w_up, w_down):
    return jnp.square(jax.nn.relu(h @ w_up)) @ w_down


def setup_inputs(seed: int = 0) -> dict:
    key = jax.random.key(seed)
    ks = iter(jax.random.split(key, 32))

    def nrm(shape, scale):
        return jax.random.normal(next(ks), shape, jnp.float32) * scale

    n_pages = PAST_LEN // PAGE_SIZE
    n_used = DEC_BATCH * n_pages
    n_pool = n_used + max(1, n_used // 4)
    w_buf = min(NSA_WINDOW, PAST_LEN)
    x_prompt = nrm((BATCH, SEQ, D_MODEL), 1.0)
    x_sample = nrm((DEC_BATCH, DEC_SEQ, D_MODEL), 1.0)
    cache_nsa = nrm((n_pool, DEPTH, PAGE_SIZE, 4, HEAD_DIM), 1.0)
    cache_dsa = nrm((n_pool, DEPTH, PAGE_SIZE, 3, HEAD_DIM), 1.0)
    state_win = nrm((DEC_BATCH, DEPTH, w_buf, 2, HEAD_DIM), 1.0)
    state_conv = nrm((DEC_BATCH, DEPTH, CONV_WIDTH - 1, CONV_CH), 0.5)
    page_table = jax.random.permutation(next(ks), n_pool)[:n_used].reshape(DEC_BATCH, n_pages).astype(jnp.int32)
    return {
        'x_prompt': x_prompt,
        'x_sample': x_sample,
        'cache_nsa': cache_nsa,
        'cache_dsa': cache_dsa,
        'state_win': state_win,
        'state_conv': state_conv,
        'page_table': page_table,
        'norm1_g': 1.0 + nrm((DEPTH, D_MODEL), 0.01),
        'norm2_g': 1.0 + nrm((DEPTH, D_MODEL), 0.01),
        'final_g': 1.0 + nrm((D_MODEL,), 0.01),
        'w_in': nrm((DEPTH, D_MODEL, D_IN), D_MODEL ** -0.5),
        'cmp_pos': nrm((DEPTH, 2, NSA_BLOCK, HEAD_DIM), 0.1),
        'cmp_w1': nrm((DEPTH, 2, NSA_BLOCK * HEAD_DIM, HEAD_DIM), (NSA_BLOCK * HEAD_DIM) ** -0.5),
        'cmp_w2': nrm((DEPTH, 2, HEAD_DIM, HEAD_DIM), HEAD_DIM ** -0.5),
        'conv_w': nrm((DEPTH, CONV_WIDTH, CONV_CH), CONV_WIDTH ** -0.5),
        'conv_b': nrm((DEPTH, CONV_CH), 0.02),
        'conv_ln_g': 1.0 + nrm((DEPTH, CONV_CH), 0.01),
        'conv_ln_b': nrm((DEPTH, CONV_CH), 0.02),
        'sgu_ln_g': 1.0 + nrm((DEPTH, SGU_CH), 0.01),
        'sgu_ln_b': nrm((DEPTH, SGU_CH), 0.02),
        'sgu_w': nrm((DEPTH, SGU_GROUPS, SGU_CHUNK, SGU_CHUNK), SGU_CHUNK ** -0.5),
        'sgu_b': 1.0 + nrm((DEPTH, SGU_GROUPS, SGU_CHUNK), 0.02),
        'w_branch': nrm((DEPTH, N_BRANCH, BRANCH_WIDTH, D_MODEL), BRANCH_WIDTH ** -0.5),
        'w_gate': nrm((DEPTH, D_MODEL, N_BRANCH * D_MODEL), D_MODEL ** -0.5),
        'w_out': nrm((DEPTH, D_MODEL, D_MODEL), D_MODEL ** -0.5),
        'w_up': nrm((DEPTH, D_MODEL, D_FF), D_MODEL ** -0.5),
        'w_down': nrm((DEPTH, D_FF, D_MODEL), D_FF ** -0.5),
    }


def reference(x_prompt, x_sample, cache_nsa, cache_dsa, state_win, state_conv, page_table,
              norm1_g, norm2_g, final_g, w_in, cmp_pos, cmp_w1, cmp_w2,
              conv_w, conv_b, conv_ln_g, conv_ln_b, sgu_ln_g, sgu_ln_b, sgu_w, sgu_b,
              w_branch, w_gate, w_out, w_up, w_down):
    B, S, _ = x_prompt.shape
    Bd, T, _ = x_sample.shape
    past_len = page_table.shape[1] * cache_nsa.shape[2]
    w_buf = state_win.shape[2]
    w_prompt = min(NSA_WINDOW, S)
    blk_ids = jnp.arange(S // Q_BLOCK)
    q_pos_s = past_len + jnp.arange(T)
    win_pos_s = past_len - w_buf + jnp.arange(w_buf + T)

    xp, xs = x_prompt, x_sample
    nsa_p, nsa_s, dsa_p, dsa_s = [], [], [], []
    win_p, win_s, conv_p, conv_s, sgu_s = [], [], [], [], []
    for l in range(DEPTH):
        h = rmsnorm(xp, norm1_g[l])
        qa, kva, ga, glu, uv, qd, kvd, qi, ki, wi = split_proj(h @ w_in[l])
        qa = qa.reshape(B, S, NSA_HEADS, HEAD_DIM)
        kva = kva.reshape(B, S, 6, HEAD_DIM)
        ga = ga.reshape(B, S, NSA_HEADS, 3)
        kc = compress_blocks(kva[:, :, 0], cmp_pos[l, 0], cmp_w1[l, 0], cmp_w2[l, 0])
        vc = compress_blocks(kva[:, :, 1], cmp_pos[l, 1], cmp_w1[l, 1], cmp_w2[l, 1])
        win_pad = jnp.pad(kva[:, :, 4:], ((0, 0), (NSA_WINDOW, 0), (0, 0), (0, 0)))

        def nsa_block(args):
            i, qb, gb = args
            start = i * Q_BLOCK
            wk = lax.dynamic_slice_in_dim(win_pad, start, NSA_WINDOW + Q_BLOCK, axis=1)
            return nsa_queries(qb, gb, start + jnp.arange(Q_BLOCK), kc, vc, kva[:, :, 2], kva[:, :, 3],
                               wk[:, :, 0], wk[:, :, 1], start - NSA_WINDOW + jnp.arange(NSA_WINDOW + Q_BLOCK))

        oA = from_blocks(lax.map(nsa_block, (blk_ids, to_blocks(qa), to_blocks(ga))))
        kvd = kvd.reshape(B, S, 2, HEAD_DIM)

        def dsa_block(args):
            i, qb, qib, wib = args
            return dsa_queries(qb, qib, wib, i * Q_BLOCK + jnp.arange(Q_BLOCK), kvd[:, :, 0], kvd[:, :, 1], ki)

        oD = from_blocks(lax.map(dsa_block, (blk_ids, to_blocks(qd.reshape(B, S, DSA_HEADS, HEAD_DIM)),
                                             to_blocks(qi.reshape(B, S, IDX_HEADS, HEAD_DIM)), to_blocks(wi))))
        oB, conv_buf = conformer_conv(glu, jnp.zeros((B, CONV_WIDTH - 1, CONV_CH), glu.dtype),
                                      conv_w[l], conv_b[l], conv_ln_g[l], conv_ln_b[l])
        oC, _ = spatial_gating(uv, sgu_ln_g[l], sgu_ln_b[l], sgu_w[l], sgu_b[l])
        xp = xp + merge_branches(h, (oA, oB, oC, oD), w_branch[l], w_gate[l], w_out[l])
        xp = xp + sq_relu_mlp(rmsnorm(xp, norm2_g[l]), w_up[l], w_down[l])
        nsa_p.append(kva[:, :, :4])
        dsa_p.append(jnp.concatenate([kvd, ki[:, :, None]], axis=2))
        win_p.append(kva[:, S - w_prompt:, 4:])
        conv_p.append(conv_buf)

        h = rmsnorm(xs, norm1_g[l])
        qa, kva, ga, glu, uv, qd, kvd, qi, ki, wi = split_proj(h @ w_in[l])
        qa = qa.reshape(Bd, T, NSA_HEADS, HEAD_DIM)
        kva = kva.reshape(Bd, T, 6, HEAD_DIM)
        ga = ga.reshape(Bd, T, NSA_HEADS, 3)
        nsa_full = jnp.concatenate(
            [cache_nsa[page_table, l].reshape(Bd, past_len, 4, HEAD_DIM), kva[:, :, :4]], axis=1)
        kc = compress_blocks(nsa_full[:, :, 0], cmp_pos[l, 0], cmp_w1[l, 0], cmp_w2[l, 0])
        vc = compress_blocks(nsa_full[:, :, 1], cmp_pos[l, 1], cmp_w1[l, 1], cmp_w2[l, 1])
        win = jnp.concatenate([state_win[:, l], kva[:, :, 4:]], axis=1)
        oA = nsa_queries(qa, ga, q_pos_s, kc, vc, nsa_full[:, :, 2], nsa_full[:, :, 3],
                         win[:, :, 0], win[:, :, 1], win_pos_s)
        dsa_new = jnp.concatenate([kvd.reshape(Bd, T, 2, HEAD_DIM), ki[:, :, None]], axis=2)
        dsa_full = jnp.concatenate(
            [cache_dsa[page_table, l].reshape(Bd, past_len, 3, HEAD_DIM), dsa_new], axis=1)
        oD = dsa_queries(qd.reshape(Bd, T, DSA_HEADS, HEAD_DIM), qi.reshape(Bd, T, IDX_HEADS, HEAD_DIM), wi,
                         q_pos_s, dsa_full[:, :, 0], dsa_full[:, :, 1], dsa_full[:, :, 2])
        oB, conv_buf = conformer_conv(glu, state_conv[:, l], conv_w[l], conv_b[l], conv_ln_g[l], conv_ln_b[l])
        oC, v_rows = spatial_gating(uv, sgu_ln_g[l], sgu_ln_b[l], sgu_w[l], sgu_b[l])
        xs = xs + merge_branches(h, (oA, oB, oC, oD), w_branch[l], w_gate[l], w_out[l])
        xs = xs + sq_relu_mlp(rmsnorm(xs, norm2_g[l]), w_up[l], w_down[l])
        nsa_s.append(kva[:, :, :4])
        dsa_s.append(dsa_new)
        win_s.append(win[:, T:])
        conv_s.append(conv_buf)
        sgu_s.append(v_rows)

    y_prompt = rmsnorm(xp, final_g)
    y_sample = rmsnorm(xs, final_g)
    return (y_prompt, y_sample,
            jnp.stack(nsa_p, axis=1), jnp.stack(nsa_s, axis=1),
            jnp.stack(dsa_p, axis=1), jnp.stack(dsa_s, axis=1),
            jnp.stack(win_p, axis=1), jnp.stack(win_s, axis=1),
            jnp.stack(conv_p, axis=1), jnp.stack(conv_s, axis=1),
            jnp.stack(sgu_s, axis=1))
```

```python
import functools

import numpy as np
import jax
import jax.numpy as jnp
from jax import lax
from jax.experimental import pallas as pl
from jax.experimental.pallas import tpu as pltpu

F32, BF16, I32 = jnp.float32, jnp.bfloat16, jnp.int32

HEAD_DIM = 64
NSA_HEADS = 4
NSA_BLOCK = 64
NSA_N_SEL = 16
NSA_WINDOW = 512
FORCE_BONUS = 1.0e4
CONV_CH = 256
CONV_WIDTH = 31
SGU_CH = 256
SGU_GROUPS = 4
SGU_CHUNK = 128
DSA_HEADS = 4
IDX_HEADS = 4
DSA_TOPK = 256
N_BRANCH = 4
BRANCH_WIDTH = 256
Q_BLOCK = 128
EPS = 1e-6
NEG = -1e30

SPLIT_WIDTHS = (256, 384, 12, 512, 512, 256, 128, 256, 64, 4)

CK = 512
NB_PAD = 128
SAMPLE_ROWS = 16
VMEM_LIMIT = 56 * 1024 * 1024
INT_MIN = -2 ** 31
NEG_KEY = int(np.float32(NEG).view(np.int32)) ^ 0x7FFFFFFF

W_QA, W_KVA, W_GLU, W_UV, W_QD, W_QI, W_MISC, W_DSA = 0, 256, 640, 1152, 1664, 1920, 2176, 2304
W_TOTAL = 2496
MISC_WI, MISC_GA = 0, 4


def _cparams(n_axes):
    return pltpu.CompilerParams(dimension_semantics=("arbitrary",) * n_axes,
                                vmem_limit_bytes=VMEM_LIMIT)


def _dot(a, b):
    return jnp.dot(a, b, preferred_element_type=F32)


def _dot_nt(a, b):
    return lax.dot_general(a, b, (((1,), (1,)), ((), ())), preferred_element_type=F32)


def _rms(x, g):
    return x * lax.rsqrt(jnp.mean(x * x, axis=-1, keepdims=True) + EPS) * g


def _layernorm(x, g, b):
    mu = jnp.mean(x, axis=-1, keepdims=True)
    xc = x - mu
    var = jnp.mean(xc * xc, axis=-1, keepdims=True)
    return xc * lax.rsqrt(var + EPS) * g + b


def _sigmoid(x):
    return 1.0 / (1.0 + jnp.exp(-x))


def _tile4(a):
    return jnp.concatenate([a, a, a, a], axis=0)


def _proj_kernel(x_ref, g_ref, w_ref, wt_ref, qn_ref, kva_ref, glu_ref, uv_ref, qd_ref, qi_ref,
                 misc_ref, dsa_ref, *t_refs):
    hb = _rms(x_ref[...], g_ref[...]).astype(BF16)

    def slab(off, width):
        return _dot(hb, w_ref[:, off:off + width])

    qa = slab(W_QA, 256)
    qd = slab(W_QD, 256)
    qi = slab(W_QI, 256)
    for h in range(4):
        cols = slice(h * HEAD_DIM, (h + 1) * HEAD_DIM)
        qn_ref[h] = qa[:, cols].astype(BF16)
        qd_ref[h] = qd[:, cols].astype(BF16)
        qi_ref[h] = qi[:, cols].astype(BF16)
    kv = slab(W_KVA, 384)
    kva_ref[...] = kv
    glu_ref[...] = slab(W_GLU, 512)
    uv_ref[...] = slab(W_UV, 512)
    misc_ref[...] = slab(W_MISC, 128)
    dr = slab(W_DSA, 192)
    dsa_ref[...] = dr
    if t_refs:
        kt_ref, v_ref = t_refs
        kt_ref[0] = _dot_nt(wt_ref[...], hb).astype(BF16)
        v_ref[0] = kv[:, 192:256].astype(BF16)
        v_ref[1] = kv[:, 320:384].astype(BF16)
        v_ref[2] = dr[:, 64:128].astype(BF16)


def _proj(x, g, w, wt, *, tm, emit_t):
    n, d = x.shape
    sds = jax.ShapeDtypeStruct
    out_shape = [sds((4, n, HEAD_DIM), BF16), sds((n, 384), F32), sds((n, 512), F32), sds((n, 512), F32),
                 sds((4, n, HEAD_DIM), BF16), sds((4, n, HEAD_DIM), BF16), sds((n, 128), F32), sds((n, 192), F32)]
    head_spec = pl.BlockSpec((4, tm, HEAD_DIM), lambda i: (0, i, 0))

    def row_spec(width):
        return pl.BlockSpec((tm, width), lambda i: (i, 0))

    out_specs = [head_spec, row_spec(384), row_spec(512), row_spec(512), head_spec, head_spec,
                 row_spec(128), row_spec(192)]
    if emit_t:
        assert tm == CK
        out_shape += [sds((n // CK, 256, CK), BF16), sds((3, n, HEAD_DIM), BF16)]
        out_specs += [pl.BlockSpec((1, 256, CK), lambda i: (i, 0, 0)),
                      pl.BlockSpec((3, tm, HEAD_DIM), lambda i: (0, i, 0))]
    return pl.pallas_call(
        _proj_kernel,
        out_shape=out_shape,
        grid=(n // tm,),
        in_specs=[row_spec(d), pl.BlockSpec((1, d), lambda i: (0, 0)),
                  pl.BlockSpec((d, W_TOTAL), lambda i: (0, 0)),
                  pl.BlockSpec((256, d), lambda i: (0, 0))],
        out_specs=out_specs,
        compiler_params=_cparams(1),
        name="proj",
    )(x, g, w, wt)


def _compress_kernel(x_ref, pos_ref, w1_ref, w2_ref, o_ref):
    xb = (x_ref[...] + pos_ref[...]).astype(BF16)
    a = _dot(xb, w1_ref[...])
    a = a * _sigmoid(a)
    o_ref[...] = _dot(a.astype(BF16), w2_ref[...]).astype(o_ref.dtype)


def _compress(x, pos, w1, w2, *, tr):
    _, r, width = x.shape
    return pl.pallas_call(
        _compress_kernel,
        out_shape=jax.ShapeDtypeStruct((2, r, HEAD_DIM), BF16),
        grid=(2, r // tr),
        in_specs=[pl.BlockSpec((None, tr, width), lambda s, i: (s, i, 0)),
                  pl.BlockSpec((None, 1, width), lambda s, i: (s, 0, 0)),
                  pl.BlockSpec((None, width, HEAD_DIM), lambda s, i: (s, 0, 0)),
                  pl.BlockSpec((None, HEAD_DIM, HEAD_DIM), lambda s, i: (s, 0, 0))],
        out_specs=pl.BlockSpec((None, tr, HEAD_DIM), lambda s, i: (s, i, 0)),
        compiler_params=_cparams(2),
        name="compress",
    )(x, pos, w1, w2)


def _masked_probs(s, mask):
    s = jnp.where(mask, s, NEG)
    m = jnp.max(s, axis=-1, keepdims=True)
    p = jnp.where(mask, jnp.exp(s - m), 0.0)
    return p / jnp.maximum(jnp.sum(p, axis=-1, keepdims=True), 1e-30)


def _softmax_step(s, mask, v, carry):
    m, l, acc = carry
    s = jnp.where(mask, s, NEG)
    m_new = jnp.maximum(m, jnp.max(s, axis=-1, keepdims=True))
    alpha = jnp.exp(m - m_new)
    p = jnp.where(mask, jnp.exp(s - m_new), 0.0)
    l = alpha * l + jnp.sum(p, axis=-1, keepdims=True)
    acc = alpha * acc + _dot(p.astype(BF16), v)
    return m_new, l, acc


def _softmax_init(rows):
    return (jnp.full((rows, 1), NEG, F32), jnp.zeros((rows, 1), F32), jnp.zeros((rows, HEAD_DIM), F32))


def _softmax_done(carry):
    _, l, acc = carry
    return acc / jnp.maximum(l, 1e-30)


def _query_positions(tq, pos0, pos_stride):
    i = pl.program_id(1)
    start = pos0 + i * (tq * pos_stride)
    last = start + (tq - 1) * pos_stride
    t = lax.broadcasted_iota(I32, (tq, 1), 0)
    qpos = start + t * pos_stride
    return start, last, qpos


def _heads_as_rows(ref):
    return jnp.concatenate([ref[h] for h in range(4)], axis=0)


def _key_positions(c):
    return c * CK + lax.broadcasted_iota(I32, (1, CK), 1)


def _chunk_rows(ref, c):
    return ref[pl.ds(pl.multiple_of(c * CK, CK), CK), :]


def _nsa_kernel(q_ref, misc_ref, kc_ref, vc_ref, kst_ref, vs_ref, kwt_ref, vw_ref, o_ref, *,
                tq, pos0, pos_stride, win_base, n_sel):
    nb = kc_ref.shape[0]
    scale = HEAD_DIM ** -0.5
    start, last, qpos = _query_positions(tq, pos0, pos_stride)
    qpos4 = _tile4(qpos)
    q = _heads_as_rows(q_ref)

    jb = lax.broadcasted_iota(I32, (1, nb), 1)
    mask_c = ((jb + 1) * NSA_BLOCK - 1) <= qpos4
    p_c = _masked_probs(_dot_nt(q, kc_ref[...]) * scale, mask_c)
    o_c = _dot(p_c.astype(BF16), vc_ref[...])
    imp = p_c[0:tq] + p_c[tq:2 * tq] + p_c[2 * tq:3 * tq] + p_c[3 * tq:4 * tq]

    cur = qpos // NSA_BLOCK
    forced = (jb == 0) | (jb == cur) | (jb == cur - 1)
    score = jnp.where(jb <= cur, imp, -1.0) + jnp.where(forced, FORCE_BONUS, 0.0)
    jf = jb.astype(F32)

    def pick(_, carry):
        sc, sel = carry
        m = jnp.max(sc, axis=-1, keepdims=True)
        first = jnp.min(jnp.where(sc == m, jf, 1e9), axis=-1, keepdims=True)
        hit = jf == first
        return jnp.where(hit, -3e38, sc), jnp.where(hit, 1.0, sel)

    _, sel = lax.fori_loop(0, n_sel, pick, (score, jnp.zeros((tq, nb), F32)))
    sel4 = _tile4(sel).astype(BF16)

    jrow = lax.broadcasted_iota(I32, (nb, CK), 0)
    kcol = lax.broadcasted_iota(I32, (nb, CK), 1)

    def sel_chunk(c, carry):
        expand = jnp.where(jrow == ((c * CK + kcol) // NSA_BLOCK), 1.0, 0.0).astype(BF16)
        chosen = _dot(sel4, expand)
        kpos = _key_positions(c)
        mask = (chosen > 0.5) & (kpos <= qpos4)
        s = _dot(q, kst_ref[c]) * scale
        return _softmax_step(s, mask, _chunk_rows(vs_ref, c), carry)

    o_s = _softmax_done(lax.fori_loop(0, last // CK + 1, sel_chunk, _softmax_init(4 * tq)))

    carry = _softmax_init(4 * tq)
    c1 = start // CK
    for back in (1, 0):
        c = c1 - back
        cc = jnp.maximum(c - win_base, 0)
        kpos = _key_positions(c)
        dpos = qpos4 - kpos
        mask = (dpos >= 0) & (dpos < NSA_WINDOW) & (kpos >= 0)
        s = _dot(q, kwt_ref[cc]) * scale
        carry = _softmax_step(s, mask, _chunk_rows(vw_ref, cc), carry)
    o_w = _softmax_done(carry)

    g = _sigmoid(misc_ref[...])
    outs = []
    for h in range(4):
        rows = slice(h * tq, (h + 1) * tq)
        col = MISC_GA + 3 * h
        outs.append(g[:, col:col + 1] * o_c[rows] + g[:, col + 1:col + 2] * o_s[rows]
                    + g[:, col + 2:col + 3] * o_w[rows])
    o_ref[...] = jnp.concatenate(outs, axis=-1).astype(o_ref.dtype)


def _nsa(q, misc, kc, vc, ks, vs, kw, vw, *, nbatch, nq, tq, pos0, pos_stride, win_base, n_sel):
    nrows = q.shape[1]
    kern = functools.partial(_nsa_kernel, tq=tq, pos0=pos0, pos_stride=pos_stride, win_base=win_base,
                             n_sel=n_sel)
    return pl.pallas_call(
        kern,
        out_shape=jax.ShapeDtypeStruct((nrows, 4 * HEAD_DIM), BF16),
        grid=(nbatch, nq),
        in_specs=[pl.BlockSpec((4, tq, HEAD_DIM), lambda b, i: (0, b * nq + i, 0)),
                  pl.BlockSpec((tq, 128), lambda b, i: (b * nq + i, 0)),
                  pl.BlockSpec((None, NB_PAD, HEAD_DIM), lambda b, i: (b, 0, 0)),
                  pl.BlockSpec((None, NB_PAD, HEAD_DIM), lambda b, i: (b, 0, 0)),
                  ks[1], vs[1], kw[1], vw[1]],
        out_specs=pl.BlockSpec((tq, 4 * HEAD_DIM), lambda b, i: (b * nq + i, 0)),
        compiler_params=_cparams(2),
        name="nsa",
    )(q, misc, kc, vc, ks[0], vs[0], kw[0], vw[0])


def _dsa_kernel(qd_ref, qi_ref, misc_ref, kdt_ref, vd_ref, kit_ref, o_ref, key_scr, cut_scr, *,
                tq, pos0, pos_stride, n_keep):
    n_chunks_total = kdt_ref.shape[0]
    scale = HEAD_DIM ** -0.5
    start, last, qpos = _query_positions(tq, pos0, pos_stride)
    qpos4 = _tile4(qpos)
    n_chunks = last // CK + 1
    qd = _heads_as_rows(qd_ref)
    qi = _heads_as_rows(qi_ref)
    w = misc_ref[:, MISC_WI:MISC_WI + 4] * (IDX_HEADS ** -0.5 * HEAD_DIM ** -0.5)
    wcol = jnp.concatenate([w[:, h:h + 1] for h in range(4)], axis=0)

    def index_chunk(c, _):
        rel = jnp.maximum(_dot(qi, kit_ref[c]), 0.0) * wcol
        sc = rel[0:tq] + rel[tq:2 * tq] + rel[2 * tq:3 * tq] + rel[3 * tq:4 * tq]
        sc = jnp.where(sc == 0.0, 0.0, sc)
        sc = jnp.where(_key_positions(c) <= qpos, sc, NEG)
        bits = lax.bitcast_convert_type(sc, I32)
        key_scr[c] = jnp.where(bits < 0, bits ^ 0x7FFFFFFF, bits)
        return 0

    lax.fori_loop(0, n_chunks, index_chunk, 0)

    def count(pred):
        def body(c, acc):
            ind = jnp.where(pred(key_scr[c], c), 1.0, 0.0)
            return acc + (ind[:, 0:128] + ind[:, 128:256] + ind[:, 256:384] + ind[:, 384:512])

        acc = lax.fori_loop(0, n_chunks, body, jnp.zeros((tq, 128), F32))
        return jnp.sum(acc, axis=-1, keepdims=True)

    keep = float(n_keep)
    thr = jnp.where(count(lambda k, c: k >= 0) >= keep, 0, INT_MIN).astype(I32)

    def thr_bit(b, thr):
        cand = thr | jnp.left_shift(jnp.int32(1), 30 - b)
        return jnp.where(count(lambda k, c: k >= cand) >= keep, cand, thr)

    thr = lax.fori_loop(0, 31, thr_bit, thr)

    n_gt = count(lambda k, c: k > thr)
    n_ge = count(lambda k, c: k >= thr)
    need = keep - n_gt
    tie = (n_ge > keep) & (thr > NEG_KEY)
    n_pos_bits = (n_chunks_total * CK - 1).bit_length()
    cut_scr[...] = jnp.full(cut_scr.shape, n_chunks_total * CK, I32)

    @pl.when(jnp.max(jnp.where(tie, 1.0, 0.0)) > 0.5)
    def _():
        def cut_bit(b, cut):
            cand = cut + jnp.left_shift(jnp.int32(1), n_pos_bits - 1 - b)
            n = count(lambda k, c: (k == thr) & (_key_positions(c) < cand))
            return jnp.where(n < need, cand, cut)

        cut = lax.fori_loop(0, n_pos_bits, cut_bit, jnp.zeros((tq, 1), I32))
        cut_scr[...] = jnp.broadcast_to(cut, cut_scr.shape)

    thr4 = _tile4(thr)
    cut4 = _tile4(cut_scr[:, 0:1])

    def attend_chunk(c, carry):
        k4 = _tile4(key_scr[c])
        kpos = _key_positions(c)
        mask = ((k4 > thr4) | ((k4 == thr4) & (kpos <= cut4))) & (kpos <= qpos4)
        s = _dot(qd, kdt_ref[c]) * scale
        return _softmax_step(s, mask, _chunk_rows(vd_ref, c), carry)

    o = _softmax_done(lax.fori_loop(0, n_chunks, attend_chunk, _softmax_init(4 * tq)))
    o_ref[...] = jnp.concatenate([o[h * tq:(h + 1) * tq] for h in range(4)], axis=-1).astype(o_ref.dtype)


def _dsa(qd, qi, misc, kd, vd, ki, *, nbatch, nq, tq, pos0, pos_stride, n_keep, n_chunks):
    nrows = qd.shape[1]
    kern = functools.partial(_dsa_kernel, tq=tq, pos0=pos0, pos_stride=pos_stride, n_keep=n_keep)
    head_spec = pl.BlockSpec((4, tq, HEAD_DIM), lambda b, i: (0, b * nq + i, 0))
    return pl.pallas_call(
        kern,
        out_shape=jax.ShapeDtypeStruct((nrows, 4 * HEAD_DIM), BF16),
        grid=(nbatch, nq),
        in_specs=[head_spec, head_spec, pl.BlockSpec((tq, 128), lambda b, i: (b * nq + i, 0)),
                  kd[1], vd[1], ki[1]],
        out_specs=pl.BlockSpec((tq, 4 * HEAD_DIM), lambda b, i: (b * nq + i, 0)),
        scratch_shapes=[pltpu.VMEM((n_chunks, tq, CK), I32), pltpu.VMEM((tq, 128), I32)],
        compiler_params=_cparams(2),
        name="dsa",
    )(qd, qi, misc, kd[0], vd[0], ki[0])


CONV_HALO = 32


def _conv_kernel(cur_ref, halo_ref, w_ref, b_ref, g_ref, beta_ref, o_ref, tail_ref, hx_scr, *, tc):
    i = pl.program_id(1)

    def glu(x):
        return x[:, :CONV_CH] * _sigmoid(x[:, CONV_CH:])

    hx_scr[0:CONV_HALO] = jnp.where(i > 0, glu(halo_ref[...]), 0.0)
    hx_scr[CONV_HALO:CONV_HALO + tc] = glu(cur_ref[...])
    first = CONV_HALO - (CONV_WIDTH - 1)
    y = jnp.broadcast_to(b_ref[...], (tc, CONV_CH))
    for k in range(CONV_WIDTH):
        y = y + hx_scr[first + k:first + k + tc] * w_ref[k:k + 1, :]
    y = _layernorm(y, g_ref[...], beta_ref[...])
    o_ref[...] = (y * _sigmoid(y)).astype(o_ref.dtype)
    tail_ref[...] = hx_scr[tc:tc + CONV_HALO]


def _conv(glu, w, b, g, beta, *, nbatch, seq, tc):
    n = glu.shape[0]
    nt = seq // tc
    per = tc // CONV_HALO
    vec = pl.BlockSpec((1, CONV_CH), lambda bb, i: (0, 0))
    return pl.pallas_call(
        functools.partial(_conv_kernel, tc=tc),
        out_shape=[jax.ShapeDtypeStruct((n, CONV_CH), BF16),
                   jax.ShapeDtypeStruct((nbatch, CONV_HALO, CONV_CH), F32)],
        grid=(nbatch, nt),
        in_specs=[pl.BlockSpec((tc, 2 * CONV_CH), lambda bb, i: (bb * nt + i, 0)),
                  pl.BlockSpec((CONV_HALO, 2 * CONV_CH),
                               lambda bb, i: (jnp.maximum((bb * nt + i) * per - 1, 0), 0)),
                  pl.BlockSpec((CONV_WIDTH, CONV_CH), lambda bb, i: (0, 0)), vec, vec, vec],
        out_specs=[pl.BlockSpec((tc, CONV_CH), lambda bb, i: (bb * nt + i, 0)),
                   pl.BlockSpec((None, CONV_HALO, CONV_CH), lambda bb, i: (bb, 0, 0))],
        scratch_shapes=[pltpu.VMEM((tc + CONV_HALO, CONV_CH), F32)],
        compiler_params=_cparams(2),
        name="conv",
    )(glu, glu, w, b, g, beta)


def _gelu(x):
    return 0.5 * x * (1.0 + lax.erf(x * (2.0 ** -0.5)))


def _sgu_kernel(uv_ref, g_ref, beta_ref, w_ref, bias_ref, o_ref, *, ts):
    a = _gelu(uv_ref[...])
    u = a[:, :SGU_CH]
    vn = _layernorm(a[:, SGU_CH:], g_ref[...], beta_ref[...]).astype(BF16)
    ri = lax.broadcasted_iota(I32, (SGU_CHUNK, SGU_CHUNK), 0)
    ci = lax.broadcasted_iota(I32, (SGU_CHUNK, SGU_CHUNK), 1)
    group = lax.broadcasted_iota(I32, (SGU_CHUNK, SGU_CH), 1) // (SGU_CH // SGU_GROUPS)
    ws = [jnp.where(ci <= ri, w_ref[gi], 0.0).astype(BF16) for gi in range(SGU_GROUPS)]
    for c in range(ts // SGU_CHUNK):
        rows = slice(c * SGU_CHUNK, (c + 1) * SGU_CHUNK)
        mixed = bias_ref[...]
        for gi in range(SGU_GROUPS):
            mixed = mixed + jnp.where(group == gi, _dot(ws[gi], vn[rows]), 0.0)
        o_ref[rows, :] = (u[rows] * mixed).astype(o_ref.dtype)


def _sgu(uv, g, beta, w, bias, *, ts):
    n = uv.shape[0]
    vec = pl.BlockSpec((1, SGU_CH), lambda i: (0, 0))
    return pl.pallas_call(
        functools.partial(_sgu_kernel, ts=ts),
        out_shape=jax.ShapeDtypeStruct((n, SGU_CH), BF16),
        grid=(n // ts,),
        in_specs=[pl.BlockSpec((ts, 2 * SGU_CH), lambda i: (i, 0)), vec, vec,
                  pl.BlockSpec((SGU_GROUPS, SGU_CHUNK, SGU_CHUNK), lambda i: (0, 0, 0)),
                  pl.BlockSpec((SGU_CHUNK, SGU_CH), lambda i: (0, 0))],
        out_specs=pl.BlockSpec((ts, SGU_CH), lambda i: (i, 0)),
        compiler_params=_cparams(1),
        name="sgu",
    )(uv, g, beta, w, bias)


def _step_mix_kernel(glu_ref, uv_ref, hist_ref, cw_ref, cb_ref, cg_ref, cbeta_ref, sg_ref, sbeta_ref,
                     sdiag_ref, sbias_ref, ob_ref, oc_ref, hx_ref, v_ref):
    x = glu_ref[...]
    hx = x[:, :CONV_CH] * _sigmoid(x[:, CONV_CH:])
    hx_ref[...] = hx
    y = cb_ref[...] + hx * cw_ref[CONV_WIDTH - 1:CONV_WIDTH, :]
    for k in range(CONV_WIDTH - 1):
        y = y + hist_ref[k] * cw_ref[k:k + 1, :]
    y = _layernorm(y, cg_ref[...], cbeta_ref[...])
    ob_ref[...] = (y * _sigmoid(y)).astype(ob_ref.dtype)
    a = _gelu(uv_ref[...])
    v = a[:, SGU_CH:]
    v_ref[...] = v
    vn = _layernorm(v, sg_ref[...], sbeta_ref[...])
    oc_ref[...] = (a[:, :SGU_CH] * (sdiag_ref[...] * vn + sbias_ref[...])).astype(oc_ref.dtype)


def _step_mix(glu, uv, hist, cw, cb, cg, cbeta, sg, sbeta, sdiag, sbias):
    n = glu.shape[0]
    sds = jax.ShapeDtypeStruct
    return pl.pallas_call(
        _step_mix_kernel,
        out_shape=[sds((n, CONV_CH), BF16), sds((n, SGU_CH), BF16), sds((n, CONV_CH), F32), sds((n, SGU_CH), F32)],
        compiler_params=pltpu.CompilerParams(vmem_limit_bytes=VMEM_LIMIT),
        name="step_mix",
    )(glu, uv, hist, cw, cb, cg, cbeta, sg, sbeta, sdiag, sbias)


def _merge_kernel(x_ref, g_ref, oa_ref, ob_ref, oc_ref, od_ref, wg_ref, wb_ref, wo_ref, y_ref):
    x = x_ref[...]
    d = x.shape[-1]
    hb = _rms(x, g_ref[...]).astype(BF16)
    acc = jnp.zeros(x.shape, F32)
    for k, o_ref in enumerate((oa_ref, ob_ref, oc_ref, od_ref)):
        gate = _sigmoid(_dot(hb, wg_ref[:, k * d:(k + 1) * d]))
        acc = acc + gate * _dot(o_ref[...], wb_ref[k])
    y_ref[...] = x + _dot(acc.astype(BF16), wo_ref[...])


def _merge(x, g, oa, ob, oc, od, wg, wb, wo, *, tm):
    n, d = x.shape
    row = pl.BlockSpec((tm, d), lambda i: (i, 0))
    br = pl.BlockSpec((tm, BRANCH_WIDTH), lambda i: (i, 0))
    once = pl.Buffered(1)
    return pl.pallas_call(
        _merge_kernel,
        out_shape=jax.ShapeDtypeStruct((n, d), F32),
        grid=(n // tm,),
        in_specs=[row, pl.BlockSpec((1, d), lambda i: (0, 0)), br, br, br, br,
                  pl.BlockSpec((d, N_BRANCH * d), lambda i: (0, 0), pipeline_mode=once),
                  pl.BlockSpec((N_BRANCH, BRANCH_WIDTH, d), lambda i: (0, 0, 0), pipeline_mode=once),
                  pl.BlockSpec((d, d), lambda i: (0, 0), pipeline_mode=once)],
        out_specs=row,
        compiler_params=_cparams(1),
        name="merge",
    )(x, g, oa, ob, oc, od, wg, wb, wo)


def _mlp_kernel(x_ref, g_ref, wu_ref, wd_ref, gf_ref, y_ref, *n_ref):
    x = x_ref[...]
    hb = _rms(x, g_ref[...]).astype(BF16)
    a = jnp.square(jnp.maximum(_dot(hb, wu_ref[...]), 0.0)).astype(BF16)
    y = x + _dot(a, wd_ref[...])
    y_ref[...] = y
    if n_ref:
        n_ref[0][...] = _rms(y, gf_ref[...])


def _mlp(x, g, wu, wd, gf, *, tm, final):
    n, d = x.shape
    row = pl.BlockSpec((tm, d), lambda i: (i, 0))
    vec = pl.BlockSpec((1, d), lambda i: (0, 0))
    once = pl.Buffered(1)
    out_shape = [jax.ShapeDtypeStruct((n, d), F32)] * (2 if final else 1)
    return pl.pallas_call(
        _mlp_kernel,
        out_shape=out_shape,
        grid=(n // tm,),
        in_specs=[row, vec, pl.BlockSpec(wu.shape, lambda i: (0, 0), pipeline_mode=once),
                  pl.BlockSpec(wd.shape, lambda i: (0, 0), pipeline_mode=once), vec],
        out_specs=[row] * (2 if final else 1),
        compiler_params=_cparams(1),
        name="mlp",
    )(x, g, wu, wd, gf)


def _select_rows(width, first):
    r = lax.broadcasted_iota(I32, (HEAD_DIM, width), 0)
    c = lax.broadcasted_iota(I32, (HEAD_DIM, width), 1)
    return jnp.where(c == r + first, 1.0, 0.0).astype(BF16)


def _new_row_chunk(row):
    first = lax.broadcasted_iota(I32, (CK, row.shape[-1]), 0) == 0
    return jnp.where(first, jnp.broadcast_to(row, (CK, row.shape[-1])), 0.0)


def _assemble_kernel(pt_ref, *refs, n_pages, page):
    nsa_pages = refs[:n_pages]
    dsa_pages = refs[n_pages:2 * n_pages]
    win_ref, nsa_new_ref, dsa_new_ref = refs[2 * n_pages:2 * n_pages + 3]
    cmp_ref, kst_ref, vs_ref, kwt_ref, vw_ref, kdt_ref, vd_ref, kit_ref = refs[2 * n_pages + 3:]
    per_chunk = CK // page
    pick_kslc = _select_rows(256, 128)
    pick_kd = _select_rows(192, 0)
    pick_ki = _select_rows(192, 128)
    pick_kwin = _select_rows(128, 0)

    def put_nsa(x, c, lanes, rows):
        kst_ref[c, :, lanes] = _dot_nt(pick_kslc, x.astype(BF16)).astype(BF16)
        vs_ref[rows, :] = x[:, 192:256].astype(BF16)

    def put_dsa(x, c, lanes, rows):
        xb = x.astype(BF16)
        kdt_ref[c, :, lanes] = _dot_nt(pick_kd, xb).astype(BF16)
        kit_ref[c, :, lanes] = _dot_nt(pick_ki, xb).astype(BF16)
        vd_ref[rows, :] = x[:, 64:128].astype(BF16)

    for p in range(n_pages):
        c, r = divmod(p, per_chunk)
        lanes = slice(r * page, (r + 1) * page)
        rows = slice(p * page, (p + 1) * page)
        x = nsa_pages[p][...]
        cmp_ref[rows, :] = x[:, 0:128]
        put_nsa(x, c, lanes, rows)
        put_dsa(dsa_pages[p][...], c, lanes, rows)

    c_new = n_pages // per_chunk
    new_rows = slice(c_new * CK, (c_new + 1) * CK)
    nsa_new = nsa_new_ref[...]
    put_nsa(_new_row_chunk(nsa_new[:, 0:256]), c_new, slice(0, CK), new_rows)
    put_dsa(_new_row_chunk(dsa_new_ref[...]), c_new, slice(0, CK), new_rows)

    def put_win(x, c, rows):
        kwt_ref[c] = _dot_nt(pick_kwin, x.astype(BF16)).astype(BF16)
        vw_ref[rows, :] = x[:, 64:128].astype(BF16)

    put_win(win_ref[...], 0, slice(0, CK))
    put_win(_new_row_chunk(nsa_new[:, 256:384]), 1, slice(CK, 2 * CK))


def _assemble(page_table, cache_nsa, cache_dsa, state_win, nsa_new, dsa_new, *, layer):
    nseq, n_pages = page_table.shape
    page = cache_nsa.shape[2]
    n_chunks = n_pages * page // CK + 1
    keys = n_chunks * CK
    sds = jax.ShapeDtypeStruct

    def page_spec(width, p):
        return pl.BlockSpec((None, None, page, width), lambda b, pt: (pt[b, p], layer, 0, 0))

    def per_seq(*shape):
        return pl.BlockSpec((None,) + shape, lambda b, pt: (b,) + (0,) * len(shape))

    grid_spec = pltpu.PrefetchScalarGridSpec(
        num_scalar_prefetch=1,
        grid=(nseq,),
        in_specs=([page_spec(256, p) for p in range(n_pages)] + [page_spec(192, p) for p in range(n_pages)]
                  + [pl.BlockSpec((None, None, CK, 128), lambda b, pt: (b, layer, 0, 0)),
                     per_seq(1, 384), per_seq(1, 192)]),
        out_specs=[per_seq(n_pages * page, 128), per_seq(n_chunks, HEAD_DIM, CK), per_seq(keys, HEAD_DIM),
                   per_seq(2, HEAD_DIM, CK), per_seq(2 * CK, HEAD_DIM),
                   per_seq(n_chunks, HEAD_DIM, CK), per_seq(keys, HEAD_DIM), per_seq(n_chunks, HEAD_DIM, CK)],
    )
    return pl.pallas_call(
        functools.partial(_assemble_kernel, n_pages=n_pages, page=page),
        out_shape=[sds((nseq, n_pages * page, 128), F32), sds((nseq, n_chunks, HEAD_DIM, CK), BF16),
                   sds((nseq, keys, HEAD_DIM), BF16), sds((nseq, 2, HEAD_DIM, CK), BF16),
                   sds((nseq, 2 * CK, HEAD_DIM), BF16), sds((nseq, n_chunks, HEAD_DIM, CK), BF16),
                   sds((nseq, keys, HEAD_DIM), BF16), sds((nseq, n_chunks, HEAD_DIM, CK), BF16)],
        grid_spec=grid_spec,
        compiler_params=_cparams(1),
        name="assemble",
    )(page_table, *([cache_nsa] * n_pages), *([cache_dsa] * n_pages), state_win, nsa_new, dsa_new)


def _prep_w_in(w_in):
    pts = np.cumsum(SPLIT_WIDTHS)[:-1].tolist()
    qa, kva, ga, glu, uv, qd, kvd, qi, ki, wi = jnp.split(w_in, pts, axis=-1)
    pad = jnp.zeros(w_in.shape[:2] + (128 - 16,), w_in.dtype)
    w = jnp.concatenate([qa, kva, glu, uv, qd, qi, wi, ga, pad, kvd, ki], axis=-1).astype(BF16)
    wt = jnp.concatenate([kva[..., 128:192], kva[..., 256:320], kvd[..., 0:64], ki], axis=-1)
    return w, jnp.swapaxes(wt, 1, 2).astype(BF16)


def _seq_spec(*shape):
    return pl.BlockSpec((None,) + shape, lambda b, i: (b,) + (0,) * len(shape))


def kernel(x_prompt, x_sample, cache_nsa, cache_dsa, state_win, state_conv, page_table, norm1_g, norm2_g,
           final_g, w_in, cmp_pos, cmp_w1, cmp_w2, conv_w, conv_b, conv_ln_g, conv_ln_b, sgu_ln_g, sgu_ln_b,
           sgu_w, sgu_b, w_branch, w_gate, w_out, w_up, w_down):
    nb_, seq, d = x_prompt.shape
    nseq, t_dec, _ = x_sample.shape
    depth = w_in.shape[0]
    n_pool, _, page, _, _ = cache_nsa.shape
    n_pages = page_table.shape[1]
    past = n_pages * page
    w_buf = state_win.shape[2]
    assert t_dec == 1 and seq % CK == 0 and past % CK == 0 and w_buf == CK == NSA_WINDOW
    nbp = seq // NSA_BLOCK
    assert nbp <= NB_PAD and past // NSA_BLOCK < NB_PAD
    n_tok = nb_ * seq
    sr = SAMPLE_ROWS

    w_proj, w_proj_t = _prep_w_in(w_in)
    wg, wb, wo = w_gate.astype(BF16), w_branch.astype(BF16), w_out.astype(BF16)
    wu, wd = w_up.astype(BF16), w_down.astype(BF16)
    c_pos = cmp_pos.reshape(depth, 2, 1, NSA_BLOCK * HEAD_DIM)
    c_w1, c_w2 = cmp_w1.astype(BF16), cmp_w2.astype(BF16)
    sgu_bias = jnp.repeat(jnp.swapaxes(sgu_b, 1, 2), SGU_CH // SGU_GROUPS, axis=2)
    sgu_diag = jnp.repeat(sgu_w[:, :, 0, 0], SGU_CH // SGU_GROUPS, axis=1)[:, None, :]
    sgu_bias0 = sgu_bias[:, 0:1, :]
    vec = lambda a, l: a[l][None, :]
    cache_nsa4 = cache_nsa.reshape(n_pool, depth, page, 4 * HEAD_DIM)
    cache_dsa4 = cache_dsa.reshape(n_pool, depth, page, 3 * HEAD_DIM)
    state_win4 = state_win.reshape(nseq, depth, w_buf, 2 * HEAD_DIM)
    conv_hist = jnp.transpose(state_conv, (1, 2, 0, 3))

    n_chunks_p = seq // CK
    n_chunks_s = past // CK + 1
    nq = seq // Q_BLOCK

    def chunks_of(arr, slot, n_chunks):
        return arr, pl.BlockSpec((n_chunks, HEAD_DIM, CK), lambda b, i: (b, slot, 0))

    def rows_of(arr, slot, n_rows):
        return arr, pl.BlockSpec((None, n_rows, HEAD_DIM), lambda b, i: (slot, b, 0))

    xp = x_prompt.reshape(n_tok, d)
    xs = x_sample.reshape(nseq, d)
    outs = {k: [] for k in ("nsa_p", "nsa_s", "dsa_p", "dsa_s", "win_p", "win_s", "conv_p", "conv_s", "sgu_s")}
    yp = ys = None
    for l in range(depth):
        final = l == depth - 1
        qn, kva, glu, uv, qd, qi, misc, dsa_rows, kt4, v4 = _proj(
            xp, vec(norm1_g, l), w_proj[l], w_proj_t[l], tm=CK, emit_t=True)
        kva3 = kva.reshape(nb_, seq, 6, HEAD_DIM)
        cmp_in = jnp.stack([kva3[:, :, 0].reshape(nb_ * nbp, NSA_BLOCK * HEAD_DIM),
                            kva3[:, :, 1].reshape(nb_ * nbp, NSA_BLOCK * HEAD_DIM)])
        kvc = _compress(cmp_in, c_pos[l], c_w1[l], c_w2[l], tr=nb_ * nbp)
        kvc = jnp.pad(kvc.reshape(2, nb_, nbp, HEAD_DIM), ((0, 0), (0, 0), (0, NB_PAD - nbp), (0, 0)))
        kc, vc = kvc[0], kvc[1]
        o_a = _nsa(qn, misc, kc, vc, chunks_of(kt4, 0, n_chunks_p), rows_of(v4, 0, seq),
                   chunks_of(kt4, 1, n_chunks_p), rows_of(v4, 1, seq),
                   nbatch=nb_, nq=nq, tq=Q_BLOCK, pos0=0, pos_stride=1, win_base=0,
                   n_sel=min(NSA_N_SEL, seq // NSA_BLOCK))
        o_d = _dsa(qd, qi, misc, chunks_of(kt4, 2, n_chunks_p), rows_of(v4, 2, seq),
                   chunks_of(kt4, 3, n_chunks_p), nbatch=nb_, nq=nq, tq=Q_BLOCK, pos0=0, pos_stride=1,
                   n_keep=min(DSA_TOPK, seq // 4), n_chunks=n_chunks_p)
        o_b, conv_tail = _conv(glu, conv_w[l], vec(conv_b, l), vec(conv_ln_g, l), vec(conv_ln_b, l),
                               nbatch=nb_, seq=seq, tc=CK)
        o_c = _sgu(uv, vec(sgu_ln_g, l), vec(sgu_ln_b, l), sgu_w[l], sgu_bias[l], ts=CK)
        xp = _merge(xp, vec(norm1_g, l), o_a, o_b, o_c, o_d, wg[l], wb[l], wo[l], tm=256)
        res = _mlp(xp, vec(norm2_g, l), wu[l], wd[l], final_g[None, :], tm=256, final=final)
        xp = res[0]
        if final:
            yp = res[1]
        outs["nsa_p"].append(kva3[:, :, :4])
        outs["dsa_p"].append(dsa_rows.reshape(nb_, seq, 3, HEAD_DIM))
        outs["win_p"].append(kva3[:, seq - min(NSA_WINDOW, seq):, 4:])
        outs["conv_p"].append(conv_tail[:, CONV_HALO - (CONV_WIDTH - 1):])

        qn, kva, glu, uv, qd, qi, misc, dsa_rows = _proj(
            xs, vec(norm1_g, l), w_proj[l], w_proj_t[l], tm=nseq, emit_t=False)
        cmp_past, kst, vs, kwt, vw, kdt, vdd, kit = _assemble(
            page_table, cache_nsa4, cache_dsa4, state_win4, kva[:, None, :], dsa_rows[:, None, :], layer=l)
        nbs = past // NSA_BLOCK
        cmp_in = jnp.stack([cmp_past[:, :, 0:64].reshape(nseq * nbs, NSA_BLOCK * HEAD_DIM),
                            cmp_past[:, :, 64:128].reshape(nseq * nbs, NSA_BLOCK * HEAD_DIM)])
        kvc = _compress(cmp_in, c_pos[l], c_w1[l], c_w2[l], tr=256)
        kvc = jnp.pad(kvc.reshape(2, nseq, nbs, HEAD_DIM), ((0, 0), (0, 0), (0, NB_PAD - nbs), (0, 0)))

        def pad_rows(a):
            a = a[..., :, None, :]
            widths = [(0, 0)] * (a.ndim - 2) + [(0, sr - 1), (0, 0)]
            a = jnp.pad(a, widths)
            return a.reshape(a.shape[:-3] + (nseq * sr, a.shape[-1]))

        misc_r = pad_rows(misc)
        o_a = _nsa(pad_rows(qn), misc_r, kvc[0], kvc[1],
                   (kst, _seq_spec(n_chunks_s, HEAD_DIM, CK)), (vs, _seq_spec(n_chunks_s * CK, HEAD_DIM)),
                   (kwt, _seq_spec(2, HEAD_DIM, CK)), (vw, _seq_spec(2 * CK, HEAD_DIM)),
                   nbatch=nseq, nq=1, tq=sr, pos0=past, pos_stride=0, win_base=past // CK - 1,
                   n_sel=min(NSA_N_SEL, nbs + 1))
        o_d = _dsa(pad_rows(qd), pad_rows(qi), misc_r,
                   (kdt, _seq_spec(n_chunks_s, HEAD_DIM, CK)), (vdd, _seq_spec(n_chunks_s * CK, HEAD_DIM)),
                   (kit, _seq_spec(n_chunks_s, HEAD_DIM, CK)), nbatch=nseq, nq=1, tq=sr, pos0=past,
                   pos_stride=0, n_keep=min(DSA_TOPK, (past + 1) // 4), n_chunks=n_chunks_s)
        o_a = o_a.reshape(nseq, sr, -1)[:, 0]
        o_d = o_d.reshape(nseq, sr, -1)[:, 0]
        o_b, o_c, hx_new, v_rows = _step_mix(
            glu, uv, conv_hist[l], conv_w[l], vec(conv_b, l), vec(conv_ln_g, l), vec(conv_ln_b, l),
            vec(sgu_ln_g, l), vec(sgu_ln_b, l), sgu_diag[l], sgu_bias0[l])
        xs = _merge(xs, vec(norm1_g, l), o_a, o_b, o_c, o_d, wg[l], wb[l], wo[l], tm=nseq)
        res = _mlp(xs, vec(norm2_g, l), wu[l], wd[l], final_g[None, :], tm=nseq, final=final)
        xs = res[0]
        if final:
            ys = res[1]
        kva3 = kva.reshape(nseq, 1, 6, HEAD_DIM)
        outs["nsa_s"].append(kva3[:, :, :4])
        outs["dsa_s"].append(dsa_rows.reshape(nseq, 1, 3, HEAD_DIM))
        outs["win_s"].append(jnp.concatenate([state_win[:, l], kva3[:, :, 4:]], axis=1)[:, 1:])
        outs["conv_s"].append(jnp.concatenate([state_conv[:, l], hx_new[:, None, :]], axis=1)[:, 1:])
        outs["sgu_s"].append(v_rows[:, None, :])

    st = lambda k: jnp.stack(outs[k], axis=1)
    return (yp.reshape(nb_, seq, d), ys.reshape(nseq, 1, d), st("nsa_p"), st("nsa_s"), st("dsa_p"), st("dsa_s"),
            st("win_p"), st("win_s"), st("conv_p"), st("conv_s"), st("sgu_s"))
```

```python
import functools

import numpy as np
import jax
import jax.numpy as jnp
from jax import lax
from jax.experimental import pallas as pl
from jax.experimental.pallas import tpu as pltpu

F32, BF16, I32 = jnp.float32, jnp.bfloat16, jnp.int32

HEAD_DIM = 64
NSA_HEADS = 4
NSA_BLOCK = 64
NSA_N_SEL = 16
NSA_WINDOW = 512
FORCE_BONUS = 1.0e4
CONV_CH = 256
CONV_WIDTH = 31
SGU_CH = 256
SGU_GROUPS = 4
SGU_CHUNK = 128
DSA_HEADS = 4
IDX_HEADS = 4
DSA_TOPK = 256
N_BRANCH = 4
BRANCH_WIDTH = 256
Q_BLOCK = 128
EPS = 1e-6
NEG = -1e30

SPLIT_WIDTHS = (256, 384, 12, 512, 512, 256, 128, 256, 64, 4)

CK = 512
NB_PAD = 128
SAMPLE_ROWS = 16
VMEM_LIMIT = 56 * 1024 * 1024
INT_MIN = -2 ** 31
NEG_KEY = int(np.float32(NEG).view(np.int32)) ^ 0x7FFFFFFF

W_QA, W_KVA, W_GLU, W_UV, W_QD, W_QI, W_MISC, W_DSA = 0, 256, 640, 1152, 1664, 1920, 2176, 2304
W_TOTAL = 2496
MISC_WI, MISC_GA = 0, 4


def _cparams(n_axes):
    return pltpu.CompilerParams(dimension_semantics=("arbitrary",) * n_axes,
                                vmem_limit_bytes=VMEM_LIMIT)


def _dot(a, b):
    return jnp.dot(a, b, preferred_element_type=F32)


def _dot_nt(a, b):
    return lax.dot_general(a, b, (((1,), (1,)), ((), ())), preferred_element_type=F32)


def _rms(x, g):
    return x * lax.rsqrt(jnp.mean(x * x, axis=-1, keepdims=True) + EPS) * g


def _layernorm(x, g, b):
    mu = jnp.mean(x, axis=-1, keepdims=True)
    xc = x - mu
    var = jnp.mean(xc * xc, axis=-1, keepdims=True)
    return xc * lax.rsqrt(var + EPS) * g + b


def _sigmoid(x):
    return 1.0 / (1.0 + jnp.exp(-x))


def _tile4(a):
    return jnp.concatenate([a, a, a, a], axis=0)


def _proj_kernel(x_ref, g_ref, w_ref, wt_ref, qn_ref, kva_ref, glu_ref, uv_ref, qd_ref, qi_ref,
                 misc_ref, dsa_ref, *t_refs):
    hb = _rms(x_ref[...], g_ref[...]).astype(BF16)

    def slab(off, width):
        return _dot(hb, w_ref[:, off:off + width])

    qa = slab(W_QA, 256)
    qd = slab(W_QD, 256)
    qi = slab(W_QI, 256)
    for h in range(4):
        cols = slice(h * HEAD_DIM, (h + 1) * HEAD_DIM)
        qn_ref[h] = qa[:, cols].astype(BF16)
        qd_ref[h] = qd[:, cols].astype(BF16)
        qi_ref[h] = qi[:, cols].astype(BF16)
    kv = slab(W_KVA, 384)
    kva_ref[...] = kv
    glu_ref[...] = slab(W_GLU, 512)
    uv_ref[...] = slab(W_UV, 512)
    misc_ref[...] = slab(W_MISC, 128)
    dr = slab(W_DSA, 192)
    dsa_ref[...] = dr
    if t_refs:
        kt_ref, v_ref = t_refs
        kt_ref[0] = _dot_nt(wt_ref[...], hb).astype(BF16)
        v_ref[0] = kv[:, 192:256].astype(BF16)
        v_ref[1] = kv[:, 320:384].astype(BF16)
        v_ref[2] = dr[:, 64:128].astype(BF16)


def _proj(x, g, w, wt, *, tm, emit_t):
    n, d = x.shape
    sds = jax.ShapeDtypeStruct
    out_shape = [sds((4, n, HEAD_DIM), BF16), sds((n, 384), F32), sds((n, 512), F32), sds((n, 512), F32),
                 sds((4, n, HEAD_DIM), BF16), sds((4, n, HEAD_DIM), BF16), sds((n, 128), F32), sds((n, 192), F32)]
    head_spec = pl.BlockSpec((4, tm, HEAD_DIM), lambda i: (0, i, 0))

    def row_spec(width):
        return pl.BlockSpec((tm, width), lambda i: (i, 0))

    out_specs = [head_spec, row_spec(384), row_spec(512), row_spec(512), head_spec, head_spec,
                 row_spec(128), row_spec(192)]
    if emit_t:
        assert tm == CK
        out_shape += [sds((n // CK, 256, CK), BF16), sds((3, n, HEAD_DIM), BF16)]
        out_specs += [pl.BlockSpec((1, 256, CK), lambda i: (i, 0, 0)),
                      pl.BlockSpec((3, tm, HEAD_DIM), lambda i: (0, i, 0))]
    return pl.pallas_call(
        _proj_kernel,
        out_shape=out_shape,
        grid=(n // tm,),
        in_specs=[row_spec(d), pl.BlockSpec((1, d), lambda i: (0, 0)),
                  pl.BlockSpec((d, W_TOTAL), lambda i: (0, 0)),
                  pl.BlockSpec((256, d), lambda i: (0, 0))],
        out_specs=out_specs,
        compiler_params=_cparams(1),
        name="proj",
    )(x, g, w, wt)


def _compress_kernel(x_ref, pos_ref, w1_ref, w2_ref, o_ref):
    xb = (x_ref[...] + pos_ref[...]).astype(BF16)
    a = _dot(xb, w1_ref[...])
    a = a * _sigmoid(a)
    o_ref[...] = _dot(a.astype(BF16), w2_ref[...]).astype(o_ref.dtype)


def _compress(x, pos, w1, w2, *, tr):
    _, r, width = x.shape
    return pl.pallas_call(
        _compress_kernel,
        out_shape=jax.ShapeDtypeStruct((2, r, HEAD_DIM), BF16),
        grid=(2, r // tr),
        in_specs=[pl.BlockSpec((None, tr, width), lambda s, i: (s, i, 0)),
                  pl.BlockSpec((None, 1, width), lambda s, i: (s, 0, 0)),
                  pl.BlockSpec((None, width, HEAD_DIM), lambda s, i: (s, 0, 0)),
                  pl.BlockSpec((None, HEAD_DIM, HEAD_DIM), lambda s, i: (s, 0, 0))],
        out_specs=pl.BlockSpec((None, tr, HEAD_DIM), lambda s, i: (s, i, 0)),
        compiler_params=_cparams(2),
        name="compress",
    )(x, pos, w1, w2)


def _masked_probs(s, mask):
    s = jnp.where(mask, s, NEG)
    m = jnp.max(s, axis=-1, keepdims=True)
    p = jnp.where(mask, jnp.exp(s - m), 0.0)
    return p / jnp.maximum(jnp.sum(p, axis=-1, keepdims=True), 1e-30)


def _softmax_step(s, bias, v, carry):
    m, l, acc = carry
    tq = bias.shape[0]
    s = s.reshape(4, tq, CK) + bias[None]
    m_new = jnp.maximum(m, jnp.max(s, axis=-1, keepdims=True))
    alpha = jnp.exp(m - m_new)
    p = jnp.exp(s - m_new)
    l = alpha * l + jnp.sum(p, axis=-1, keepdims=True)
    acc = alpha.reshape(4 * tq, 1) * acc + _dot(p.reshape(4 * tq, CK).astype(BF16), v)
    return m_new, l, acc


def _softmax_init(tq):
    return (jnp.full((4, tq, 1), NEG, F32), jnp.zeros((4, tq, 1), F32), jnp.zeros((4 * tq, HEAD_DIM), F32))


def _softmax_done(carry):
    _, l, acc = carry
    return acc / jnp.maximum(l.reshape(acc.shape[0], 1), 1e-30)


def _mask_bias(allowed):
    return jnp.where(allowed, 0.0, NEG)


def _query_positions(tq, pos0, pos_stride):
    i = pl.program_id(1)
    start = pos0 + i * (tq * pos_stride)
    last = start + (tq - 1) * pos_stride
    t = lax.broadcasted_iota(I32, (tq, 1), 0)
    qpos = start + t * pos_stride
    return start, last, qpos


def _heads_as_rows(ref, scale):
    return jnp.concatenate([ref[h] for h in range(4)], axis=0) * scale


def _key_positions(c):
    return c * CK + lax.broadcasted_iota(I32, (1, CK), 1)


def _chunk_rows(ref, c):
    return ref[pl.ds(pl.multiple_of(c * CK, CK), CK), :]


PICK_LANES = 128


def _nsa_kernel(q_ref, misc_ref, kc_ref, vc_ref, kst_ref, vs_ref, kwt_ref, vw_ref, o_ref, *,
                tq, pos0, pos_stride, win_base, n_sel):
    nb = kc_ref.shape[0]
    start, last, qpos = _query_positions(tq, pos0, pos_stride)
    qpos4 = _tile4(qpos)
    q = _heads_as_rows(q_ref, HEAD_DIM ** -0.5)

    jb = lax.broadcasted_iota(I32, (1, nb), 1)
    mask_c = ((jb + 1) * NSA_BLOCK - 1) <= qpos4
    p_c = _masked_probs(_dot_nt(q, kc_ref[...]), mask_c)
    o_c = _dot(p_c.astype(BF16), vc_ref[...])
    imp = p_c[0:tq] + p_c[tq:2 * tq] + p_c[2 * tq:3 * tq] + p_c[3 * tq:4 * tq]

    cur = qpos // NSA_BLOCK
    forced = (jb == 0) | (jb == cur) | (jb == cur - 1)
    score = jnp.where(jb <= cur, imp, -1.0) + jnp.where(forced, FORCE_BONUS, 0.0)
    if tq < PICK_LANES:
        score = jnp.concatenate([score, jnp.zeros((PICK_LANES - tq, nb), F32)], axis=0)
    jcol = lax.broadcasted_iota(I32, (nb, PICK_LANES), 0).astype(F32)

    def pick(_, carry):
        sc, sel = carry
        m = jnp.max(sc, axis=0, keepdims=True)
        first = jnp.min(jnp.where(sc == m, jcol, 1e9), axis=0, keepdims=True)
        hit = jcol == first
        return jnp.where(hit, -3e38, sc), jnp.where(hit, 1.0, sel)

    _, sel_t = lax.fori_loop(0, n_sel, pick, (score.T, jnp.zeros((nb, PICK_LANES), F32)))
    sel = sel_t.T[0:tq].astype(BF16)

    jrow = lax.broadcasted_iota(I32, (nb, CK), 0)
    kcol = lax.broadcasted_iota(I32, (nb, CK), 1)

    def sel_chunk(c, carry):
        expand = jnp.where(jrow == ((c * CK + kcol) // NSA_BLOCK), 1.0, 0.0).astype(BF16)
        bias = _mask_bias((_dot(sel, expand) > 0.5) & (_key_positions(c) <= qpos))
        return _softmax_step(_dot(q, kst_ref[c]), bias, _chunk_rows(vs_ref, c), carry)

    o_s = _softmax_done(lax.fori_loop(0, last // CK + 1, sel_chunk, _softmax_init(tq)))

    carry = _softmax_init(tq)
    c1 = start // CK
    for back in (1, 0):
        c = c1 - back
        cc = jnp.maximum(c - win_base, 0)
        kpos = _key_positions(c)
        dpos = qpos - kpos
        bias = _mask_bias((dpos >= 0) & (dpos < NSA_WINDOW) & (kpos >= 0))
        carry = _softmax_step(_dot(q, kwt_ref[cc]), bias, _chunk_rows(vw_ref, cc), carry)
    o_w = _softmax_done(carry)

    g = _sigmoid(misc_ref[...])
    outs = []
    for h in range(4):
        rows = slice(h * tq, (h + 1) * tq)
        col = MISC_GA + 3 * h
        outs.append(g[:, col:col + 1] * o_c[rows] + g[:, col + 1:col + 2] * o_s[rows]
                    + g[:, col + 2:col + 3] * o_w[rows])
    o_ref[...] = jnp.concatenate(outs, axis=-1).astype(o_ref.dtype)


def _nsa(q, misc, kc, vc, ks, vs, kw, vw, *, nbatch, nq, tq, pos0, pos_stride, win_base, n_sel):
    nrows = q.shape[1]
    kern = functools.partial(_nsa_kernel, tq=tq, pos0=pos0, pos_stride=pos_stride, win_base=win_base,
                             n_sel=n_sel)
    return pl.pallas_call(
        kern,
        out_shape=jax.ShapeDtypeStruct((nrows, 4 * HEAD_DIM), BF16),
        grid=(nbatch, nq),
        in_specs=[pl.BlockSpec((4, tq, HEAD_DIM), lambda b, i: (0, b * nq + i, 0)),
                  pl.BlockSpec((tq, 128), lambda b, i: (b * nq + i, 0)),
                  pl.BlockSpec((None, NB_PAD, HEAD_DIM), lambda b, i: (b, 0, 0)),
                  pl.BlockSpec((None, NB_PAD, HEAD_DIM), lambda b, i: (b, 0, 0)),
                  ks[1], vs[1], kw[1], vw[1]],
        out_specs=pl.BlockSpec((tq, 4 * HEAD_DIM), lambda b, i: (b * nq + i, 0)),
        compiler_params=_cparams(2),
        name="nsa",
    )(q, misc, kc, vc, ks[0], vs[0], kw[0], vw[0])


I16 = jnp.int16
I16_MIN = -2 ** 15
LANES = 128


def _dsa_kernel(qd_ref, qi_ref, misc_ref, kdt_ref, vd_ref, kit_ref, o_ref, key_scr, half_scr, cut_scr, *,
                tq, pos0, pos_stride, n_keep, real_rows):
    n_chunks_total = kdt_ref.shape[0]
    start, last, qpos = _query_positions(tq, pos0, pos_stride)
    n_chunks = last // CK + 1
    qd = _heads_as_rows(qd_ref, HEAD_DIM ** -0.5)
    qi = _heads_as_rows(qi_ref, 1.0)
    w = misc_ref[:, MISC_WI:MISC_WI + 4] * (IDX_HEADS ** -0.5 * HEAD_DIM ** -0.5)
    wcol = jnp.concatenate([w[:, h:h + 1] for h in range(4)], axis=0)

    def index_chunk(c, _):
        rel = jnp.maximum(_dot(qi, kit_ref[c]), 0.0) * wcol
        sc = rel[0:tq] + rel[tq:2 * tq] + rel[2 * tq:3 * tq] + rel[3 * tq:4 * tq]
        sc = jnp.where(sc == 0.0, 0.0, sc)
        sc = jnp.where(_key_positions(c) <= qpos, sc, NEG)
        bits = lax.bitcast_convert_type(sc, I32)
        key = jnp.where(bits < 0, bits ^ 0x7FFFFFFF, bits)
        key_scr[c] = key
        half_scr[c] = (key >> 16).astype(I16)
        return 0

    lax.fori_loop(0, n_chunks, index_chunk, 0)

    one16 = jnp.ones((tq, LANES), I16)
    zero16 = jnp.zeros((tq, LANES), I16)

    def lanes16(t):
        return jnp.broadcast_to(t, (tq, LANES)).astype(I16)

    def count16(*preds):
        def body(c, accs):
            x = half_scr[c]
            for j in range(CK // LANES):
                xs = x[:, j * LANES:(j + 1) * LANES]
                accs = tuple(a + jnp.where(p(xs), one16, zero16) for a, p in zip(accs, preds))
            return accs

        accs = lax.fori_loop(0, n_chunks, body, (zero16,) * len(preds))
        return tuple(jnp.sum(a.astype(I32).astype(F32), axis=-1, keepdims=True) for a in accs)

    def bisect16(need):
        zero = lanes16(jnp.zeros((tq, 1), I32))
        t = jnp.where(count16(lambda x: x >= zero)[0] >= need, 0, I16_MIN).astype(I32)

        def bit(b, t):
            cand = t | jnp.left_shift(jnp.int32(1), 14 - b)
            c16 = lanes16(cand)
            return jnp.where(count16(lambda x: x >= c16)[0] >= need, cand, t)

        return lax.fori_loop(0, 15, bit, t)

    keep = float(n_keep)
    hi = bisect16(keep)
    hi16 = lanes16(hi)
    n_hi, n_mem = count16(lambda x: x > hi16, lambda x: x == hi16)

    def low_halves(c, _):
        low = ((key_scr[c] & 0xFFFF) - 2 ** 15).astype(I16)
        x = half_scr[c]
        parts = []
        for j in range(CK // LANES):
            cols = slice(j * LANES, (j + 1) * LANES)
            parts.append(jnp.where(x[:, cols] == hi16, low[:, cols], jnp.full((tq, LANES), I16_MIN, I16)))
        half_scr[c] = jnp.concatenate(parts, axis=-1)
        return 0

    lax.fori_loop(0, n_chunks, low_halves, 0)
    lo = bisect16(keep - n_hi)
    lo16 = lanes16(lo)
    thr = jnp.left_shift(hi, 16) + (lo + 2 ** 15)
    n_gt_lo, n_ge_lo = count16(lambda x: x > lo16, lambda x: x >= lo16)
    n_gt = n_hi + n_gt_lo
    n_ge = n_hi + jnp.where(lo == I16_MIN, n_mem, n_ge_lo)

    def count(pred):
        def body(c, acc):
            ind = jnp.where(pred(key_scr[c], c), 1.0, 0.0)
            return acc + (ind[:, 0:128] + ind[:, 128:256] + ind[:, 256:384] + ind[:, 384:512])

        acc = lax.fori_loop(0, n_chunks, body, jnp.zeros((tq, LANES), F32))
        return jnp.sum(acc, axis=-1, keepdims=True)

    need = keep - n_gt
    is_real = lax.broadcasted_iota(I32, (tq, 1), 0) < real_rows
    tie = (n_ge > keep) & (thr > NEG_KEY) & is_real
    n_pos_bits = (n_chunks_total * CK - 1).bit_length()
    cut_scr[...] = jnp.full(cut_scr.shape, n_chunks_total * CK, I32)

    @pl.when(jnp.max(jnp.where(tie, 1.0, 0.0)) > 0.5)
    def _():
        def cut_bit(b, cut):
            cand = cut + jnp.left_shift(jnp.int32(1), n_pos_bits - 1 - b)
            n = count(lambda k, c: (k == thr) & (_key_positions(c) < cand))
            return jnp.where(n < need, cand, cut)

        cut = lax.fori_loop(0, n_pos_bits, cut_bit, jnp.zeros((tq, 1), I32))
        cut_scr[...] = jnp.broadcast_to(cut, cut_scr.shape)

    cut = cut_scr[:, 0:1]

    def attend_chunk(c, carry):
        k = key_scr[c]
        kpos = _key_positions(c)
        bias = _mask_bias(((k > thr) | ((k == thr) & (kpos <= cut))) & (kpos <= qpos))
        return _softmax_step(_dot(qd, kdt_ref[c]), bias, _chunk_rows(vd_ref, c), carry)

    o = _softmax_done(lax.fori_loop(0, n_chunks, attend_chunk, _softmax_init(tq)))
    o_ref[...] = jnp.concatenate([o[h * tq:(h + 1) * tq] for h in range(4)], axis=-1).astype(o_ref.dtype)


def _dsa(qd, qi, misc, kd, vd, ki, *, nbatch, nq, tq, pos0, pos_stride, n_keep, n_chunks, real_rows):
    nrows = qd.shape[1]
    kern = functools.partial(_dsa_kernel, tq=tq, pos0=pos0, pos_stride=pos_stride, n_keep=n_keep,
                             real_rows=real_rows)
    head_spec = pl.BlockSpec((4, tq, HEAD_DIM), lambda b, i: (0, b * nq + i, 0))
    return pl.pallas_call(
        kern,
        out_shape=jax.ShapeDtypeStruct((nrows, 4 * HEAD_DIM), BF16),
        grid=(nbatch, nq),
        in_specs=[head_spec, head_spec, pl.BlockSpec((tq, 128), lambda b, i: (b * nq + i, 0)),
                  kd[1], vd[1], ki[1]],
        out_specs=pl.BlockSpec((tq, 4 * HEAD_DIM), lambda b, i: (b * nq + i, 0)),
        scratch_shapes=[pltpu.VMEM((n_chunks, tq, CK), I32), pltpu.VMEM((n_chunks, tq, CK), I16),
                        pltpu.VMEM((tq, LANES), I32)],
        compiler_params=_cparams(2),
        name="dsa",
    )(qd, qi, misc, kd[0], vd[0], ki[0])


CONV_HALO = 32


def _conv_kernel(cur_ref, halo_ref, w_ref, b_ref, g_ref, beta_ref, o_ref, tail_ref, hx_scr, *, tc):
    i = pl.program_id(1)

    def glu(x):
        return x[:, :CONV_CH] * _sigmoid(x[:, CONV_CH:])

    hx_scr[0:CONV_HALO] = jnp.where(i > 0, glu(halo_ref[...]), 0.0)
    hx_scr[CONV_HALO:CONV_HALO + tc] = glu(cur_ref[...])
    first = CONV_HALO - (CONV_WIDTH - 1)
    y = jnp.broadcast_to(b_ref[...], (tc, CONV_CH))
    for k in range(CONV_WIDTH):
        y = y + hx_scr[first + k:first + k + tc] * w_ref[k:k + 1, :]
    y = _layernorm(y, g_ref[...], beta_ref[...])
    o_ref[...] = (y * _sigmoid(y)).astype(o_ref.dtype)
    tail_ref[...] = hx_scr[tc:tc + CONV_HALO]


def _conv(glu, w, b, g, beta, *, nbatch, seq, tc):
    n = glu.shape[0]
    nt = seq // tc
    per = tc // CONV_HALO
    vec = pl.BlockSpec((1, CONV_CH), lambda bb, i: (0, 0))
    return pl.pallas_call(
        functools.partial(_conv_kernel, tc=tc),
        out_shape=[jax.ShapeDtypeStruct((n, CONV_CH), BF16),
                   jax.ShapeDtypeStruct((nbatch, CONV_HALO, CONV_CH), F32)],
        grid=(nbatch, nt),
        in_specs=[pl.BlockSpec((tc, 2 * CONV_CH), lambda bb, i: (bb * nt + i, 0)),
                  pl.BlockSpec((CONV_HALO, 2 * CONV_CH),
                               lambda bb, i: (jnp.maximum((bb * nt + i) * per - 1, 0), 0)),
                  pl.BlockSpec((CONV_WIDTH, CONV_CH), lambda bb, i: (0, 0)), vec, vec, vec],
        out_specs=[pl.BlockSpec((tc, CONV_CH), lambda bb, i: (bb * nt + i, 0)),
                   pl.BlockSpec((None, CONV_HALO, CONV_CH), lambda bb, i: (bb, 0, 0))],
        scratch_shapes=[pltpu.VMEM((tc + CONV_HALO, CONV_CH), F32)],
        compiler_params=_cparams(2),
        name="conv",
    )(glu, glu, w, b, g, beta)


def _gelu(x):
    return 0.5 * x * (1.0 + lax.erf(x * (2.0 ** -0.5)))


def _sgu_kernel(uv_ref, g_ref, beta_ref, w_ref, bias_ref, o_ref, *, ts):
    a = _gelu(uv_ref[...])
    u = a[:, :SGU_CH]
    vn = _layernorm(a[:, SGU_CH:], g_ref[...], beta_ref[...]).astype(BF16)
    ri = lax.broadcasted_iota(I32, (SGU_CHUNK, SGU_CHUNK), 0)
    ci = lax.broadcasted_iota(I32, (SGU_CHUNK, SGU_CHUNK), 1)
    group = lax.broadcasted_iota(I32, (SGU_CHUNK, SGU_CH), 1) // (SGU_CH // SGU_GROUPS)
    ws = [jnp.where(ci <= ri, w_ref[gi], 0.0).astype(BF16) for gi in range(SGU_GROUPS)]
    for c in range(ts // SGU_CHUNK):
        rows = slice(c * SGU_CHUNK, (c + 1) * SGU_CHUNK)
        mixed = bias_ref[...]
        for gi in range(SGU_GROUPS):
            mixed = mixed + jnp.where(group == gi, _dot(ws[gi], vn[rows]), 0.0)
        o_ref[rows, :] = (u[rows] * mixed).astype(o_ref.dtype)


def _sgu(uv, g, beta, w, bias, *, ts):
    n = uv.shape[0]
    vec = pl.BlockSpec((1, SGU_CH), lambda i: (0, 0))
    return pl.pallas_call(
        functools.partial(_sgu_kernel, ts=ts),
        out_shape=jax.ShapeDtypeStruct((n, SGU_CH), BF16),
        grid=(n // ts,),
        in_specs=[pl.BlockSpec((ts, 2 * SGU_CH), lambda i: (i, 0)), vec, vec,
                  pl.BlockSpec((SGU_GROUPS, SGU_CHUNK, SGU_CHUNK), lambda i: (0, 0, 0)),
                  pl.BlockSpec((SGU_CHUNK, SGU_CH), lambda i: (0, 0))],
        out_specs=pl.BlockSpec((ts, SGU_CH), lambda i: (i, 0)),
        compiler_params=_cparams(1),
        name="sgu",
    )(uv, g, beta, w, bias)


def _step_mix_kernel(glu_ref, uv_ref, hist_ref, cw_ref, cb_ref, cg_ref, cbeta_ref, sg_ref, sbeta_ref,
                     sdiag_ref, sbias_ref, ob_ref, oc_ref, hx_ref, v_ref):
    x = glu_ref[...]
    hx = x[:, :CONV_CH] * _sigmoid(x[:, CONV_CH:])
    hx_ref[...] = hx
    y = cb_ref[...] + hx * cw_ref[CONV_WIDTH - 1:CONV_WIDTH, :]
    for k in range(CONV_WIDTH - 1):
        y = y + hist_ref[k] * cw_ref[k:k + 1, :]
    y = _layernorm(y, cg_ref[...], cbeta_ref[...])
    ob_ref[...] = (y * _sigmoid(y)).astype(ob_ref.dtype)
    a = _gelu(uv_ref[...])
    v = a[:, SGU_CH:]
    v_ref[...] = v
    vn = _layernorm(v, sg_ref[...], sbeta_ref[...])
    oc_ref[...] = (a[:, :SGU_CH] * (sdiag_ref[...] * vn + sbias_ref[...])).astype(oc_ref.dtype)


def _step_mix(glu, uv, hist, cw, cb, cg, cbeta, sg, sbeta, sdiag, sbias):
    n = glu.shape[0]
    sds = jax.ShapeDtypeStruct
    return pl.pallas_call(
        _step_mix_kernel,
        out_shape=[sds((n, CONV_CH), BF16), sds((n, SGU_CH), BF16), sds((n, CONV_CH), F32), sds((n, SGU_CH), F32)],
        compiler_params=pltpu.CompilerParams(vmem_limit_bytes=VMEM_LIMIT),
        name="step_mix",
    )(glu, uv, hist, cw, cb, cg, cbeta, sg, sbeta, sdiag, sbias)


def _merge_kernel(x_ref, g_ref, oa_ref, ob_ref, oc_ref, od_ref, wg_ref, wb_ref, wo_ref, y_ref):
    x = x_ref[...]
    d = x.shape[-1]
    hb = _rms(x, g_ref[...]).astype(BF16)
    acc = jnp.zeros(x.shape, F32)
    for k, o_ref in enumerate((oa_ref, ob_ref, oc_ref, od_ref)):
        gate = _sigmoid(_dot(hb, wg_ref[:, k * d:(k + 1) * d]))
        acc = acc + gate * _dot(o_ref[...], wb_ref[k])
    y_ref[...] = x + _dot(acc.astype(BF16), wo_ref[...])


def _merge(x, g, oa, ob, oc, od, wg, wb, wo, *, tm):
    n, d = x.shape
    row = pl.BlockSpec((tm, d), lambda i: (i, 0))
    br = pl.BlockSpec((tm, BRANCH_WIDTH), lambda i: (i, 0))
    once = pl.Buffered(1)
    return pl.pallas_call(
        _merge_kernel,
        out_shape=jax.ShapeDtypeStruct((n, d), F32),
        grid=(n // tm,),
        in_specs=[row, pl.BlockSpec((1, d), lambda i: (0, 0)), br, br, br, br,
                  pl.BlockSpec((d, N_BRANCH * d), lambda i: (0, 0), pipeline_mode=once),
                  pl.BlockSpec((N_BRANCH, BRANCH_WIDTH, d), lambda i: (0, 0, 0), pipeline_mode=once),
                  pl.BlockSpec((d, d), lambda i: (0, 0), pipeline_mode=once)],
        out_specs=row,
        compiler_params=_cparams(1),
        name="merge",
    )(x, g, oa, ob, oc, od, wg, wb, wo)


def _mlp_kernel(x_ref, g_ref, wu_ref, wd_ref, gf_ref, y_ref, *n_ref):
    x = x_ref[...]
    hb = _rms(x, g_ref[...]).astype(BF16)
    a = jnp.square(jnp.maximum(_dot(hb, wu_ref[...]), 0.0)).astype(BF16)
    y = x + _dot(a, wd_ref[...])
    y_ref[...] = y
    if n_ref:
        n_ref[0][...] = _rms(y, gf_ref[...])


def _mlp(x, g, wu, wd, gf, *, tm, final):
    n, d = x.shape
    row = pl.BlockSpec((tm, d), lambda i: (i, 0))
    vec = pl.BlockSpec((1, d), lambda i: (0, 0))
    once = pl.Buffered(1)
    out_shape = [jax.ShapeDtypeStruct((n, d), F32)] * (2 if final else 1)
    return pl.pallas_call(
        _mlp_kernel,
        out_shape=out_shape,
        grid=(n // tm,),
        in_specs=[row, vec, pl.BlockSpec(wu.shape, lambda i: (0, 0), pipeline_mode=once),
                  pl.BlockSpec(wd.shape, lambda i: (0, 0), pipeline_mode=once), vec],
        out_specs=[row] * (2 if final else 1),
        compiler_params=_cparams(1),
        name="mlp",
    )(x, g, wu, wd, gf)


def _select_rows(width, first):
    r = lax.broadcasted_iota(I32, (HEAD_DIM, width), 0)
    c = lax.broadcasted_iota(I32, (HEAD_DIM, width), 1)
    return jnp.where(c == r + first, 1.0, 0.0).astype(BF16)


def _new_row_chunk(row):
    first = lax.broadcasted_iota(I32, (CK, row.shape[-1]), 0) == 0
    return jnp.where(first, jnp.broadcast_to(row, (CK, row.shape[-1])), 0.0)


def _assemble_kernel(pt_ref, *refs, n_pages, page):
    nsa_pages = refs[:n_pages]
    dsa_pages = refs[n_pages:2 * n_pages]
    win_ref, nsa_new_ref, dsa_new_ref = refs[2 * n_pages:2 * n_pages + 3]
    cmp_ref, kst_ref, vs_ref, kwt_ref, vw_ref, kdt_ref, vd_ref, kit_ref = refs[2 * n_pages + 3:]
    per_chunk = CK // page
    pick_kslc = _select_rows(256, 128)
    pick_kd = _select_rows(192, 0)
    pick_ki = _select_rows(192, 128)
    pick_kwin = _select_rows(128, 0)

    def put_nsa(x, c, lanes, rows):
        kst_ref[c, :, lanes] = _dot_nt(pick_kslc, x.astype(BF16)).astype(BF16)
        vs_ref[rows, :] = x[:, 192:256].astype(BF16)

    def put_dsa(x, c, lanes, rows):
        xb = x.astype(BF16)
        kdt_ref[c, :, lanes] = _dot_nt(pick_kd, xb).astype(BF16)
        kit_ref[c, :, lanes] = _dot_nt(pick_ki, xb).astype(BF16)
        vd_ref[rows, :] = x[:, 64:128].astype(BF16)

    for p in range(n_pages):
        c, r = divmod(p, per_chunk)
        lanes = slice(r * page, (r + 1) * page)
        rows = slice(p * page, (p + 1) * page)
        x = nsa_pages[p][...]
        cmp_ref[rows, :] = x[:, 0:128]
        put_nsa(x, c, lanes, rows)
        put_dsa(dsa_pages[p][...], c, lanes, rows)

    c_new = n_pages // per_chunk
    new_rows = slice(c_new * CK, (c_new + 1) * CK)
    nsa_new = nsa_new_ref[...]
    put_nsa(_new_row_chunk(nsa_new[:, 0:256]), c_new, slice(0, CK), new_rows)
    put_dsa(_new_row_chunk(dsa_new_ref[...]), c_new, slice(0, CK), new_rows)

    def put_win(x, c, rows):
        kwt_ref[c] = _dot_nt(pick_kwin, x.astype(BF16)).astype(BF16)
        vw_ref[rows, :] = x[:, 64:128].astype(BF16)

    put_win(win_ref[...], 0, slice(0, CK))
    put_win(_new_row_chunk(nsa_new[:, 256:384]), 1, slice(CK, 2 * CK))


def _assemble(page_table, cache_nsa, cache_dsa, state_win, nsa_new, dsa_new, *, layer):
    nseq, n_pages = page_table.shape
    page = cache_nsa.shape[2]
    n_chunks = n_pages * page // CK + 1
    keys = n_chunks * CK
    sds = jax.ShapeDtypeStruct

    def page_spec(width, p):
        return pl.BlockSpec((None, None, page, width), lambda b, pt: (pt[b, p], layer, 0, 0))

    def per_seq(*shape):
        return pl.BlockSpec((None,) + shape, lambda b, pt: (b,) + (0,) * len(shape))

    grid_spec = pltpu.PrefetchScalarGridSpec(
        num_scalar_prefetch=1,
        grid=(nseq,),
        in_specs=([page_spec(256, p) for p in range(n_pages)] + [page_spec(192, p) for p in range(n_pages)]
                  + [pl.BlockSpec((None, None, CK, 128), lambda b, pt: (b, layer, 0, 0)),
                     per_seq(1, 384), per_seq(1, 192)]),
        out_specs=[per_seq(n_pages * page, 128), per_seq(n_chunks, HEAD_DIM, CK), per_seq(keys, HEAD_DIM),
                   per_seq(2, HEAD_DIM, CK), per_seq(2 * CK, HEAD_DIM),
                   per_seq(n_chunks, HEAD_DIM, CK), per_seq(keys, HEAD_DIM), per_seq(n_chunks, HEAD_DIM, CK)],
    )
    return pl.pallas_call(
        functools.partial(_assemble_kernel, n_pages=n_pages, page=page),
        out_shape=[sds((nseq, n_pages * page, 128), F32), sds((nseq, n_chunks, HEAD_DIM, CK), BF16),
                   sds((nseq, keys, HEAD_DIM), BF16), sds((nseq, 2, HEAD_DIM, CK), BF16),
                   sds((nseq, 2 * CK, HEAD_DIM), BF16), sds((nseq, n_chunks, HEAD_DIM, CK), BF16),
                   sds((nseq, keys, HEAD_DIM), BF16), sds((nseq, n_chunks, HEAD_DIM, CK), BF16)],
        grid_spec=grid_spec,
        compiler_params=_cparams(1),
        name="assemble",
    )(page_table, *([cache_nsa] * n_pages), *([cache_dsa] * n_pages), state_win, nsa_new, dsa_new)


def _prep_w_in(w_in):
    pts = np.cumsum(SPLIT_WIDTHS)[:-1].tolist()
    qa, kva, ga, glu, uv, qd, kvd, qi, ki, wi = jnp.split(w_in, pts, axis=-1)
    pad = jnp.zeros(w_in.shape[:2] + (128 - 16,), w_in.dtype)
    w = jnp.concatenate([qa, kva, glu, uv, qd, qi, wi, ga, pad, kvd, ki], axis=-1).astype(BF16)
    wt = jnp.concatenate([kva[..., 128:192], kva[..., 256:320], kvd[..., 0:64], ki], axis=-1)
    return w, jnp.swapaxes(wt, 1, 2).astype(BF16)


def _seq_spec(*shape):
    return pl.BlockSpec((None,) + shape, lambda b, i: (b,) + (0,) * len(shape))


def kernel(x_prompt, x_sample, cache_nsa, cache_dsa, state_win, state_conv, page_table, norm1_g, norm2_g,
           final_g, w_in, cmp_pos, cmp_w1, cmp_w2, conv_w, conv_b, conv_ln_g, conv_ln_b, sgu_ln_g, sgu_ln_b,
           sgu_w, sgu_b, w_branch, w_gate, w_out, w_up, w_down):
    nb_, seq, d = x_prompt.shape
    nseq, t_dec, _ = x_sample.shape
    depth = w_in.shape[0]
    n_pool, _, page, _, _ = cache_nsa.shape
    n_pages = page_table.shape[1]
    past = n_pages * page
    w_buf = state_win.shape[2]
    assert t_dec == 1 and seq % CK == 0 and past % CK == 0 and w_buf == CK == NSA_WINDOW
    nbp = seq // NSA_BLOCK
    assert nbp <= NB_PAD and past // NSA_BLOCK < NB_PAD
    n_tok = nb_ * seq
    sr = SAMPLE_ROWS

    w_proj, w_proj_t = _prep_w_in(w_in)
    wg, wb, wo = w_gate.astype(BF16), w_branch.astype(BF16), w_out.astype(BF16)
    wu, wd = w_up.astype(BF16), w_down.astype(BF16)
    c_pos = cmp_pos.reshape(depth, 2, 1, NSA_BLOCK * HEAD_DIM)
    c_w1, c_w2 = cmp_w1.astype(BF16), cmp_w2.astype(BF16)
    sgu_bias = jnp.repeat(jnp.swapaxes(sgu_b, 1, 2), SGU_CH // SGU_GROUPS, axis=2)
    sgu_diag = jnp.repeat(sgu_w[:, :, 0, 0], SGU_CH // SGU_GROUPS, axis=1)[:, None, :]
    sgu_bias0 = sgu_bias[:, 0:1, :]
    vec = lambda a, l: a[l][None, :]
    cache_nsa4 = cache_nsa.reshape(n_pool, depth, page, 4 * HEAD_DIM)
    cache_dsa4 = cache_dsa.reshape(n_pool, depth, page, 3 * HEAD_DIM)
    state_win4 = state_win.reshape(nseq, depth, w_buf, 2 * HEAD_DIM)
    conv_hist = jnp.transpose(state_conv, (1, 2, 0, 3))

    n_chunks_p = seq // CK
    n_chunks_s = past // CK + 1
    nq = seq // Q_BLOCK

    def chunks_of(arr, slot, n_chunks):
        return arr, pl.BlockSpec((n_chunks, HEAD_DIM, CK), lambda b, i: (b, slot, 0))

    def rows_of(arr, slot, n_rows):
        return arr, pl.BlockSpec((None, n_rows, HEAD_DIM), lambda b, i: (slot, b, 0))

    xp = x_prompt.reshape(n_tok, d)
    xs = x_sample.reshape(nseq, d)
    outs = {k: [] for k in ("nsa_p", "nsa_s", "dsa_p", "dsa_s", "win_p", "win_s", "conv_p", "conv_s", "sgu_s")}
    yp = ys = None
    for l in range(depth):
        final = l == depth - 1
        qn, kva, glu, uv, qd, qi, misc, dsa_rows, kt4, v4 = _proj(
            xp, vec(norm1_g, l), w_proj[l], w_proj_t[l], tm=CK, emit_t=True)
        kva3 = kva.reshape(nb_, seq, 6, HEAD_DIM)
        cmp_in = jnp.stack([kva3[:, :, 0].reshape(nb_ * nbp, NSA_BLOCK * HEAD_DIM),
                            kva3[:, :, 1].reshape(nb_ * nbp, NSA_BLOCK * HEAD_DIM)])
        kvc = _compress(cmp_in, c_pos[l], c_w1[l], c_w2[l], tr=nb_ * nbp)
        kvc = jnp.pad(kvc.reshape(2, nb_, nbp, HEAD_DIM), ((0, 0), (0, 0), (0, NB_PAD - nbp), (0, 0)))
        kc, vc = kvc[0], kvc[1]
        o_a = _nsa(qn, misc, kc, vc, chunks_of(kt4, 0, n_chunks_p), rows_of(v4, 0, seq),
                   chunks_of(kt4, 1, n_chunks_p), rows_of(v4, 1, seq),
                   nbatch=nb_, nq=nq, tq=Q_BLOCK, pos0=0, pos_stride=1, win_base=0,
                   n_sel=min(NSA_N_SEL, seq // NSA_BLOCK))
        o_d = _dsa(qd, qi, misc, chunks_of(kt4, 2, n_chunks_p), rows_of(v4, 2, seq),
                   chunks_of(kt4, 3, n_chunks_p), nbatch=nb_, nq=nq, tq=Q_BLOCK, pos0=0, pos_stride=1,
                   n_keep=min(DSA_TOPK, seq // 4), n_chunks=n_chunks_p, real_rows=Q_BLOCK)
        o_b, conv_tail = _conv(glu, conv_w[l], vec(conv_b, l), vec(conv_ln_g, l), vec(conv_ln_b, l),
                               nbatch=nb_, seq=seq, tc=CK)
        o_c = _sgu(uv, vec(sgu_ln_g, l), vec(sgu_ln_b, l), sgu_w[l], sgu_bias[l], ts=CK)
        xp = _merge(xp, vec(norm1_g, l), o_a, o_b, o_c, o_d, wg[l], wb[l], wo[l], tm=256)
        res = _mlp(xp, vec(norm2_g, l), wu[l], wd[l], final_g[None, :], tm=256, final=final)
        xp = res[0]
        if final:
            yp = res[1]
        outs["nsa_p"].append(kva3[:, :, :4])
        outs["dsa_p"].append(dsa_rows.reshape(nb_, seq, 3, HEAD_DIM))
        outs["win_p"].append(kva3[:, seq - min(NSA_WINDOW, seq):, 4:])
        outs["conv_p"].append(conv_tail[:, CONV_HALO - (CONV_WIDTH - 1):])

        qn, kva, glu, uv, qd, qi, misc, dsa_rows = _proj(
            xs, vec(norm1_g, l), w_proj[l], w_proj_t[l], tm=nseq, emit_t=False)
        cmp_past, kst, vs, kwt, vw, kdt, vdd, kit = _assemble(
            page_table, cache_nsa4, cache_dsa4, state_win4, kva[:, None, :], dsa_rows[:, None, :], layer=l)
        nbs = past // NSA_BLOCK
        cmp_in = jnp.stack([cmp_past[:, :, 0:64].reshape(nseq * nbs, NSA_BLOCK * HEAD_DIM),
                            cmp_past[:, :, 64:128].reshape(nseq * nbs, NSA_BLOCK * HEAD_DIM)])
        kvc = _compress(cmp_in, c_pos[l], c_w1[l], c_w2[l], tr=min(256, nseq * nbs))
        kvc = jnp.pad(kvc.reshape(2, nseq, nbs, HEAD_DIM), ((0, 0), (0, 0), (0, NB_PAD - nbs), (0, 0)))

        def pad_rows(a):
            a = a[..., :, None, :]
            widths = [(0, 0)] * (a.ndim - 2) + [(0, sr - 1), (0, 0)]
            a = jnp.pad(a, widths)
            return a.reshape(a.shape[:-3] + (nseq * sr, a.shape[-1]))

        misc_r = pad_rows(misc)
        o_a = _nsa(pad_rows(qn), misc_r, kvc[0], kvc[1],
                   (kst, _seq_spec(n_chunks_s, HEAD_DIM, CK)), (vs, _seq_spec(n_chunks_s * CK, HEAD_DIM)),
                   (kwt, _seq_spec(2, HEAD_DIM, CK)), (vw, _seq_spec(2 * CK, HEAD_DIM)),
                   nbatch=nseq, nq=1, tq=sr, pos0=past, pos_stride=0, win_base=past // CK - 1,
                   n_sel=min(NSA_N_SEL, nbs + 1))
        o_d = _dsa(pad_rows(qd), pad_rows(qi), misc_r,
                   (kdt, _seq_spec(n_chunks_s, HEAD_DIM, CK)), (vdd, _seq_spec(n_chunks_s * CK, HEAD_DIM)),
                   (kit, _seq_spec(n_chunks_s, HEAD_DIM, CK)), nbatch=nseq, nq=1, tq=sr, pos0=past,
                   pos_stride=0, n_keep=min(DSA_TOPK, (past + 1) // 4), n_chunks=n_chunks_s, real_rows=1)
        o_a = o_a.reshape(nseq, sr, -1)[:, 0]
        o_d = o_d.reshape(nseq, sr, -1)[:, 0]
        o_b, o_c, hx_new, v_rows = _step_mix(
            glu, uv, conv_hist[l], conv_w[l], vec(conv_b, l), vec(conv_ln_g, l), vec(conv_ln_b, l),
            vec(sgu_ln_g, l), vec(sgu_ln_b, l), sgu_diag[l], sgu_bias0[l])
        xs = _merge(xs, vec(norm1_g, l), o_a, o_b, o_c, o_d, wg[l], wb[l], wo[l], tm=nseq)
        res = _mlp(xs, vec(norm2_g, l), wu[l], wd[l], final_g[None, :], tm=nseq, final=final)
        xs = res[0]
        if final:
            ys = res[1]
        kva3 = kva.reshape(nseq, 1, 6, HEAD_DIM)
        outs["nsa_s"].append(kva3[:, :, :4])
        outs["dsa_s"].append(dsa_rows.reshape(nseq, 1, 3, HEAD_DIM))
        outs["win_s"].append(jnp.concatenate([state_win[:, l], kva3[:, :, 4:]], axis=1)[:, 1:])
        outs["conv_s"].append(jnp.concatenate([state_conv[:, l], hx_new[:, None, :]], axis=1)[:, 1:])
        outs["sgu_s"].append(v_rows[:, None, :])

    st = lambda k: jnp.stack(outs[k], axis=1)
    return (yp.reshape(nb_, seq, d), ys.reshape(nseq, 1, d), st("nsa_p"), st("nsa_s"), st("dsa_p"), st("dsa_s"),
            st("win_p"), st("win_s"), st("conv_p"), st("conv_s"), st("sgu_s"))
```

```python
import functools

import numpy as np
import jax
import jax.numpy as jnp
from jax import lax
from jax.experimental import pallas as pl
from jax.experimental.pallas import tpu as pltpu

F32, BF16, I32 = jnp.float32, jnp.bfloat16, jnp.int32

HEAD_DIM = 64
NSA_HEADS = 4
NSA_BLOCK = 64
NSA_N_SEL = 16
NSA_WINDOW = 512
FORCE_BONUS = 1.0e4
CONV_CH = 256
CONV_WIDTH = 31
SGU_CH = 256
SGU_GROUPS = 4
SGU_CHUNK = 128
DSA_HEADS = 4
IDX_HEADS = 4
DSA_TOPK = 256
N_BRANCH = 4
BRANCH_WIDTH = 256
Q_BLOCK = 128
EPS = 1e-6
NEG = -1e30

SPLIT_WIDTHS = (256, 384, 12, 512, 512, 256, 128, 256, 64, 4)

CK = 512
NB_PAD = 128
SAMPLE_ROWS = 16
VMEM_LIMIT = 56 * 1024 * 1024
INT_MIN = -2 ** 31
NEG_KEY = int(np.float32(NEG).view(np.int32)) ^ 0x7FFFFFFF

W_QA, W_KVA, W_GLU, W_UV, W_QD, W_QI, W_MISC, W_DSA = 0, 256, 640, 1152, 1664, 1920, 2176, 2304
W_TOTAL = 2496
MISC_WI, MISC_GA = 0, 4


def _cparams(n_axes):
    return pltpu.CompilerParams(dimension_semantics=("arbitrary",) * n_axes,
                                vmem_limit_bytes=VMEM_LIMIT)


def _dot(a, b):
    return jnp.dot(a, b, preferred_element_type=F32)


def _dot_nt(a, b):
    return lax.dot_general(a, b, (((1,), (1,)), ((), ())), preferred_element_type=F32)


def _rms(x, g):
    return x * lax.rsqrt(jnp.mean(x * x, axis=-1, keepdims=True) + EPS) * g


def _layernorm(x, g, b):
    mu = jnp.mean(x, axis=-1, keepdims=True)
    xc = x - mu
    var = jnp.mean(xc * xc, axis=-1, keepdims=True)
    return xc * lax.rsqrt(var + EPS) * g + b


def _sigmoid(x):
    return 1.0 / (1.0 + jnp.exp(-x))


def _tile4(a):
    return jnp.concatenate([a, a, a, a], axis=0)


def _proj_kernel(x_ref, g_ref, w_ref, wt_ref, qn_ref, kva_ref, glu_ref, uv_ref, qd_ref, qi_ref,
                 misc_ref, dsa_ref, *t_refs):
    hb = _rms(x_ref[...], g_ref[...]).astype(BF16)

    def slab(off, width):
        return _dot(hb, w_ref[:, off:off + width])

    qa = slab(W_QA, 256)
    qd = slab(W_QD, 256)
    qi = slab(W_QI, 256)
    for h in range(4):
        cols = slice(h * HEAD_DIM, (h + 1) * HEAD_DIM)
        qn_ref[h] = qa[:, cols].astype(BF16)
        qd_ref[h] = qd[:, cols].astype(BF16)
        qi_ref[h] = qi[:, cols].astype(BF16)
    kv = slab(W_KVA, 384)
    kva_ref[...] = kv
    glu_ref[...] = slab(W_GLU, 512)
    uv_ref[...] = slab(W_UV, 512)
    misc_ref[...] = slab(W_MISC, 128)
    dr = slab(W_DSA, 192)
    dsa_ref[...] = dr
    if t_refs:
        kt_ref, v_ref = t_refs
        kt_ref[0] = _dot_nt(wt_ref[...], hb).astype(BF16)
        v_ref[0] = kv[:, 192:256].astype(BF16)
        v_ref[1] = kv[:, 320:384].astype(BF16)
        v_ref[2] = dr[:, 64:128].astype(BF16)


def _proj(x, g, w, wt, *, tm, emit_t):
    n, d = x.shape
    sds = jax.ShapeDtypeStruct
    out_shape = [sds((4, n, HEAD_DIM), BF16), sds((n, 384), F32), sds((n, 512), F32), sds((n, 512), F32),
                 sds((4, n, HEAD_DIM), BF16), sds((4, n, HEAD_DIM), BF16), sds((n, 128), F32), sds((n, 192), F32)]
    head_spec = pl.BlockSpec((4, tm, HEAD_DIM), lambda i: (0, i, 0))

    def row_spec(width):
        return pl.BlockSpec((tm, width), lambda i: (i, 0))

    out_specs = [head_spec, row_spec(384), row_spec(512), row_spec(512), head_spec, head_spec,
                 row_spec(128), row_spec(192)]
    if emit_t:
        assert tm == CK
        out_shape += [sds((n // CK, 256, CK), BF16), sds((3, n, HEAD_DIM), BF16)]
        out_specs += [pl.BlockSpec((1, 256, CK), lambda i: (i, 0, 0)),
                      pl.BlockSpec((3, tm, HEAD_DIM), lambda i: (0, i, 0))]
    return pl.pallas_call(
        _proj_kernel,
        out_shape=out_shape,
        grid=(n // tm,),
        in_specs=[row_spec(d), pl.BlockSpec((1, d), lambda i: (0, 0)),
                  pl.BlockSpec((d, W_TOTAL), lambda i: (0, 0)),
                  pl.BlockSpec((256, d), lambda i: (0, 0))],
        out_specs=out_specs,
        compiler_params=_cparams(1),
        name="proj",
    )(x, g, w, wt)


def _compress_kernel(x_ref, pos_ref, w1_ref, w2_ref, o_ref):
    xb = (x_ref[...] + pos_ref[...]).astype(BF16)
    a = _dot(xb, w1_ref[...])
    a = a * _sigmoid(a)
    o_ref[...] = _dot(a.astype(BF16), w2_ref[...]).astype(o_ref.dtype)


def _compress(x, pos, w1, w2, *, tr):
    _, r, width = x.shape
    return pl.pallas_call(
        _compress_kernel,
        out_shape=jax.ShapeDtypeStruct((2, r, HEAD_DIM), BF16),
        grid=(2, r // tr),
        in_specs=[pl.BlockSpec((None, tr, width), lambda s, i: (s, i, 0)),
                  pl.BlockSpec((None, 1, width), lambda s, i: (s, 0, 0)),
                  pl.BlockSpec((None, width, HEAD_DIM), lambda s, i: (s, 0, 0)),
                  pl.BlockSpec((None, HEAD_DIM, HEAD_DIM), lambda s, i: (s, 0, 0))],
        out_specs=pl.BlockSpec((None, tr, HEAD_DIM), lambda s, i: (s, i, 0)),
        compiler_params=_cparams(2),
        name="compress",
    )(x, pos, w1, w2)


def _masked_probs(s, mask):
    s = jnp.where(mask, s, NEG)
    m = jnp.max(s, axis=-1, keepdims=True)
    p = jnp.where(mask, jnp.exp(s - m), 0.0)
    return p / jnp.maximum(jnp.sum(p, axis=-1, keepdims=True), 1e-30)


def _softmax_step(s, bias, v, carry):
    m, l, acc = carry
    tq = bias.shape[0]
    s = s.reshape(4, tq, CK) + bias[None]
    m_new = jnp.maximum(m, jnp.max(s, axis=-1, keepdims=True))
    alpha = jnp.exp(m - m_new)
    p = jnp.exp(s - m_new)
    l = alpha * l + jnp.sum(p, axis=-1, keepdims=True)
    acc = alpha.reshape(4 * tq, 1) * acc + _dot(p.reshape(4 * tq, CK).astype(BF16), v)
    return m_new, l, acc


def _softmax_init(tq):
    return (jnp.full((4, tq, 1), NEG, F32), jnp.zeros((4, tq, 1), F32), jnp.zeros((4 * tq, HEAD_DIM), F32))


def _softmax_done(carry):
    _, l, acc = carry
    return acc / jnp.maximum(l.reshape(acc.shape[0], 1), 1e-30)


def _mask_bias(allowed):
    return jnp.where(allowed, 0.0, NEG)


def _query_positions(tq, pos0, pos_stride):
    i = pl.program_id(1)
    start = pos0 + i * (tq * pos_stride)
    last = start + (tq - 1) * pos_stride
    t = lax.broadcasted_iota(I32, (tq, 1), 0)
    qpos = start + t * pos_stride
    return start, last, qpos


def _heads_as_rows(ref, scale):
    return jnp.concatenate([ref[h] for h in range(4)], axis=0) * scale


def _key_positions(c):
    return c * CK + lax.broadcasted_iota(I32, (1, CK), 1)


def _chunk_rows(ref, c):
    return ref[pl.ds(pl.multiple_of(c * CK, CK), CK), :]


PICK_LANES = 128


def _nsa_kernel(q_ref, misc_ref, kc_ref, vc_ref, kst_ref, vs_ref, kwt_ref, vw_ref, o_ref, *,
                tq, pos0, pos_stride, win_base, n_sel):
    nb = kc_ref.shape[0]
    start, last, qpos = _query_positions(tq, pos0, pos_stride)
    qpos4 = _tile4(qpos)
    q = _heads_as_rows(q_ref, HEAD_DIM ** -0.5)

    jb = lax.broadcasted_iota(I32, (1, nb), 1)
    mask_c = ((jb + 1) * NSA_BLOCK - 1) <= qpos4
    p_c = _masked_probs(_dot_nt(q, kc_ref[...]), mask_c)
    o_c = _dot(p_c.astype(BF16), vc_ref[...])
    imp = p_c[0:tq] + p_c[tq:2 * tq] + p_c[2 * tq:3 * tq] + p_c[3 * tq:4 * tq]

    cur = qpos // NSA_BLOCK
    forced = (jb == 0) | (jb == cur) | (jb == cur - 1)
    score = jnp.where(jb <= cur, imp, -1.0) + jnp.where(forced, FORCE_BONUS, 0.0)
    if tq < PICK_LANES:
        score = jnp.concatenate([score, jnp.zeros((PICK_LANES - tq, nb), F32)], axis=0)
    jcol = lax.broadcasted_iota(I32, (nb, PICK_LANES), 0).astype(F32)

    def pick(_, carry):
        sc, sel = carry
        m = jnp.max(sc, axis=0, keepdims=True)
        first = jnp.min(jnp.where(sc == m, jcol, 1e9), axis=0, keepdims=True)
        hit = jcol == first
        return jnp.where(hit, -3e38, sc), jnp.where(hit, 1.0, sel)

    _, sel_t = lax.fori_loop(0, n_sel, pick, (score.T, jnp.zeros((nb, PICK_LANES), F32)))
    sel = sel_t.T[0:tq].astype(BF16)

    jrow = lax.broadcasted_iota(I32, (nb, CK), 0)
    kcol = lax.broadcasted_iota(I32, (nb, CK), 1)

    def sel_chunk(c, carry):
        expand = jnp.where(jrow == ((c * CK + kcol) // NSA_BLOCK), 1.0, 0.0).astype(BF16)
        bias = _mask_bias((_dot(sel, expand) > 0.5) & (_key_positions(c) <= qpos))
        return _softmax_step(_dot(q, kst_ref[c]), bias, _chunk_rows(vs_ref, c), carry)

    o_s = _softmax_done(lax.fori_loop(0, last // CK + 1, sel_chunk, _softmax_init(tq)))

    carry = _softmax_init(tq)
    c1 = start // CK
    for back in (1, 0):
        c = c1 - back
        cc = jnp.maximum(c - win_base, 0)
        kpos = _key_positions(c)
        dpos = qpos - kpos
        bias = _mask_bias((dpos >= 0) & (dpos < NSA_WINDOW) & (kpos >= 0))
        carry = _softmax_step(_dot(q, kwt_ref[cc]), bias, _chunk_rows(vw_ref, cc), carry)
    o_w = _softmax_done(carry)

    g = _sigmoid(misc_ref[...])
    outs = []
    for h in range(4):
        rows = slice(h * tq, (h + 1) * tq)
        col = MISC_GA + 3 * h
        outs.append(g[:, col:col + 1] * o_c[rows] + g[:, col + 1:col + 2] * o_s[rows]
                    + g[:, col + 2:col + 3] * o_w[rows])
    o_ref[...] = jnp.concatenate(outs, axis=-1).astype(o_ref.dtype)


def _nsa(q, misc, kc, vc, ks, vs, kw, vw, *, nbatch, nq, tq, pos0, pos_stride, win_base, n_sel):
    nrows = q.shape[1]
    kern = functools.partial(_nsa_kernel, tq=tq, pos0=pos0, pos_stride=pos_stride, win_base=win_base,
                             n_sel=n_sel)
    return pl.pallas_call(
        kern,
        out_shape=jax.ShapeDtypeStruct((nrows, 4 * HEAD_DIM), BF16),
        grid=(nbatch, nq),
        in_specs=[pl.BlockSpec((4, tq, HEAD_DIM), lambda b, i: (0, b * nq + i, 0)),
                  pl.BlockSpec((tq, 128), lambda b, i: (b * nq + i, 0)),
                  pl.BlockSpec((None, NB_PAD, HEAD_DIM), lambda b, i: (b, 0, 0)),
                  pl.BlockSpec((None, NB_PAD, HEAD_DIM), lambda b, i: (b, 0, 0)),
                  ks[1], vs[1], kw[1], vw[1]],
        out_specs=pl.BlockSpec((tq, 4 * HEAD_DIM), lambda b, i: (b * nq + i, 0)),
        compiler_params=_cparams(2),
        name="nsa",
    )(q, misc, kc, vc, ks[0], vs[0], kw[0], vw[0])


LANES = 128
COUNT_UNROLL = 4


def _index_keys(qi, wcol, kit, kpos, qpos, tq):
    rel = jnp.maximum(_dot(qi, kit), 0.0) * wcol
    sc = rel[0:tq] + rel[tq:2 * tq] + rel[2 * tq:3 * tq] + rel[3 * tq:4 * tq]
    sc = jnp.where(sc == 0.0, 0.0, sc)
    sc = jnp.where(kpos <= qpos, sc, NEG)
    bits = lax.bitcast_convert_type(sc, I32)
    return jnp.where(bits < 0, bits ^ 0x7FFFFFFF, bits)


def _index_weights(misc_ref):
    w = misc_ref[:, MISC_WI:MISC_WI + 4] * (IDX_HEADS ** -0.5 * HEAD_DIM ** -0.5)
    return jnp.concatenate([w[:, h:h + 1] for h in range(4)], axis=0)


def _chunk_loop(n_chunks, body, init):
    if isinstance(n_chunks, int):
        carry = init
        for c in range(n_chunks):
            carry = body(c, carry)
        return carry

    def several(g, carry):
        for u in range(COUNT_UNROLL):
            carry = body(g * COUNT_UNROLL + u, carry)
        return carry

    n_groups = n_chunks // COUNT_UNROLL
    carry = lax.fori_loop(0, n_groups, several, init)
    return lax.fori_loop(n_groups * COUNT_UNROLL, n_chunks, body, carry)


def _top_keys(get_keys, n_chunks, rows, n_keep, n_positions, cut_scr):
    def count(pred):
        def one(c, acc):
            ind = jnp.where(pred(get_keys(c), c), 1.0, 0.0)
            return acc + (ind[:, 0:128] + ind[:, 128:256] + ind[:, 256:384] + ind[:, 384:512])

        acc = _chunk_loop(n_chunks, one, jnp.zeros((rows, LANES), F32))
        return jnp.sum(acc, axis=-1, keepdims=True)

    keep = float(n_keep)
    thr = jnp.where(count(lambda k, c: k >= 0) >= keep, 0, INT_MIN).astype(I32)

    def thr_bit(b, thr):
        cand = thr | jnp.left_shift(jnp.int32(1), 30 - b)
        return jnp.where(count(lambda k, c: k >= cand) >= keep, cand, thr)

    thr = lax.fori_loop(0, 31, thr_bit, thr)
    n_gt = count(lambda k, c: k > thr)
    n_ge = count(lambda k, c: k >= thr)
    need = keep - n_gt
    tie = (n_ge > keep) & (thr > NEG_KEY)
    n_pos_bits = (n_positions - 1).bit_length()
    cut_scr[...] = jnp.full(cut_scr.shape, n_positions, I32)

    @pl.when(jnp.max(jnp.where(tie, 1.0, 0.0)) > 0.5)
    def _():
        def cut_bit(b, cut):
            cand = cut + jnp.left_shift(jnp.int32(1), n_pos_bits - 1 - b)
            n = count(lambda k, c: (k == thr) & (_key_positions(c) < cand))
            return jnp.where(n < need, cand, cut)

        cut = lax.fori_loop(0, n_pos_bits, cut_bit, jnp.zeros((rows, 1), I32))
        cut_scr[...] = jnp.broadcast_to(cut, cut_scr.shape)

    return thr, cut_scr[:, 0:1]


def _selected_bias(k, thr, cut, kpos, qpos):
    return _mask_bias(((k > thr) | ((k == thr) & (kpos <= cut))) & (kpos <= qpos))


def _heads_to_lanes(o, tq):
    return jnp.concatenate([o[h * tq:(h + 1) * tq] for h in range(4)], axis=-1)


def _dsa_kernel(qd_ref, qi_ref, misc_ref, kdt_ref, vd_ref, kit_ref, o_ref, key_scr, cut_scr, *, tq, n_keep):
    start, last, qpos = _query_positions(tq, 0, 1)
    n_chunks = last // CK + 1
    qd = _heads_as_rows(qd_ref, HEAD_DIM ** -0.5)
    qi = _heads_as_rows(qi_ref, 1.0)
    wcol = _index_weights(misc_ref)

    def index_chunk(c, _):
        key_scr[c] = _index_keys(qi, wcol, kit_ref[c], _key_positions(c), qpos, tq)
        return 0

    lax.fori_loop(0, n_chunks, index_chunk, 0)
    thr, cut = _top_keys(lambda c: key_scr[c], n_chunks, tq, n_keep, kdt_ref.shape[0] * CK, cut_scr)

    def attend_chunk(c, carry):
        bias = _selected_bias(key_scr[c], thr, cut, _key_positions(c), qpos)
        return _softmax_step(_dot(qd, kdt_ref[c]), bias, _chunk_rows(vd_ref, c), carry)

    o = _softmax_done(lax.fori_loop(0, n_chunks, attend_chunk, _softmax_init(tq)))
    o_ref[...] = _heads_to_lanes(o, tq).astype(o_ref.dtype)


def _dsa(qd, qi, misc, kd, vd, ki, *, nbatch, nq, tq, n_keep, n_chunks):
    nrows = qd.shape[1]
    head_spec = pl.BlockSpec((4, tq, HEAD_DIM), lambda b, i: (0, b * nq + i, 0))
    return pl.pallas_call(
        functools.partial(_dsa_kernel, tq=tq, n_keep=n_keep),
        out_shape=jax.ShapeDtypeStruct((nrows, 4 * HEAD_DIM), BF16),
        grid=(nbatch, nq),
        in_specs=[head_spec, head_spec, pl.BlockSpec((tq, 128), lambda b, i: (b * nq + i, 0)),
                  kd[1], vd[1], ki[1]],
        out_specs=pl.BlockSpec((tq, 4 * HEAD_DIM), lambda b, i: (b * nq + i, 0)),
        scratch_shapes=[pltpu.VMEM((n_chunks, tq, CK), I32), pltpu.VMEM((tq, LANES), I32)],
        compiler_params=_cparams(2),
        name="dsa",
    )(qd, qi, misc, kd[0], vd[0], ki[0])


def _dsa_step_index_kernel(qi_ref, misc_ref, kit_ref, key_ref, *, tq, pos, n_chunks):
    qi = _heads_as_rows(qi_ref, 1.0)
    wcol = _index_weights(misc_ref)
    for c in range(n_chunks):
        key_ref[c] = _index_keys(qi, wcol, kit_ref[c], _key_positions(c), pos, tq)[0:1, :]


def _dsa_step_top_kernel(key_ref, thr_ref, cut_ref, cut_scr, *, n_keep, n_chunks):
    rows = key_ref.shape[1]
    thr, cut = _top_keys(lambda c: key_ref[c], n_chunks, rows, n_keep, n_chunks * CK, cut_scr)
    thr_ref[...] = jnp.broadcast_to(thr, thr_ref.shape)
    cut_ref[...] = jnp.broadcast_to(cut, cut_ref.shape)


def _dsa_step_attend_kernel(qd_ref, key_ref, thr_ref, cut_ref, kdt_ref, vd_ref, o_ref, *, tq, pos, n_chunks):
    qd = _heads_as_rows(qd_ref, HEAD_DIM ** -0.5)
    thr = thr_ref[:, 0:1]
    cut = cut_ref[:, 0:1]
    carry = _softmax_init(tq)
    for c in range(n_chunks):
        bias = _selected_bias(key_ref[c], thr, cut, _key_positions(c), pos)
        carry = _softmax_step(_dot(qd, kdt_ref[c]), jnp.broadcast_to(bias, (tq, CK)),
                              vd_ref[c * CK:(c + 1) * CK, :], carry)
    o_ref[...] = _heads_to_lanes(_softmax_done(carry), tq).astype(o_ref.dtype)


def _dsa_step(qd, qi, misc, kdt, vd, kit, *, nseq, tq, pos, n_keep, n_chunks):
    head_spec = pl.BlockSpec((4, tq, HEAD_DIM), lambda b: (0, b, 0))
    chunk_spec = pl.BlockSpec((None, n_chunks, HEAD_DIM, CK), lambda b: (b, 0, 0, 0))
    key_spec = pl.BlockSpec((n_chunks, None, 1, CK), lambda b: (0, b, 0, 0))
    row_spec = pl.BlockSpec((None, 1, LANES), lambda b: (b, 0, 0))
    keys = pl.pallas_call(
        functools.partial(_dsa_step_index_kernel, tq=tq, pos=pos, n_chunks=n_chunks),
        out_shape=jax.ShapeDtypeStruct((n_chunks, nseq, 1, CK), I32),
        grid=(nseq,),
        in_specs=[head_spec, pl.BlockSpec((tq, 128), lambda b: (b, 0)), chunk_spec],
        out_specs=key_spec,
        compiler_params=_cparams(1),
        name="dsa_step_index",
    )(qi, misc, kit)
    thr, cut = pl.pallas_call(
        functools.partial(_dsa_step_top_kernel, n_keep=n_keep, n_chunks=n_chunks),
        out_shape=[jax.ShapeDtypeStruct((nseq, LANES), I32)] * 2,
        scratch_shapes=[pltpu.VMEM((nseq, LANES), I32)],
        compiler_params=pltpu.CompilerParams(vmem_limit_bytes=VMEM_LIMIT),
        name="dsa_step_top",
    )(keys.reshape(n_chunks, nseq, CK))
    return pl.pallas_call(
        functools.partial(_dsa_step_attend_kernel, tq=tq, pos=pos, n_chunks=n_chunks),
        out_shape=jax.ShapeDtypeStruct((nseq * tq, 4 * HEAD_DIM), BF16),
        grid=(nseq,),
        in_specs=[head_spec, key_spec, row_spec, row_spec, chunk_spec,
                  pl.BlockSpec((None, n_chunks * CK, HEAD_DIM), lambda b: (b, 0, 0))],
        out_specs=pl.BlockSpec((tq, 4 * HEAD_DIM), lambda b: (b, 0)),
        compiler_params=_cparams(1),
        name="dsa_step_attend",
    )(qd, keys, thr[:, None, :], cut[:, None, :], kdt, vd)


CONV_HALO = 32


def _conv_kernel(cur_ref, halo_ref, w_ref, b_ref, g_ref, beta_ref, o_ref, tail_ref, hx_scr, *, tc):
    i = pl.program_id(1)

    def glu(x):
        return x[:, :CONV_CH] * _sigmoid(x[:, CONV_CH:])

    hx_scr[0:CONV_HALO] = jnp.where(i > 0, glu(halo_ref[...]), 0.0)
    hx_scr[CONV_HALO:CONV_HALO + tc] = glu(cur_ref[...])
    first = CONV_HALO - (CONV_WIDTH - 1)
    y = jnp.broadcast_to(b_ref[...], (tc, CONV_CH))
    for k in range(CONV_WIDTH):
        y = y + hx_scr[first + k:first + k + tc] * w_ref[k:k + 1, :]
    y = _layernorm(y, g_ref[...], beta_ref[...])
    o_ref[...] = (y * _sigmoid(y)).astype(o_ref.dtype)
    tail_ref[...] = hx_scr[tc:tc + CONV_HALO]


def _conv(glu, w, b, g, beta, *, nbatch, seq, tc):
    n = glu.shape[0]
    nt = seq // tc
    per = tc // CONV_HALO
    vec = pl.BlockSpec((1, CONV_CH), lambda bb, i: (0, 0))
    return pl.pallas_call(
        functools.partial(_conv_kernel, tc=tc),
        out_shape=[jax.ShapeDtypeStruct((n, CONV_CH), BF16),
                   jax.ShapeDtypeStruct((nbatch, CONV_HALO, CONV_CH), F32)],
        grid=(nbatch, nt),
        in_specs=[pl.BlockSpec((tc, 2 * CONV_CH), lambda bb, i: (bb * nt + i, 0)),
                  pl.BlockSpec((CONV_HALO, 2 * CONV_CH),
                               lambda bb, i: (jnp.maximum((bb * nt + i) * per - 1, 0), 0)),
                  pl.BlockSpec((CONV_WIDTH, CONV_CH), lambda bb, i: (0, 0)), vec, vec, vec],
        out_specs=[pl.BlockSpec((tc, CONV_CH), lambda bb, i: (bb * nt + i, 0)),
                   pl.BlockSpec((None, CONV_HALO, CONV_CH), lambda bb, i: (bb, 0, 0))],
        scratch_shapes=[pltpu.VMEM((tc + CONV_HALO, CONV_CH), F32)],
        compiler_params=_cparams(2),
        name="conv",
    )(glu, glu, w, b, g, beta)


def _gelu(x):
    return 0.5 * x * (1.0 + lax.erf(x * (2.0 ** -0.5)))


def _sgu_kernel(uv_ref, g_ref, beta_ref, w_ref, bias_ref, o_ref, *, ts):
    a = _gelu(uv_ref[...])
    u = a[:, :SGU_CH]
    vn = _layernorm(a[:, SGU_CH:], g_ref[...], beta_ref[...]).astype(BF16)
    ri = lax.broadcasted_iota(I32, (SGU_CHUNK, SGU_CHUNK), 0)
    ci = lax.broadcasted_iota(I32, (SGU_CHUNK, SGU_CHUNK), 1)
    group = lax.broadcasted_iota(I32, (SGU_CHUNK, SGU_CH), 1) // (SGU_CH // SGU_GROUPS)
    ws = [jnp.where(ci <= ri, w_ref[gi], 0.0).astype(BF16) for gi in range(SGU_GROUPS)]
    for c in range(ts // SGU_CHUNK):
        rows = slice(c * SGU_CHUNK, (c + 1) * SGU_CHUNK)
        mixed = bias_ref[...]
        for gi in range(SGU_GROUPS):
            mixed = mixed + jnp.where(group == gi, _dot(ws[gi], vn[rows]), 0.0)
        o_ref[rows, :] = (u[rows] * mixed).astype(o_ref.dtype)


def _sgu(uv, g, beta, w, bias, *, ts):
    n = uv.shape[0]
    vec = pl.BlockSpec((1, SGU_CH), lambda i: (0, 0))
    return pl.pallas_call(
        functools.partial(_sgu_kernel, ts=ts),
        out_shape=jax.ShapeDtypeStruct((n, SGU_CH), BF16),
        grid=(n // ts,),
        in_specs=[pl.BlockSpec((ts, 2 * SGU_CH), lambda i: (i, 0)), vec, vec,
                  pl.BlockSpec((SGU_GROUPS, SGU_CHUNK, SGU_CHUNK), lambda i: (0, 0, 0)),
                  pl.BlockSpec((SGU_CHUNK, SGU_CH), lambda i: (0, 0))],
        out_specs=pl.BlockSpec((ts, SGU_CH), lambda i: (i, 0)),
        compiler_params=_cparams(1),
        name="sgu",
    )(uv, g, beta, w, bias)


def _step_mix_kernel(glu_ref, uv_ref, hist_ref, cw_ref, cb_ref, cg_ref, cbeta_ref, sg_ref, sbeta_ref,
                     sdiag_ref, sbias_ref, ob_ref, oc_ref, hx_ref, v_ref):
    x = glu_ref[...]
    hx = x[:, :CONV_CH] * _sigmoid(x[:, CONV_CH:])
    hx_ref[...] = hx
    y = cb_ref[...] + hx * cw_ref[CONV_WIDTH - 1:CONV_WIDTH, :]
    for k in range(CONV_WIDTH - 1):
        y = y + hist_ref[k] * cw_ref[k:k + 1, :]
    y = _layernorm(y, cg_ref[...], cbeta_ref[...])
    ob_ref[...] = (y * _sigmoid(y)).astype(ob_ref.dtype)
    a = _gelu(uv_ref[...])
    v = a[:, SGU_CH:]
    v_ref[...] = v
    vn = _layernorm(v, sg_ref[...], sbeta_ref[...])
    oc_ref[...] = (a[:, :SGU_CH] * (sdiag_ref[...] * vn + sbias_ref[...])).astype(oc_ref.dtype)


def _step_mix(glu, uv, hist, cw, cb, cg, cbeta, sg, sbeta, sdiag, sbias):
    n = glu.shape[0]
    sds = jax.ShapeDtypeStruct
    return pl.pallas_call(
        _step_mix_kernel,
        out_shape=[sds((n, CONV_CH), BF16), sds((n, SGU_CH), BF16), sds((n, CONV_CH), F32), sds((n, SGU_CH), F32)],
        compiler_params=pltpu.CompilerParams(vmem_limit_bytes=VMEM_LIMIT),
        name="step_mix",
    )(glu, uv, hist, cw, cb, cg, cbeta, sg, sbeta, sdiag, sbias)


def _merge_kernel(x_ref, g_ref, oa_ref, ob_ref, oc_ref, od_ref, wg_ref, wb_ref, wo_ref, y_ref):
    x = x_ref[...]
    d = x.shape[-1]
    hb = _rms(x, g_ref[...]).astype(BF16)
    acc = jnp.zeros(x.shape, F32)
    for k, o_ref in enumerate((oa_ref, ob_ref, oc_ref, od_ref)):
        gate = _sigmoid(_dot(hb, wg_ref[:, k * d:(k + 1) * d]))
        acc = acc + gate * _dot(o_ref[...], wb_ref[k])
    y_ref[...] = x + _dot(acc.astype(BF16), wo_ref[...])


def _merge(x, g, oa, ob, oc, od, wg, wb, wo, *, tm):
    n, d = x.shape
    row = pl.BlockSpec((tm, d), lambda i: (i, 0))
    br = pl.BlockSpec((tm, BRANCH_WIDTH), lambda i: (i, 0))
    once = pl.Buffered(1)
    return pl.pallas_call(
        _merge_kernel,
        out_shape=jax.ShapeDtypeStruct((n, d), F32),
        grid=(n // tm,),
        in_specs=[row, pl.BlockSpec((1, d), lambda i: (0, 0)), br, br, br, br,
                  pl.BlockSpec((d, N_BRANCH * d), lambda i: (0, 0), pipeline_mode=once),
                  pl.BlockSpec((N_BRANCH, BRANCH_WIDTH, d), lambda i: (0, 0, 0), pipeline_mode=once),
                  pl.BlockSpec((d, d), lambda i: (0, 0), pipeline_mode=once)],
        out_specs=row,
        compiler_params=_cparams(1),
        name="merge",
    )(x, g, oa, ob, oc, od, wg, wb, wo)


def _mlp_kernel(x_ref, g_ref, wu_ref, wd_ref, gf_ref, y_ref, *n_ref):
    x = x_ref[...]
    hb = _rms(x, g_ref[...]).astype(BF16)
    a = jnp.square(jnp.maximum(_dot(hb, wu_ref[...]), 0.0)).astype(BF16)
    y = x + _dot(a, wd_ref[...])
    y_ref[...] = y
    if n_ref:
        n_ref[0][...] = _rms(y, gf_ref[...])


def _mlp(x, g, wu, wd, gf, *, tm, final):
    n, d = x.shape
    row = pl.BlockSpec((tm, d), lambda i: (i, 0))
    vec = pl.BlockSpec((1, d), lambda i: (0, 0))
    once = pl.Buffered(1)
    out_shape = [jax.ShapeDtypeStruct((n, d), F32)] * (2 if final else 1)
    return pl.pallas_call(
        _mlp_kernel,
        out_shape=out_shape,
        grid=(n // tm,),
        in_specs=[row, vec, pl.BlockSpec(wu.shape, lambda i: (0, 0), pipeline_mode=once),
                  pl.BlockSpec(wd.shape, lambda i: (0, 0), pipeline_mode=once), vec],
        out_specs=[row] * (2 if final else 1),
        compiler_params=_cparams(1),
        name="mlp",
    )(x, g, wu, wd, gf)


def _select_rows(width, first):
    r = lax.broadcasted_iota(I32, (HEAD_DIM, width), 0)
    c = lax.broadcasted_iota(I32, (HEAD_DIM, width), 1)
    return jnp.where(c == r + first, 1.0, 0.0).astype(BF16)


def _new_row_chunk(row):
    first = lax.broadcasted_iota(I32, (CK, row.shape[-1]), 0) == 0
    return jnp.where(first, jnp.broadcast_to(row, (CK, row.shape[-1])), 0.0)


def _assemble_kernel(pt_ref, *refs, n_pages, page):
    nsa_pages = refs[:n_pages]
    dsa_pages = refs[n_pages:2 * n_pages]
    win_ref, nsa_new_ref, dsa_new_ref = refs[2 * n_pages:2 * n_pages + 3]
    cmp_ref, kst_ref, vs_ref, kwt_ref, vw_ref, kdt_ref, vd_ref, kit_ref = refs[2 * n_pages + 3:]
    per_chunk = CK // page
    pick_kslc = _select_rows(256, 128)
    pick_kd = _select_rows(192, 0)
    pick_ki = _select_rows(192, 128)
    pick_kwin = _select_rows(128, 0)

    def put_nsa(x, c, lanes, rows):
        kst_ref[c, :, lanes] = _dot_nt(pick_kslc, x.astype(BF16)).astype(BF16)
        vs_ref[rows, :] = x[:, 192:256].astype(BF16)

    def put_dsa(x, c, lanes, rows):
        xb = x.astype(BF16)
        kdt_ref[c, :, lanes] = _dot_nt(pick_kd, xb).astype(BF16)
        kit_ref[c, :, lanes] = _dot_nt(pick_ki, xb).astype(BF16)
        vd_ref[rows, :] = x[:, 64:128].astype(BF16)

    for p in range(n_pages):
        c, r = divmod(p, per_chunk)
        lanes = slice(r * page, (r + 1) * page)
        rows = slice(p * page, (p + 1) * page)
        x = nsa_pages[p][...]
        cmp_ref[rows, :] = x[:, 0:128]
        put_nsa(x, c, lanes, rows)
        put_dsa(dsa_pages[p][...], c, lanes, rows)

    c_new = n_pages // per_chunk
    new_rows = slice(c_new * CK, (c_new + 1) * CK)
    nsa_new = nsa_new_ref[...]
    put_nsa(_new_row_chunk(nsa_new[:, 0:256]), c_new, slice(0, CK), new_rows)
    put_dsa(_new_row_chunk(dsa_new_ref[...]), c_new, slice(0, CK), new_rows)

    def put_win(x, c, rows):
        kwt_ref[c] = _dot_nt(pick_kwin, x.astype(BF16)).astype(BF16)
        vw_ref[rows, :] = x[:, 64:128].astype(BF16)

    put_win(win_ref[...], 0, slice(0, CK))
    put_win(_new_row_chunk(nsa_new[:, 256:384]), 1, slice(CK, 2 * CK))


def _assemble(page_table, cache_nsa, cache_dsa, state_win, nsa_new, dsa_new, *, layer):
    nseq, n_pages = page_table.shape
    page = cache_nsa.shape[2]
    n_chunks = n_pages * page // CK + 1
    keys = n_chunks * CK
    sds = jax.ShapeDtypeStruct

    def page_spec(width, p):
        return pl.BlockSpec((None, None, page, width), lambda b, pt: (pt[b, p], layer, 0, 0))

    def per_seq(*shape):
        return pl.BlockSpec((None,) + shape, lambda b, pt: (b,) + (0,) * len(shape))

    grid_spec = pltpu.PrefetchScalarGridSpec(
        num_scalar_prefetch=1,
        grid=(nseq,),
        in_specs=([page_spec(256, p) for p in range(n_pages)] + [page_spec(192, p) for p in range(n_pages)]
                  + [pl.BlockSpec((None, None, CK, 128), lambda b, pt: (b, layer, 0, 0)),
                     per_seq(1, 384), per_seq(1, 192)]),
        out_specs=[per_seq(n_pages * page, 128), per_seq(n_chunks, HEAD_DIM, CK), per_seq(keys, HEAD_DIM),
                   per_seq(2, HEAD_DIM, CK), per_seq(2 * CK, HEAD_DIM),
                   per_seq(n_chunks, HEAD_DIM, CK), per_seq(keys, HEAD_DIM), per_seq(n_chunks, HEAD_DIM, CK)],
    )
    return pl.pallas_call(
        functools.partial(_assemble_kernel, n_pages=n_pages, page=page),
        out_shape=[sds((nseq, n_pages * page, 128), F32), sds((nseq, n_chunks, HEAD_DIM, CK), BF16),
                   sds((nseq, keys, HEAD_DIM), BF16), sds((nseq, 2, HEAD_DIM, CK), BF16),
                   sds((nseq, 2 * CK, HEAD_DIM), BF16), sds((nseq, n_chunks, HEAD_DIM, CK), BF16),
                   sds((nseq, keys, HEAD_DIM), BF16), sds((nseq, n_chunks, HEAD_DIM, CK), BF16)],
        grid_spec=grid_spec,
        compiler_params=_cparams(1),
        name="assemble",
    )(page_table, *([cache_nsa] * n_pages), *([cache_dsa] * n_pages), state_win, nsa_new, dsa_new)


def _prep_w_in(w_in):
    pts = np.cumsum(SPLIT_WIDTHS)[:-1].tolist()
    qa, kva, ga, glu, uv, qd, kvd, qi, ki, wi = jnp.split(w_in, pts, axis=-1)
    pad = jnp.zeros(w_in.shape[:2] + (128 - 16,), w_in.dtype)
    w = jnp.concatenate([qa, kva, glu, uv, qd, qi, wi, ga, pad, kvd, ki], axis=-1).astype(BF16)
    wt = jnp.concatenate([kva[..., 128:192], kva[..., 256:320], kvd[..., 0:64], ki], axis=-1)
    return w, jnp.swapaxes(wt, 1, 2).astype(BF16)


def _seq_spec(*shape):
    return pl.BlockSpec((None,) + shape, lambda b, i: (b,) + (0,) * len(shape))


def kernel(x_prompt, x_sample, cache_nsa, cache_dsa, state_win, state_conv, page_table, norm1_g, norm2_g,
           final_g, w_in, cmp_pos, cmp_w1, cmp_w2, conv_w, conv_b, conv_ln_g, conv_ln_b, sgu_ln_g, sgu_ln_b,
           sgu_w, sgu_b, w_branch, w_gate, w_out, w_up, w_down):
    nb_, seq, d = x_prompt.shape
    nseq, t_dec, _ = x_sample.shape
    depth = w_in.shape[0]
    n_pool, _, page, _, _ = cache_nsa.shape
    n_pages = page_table.shape[1]
    past = n_pages * page
    w_buf = state_win.shape[2]
    assert t_dec == 1 and seq % CK == 0 and past % CK == 0 and w_buf == CK == NSA_WINDOW
    nbp = seq // NSA_BLOCK
    assert nbp <= NB_PAD and past // NSA_BLOCK < NB_PAD
    n_tok = nb_ * seq
    sr = SAMPLE_ROWS

    w_proj, w_proj_t = _prep_w_in(w_in)
    wg, wb, wo = w_gate.astype(BF16), w_branch.astype(BF16), w_out.astype(BF16)
    wu, wd = w_up.astype(BF16), w_down.astype(BF16)
    c_pos = cmp_pos.reshape(depth, 2, 1, NSA_BLOCK * HEAD_DIM)
    c_w1, c_w2 = cmp_w1.astype(BF16), cmp_w2.astype(BF16)
    sgu_bias = jnp.repeat(jnp.swapaxes(sgu_b, 1, 2), SGU_CH // SGU_GROUPS, axis=2)
    sgu_diag = jnp.repeat(sgu_w[:, :, 0, 0], SGU_CH // SGU_GROUPS, axis=1)[:, None, :]
    sgu_bias0 = sgu_bias[:, 0:1, :]
    vec = lambda a, l: a[l][None, :]
    cache_nsa4 = cache_nsa.reshape(n_pool, depth, page, 4 * HEAD_DIM)
    cache_dsa4 = cache_dsa.reshape(n_pool, depth, page, 3 * HEAD_DIM)
    state_win4 = state_win.reshape(nseq, depth, w_buf, 2 * HEAD_DIM)
    conv_hist = jnp.transpose(state_conv, (1, 2, 0, 3))

    n_chunks_p = seq // CK
    n_chunks_s = past // CK + 1
    nq = seq // Q_BLOCK

    def chunks_of(arr, slot, n_chunks):
        return arr, pl.BlockSpec((n_chunks, HEAD_DIM, CK), lambda b, i: (b, slot, 0))

    def rows_of(arr, slot, n_rows):
        return arr, pl.BlockSpec((None, n_rows, HEAD_DIM), lambda b, i: (slot, b, 0))

    xp = x_prompt.reshape(n_tok, d)
    xs = x_sample.reshape(nseq, d)
    outs = {k: [] for k in ("nsa_p", "nsa_s", "dsa_p", "dsa_s", "win_p", "win_s", "conv_p", "conv_s", "sgu_s")}
    yp = ys = None
    for l in range(depth):
        final = l == depth - 1
        qn, kva, glu, uv, qd, qi, misc, dsa_rows, kt4, v4 = _proj(
            xp, vec(norm1_g, l), w_proj[l], w_proj_t[l], tm=CK, emit_t=True)
        kva3 = kva.reshape(nb_, seq, 6, HEAD_DIM)
        cmp_in = jnp.stack([kva3[:, :, 0].reshape(nb_ * nbp, NSA_BLOCK * HEAD_DIM),
                            kva3[:, :, 1].reshape(nb_ * nbp, NSA_BLOCK * HEAD_DIM)])
        kvc = _compress(cmp_in, c_pos[l], c_w1[l], c_w2[l], tr=nb_ * nbp)
        kvc = jnp.pad(kvc.reshape(2, nb_, nbp, HEAD_DIM), ((0, 0), (0, 0), (0, NB_PAD - nbp), (0, 0)))
        kc, vc = kvc[0], kvc[1]
        o_a = _nsa(qn, misc, kc, vc, chunks_of(kt4, 0, n_chunks_p), rows_of(v4, 0, seq),
                   chunks_of(kt4, 1, n_chunks_p), rows_of(v4, 1, seq),
                   nbatch=nb_, nq=nq, tq=Q_BLOCK, pos0=0, pos_stride=1, win_base=0,
                   n_sel=min(NSA_N_SEL, seq // NSA_BLOCK))
        o_d = _dsa(qd, qi, misc, chunks_of(kt4, 2, n_chunks_p), rows_of(v4, 2, seq),
                   chunks_of(kt4, 3, n_chunks_p), nbatch=nb_, nq=nq, tq=Q_BLOCK,
                   n_keep=min(DSA_TOPK, seq // 4), n_chunks=n_chunks_p)
        o_b, conv_tail = _conv(glu, conv_w[l], vec(conv_b, l), vec(conv_ln_g, l), vec(conv_ln_b, l),
                               nbatch=nb_, seq=seq, tc=CK)
        o_c = _sgu(uv, vec(sgu_ln_g, l), vec(sgu_ln_b, l), sgu_w[l], sgu_bias[l], ts=CK)
        xp = _merge(xp, vec(norm1_g, l), o_a, o_b, o_c, o_d, wg[l], wb[l], wo[l], tm=256)
        res = _mlp(xp, vec(norm2_g, l), wu[l], wd[l], final_g[None, :], tm=256, final=final)
        xp = res[0]
        if final:
            yp = res[1]
        outs["nsa_p"].append(kva3[:, :, :4])
        outs["dsa_p"].append(dsa_rows.reshape(nb_, seq, 3, HEAD_DIM))
        outs["win_p"].append(kva3[:, seq - min(NSA_WINDOW, seq):, 4:])
        outs["conv_p"].append(conv_tail[:, CONV_HALO - (CONV_WIDTH - 1):])

        qn, kva, glu, uv, qd, qi, misc, dsa_rows = _proj(
            xs, vec(norm1_g, l), w_proj[l], w_proj_t[l], tm=nseq, emit_t=False)
        cmp_past, kst, vs, kwt, vw, kdt, vdd, kit = _assemble(
            page_table, cache_nsa4, cache_dsa4, state_win4, kva[:, None, :], dsa_rows[:, None, :], layer=l)
        nbs = past // NSA_BLOCK
        cmp_in = jnp.stack([cmp_past[:, :, 0:64].reshape(nseq * nbs, NSA_BLOCK * HEAD_DIM),
                            cmp_past[:, :, 64:128].reshape(nseq * nbs, NSA_BLOCK * HEAD_DIM)])
        kvc = _compress(cmp_in, c_pos[l], c_w1[l], c_w2[l], tr=min(256, nseq * nbs))
        kvc = jnp.pad(kvc.reshape(2, nseq, nbs, HEAD_DIM), ((0, 0), (0, 0), (0, NB_PAD - nbs), (0, 0)))

        def pad_rows(a):
            a = a[..., :, None, :]
            widths = [(0, 0)] * (a.ndim - 2) + [(0, sr - 1), (0, 0)]
            a = jnp.pad(a, widths)
            return a.reshape(a.shape[:-3] + (nseq * sr, a.shape[-1]))

        misc_r = pad_rows(misc)
        o_a = _nsa(pad_rows(qn), misc_r, kvc[0], kvc[1],
                   (kst, _seq_spec(n_chunks_s, HEAD_DIM, CK)), (vs, _seq_spec(n_chunks_s * CK, HEAD_DIM)),
                   (kwt, _seq_spec(2, HEAD_DIM, CK)), (vw, _seq_spec(2 * CK, HEAD_DIM)),
                   nbatch=nseq, nq=1, tq=sr, pos0=past, pos_stride=0, win_base=past // CK - 1,
                   n_sel=min(NSA_N_SEL, nbs + 1))
        o_d = _dsa_step(pad_rows(qd), pad_rows(qi), misc_r, kdt, vdd, kit, nseq=nseq, tq=sr, pos=past,
                        n_keep=min(DSA_TOPK, (past + 1) // 4), n_chunks=n_chunks_s)
        o_a = o_a.reshape(nseq, sr, -1)[:, 0]
        o_d = o_d.reshape(nseq, sr, -1)[:, 0]
        o_b, o_c, hx_new, v_rows = _step_mix(
            glu, uv, conv_hist[l], conv_w[l], vec(conv_b, l), vec(conv_ln_g, l), vec(conv_ln_b, l),
            vec(sgu_ln_g, l), vec(sgu_ln_b, l), sgu_diag[l], sgu_bias0[l])
        xs = _merge(xs, vec(norm1_g, l), o_a, o_b, o_c, o_d, wg[l], wb[l], wo[l], tm=nseq)
        res = _mlp(xs, vec(norm2_g, l), wu[l], wd[l], final_g[None, :], tm=nseq, final=final)
        xs = res[0]
        if final:
            ys = res[1]
        kva3 = kva.reshape(nseq, 1, 6, HEAD_DIM)
        outs["nsa_s"].append(kva3[:, :, :4])
        outs["dsa_s"].append(dsa_rows.reshape(nseq, 1, 3, HEAD_DIM))
        outs["win_s"].append(jnp.concatenate([state_win[:, l], kva3[:, :, 4:]], axis=1)[:, 1:])
        outs["conv_s"].append(jnp.concatenate([state_conv[:, l], hx_new[:, None, :]], axis=1)[:, 1:])
        outs["sgu_s"].append(v_rows[:, None, :])

    st = lambda k: jnp.stack(outs[k], axis=1)
    return (yp.reshape(nb_, seq, d), ys.reshape(nseq, 1, d), st("nsa_p"), st("nsa_s"), st("dsa_p"), st("dsa_s"),
            st("win_p"), st("win_s"), st("conv_p"), st("conv_s"), st("sgu_s"))
```

```python
import functools

import numpy as np
import jax
import jax.numpy as jnp
from jax import lax
from jax.experimental import pallas as pl
from jax.experimental.pallas import tpu as pltpu

F32, BF16, I32 = jnp.float32, jnp.bfloat16, jnp.int32

HEAD_DIM = 64
NSA_HEADS = 4
NSA_BLOCK = 64
NSA_N_SEL = 16
NSA_WINDOW = 512
FORCE_BONUS = 1.0e4
CONV_CH = 256
CONV_WIDTH = 31
SGU_CH = 256
SGU_GROUPS = 4
SGU_CHUNK = 128
DSA_HEADS = 4
IDX_HEADS = 4
DSA_TOPK = 256
N_BRANCH = 4
BRANCH_WIDTH = 256
Q_BLOCK = 128
EPS = 1e-6
NEG = -1e30

SPLIT_WIDTHS = (256, 384, 12, 512, 512, 256, 128, 256, 64, 4)

CK = 512
NB_PAD = 128
SAMPLE_ROWS = 16
VMEM_LIMIT = 56 * 1024 * 1024
INT_MIN = -2 ** 31

W_QA, W_KVA, W_GLU, W_UV, W_QD, W_QI, W_MISC, W_DSA = 0, 256, 640, 1152, 1664, 1920, 2176, 2304
W_TOTAL = 2496
MISC_WI, MISC_GA = 0, 4


def _cparams(n_axes):
    return pltpu.CompilerParams(dimension_semantics=("arbitrary",) * n_axes,
                                vmem_limit_bytes=VMEM_LIMIT)


def _dot(a, b):
    return jnp.dot(a, b, preferred_element_type=F32)


def _dot_nt(a, b):
    return lax.dot_general(a, b, (((1,), (1,)), ((), ())), preferred_element_type=F32)


def _rms(x, g):
    return x * lax.rsqrt(jnp.mean(x * x, axis=-1, keepdims=True) + EPS) * g


def _layernorm(x, g, b):
    mu = jnp.mean(x, axis=-1, keepdims=True)
    xc = x - mu
    var = jnp.mean(xc * xc, axis=-1, keepdims=True)
    return xc * lax.rsqrt(var + EPS) * g + b


def _sigmoid(x):
    return 1.0 / (1.0 + jnp.exp(-x))


def _tile4(a):
    return jnp.concatenate([a, a, a, a], axis=0)


def _proj_kernel(x_ref, g_ref, w_ref, wt_ref, qn_ref, kva_ref, glu_ref, uv_ref, qd_ref, qi_ref,
                 misc_ref, dsa_ref, *t_refs):
    hb = _rms(x_ref[...], g_ref[...]).astype(BF16)

    def slab(off, width):
        return _dot(hb, w_ref[:, off:off + width])

    qa = slab(W_QA, 256)
    qd = slab(W_QD, 256)
    qi = slab(W_QI, 256)
    for h in range(4):
        cols = slice(h * HEAD_DIM, (h + 1) * HEAD_DIM)
        qn_ref[h] = qa[:, cols].astype(BF16)
        qd_ref[h] = qd[:, cols].astype(BF16)
        qi_ref[h] = qi[:, cols].astype(BF16)
    kv = slab(W_KVA, 384)
    kva_ref[...] = kv
    glu_ref[...] = slab(W_GLU, 512)
    uv_ref[...] = slab(W_UV, 512)
    misc_ref[...] = slab(W_MISC, 128)
    dr = slab(W_DSA, 192)
    dsa_ref[...] = dr
    if t_refs:
        kt_ref, v_ref = t_refs
        kt_ref[0] = _dot_nt(wt_ref[...], hb).astype(BF16)
        v_ref[0] = kv[:, 192:256].astype(BF16)
        v_ref[1] = kv[:, 320:384].astype(BF16)
        v_ref[2] = dr[:, 64:128].astype(BF16)


def _proj(x, g, w, wt, *, tm, emit_t):
    n, d = x.shape
    sds = jax.ShapeDtypeStruct
    out_shape = [sds((4, n, HEAD_DIM), BF16), sds((n, 384), F32), sds((n, 512), F32), sds((n, 512), F32),
                 sds((4, n, HEAD_DIM), BF16), sds((4, n, HEAD_DIM), BF16), sds((n, 128), F32), sds((n, 192), F32)]
    head_spec = pl.BlockSpec((4, tm, HEAD_DIM), lambda i: (0, i, 0))

    def row_spec(width):
        return pl.BlockSpec((tm, width), lambda i: (i, 0))

    out_specs = [head_spec, row_spec(384), row_spec(512), row_spec(512), head_spec, head_spec,
                 row_spec(128), row_spec(192)]
    if emit_t:
        assert tm == CK
        out_shape += [sds((n // CK, 256, CK), BF16), sds((3, n, HEAD_DIM), BF16)]
        out_specs += [pl.BlockSpec((1, 256, CK), lambda i: (i, 0, 0)),
                      pl.BlockSpec((3, tm, HEAD_DIM), lambda i: (0, i, 0))]
    return pl.pallas_call(
        _proj_kernel,
        out_shape=out_shape,
        grid=(n // tm,),
        in_specs=[row_spec(d), pl.BlockSpec((1, d), lambda i: (0, 0)),
                  pl.BlockSpec((d, W_TOTAL), lambda i: (0, 0)),
                  pl.BlockSpec((256, d), lambda i: (0, 0))],
        out_specs=out_specs,
        compiler_params=_cparams(1),
        name="proj",
    )(x, g, w, wt)


def _compress_kernel(x_ref, pos_ref, w1_ref, w2_ref, o_ref):
    xb = (x_ref[...] + pos_ref[...]).astype(BF16)
    a = _dot(xb, w1_ref[...])
    a = a * _sigmoid(a)
    o_ref[...] = _dot(a.astype(BF16), w2_ref[...]).astype(o_ref.dtype)


def _compress(x, pos, w1, w2, *, tr):
    _, r, width = x.shape
    return pl.pallas_call(
        _compress_kernel,
        out_shape=jax.ShapeDtypeStruct((2, r, HEAD_DIM), BF16),
        grid=(2, r // tr),
        in_specs=[pl.BlockSpec((None, tr, width), lambda s, i: (s, i, 0)),
                  pl.BlockSpec((None, 1, width), lambda s, i: (s, 0, 0)),
                  pl.BlockSpec((None, width, HEAD_DIM), lambda s, i: (s, 0, 0)),
                  pl.BlockSpec((None, HEAD_DIM, HEAD_DIM), lambda s, i: (s, 0, 0))],
        out_specs=pl.BlockSpec((None, tr, HEAD_DIM), lambda s, i: (s, i, 0)),
        compiler_params=_cparams(2),
        name="compress",
    )(x, pos, w1, w2)


def _masked_probs(s, mask):
    s = jnp.where(mask, s, NEG)
    m = jnp.max(s, axis=-1, keepdims=True)
    p = jnp.where(mask, jnp.exp(s - m), 0.0)
    return p / jnp.maximum(jnp.sum(p, axis=-1, keepdims=True), 1e-30)


def _softmax_step(s, bias, v, carry):
    m, l, acc = carry
    tq = bias.shape[0]
    s = s.reshape(4, tq, CK) + bias[None]
    m_new = jnp.maximum(m, jnp.max(s, axis=-1, keepdims=True))
    alpha = jnp.exp(m - m_new)
    p = jnp.exp(s - m_new)
    l = alpha * l + jnp.sum(p, axis=-1, keepdims=True)
    acc = alpha.reshape(4 * tq, 1) * acc + _dot(p.reshape(4 * tq, CK).astype(BF16), v)
    return m_new, l, acc


def _softmax_init(tq):
    return (jnp.full((4, tq, 1), NEG, F32), jnp.zeros((4, tq, 1), F32), jnp.zeros((4 * tq, HEAD_DIM), F32))


def _softmax_done(carry):
    _, l, acc = carry
    return acc / jnp.maximum(l.reshape(acc.shape[0], 1), 1e-30)


def _mask_bias(allowed):
    return jnp.where(allowed, 0.0, NEG)


def _query_positions(tq, pos0, pos_stride):
    i = pl.program_id(1)
    start = pos0 + i * (tq * pos_stride)
    last = start + (tq - 1) * pos_stride
    t = lax.broadcasted_iota(I32, (tq, 1), 0)
    qpos = start + t * pos_stride
    return start, last, qpos


def _heads_as_rows(ref, scale):
    return jnp.concatenate([ref[h] for h in range(4)], axis=0) * scale


def _key_positions(c):
    return c * CK + lax.broadcasted_iota(I32, (1, CK), 1)


def _chunk_rows(ref, c):
    return ref[pl.ds(pl.multiple_of(c * CK, CK), CK), :]


PICK_LANES = 128


def _nsa_kernel(q_ref, misc_ref, kc_ref, vc_ref, kst_ref, vs_ref, kwt_ref, vw_ref, o_ref, *,
                tq, pos0, pos_stride, win_base, n_sel):
    nb = kc_ref.shape[0]
    start, last, qpos = _query_positions(tq, pos0, pos_stride)
    qpos4 = _tile4(qpos)
    q = _heads_as_rows(q_ref, HEAD_DIM ** -0.5)

    jb = lax.broadcasted_iota(I32, (1, nb), 1)
    mask_c = ((jb + 1) * NSA_BLOCK - 1) <= qpos4
    p_c = _masked_probs(_dot_nt(q, kc_ref[...]), mask_c)
    o_c = _dot(p_c.astype(BF16), vc_ref[...])
    imp = p_c[0:tq] + p_c[tq:2 * tq] + p_c[2 * tq:3 * tq] + p_c[3 * tq:4 * tq]

    cur = qpos // NSA_BLOCK
    forced = (jb == 0) | (jb == cur) | (jb == cur - 1)
    score = jnp.where(jb <= cur, imp, -1.0) + jnp.where(forced, FORCE_BONUS, 0.0)
    if tq < PICK_LANES:
        score = jnp.concatenate([score, jnp.zeros((PICK_LANES - tq, nb), F32)], axis=0)
    jcol = lax.broadcasted_iota(I32, (nb, PICK_LANES), 0).astype(F32)

    def pick(_, carry):
        sc, sel = carry
        m = jnp.max(sc, axis=0, keepdims=True)
        first = jnp.min(jnp.where(sc == m, jcol, 1e9), axis=0, keepdims=True)
        hit = jcol == first
        return jnp.where(hit, -3e38, sc), jnp.where(hit, 1.0, sel)

    _, sel_t = lax.fori_loop(0, n_sel, pick, (score.T, jnp.zeros((nb, PICK_LANES), F32)))
    sel = sel_t.T[0:tq].astype(BF16)

    jrow = lax.broadcasted_iota(I32, (nb, CK), 0)
    kcol = lax.broadcasted_iota(I32, (nb, CK), 1)

    def sel_chunk(c, carry):
        expand = jnp.where(jrow == ((c * CK + kcol) // NSA_BLOCK), 1.0, 0.0).astype(BF16)
        bias = _mask_bias((_dot(sel, expand) > 0.5) & (_key_positions(c) <= qpos))
        return _softmax_step(_dot(q, kst_ref[c]), bias, _chunk_rows(vs_ref, c), carry)

    o_s = _softmax_done(lax.fori_loop(0, last // CK + 1, sel_chunk, _softmax_init(tq)))

    carry = _softmax_init(tq)
    c1 = start // CK
    for back in (1, 0):
        c = c1 - back
        cc = jnp.maximum(c - win_base, 0)
        kpos = _key_positions(c)
        dpos = qpos - kpos
        bias = _mask_bias((dpos >= 0) & (dpos < NSA_WINDOW) & (kpos >= 0))
        carry = _softmax_step(_dot(q, kwt_ref[cc]), bias, _chunk_rows(vw_ref, cc), carry)
    o_w = _softmax_done(carry)

    g = _sigmoid(misc_ref[...])
    outs = []
    for h in range(4):
        rows = slice(h * tq, (h + 1) * tq)
        col = MISC_GA + 3 * h
        outs.append(g[:, col:col + 1] * o_c[rows] + g[:, col + 1:col + 2] * o_s[rows]
                    + g[:, col + 2:col + 3] * o_w[rows])
    o_ref[...] = jnp.concatenate(outs, axis=-1).astype(o_ref.dtype)


def _nsa(q, misc, kc, vc, ks, vs, kw, vw, *, nbatch, nq, tq, pos0, pos_stride, win_base, n_sel):
    nrows = q.shape[1]
    kern = functools.partial(_nsa_kernel, tq=tq, pos0=pos0, pos_stride=pos_stride, win_base=win_base,
                             n_sel=n_sel)
    return pl.pallas_call(
        kern,
        out_shape=jax.ShapeDtypeStruct((nrows, 4 * HEAD_DIM), BF16),
        grid=(nbatch, nq),
        in_specs=[pl.BlockSpec((4, tq, HEAD_DIM), lambda b, i: (0, b * nq + i, 0)),
                  pl.BlockSpec((tq, 128), lambda b, i: (b * nq + i, 0)),
                  pl.BlockSpec((None, NB_PAD, HEAD_DIM), lambda b, i: (b, 0, 0)),
                  pl.BlockSpec((None, NB_PAD, HEAD_DIM), lambda b, i: (b, 0, 0)),
                  ks[1], vs[1], kw[1], vw[1]],
        out_specs=pl.BlockSpec((tq, 4 * HEAD_DIM), lambda b, i: (b * nq + i, 0)),
        compiler_params=_cparams(2),
        name="nsa",
    )(q, misc, kc, vc, ks[0], vs[0], kw[0], vw[0])


LANES = 128
COUNT_UNROLL = 4


def _index_keys(qi, wcol, kit, kpos, qpos, tq):
    rel = jnp.maximum(_dot(qi, kit), 0.0) * wcol
    sc = rel[0:tq] + rel[tq:2 * tq] + rel[2 * tq:3 * tq] + rel[3 * tq:4 * tq]
    sc = jnp.where(sc == 0.0, 0.0, sc)
    sc = jnp.where(kpos <= qpos, sc, NEG)
    bits = lax.bitcast_convert_type(sc, I32)
    return jnp.where(bits < 0, bits ^ 0x7FFFFFFF, bits)


def _index_weights(misc_ref):
    w = misc_ref[:, MISC_WI:MISC_WI + 4] * (IDX_HEADS ** -0.5 * HEAD_DIM ** -0.5)
    return jnp.concatenate([w[:, h:h + 1] for h in range(4)], axis=0)


def _chunk_loop(n_chunks, body, init):
    if isinstance(n_chunks, int):
        carry = init
        for c in range(n_chunks):
            carry = body(c, carry)
        return carry

    def several(g, carry):
        for u in range(COUNT_UNROLL):
            carry = body(g * COUNT_UNROLL + u, carry)
        return carry

    n_groups = n_chunks // COUNT_UNROLL
    carry = lax.fori_loop(0, n_groups, several, init)
    return lax.fori_loop(n_groups * COUNT_UNROLL, n_chunks, body, carry)


def _key_counter(get_keys, n_chunks, rows):
    def count(pred):
        def one(c, acc):
            ind = jnp.where(pred(get_keys(c), c), 1.0, 0.0)
            return acc + (ind[:, 0:128] + ind[:, 128:256] + ind[:, 256:384] + ind[:, 384:512])

        acc = _chunk_loop(n_chunks, one, jnp.zeros((rows, LANES), F32))
        return jnp.sum(acc, axis=-1, keepdims=True)

    return count


def _top_keys(count, n_keep):
    keep = float(n_keep)
    thr = jnp.where(count(lambda k, c: k >= 0) >= keep, 0, INT_MIN).astype(I32)

    def thr_bit(b, thr):
        cand = thr | jnp.left_shift(jnp.int32(1), 30 - b)
        return jnp.where(count(lambda k, c: k >= cand) >= keep, cand, thr)

    thr = lax.fori_loop(0, 31, thr_bit, thr)
    return thr, keep - count(lambda k, c: k > thr)


def _tie_cut(count, thr, need, rows, n_positions):
    n_pos_bits = (n_positions - 1).bit_length()

    def cut_bit(b, cut):
        cand = cut + jnp.left_shift(jnp.int32(1), n_pos_bits - 1 - b)
        n = count(lambda k, c: (k == thr) & (_key_positions(c) < cand))
        return jnp.where(n < need, cand, cut)

    return lax.fori_loop(0, n_pos_bits, cut_bit, jnp.zeros((rows, 1), I32))


def _heads_to_lanes(o, tq):
    return jnp.concatenate([o[h * tq:(h + 1) * tq] for h in range(4)], axis=-1)


def _dsa_kernel(qd_ref, qi_ref, misc_ref, kdt_ref, vd_ref, kit_ref, o_ref, key_scr, *, tq, n_keep):
    start, last, qpos = _query_positions(tq, 0, 1)
    n_chunks = last // CK + 1
    qd = _heads_as_rows(qd_ref, HEAD_DIM ** -0.5)
    qi = _heads_as_rows(qi_ref, 1.0)
    wcol = _index_weights(misc_ref)

    def index_chunk(c, _):
        key_scr[c] = _index_keys(qi, wcol, kit_ref[c], _key_positions(c), qpos, tq)
        return 0

    lax.fori_loop(0, n_chunks, index_chunk, 0)
    thr, need = _top_keys(_key_counter(lambda c: key_scr[c], n_chunks, tq), n_keep)

    tri = jnp.where(lax.broadcasted_iota(I32, (CK, CK), 0) <= lax.broadcasted_iota(I32, (CK, CK), 1),
                    1.0, 0.0).astype(BF16)

    def attend_chunk(c, carry):
        state, seen = carry
        k = key_scr[c]
        equal = jnp.where(k == thr, 1.0, 0.0)
        rank = seen + _dot(equal.astype(BF16), tri)
        taken = (k > thr) | ((k == thr) & (rank <= need))
        bias = _mask_bias(taken & (_key_positions(c) <= qpos))
        state = _softmax_step(_dot(qd, kdt_ref[c]), bias, _chunk_rows(vd_ref, c), state)
        return state, seen + jnp.sum(equal, axis=-1, keepdims=True)

    state, _ = lax.fori_loop(0, n_chunks, attend_chunk, (_softmax_init(tq), jnp.zeros((tq, 1), F32)))
    o_ref[...] = _heads_to_lanes(_softmax_done(state), tq).astype(o_ref.dtype)


def _dsa(qd, qi, misc, kd, vd, ki, *, nbatch, nq, tq, n_keep, n_chunks):
    nrows = qd.shape[1]
    head_spec = pl.BlockSpec((4, tq, HEAD_DIM), lambda b, i: (0, b * nq + i, 0))
    return pl.pallas_call(
        functools.partial(_dsa_kernel, tq=tq, n_keep=n_keep),
        out_shape=jax.ShapeDtypeStruct((nrows, 4 * HEAD_DIM), BF16),
        grid=(nbatch, nq),
        in_specs=[head_spec, head_spec, pl.BlockSpec((tq, 128), lambda b, i: (b * nq + i, 0)),
                  kd[1], vd[1], ki[1]],
        out_specs=pl.BlockSpec((tq, 4 * HEAD_DIM), lambda b, i: (b * nq + i, 0)),
        scratch_shapes=[pltpu.VMEM((n_chunks, tq, CK), I32)],
        compiler_params=_cparams(2),
        name="dsa",
    )(qd, qi, misc, kd[0], vd[0], ki[0])


def _dsa_step_index_kernel(qi_ref, misc_ref, kit_ref, key_ref, *, tq, pos, n_chunks):
    qi = _heads_as_rows(qi_ref, 1.0)
    wcol = _index_weights(misc_ref)
    for c in range(n_chunks):
        key_ref[c] = _index_keys(qi, wcol, kit_ref[c], _key_positions(c), pos, tq)[0:1, :]


def _dsa_step_top_kernel(key_ref, thr_ref, cut_ref, *, n_keep, n_chunks):
    rows = key_ref.shape[1]
    count = _key_counter(lambda c: key_ref[c], n_chunks, rows)
    thr, need = _top_keys(count, n_keep)
    thr_ref[...] = jnp.broadcast_to(thr, thr_ref.shape)
    cut_ref[...] = jnp.broadcast_to(_tie_cut(count, thr, need, rows, n_chunks * CK), cut_ref.shape)


def _dsa_step_attend_kernel(qd_ref, key_ref, thr_ref, cut_ref, kdt_ref, vd_ref, o_ref, *, tq, pos, n_chunks):
    qd = _heads_as_rows(qd_ref, HEAD_DIM ** -0.5)
    thr = thr_ref[:, 0:1]
    cut = cut_ref[:, 0:1]
    carry = _softmax_init(tq)
    for c in range(n_chunks):
        k = key_ref[c]
        kpos = _key_positions(c)
        bias = _mask_bias(((k > thr) | ((k == thr) & (kpos <= cut))) & (kpos <= pos))
        carry = _softmax_step(_dot(qd, kdt_ref[c]), jnp.broadcast_to(bias, (tq, CK)),
                              vd_ref[c * CK:(c + 1) * CK, :], carry)
    o_ref[...] = _heads_to_lanes(_softmax_done(carry), tq).astype(o_ref.dtype)


def _dsa_step(qd, qi, misc, kdt, vd, kit, *, nseq, tq, pos, n_keep, n_chunks):
    head_spec = pl.BlockSpec((4, tq, HEAD_DIM), lambda b: (0, b, 0))
    chunk_spec = pl.BlockSpec((None, n_chunks, HEAD_DIM, CK), lambda b: (b, 0, 0, 0))
    key_spec = pl.BlockSpec((n_chunks, None, 1, CK), lambda b: (0, b, 0, 0))
    row_spec = pl.BlockSpec((None, 1, LANES), lambda b: (b, 0, 0))
    keys = pl.pallas_call(
        functools.partial(_dsa_step_index_kernel, tq=tq, pos=pos, n_chunks=n_chunks),
        out_shape=jax.ShapeDtypeStruct((n_chunks, nseq, 1, CK), I32),
        grid=(nseq,),
        in_specs=[head_spec, pl.BlockSpec((tq, 128), lambda b: (b, 0)), chunk_spec],
        out_specs=key_spec,
        compiler_params=_cparams(1),
        name="dsa_step_index",
    )(qi, misc, kit)
    thr, cut = pl.pallas_call(
        functools.partial(_dsa_step_top_kernel, n_keep=n_keep, n_chunks=n_chunks),
        out_shape=[jax.ShapeDtypeStruct((nseq, LANES), I32)] * 2,
        compiler_params=pltpu.CompilerParams(vmem_limit_bytes=VMEM_LIMIT),
        name="dsa_step_top",
    )(keys.reshape(n_chunks, nseq, CK))
    return pl.pallas_call(
        functools.partial(_dsa_step_attend_kernel, tq=tq, pos=pos, n_chunks=n_chunks),
        out_shape=jax.ShapeDtypeStruct((nseq * tq, 4 * HEAD_DIM), BF16),
        grid=(nseq,),
        in_specs=[head_spec, key_spec, row_spec, row_spec, chunk_spec,
                  pl.BlockSpec((None, n_chunks * CK, HEAD_DIM), lambda b: (b, 0, 0))],
        out_specs=pl.BlockSpec((tq, 4 * HEAD_DIM), lambda b: (b, 0)),
        compiler_params=_cparams(1),
        name="dsa_step_attend",
    )(qd, keys, thr[:, None, :], cut[:, None, :], kdt, vd)


CONV_HALO = 32


def _conv_kernel(cur_ref, halo_ref, w_ref, b_ref, g_ref, beta_ref, o_ref, tail_ref, hx_scr, *, tc):
    i = pl.program_id(1)

    def glu(x):
        return x[:, :CONV_CH] * _sigmoid(x[:, CONV_CH:])

    hx_scr[0:CONV_HALO] = jnp.where(i > 0, glu(halo_ref[...]), 0.0)
    hx_scr[CONV_HALO:CONV_HALO + tc] = glu(cur_ref[...])
    first = CONV_HALO - (CONV_WIDTH - 1)
    y = jnp.broadcast_to(b_ref[...], (tc, CONV_CH))
    for k in range(CONV_WIDTH):
        y = y + hx_scr[first + k:first + k + tc] * w_ref[k:k + 1, :]
    y = _layernorm(y, g_ref[...], beta_ref[...])
    o_ref[...] = (y * _sigmoid(y)).astype(o_ref.dtype)
    tail_ref[...] = hx_scr[tc:tc + CONV_HALO]


def _conv(glu, w, b, g, beta, *, nbatch, seq, tc):
    n = glu.shape[0]
    nt = seq // tc
    per = tc // CONV_HALO
    vec = pl.BlockSpec((1, CONV_CH), lambda bb, i: (0, 0))
    return pl.pallas_call(
        functools.partial(_conv_kernel, tc=tc),
        out_shape=[jax.ShapeDtypeStruct((n, CONV_CH), BF16),
                   jax.ShapeDtypeStruct((nbatch, CONV_HALO, CONV_CH), F32)],
        grid=(nbatch, nt),
        in_specs=[pl.BlockSpec((tc, 2 * CONV_CH), lambda bb, i: (bb * nt + i, 0)),
                  pl.BlockSpec((CONV_HALO, 2 * CONV_CH),
                               lambda bb, i: (jnp.maximum((bb * nt + i) * per - 1, 0), 0)),
                  pl.BlockSpec((CONV_WIDTH, CONV_CH), lambda bb, i: (0, 0)), vec, vec, vec],
        out_specs=[pl.BlockSpec((tc, CONV_CH), lambda bb, i: (bb * nt + i, 0)),
                   pl.BlockSpec((None, CONV_HALO, CONV_CH), lambda bb, i: (bb, 0, 0))],
        scratch_shapes=[pltpu.VMEM((tc + CONV_HALO, CONV_CH), F32)],
        compiler_params=_cparams(2),
        name="conv",
    )(glu, glu, w, b, g, beta)


def _gelu(x):
    return 0.5 * x * (1.0 + lax.erf(x * (2.0 ** -0.5)))


def _sgu_kernel(uv_ref, g_ref, beta_ref, w_ref, bias_ref, o_ref, *, ts):
    a = _gelu(uv_ref[...])
    u = a[:, :SGU_CH]
    vn = _layernorm(a[:, SGU_CH:], g_ref[...], beta_ref[...]).astype(BF16)
    ri = lax.broadcasted_iota(I32, (SGU_CHUNK, SGU_CHUNK), 0)
    ci = lax.broadcasted_iota(I32, (SGU_CHUNK, SGU_CHUNK), 1)
    group = lax.broadcasted_iota(I32, (SGU_CHUNK, SGU_CH), 1) // (SGU_CH // SGU_GROUPS)
    ws = [jnp.where(ci <= ri, w_ref[gi], 0.0).astype(BF16) for gi in range(SGU_GROUPS)]
    for c in range(ts // SGU_CHUNK):
        rows = slice(c * SGU_CHUNK, (c + 1) * SGU_CHUNK)
        mixed = bias_ref[...]
        for gi in range(SGU_GROUPS):
            mixed = mixed + jnp.where(group == gi, _dot(ws[gi], vn[rows]), 0.0)
        o_ref[rows, :] = (u[rows] * mixed).astype(o_ref.dtype)


def _sgu(uv, g, beta, w, bias, *, ts):
    n = uv.shape[0]
    vec = pl.BlockSpec((1, SGU_CH), lambda i: (0, 0))
    return pl.pallas_call(
        functools.partial(_sgu_kernel, ts=ts),
        out_shape=jax.ShapeDtypeStruct((n, SGU_CH), BF16),
        grid=(n // ts,),
        in_specs=[pl.BlockSpec((ts, 2 * SGU_CH), lambda i: (i, 0)), vec, vec,
                  pl.BlockSpec((SGU_GROUPS, SGU_CHUNK, SGU_CHUNK), lambda i: (0, 0, 0)),
                  pl.BlockSpec((SGU_CHUNK, SGU_CH), lambda i: (0, 0))],
        out_specs=pl.BlockSpec((ts, SGU_CH), lambda i: (i, 0)),
        compiler_params=_cparams(1),
        name="sgu",
    )(uv, g, beta, w, bias)


def _step_mix_kernel(glu_ref, uv_ref, hist_ref, cw_ref, cb_ref, cg_ref, cbeta_ref, sg_ref, sbeta_ref,
                     sdiag_ref, sbias_ref, ob_ref, oc_ref, hx_ref, v_ref):
    x = glu_ref[...]
    hx = x[:, :CONV_CH] * _sigmoid(x[:, CONV_CH:])
    hx_ref[...] = hx
    y = cb_ref[...] + hx * cw_ref[CONV_WIDTH - 1:CONV_WIDTH, :]
    for k in range(CONV_WIDTH - 1):
        y = y + hist_ref[k] * cw_ref[k:k + 1, :]
    y = _layernorm(y, cg_ref[...], cbeta_ref[...])
    ob_ref[...] = (y * _sigmoid(y)).astype(ob_ref.dtype)
    a = _gelu(uv_ref[...])
    v = a[:, SGU_CH:]
    v_ref[...] = v
    vn = _layernorm(v, sg_ref[...], sbeta_ref[...])
    oc_ref[...] = (a[:, :SGU_CH] * (sdiag_ref[...] * vn + sbias_ref[...])).astype(oc_ref.dtype)


def _step_mix(glu, uv, hist, cw, cb, cg, cbeta, sg, sbeta, sdiag, sbias):
    n = glu.shape[0]
    sds = jax.ShapeDtypeStruct
    return pl.pallas_call(
        _step_mix_kernel,
        out_shape=[sds((n, CONV_CH), BF16), sds((n, SGU_CH), BF16), sds((n, CONV_CH), F32), sds((n, SGU_CH), F32)],
        compiler_params=pltpu.CompilerParams(vmem_limit_bytes=VMEM_LIMIT),
        name="step_mix",
    )(glu, uv, hist, cw, cb, cg, cbeta, sg, sbeta, sdiag, sbias)


def _merge_kernel(x_ref, g_ref, oa_ref, ob_ref, oc_ref, od_ref, wg_ref, wb_ref, wo_ref, y_ref):
    x = x_ref[...]
    d = x.shape[-1]
    hb = _rms(x, g_ref[...]).astype(BF16)
    acc = jnp.zeros(x.shape, F32)
    for k, o_ref in enumerate((oa_ref, ob_ref, oc_ref, od_ref)):
        gate = _sigmoid(_dot(hb, wg_ref[:, k * d:(k + 1) * d]))
        acc = acc + gate * _dot(o_ref[...], wb_ref[k])
    y_ref[...] = x + _dot(acc.astype(BF16), wo_ref[...])


def _merge(x, g, oa, ob, oc, od, wg, wb, wo, *, tm):
    n, d = x.shape
    row = pl.BlockSpec((tm, d), lambda i: (i, 0))
    br = pl.BlockSpec((tm, BRANCH_WIDTH), lambda i: (i, 0))
    once = pl.Buffered(1)
    return pl.pallas_call(
        _merge_kernel,
        out_shape=jax.ShapeDtypeStruct((n, d), F32),
        grid=(n // tm,),
        in_specs=[row, pl.BlockSpec((1, d), lambda i: (0, 0)), br, br, br, br,
                  pl.BlockSpec((d, N_BRANCH * d), lambda i: (0, 0), pipeline_mode=once),
                  pl.BlockSpec((N_BRANCH, BRANCH_WIDTH, d), lambda i: (0, 0, 0), pipeline_mode=once),
                  pl.BlockSpec((d, d), lambda i: (0, 0), pipeline_mode=once)],
        out_specs=row,
        compiler_params=_cparams(1),
        name="merge",
    )(x, g, oa, ob, oc, od, wg, wb, wo)


def _mlp_kernel(x_ref, g_ref, wu_ref, wd_ref, gf_ref, y_ref, *n_ref):
    x = x_ref[...]
    hb = _rms(x, g_ref[...]).astype(BF16)
    a = jnp.square(jnp.maximum(_dot(hb, wu_ref[...]), 0.0)).astype(BF16)
    y = x + _dot(a, wd_ref[...])
    y_ref[...] = y
    if n_ref:
        n_ref[0][...] = _rms(y, gf_ref[...])


def _mlp(x, g, wu, wd, gf, *, tm, final):
    n, d = x.shape
    row = pl.BlockSpec((tm, d), lambda i: (i, 0))
    vec = pl.BlockSpec((1, d), lambda i: (0, 0))
    once = pl.Buffered(1)
    out_shape = [jax.ShapeDtypeStruct((n, d), F32)] * (2 if final else 1)
    return pl.pallas_call(
        _mlp_kernel,
        out_shape=out_shape,
        grid=(n // tm,),
        in_specs=[row, vec, pl.BlockSpec(wu.shape, lambda i: (0, 0), pipeline_mode=once),
                  pl.BlockSpec(wd.shape, lambda i: (0, 0), pipeline_mode=once), vec],
        out_specs=[row] * (2 if final else 1),
        compiler_params=_cparams(1),
        name="mlp",
    )(x, g, wu, wd, gf)


def _select_rows(width, first):
    r = lax.broadcasted_iota(I32, (HEAD_DIM, width), 0)
    c = lax.broadcasted_iota(I32, (HEAD_DIM, width), 1)
    return jnp.where(c == r + first, 1.0, 0.0).astype(BF16)


def _new_row_chunk(row):
    first = lax.broadcasted_iota(I32, (CK, row.shape[-1]), 0) == 0
    return jnp.where(first, jnp.broadcast_to(row, (CK, row.shape[-1])), 0.0)


def _assemble_kernel(pt_ref, *refs, n_pages, page):
    nsa_pages = refs[:n_pages]
    dsa_pages = refs[n_pages:2 * n_pages]
    win_ref, nsa_new_ref, dsa_new_ref = refs[2 * n_pages:2 * n_pages + 3]
    cmp_ref, kst_ref, vs_ref, kwt_ref, vw_ref, kdt_ref, vd_ref, kit_ref = refs[2 * n_pages + 3:]
    per_chunk = CK // page
    pick_kslc = _select_rows(256, 128)
    pick_kd = _select_rows(192, 0)
    pick_ki = _select_rows(192, 128)
    pick_kwin = _select_rows(128, 0)

    def put_nsa(x, c, lanes, rows):
        kst_ref[c, :, lanes] = _dot_nt(pick_kslc, x.astype(BF16)).astype(BF16)
        vs_ref[rows, :] = x[:, 192:256].astype(BF16)

    def put_dsa(x, c, lanes, rows):
        xb = x.astype(BF16)
        kdt_ref[c, :, lanes] = _dot_nt(pick_kd, xb).astype(BF16)
        kit_ref[c, :, lanes] = _dot_nt(pick_ki, xb).astype(BF16)
        vd_ref[rows, :] = x[:, 64:128].astype(BF16)

    for p in range(n_pages):
        c, r = divmod(p, per_chunk)
        lanes = slice(r * page, (r + 1) * page)
        rows = slice(p * page, (p + 1) * page)
        x = nsa_pages[p][...]
        cmp_ref[rows, :] = x[:, 0:128]
        put_nsa(x, c, lanes, rows)
        put_dsa(dsa_pages[p][...], c, lanes, rows)

    c_new = n_pages // per_chunk
    new_rows = slice(c_new * CK, (c_new + 1) * CK)
    nsa_new = nsa_new_ref[...]
    put_nsa(_new_row_chunk(nsa_new[:, 0:256]), c_new, slice(0, CK), new_rows)
    put_dsa(_new_row_chunk(dsa_new_ref[...]), c_new, slice(0, CK), new_rows)

    def put_win(x, c, rows):
        kwt_ref[c] = _dot_nt(pick_kwin, x.astype(BF16)).astype(BF16)
        vw_ref[rows, :] = x[:, 64:128].astype(BF16)

    put_win(win_ref[...], 0, slice(0, CK))
    put_win(_new_row_chunk(nsa_new[:, 256:384]), 1, slice(CK, 2 * CK))


def _assemble(page_table, cache_nsa, cache_dsa, state_win, nsa_new, dsa_new, *, layer):
    nseq, n_pages = page_table.shape
    page = cache_nsa.shape[2]
    n_chunks = n_pages * page // CK + 1
    keys = n_chunks * CK
    sds = jax.ShapeDtypeStruct

    def page_spec(width, p):
        return pl.BlockSpec((None, None, page, width), lambda b, pt: (pt[b, p], layer, 0, 0))

    def per_seq(*shape):
        return pl.BlockSpec((None,) + shape, lambda b, pt: (b,) + (0,) * len(shape))

    grid_spec = pltpu.PrefetchScalarGridSpec(
        num_scalar_prefetch=1,
        grid=(nseq,),
        in_specs=([page_spec(256, p) for p in range(n_pages)] + [page_spec(192, p) for p in range(n_pages)]
                  + [pl.BlockSpec((None, None, CK, 128), lambda b, pt: (b, layer, 0, 0)),
                     per_seq(1, 384), per_seq(1, 192)]),
        out_specs=[per_seq(n_pages * page, 128), per_seq(n_chunks, HEAD_DIM, CK), per_seq(keys, HEAD_DIM),
                   per_seq(2, HEAD_DIM, CK), per_seq(2 * CK, HEAD_DIM),
                   per_seq(n_chunks, HEAD_DIM, CK), per_seq(keys, HEAD_DIM), per_seq(n_chunks, HEAD_DIM, CK)],
    )
    return pl.pallas_call(
        functools.partial(_assemble_kernel, n_pages=n_pages, page=page),
        out_shape=[sds((nseq, n_pages * page, 128), F32), sds((nseq, n_chunks, HEAD_DIM, CK), BF16),
                   sds((nseq, keys, HEAD_DIM), BF16), sds((nseq, 2, HEAD_DIM, CK), BF16),
                   sds((nseq, 2 * CK, HEAD_DIM), BF16), sds((nseq, n_chunks, HEAD_DIM, CK), BF16),
                   sds((nseq, keys, HEAD_DIM), BF16), sds((nseq, n_chunks, HEAD_DIM, CK), BF16)],
        grid_spec=grid_spec,
        compiler_params=_cparams(1),
        name="assemble",
    )(page_table, *([cache_nsa] * n_pages), *([cache_dsa] * n_pages), state_win, nsa_new, dsa_new)


def _prep_w_in(w_in):
    pts = np.cumsum(SPLIT_WIDTHS)[:-1].tolist()
    qa, kva, ga, glu, uv, qd, kvd, qi, ki, wi = jnp.split(w_in, pts, axis=-1)
    pad = jnp.zeros(w_in.shape[:2] + (128 - 16,), w_in.dtype)
    w = jnp.concatenate([qa, kva, glu, uv, qd, qi, wi, ga, pad, kvd, ki], axis=-1).astype(BF16)
    wt = jnp.concatenate([kva[..., 128:192], kva[..., 256:320], kvd[..., 0:64], ki], axis=-1)
    return w, jnp.swapaxes(wt, 1, 2).astype(BF16)


def _seq_spec(*shape):
    return pl.BlockSpec((None,) + shape, lambda b, i: (b,) + (0,) * len(shape))


def kernel(x_prompt, x_sample, cache_nsa, cache_dsa, state_win, state_conv, page_table, norm1_g, norm2_g,
           final_g, w_in, cmp_pos, cmp_w1, cmp_w2, conv_w, conv_b, conv_ln_g, conv_ln_b, sgu_ln_g, sgu_ln_b,
           sgu_w, sgu_b, w_branch, w_gate, w_out, w_up, w_down):
    nb_, seq, d = x_prompt.shape
    nseq, t_dec, _ = x_sample.shape
    depth = w_in.shape[0]
    n_pool, _, page, _, _ = cache_nsa.shape
    n_pages = page_table.shape[1]
    past = n_pages * page
    w_buf = state_win.shape[2]
    assert t_dec == 1 and seq % CK == 0 and past % CK == 0 and w_buf == CK == NSA_WINDOW
    nbp = seq // NSA_BLOCK
    assert nbp <= NB_PAD and past // NSA_BLOCK < NB_PAD
    n_tok = nb_ * seq
    sr = SAMPLE_ROWS

    w_proj, w_proj_t = _prep_w_in(w_in)
    wg, wb, wo = w_gate.astype(BF16), w_branch.astype(BF16), w_out.astype(BF16)
    wu, wd = w_up.astype(BF16), w_down.astype(BF16)
    c_pos = cmp_pos.reshape(depth, 2, 1, NSA_BLOCK * HEAD_DIM)
    c_w1, c_w2 = cmp_w1.astype(BF16), cmp_w2.astype(BF16)
    sgu_bias = jnp.repeat(jnp.swapaxes(sgu_b, 1, 2), SGU_CH // SGU_GROUPS, axis=2)
    sgu_diag = jnp.repeat(sgu_w[:, :, 0, 0], SGU_CH // SGU_GROUPS, axis=1)[:, None, :]
    sgu_bias0 = sgu_bias[:, 0:1, :]
    vec = lambda a, l: a[l][None, :]
    cache_nsa4 = cache_nsa.reshape(n_pool, depth, page, 4 * HEAD_DIM)
    cache_dsa4 = cache_dsa.reshape(n_pool, depth, page, 3 * HEAD_DIM)
    state_win4 = state_win.reshape(nseq, depth, w_buf, 2 * HEAD_DIM)
    conv_hist = jnp.transpose(state_conv, (1, 2, 0, 3))

    n_chunks_p = seq // CK
    n_chunks_s = past // CK + 1
    nq = seq // Q_BLOCK

    def chunks_of(arr, slot, n_chunks):
        return arr, pl.BlockSpec((n_chunks, HEAD_DIM, CK), lambda b, i: (b, slot, 0))

    def rows_of(arr, slot, n_rows):
        return arr, pl.BlockSpec((None, n_rows, HEAD_DIM), lambda b, i: (slot, b, 0))

    xp = x_prompt.reshape(n_tok, d)
    xs = x_sample.reshape(nseq, d)
    outs = {k: [] for k in ("nsa_p", "nsa_s", "dsa_p", "dsa_s", "win_p", "win_s", "conv_p", "conv_s", "sgu_s")}
    yp = ys = None
    for l in range(depth):
        final = l == depth - 1
        qn, kva, glu, uv, qd, qi, misc, dsa_rows, kt4, v4 = _proj(
            xp, vec(norm1_g, l), w_proj[l], w_proj_t[l], tm=CK, emit_t=True)
        kva3 = kva.reshape(nb_, seq, 6, HEAD_DIM)
        cmp_in = jnp.stack([kva3[:, :, 0].reshape(nb_ * nbp, NSA_BLOCK * HEAD_DIM),
                            kva3[:, :, 1].reshape(nb_ * nbp, NSA_BLOCK * HEAD_DIM)])
        kvc = _compress(cmp_in, c_pos[l], c_w1[l], c_w2[l], tr=nb_ * nbp)
        kvc = jnp.pad(kvc.reshape(2, nb_, nbp, HEAD_DIM), ((0, 0), (0, 0), (0, NB_PAD - nbp), (0, 0)))
        kc, vc = kvc[0], kvc[1]
        o_a = _nsa(qn, misc, kc, vc, chunks_of(kt4, 0, n_chunks_p), rows_of(v4, 0, seq),
                   chunks_of(kt4, 1, n_chunks_p), rows_of(v4, 1, seq),
                   nbatch=nb_, nq=nq, tq=Q_BLOCK, pos0=0, pos_stride=1, win_base=0,
                   n_sel=min(NSA_N_SEL, seq // NSA_BLOCK))
        o_d = _dsa(qd, qi, misc, chunks_of(kt4, 2, n_chunks_p), rows_of(v4, 2, seq),
                   chunks_of(kt4, 3, n_chunks_p), nbatch=nb_, nq=nq, tq=Q_BLOCK,
                   n_keep=min(DSA_TOPK, seq // 4), n_chunks=n_chunks_p)
        o_b, conv_tail = _conv(glu, conv_w[l], vec(conv_b, l), vec(conv_ln_g, l), vec(conv_ln_b, l),
                               nbatch=nb_, seq=seq, tc=CK)
        o_c = _sgu(uv, vec(sgu_ln_g, l), vec(sgu_ln_b, l), sgu_w[l], sgu_bias[l], ts=CK)
        xp = _merge(xp, vec(norm1_g, l), o_a, o_b, o_c, o_d, wg[l], wb[l], wo[l], tm=256)
        res = _mlp(xp, vec(norm2_g, l), wu[l], wd[l], final_g[None, :], tm=256, final=final)
        xp = res[0]
        if final:
            yp = res[1]
        outs["nsa_p"].append(kva3[:, :, :4])
        outs["dsa_p"].append(dsa_rows.reshape(nb_, seq, 3, HEAD_DIM))
        outs["win_p"].append(kva3[:, seq - min(NSA_WINDOW, seq):, 4:])
        outs["conv_p"].append(conv_tail[:, CONV_HALO - (CONV_WIDTH - 1):])

        qn, kva, glu, uv, qd, qi, misc, dsa_rows = _proj(
            xs, vec(norm1_g, l), w_proj[l], w_proj_t[l], tm=nseq, emit_t=False)
        cmp_past, kst, vs, kwt, vw, kdt, vdd, kit = _assemble(
            page_table, cache_nsa4, cache_dsa4, state_win4, kva[:, None, :], dsa_rows[:, None, :], layer=l)
        nbs = past // NSA_BLOCK
        cmp_in = jnp.stack([cmp_past[:, :, 0:64].reshape(nseq * nbs, NSA_BLOCK * HEAD_DIM),
                            cmp_past[:, :, 64:128].reshape(nseq * nbs, NSA_BLOCK * HEAD_DIM)])
        kvc = _compress(cmp_in, c_pos[l], c_w1[l], c_w2[l], tr=min(256, nseq * nbs))
        kvc = jnp.pad(kvc.reshape(2, nseq, nbs, HEAD_DIM), ((0, 0), (0, 0), (0, NB_PAD - nbs), (0, 0)))

        def pad_rows(a):
            a = a[..., :, None, :]
            widths = [(0, 0)] * (a.ndim - 2) + [(0, sr - 1), (0, 0)]
            a = jnp.pad(a, widths)
            return a.reshape(a.shape[:-3] + (nseq * sr, a.shape[-1]))

        misc_r = pad_rows(misc)
        o_a = _nsa(pad_rows(qn), misc_r, kvc[0], kvc[1],
                   (kst, _seq_spec(n_chunks_s, HEAD_DIM, CK)), (vs, _seq_spec(n_chunks_s * CK, HEAD_DIM)),
                   (kwt, _seq_spec(2, HEAD_DIM, CK)), (vw, _seq_spec(2 * CK, HEAD_DIM)),
                   nbatch=nseq, nq=1, tq=sr, pos0=past, pos_stride=0, win_base=past // CK - 1,
                   n_sel=min(NSA_N_SEL, nbs + 1))
        o_d = _dsa_step(pad_rows(qd), pad_rows(qi), misc_r, kdt, vdd, kit, nseq=nseq, tq=sr, pos=past,
                        n_keep=min(DSA_TOPK, (past + 1) // 4), n_chunks=n_chunks_s)
        o_a = o_a.reshape(nseq, sr, -1)[:, 0]
        o_d = o_d.reshape(nseq, sr, -1)[:, 0]
        o_b, o_c, hx_new, v_rows = _step_mix(
            glu, uv, conv_hist[l], conv_w[l], vec(conv_b, l), vec(conv_ln_g, l), vec(conv_ln_b, l),
            vec(sgu_ln_g, l), vec(sgu_ln_b, l), sgu_diag[l], sgu_bias0[l])
        xs = _merge(xs, vec(norm1_g, l), o_a, o_b, o_c, o_d, wg[l], wb[l], wo[l], tm=nseq)
        res = _mlp(xs, vec(norm2_g, l), wu[l], wd[l], final_g[None, :], tm=nseq, final=final)
        xs = res[0]
        if final:
            ys = res[1]
        kva3 = kva.reshape(nseq, 1, 6, HEAD_DIM)
        outs["nsa_s"].append(kva3[:, :, :4])
        outs["dsa_s"].append(dsa_rows.reshape(nseq, 1, 3, HEAD_DIM))
        outs["win_s"].append(jnp.concatenate([state_win[:, l], kva3[:, :, 4:]], axis=1)[:, 1:])
        outs["conv_s"].append(jnp.concatenate([state_conv[:, l], hx_new[:, None, :]], axis=1)[:, 1:])
        outs["sgu_s"].append(v_rows[:, None, :])

    st = lambda k: jnp.stack(outs[k], axis=1)
    return (yp.reshape(nb_, seq, d), ys.reshape(nseq, 1, d), st("nsa_p"), st("nsa_s"), st("dsa_p"), st("dsa_s"),
            st("win_p"), st("win_s"), st("conv_p"), st("conv_s"), st("sgu_s"))
```

```python
import functools

import numpy as np
import jax
import jax.numpy as jnp
from jax import lax
from jax.experimental import pallas as pl
from jax.experimental.pallas import tpu as pltpu

F32, BF16, I32 = jnp.float32, jnp.bfloat16, jnp.int32

HEAD_DIM = 64
NSA_HEADS = 4
NSA_BLOCK = 64
NSA_N_SEL = 16
NSA_WINDOW = 512
FORCE_BONUS = 1.0e4
CONV_CH = 256
CONV_WIDTH = 31
SGU_CH = 256
SGU_GROUPS = 4
SGU_CHUNK = 128
DSA_HEADS = 4
IDX_HEADS = 4
DSA_TOPK = 256
N_BRANCH = 4
BRANCH_WIDTH = 256
Q_BLOCK = 128
EPS = 1e-6
NEG = -1e30

SPLIT_WIDTHS = (256, 384, 12, 512, 512, 256, 128, 256, 64, 4)

CK = 512
NB_PAD = 128
SAMPLE_ROWS = 16
VMEM_LIMIT = 56 * 1024 * 1024
INT_MIN = -2 ** 31

W_QA, W_KVA, W_GLU, W_UV, W_QD, W_QI, W_MISC, W_DSA = 0, 256, 640, 1152, 1664, 1920, 2176, 2304
W_TOTAL = 2496
MISC_WI, MISC_GA = 0, 4


def _cparams(n_axes):
    return pltpu.CompilerParams(dimension_semantics=("arbitrary",) * n_axes,
                                vmem_limit_bytes=VMEM_LIMIT)


def _dot(a, b):
    return jnp.dot(a, b, preferred_element_type=F32)


def _dot_nt(a, b):
    return lax.dot_general(a, b, (((1,), (1,)), ((), ())), preferred_element_type=F32)


def _rms(x, g):
    return x * lax.rsqrt(jnp.mean(x * x, axis=-1, keepdims=True) + EPS) * g


def _layernorm(x, g, b):
    mu = jnp.mean(x, axis=-1, keepdims=True)
    xc = x - mu
    var = jnp.mean(xc * xc, axis=-1, keepdims=True)
    return xc * lax.rsqrt(var + EPS) * g + b


def _sigmoid(x):
    return 1.0 / (1.0 + jnp.exp(-x))


def _tile4(a):
    return jnp.concatenate([a, a, a, a], axis=0)


def _proj_kernel(x_ref, g_ref, w_ref, wt_ref, qn_ref, kva_ref, glu_ref, uv_ref, qd_ref, qi_ref,
                 misc_ref, dsa_ref, *t_refs):
    hb = _rms(x_ref[...], g_ref[...]).astype(BF16)

    def slab(off, width):
        return _dot(hb, w_ref[:, off:off + width])

    qa = slab(W_QA, 256)
    qd = slab(W_QD, 256)
    qi = slab(W_QI, 256)
    for h in range(4):
        cols = slice(h * HEAD_DIM, (h + 1) * HEAD_DIM)
        qn_ref[h] = qa[:, cols].astype(BF16)
        qd_ref[h] = qd[:, cols].astype(BF16)
        qi_ref[h] = qi[:, cols].astype(BF16)
    kv = slab(W_KVA, 384)
    kva_ref[...] = kv
    glu_ref[...] = slab(W_GLU, 512)
    uv_ref[...] = slab(W_UV, 512)
    misc_ref[...] = slab(W_MISC, 128)
    dr = slab(W_DSA, 192)
    dsa_ref[...] = dr
    if t_refs:
        kt_ref, v_ref = t_refs
        kt_ref[0] = _dot_nt(wt_ref[...], hb).astype(BF16)
        v_ref[0] = kv[:, 192:256].astype(BF16)
        v_ref[1] = kv[:, 320:384].astype(BF16)
        v_ref[2] = dr[:, 64:128].astype(BF16)


def _proj(x, g, w, wt, *, tm, emit_t):
    n, d = x.shape
    sds = jax.ShapeDtypeStruct
    out_shape = [sds((4, n, HEAD_DIM), BF16), sds((n, 384), F32), sds((n, 512), F32), sds((n, 512), F32),
                 sds((4, n, HEAD_DIM), BF16), sds((4, n, HEAD_DIM), BF16), sds((n, 128), F32), sds((n, 192), F32)]
    head_spec = pl.BlockSpec((4, tm, HEAD_DIM), lambda i: (0, i, 0))

    def row_spec(width):
        return pl.BlockSpec((tm, width), lambda i: (i, 0))

    out_specs = [head_spec, row_spec(384), row_spec(512), row_spec(512), head_spec, head_spec,
                 row_spec(128), row_spec(192)]
    if emit_t:
        assert tm == CK
        out_shape += [sds((n // CK, 256, CK), BF16), sds((3, n, HEAD_DIM), BF16)]
        out_specs += [pl.BlockSpec((1, 256, CK), lambda i: (i, 0, 0)),
                      pl.BlockSpec((3, tm, HEAD_DIM), lambda i: (0, i, 0))]
    return pl.pallas_call(
        _proj_kernel,
        out_shape=out_shape,
        grid=(n // tm,),
        in_specs=[row_spec(d), pl.BlockSpec((1, d), lambda i: (0, 0)),
                  pl.BlockSpec((d, W_TOTAL), lambda i: (0, 0)),
                  pl.BlockSpec((256, d), lambda i: (0, 0))],
        out_specs=out_specs,
        compiler_params=_cparams(1),
        name="proj",
    )(x, g, w, wt)


def _compress_kernel(x_ref, pos_ref, w1_ref, w2_ref, o_ref):
    n_blocks = o_ref.shape[0]
    acc = jnp.zeros((n_blocks, 2 * HEAD_DIM), F32)
    for r in range(NSA_BLOCK):
        xr = x_ref[pl.ds(r, n_blocks, stride=NSA_BLOCK), :] + pos_ref[r:r + 1, :]
        acc = acc + _dot(xr.astype(BF16), w1_ref[r])
    a = acc * _sigmoid(acc)
    o_ref[...] = _dot(a.astype(BF16), w2_ref[...]).astype(o_ref.dtype)


def _compress(x, pos, w1, w2, *, rows_per_step):
    r = x.shape[0]
    width = 2 * HEAD_DIM
    return pl.pallas_call(
        _compress_kernel,
        out_shape=jax.ShapeDtypeStruct((r // NSA_BLOCK, width), BF16),
        grid=(r // rows_per_step,),
        in_specs=[pl.BlockSpec((rows_per_step, width), lambda i: (i, 0)),
                  pl.BlockSpec((NSA_BLOCK, width), lambda i: (0, 0)),
                  pl.BlockSpec((NSA_BLOCK, width, width), lambda i: (0, 0, 0)),
                  pl.BlockSpec((width, width), lambda i: (0, 0))],
        out_specs=pl.BlockSpec((rows_per_step // NSA_BLOCK, width), lambda i: (i, 0)),
        compiler_params=_cparams(1),
        name="compress",
    )(x, pos, w1, w2)


def _masked_probs(s, mask):
    s = jnp.where(mask, s, NEG)
    m = jnp.max(s, axis=-1, keepdims=True)
    p = jnp.where(mask, jnp.exp(s - m), 0.0)
    return p / jnp.maximum(jnp.sum(p, axis=-1, keepdims=True), 1e-30)


def _softmax_step(s, bias, v, carry):
    m, l, acc = carry
    tq = bias.shape[0]
    s = s.reshape(4, tq, CK) + bias[None]
    m_new = jnp.maximum(m, jnp.max(s, axis=-1, keepdims=True))
    alpha = jnp.exp(m - m_new)
    p = jnp.exp(s - m_new)
    l = alpha * l + jnp.sum(p, axis=-1, keepdims=True)
    acc = alpha.reshape(4 * tq, 1) * acc + _dot(p.reshape(4 * tq, CK).astype(BF16), v)
    return m_new, l, acc


def _softmax_init(tq):
    return (jnp.full((4, tq, 1), NEG, F32), jnp.zeros((4, tq, 1), F32), jnp.zeros((4 * tq, HEAD_DIM), F32))


def _softmax_done(carry):
    _, l, acc = carry
    return acc / jnp.maximum(l.reshape(acc.shape[0], 1), 1e-30)


def _mask_bias(allowed):
    return jnp.where(allowed, 0.0, NEG)


def _query_positions(tq, pos0, pos_stride):
    if pos_stride == 0:
        return pos0, pos0, jnp.full((tq, 1), pos0, I32)
    i = pl.program_id(1)
    start = pos0 + i * (tq * pos_stride)
    last = start + (tq - 1) * pos_stride
    t = lax.broadcasted_iota(I32, (tq, 1), 0)
    qpos = start + t * pos_stride
    return start, last, qpos


def _heads_as_rows(ref, scale):
    return jnp.concatenate([ref[h] for h in range(4)], axis=0) * scale


def _key_positions(c):
    return c * CK + lax.broadcasted_iota(I32, (1, CK), 1)


def _chunk_rows(ref, c):
    if isinstance(c, int):
        return ref[c * CK:(c + 1) * CK, :]
    return ref[pl.ds(pl.multiple_of(c * CK, CK), CK), :]


PICK_LANES = 128


def _nsa_kernel(q_ref, misc_ref, kvc_ref, kst_ref, vs_ref, kwt_ref, vw_ref, o_ref, *,
                tq, pos0, pos_stride, win_base, n_sel):
    nb = kvc_ref.shape[0]
    start, last, qpos = _query_positions(tq, pos0, pos_stride)
    qpos4 = _tile4(qpos)
    q = _heads_as_rows(q_ref, HEAD_DIM ** -0.5)

    jb = lax.broadcasted_iota(I32, (1, nb), 1)
    mask_c = ((jb + 1) * NSA_BLOCK - 1) <= qpos4
    p_c = _masked_probs(_dot_nt(q, kvc_ref[:, 0:HEAD_DIM]), mask_c)
    o_c = _dot(p_c.astype(BF16), kvc_ref[:, HEAD_DIM:2 * HEAD_DIM])
    imp = p_c[0:tq] + p_c[tq:2 * tq] + p_c[2 * tq:3 * tq] + p_c[3 * tq:4 * tq]

    cur = qpos // NSA_BLOCK
    forced = (jb == 0) | (jb == cur) | (jb == cur - 1)
    score = jnp.where(jb <= cur, imp, -1.0) + jnp.where(forced, FORCE_BONUS, 0.0)
    if tq < PICK_LANES:
        score = jnp.concatenate([score, jnp.zeros((PICK_LANES - tq, nb), F32)], axis=0)
    jcol = lax.broadcasted_iota(I32, (nb, PICK_LANES), 0).astype(F32)

    def pick(_, carry):
        sc, sel = carry
        m = jnp.max(sc, axis=0, keepdims=True)
        first = jnp.min(jnp.where(sc == m, jcol, 1e9), axis=0, keepdims=True)
        hit = jcol == first
        return jnp.where(hit, -3e38, sc), jnp.where(hit, 1.0, sel)

    _, sel_t = lax.fori_loop(0, n_sel, pick, (score.T, jnp.zeros((nb, PICK_LANES), F32)))
    sel = sel_t.T[0:tq].astype(BF16)

    jrow = lax.broadcasted_iota(I32, (nb, CK), 0)
    kcol = lax.broadcasted_iota(I32, (nb, CK), 1)

    def sel_chunk(c, carry):
        expand = jnp.where(jrow == ((c * CK + kcol) // NSA_BLOCK), 1.0, 0.0).astype(BF16)
        bias = _mask_bias((_dot(sel, expand) > 0.5) & (_key_positions(c) <= qpos))
        return _softmax_step(_dot(q, kst_ref[c]), bias, _chunk_rows(vs_ref, c), carry)

    n_chunks = last // CK + 1
    if isinstance(n_chunks, int):
        carry = _softmax_init(tq)
        for c in range(n_chunks):
            carry = sel_chunk(c, carry)
    else:
        carry = lax.fori_loop(0, n_chunks, sel_chunk, _softmax_init(tq))
    o_s = _softmax_done(carry)

    carry = _softmax_init(tq)
    c1 = start // CK
    for back in (1, 0):
        c = c1 - back
        cc = max(c - win_base, 0) if isinstance(c, int) else jnp.maximum(c - win_base, 0)
        kpos = _key_positions(c)
        dpos = qpos - kpos
        bias = _mask_bias((dpos >= 0) & (dpos < NSA_WINDOW) & (kpos >= 0))
        carry = _softmax_step(_dot(q, kwt_ref[cc]), bias, _chunk_rows(vw_ref, cc), carry)
    o_w = _softmax_done(carry)

    g = _sigmoid(misc_ref[...])
    outs = []
    for h in range(4):
        rows = slice(h * tq, (h + 1) * tq)
        col = MISC_GA + 3 * h
        outs.append(g[:, col:col + 1] * o_c[rows] + g[:, col + 1:col + 2] * o_s[rows]
                    + g[:, col + 2:col + 3] * o_w[rows])
    o_ref[...] = jnp.concatenate(outs, axis=-1).astype(o_ref.dtype)


def _nsa(q, misc, kvc, ks, vs, kw, vw, *, nbatch, nq, tq, pos0, pos_stride, win_base, n_sel):
    nrows = q.shape[1]
    kern = functools.partial(_nsa_kernel, tq=tq, pos0=pos0, pos_stride=pos_stride, win_base=win_base,
                             n_sel=n_sel)
    return pl.pallas_call(
        kern,
        out_shape=jax.ShapeDtypeStruct((nrows, 4 * HEAD_DIM), BF16),
        grid=(nbatch, nq),
        in_specs=[pl.BlockSpec((4, tq, HEAD_DIM), lambda b, i: (0, b * nq + i, 0)),
                  pl.BlockSpec((tq, 128), lambda b, i: (b * nq + i, 0)),
                  pl.BlockSpec((None, NB_PAD, 2 * HEAD_DIM), lambda b, i: (b, 0, 0)),
                  ks[1], vs[1], kw[1], vw[1]],
        out_specs=pl.BlockSpec((tq, 4 * HEAD_DIM), lambda b, i: (b * nq + i, 0)),
        compiler_params=_cparams(2),
        name="nsa",
    )(q, misc, kvc, ks[0], vs[0], kw[0], vw[0])


LANES = 128
COUNT_UNROLL = 4


def _index_keys(qi, wcol, kit, kpos, qpos, tq):
    rel = jnp.maximum(_dot(qi, kit), 0.0) * wcol
    sc = rel[0:tq] + rel[tq:2 * tq] + rel[2 * tq:3 * tq] + rel[3 * tq:4 * tq]
    sc = jnp.where(sc == 0.0, 0.0, sc)
    sc = jnp.where(kpos <= qpos, sc, NEG)
    bits = lax.bitcast_convert_type(sc, I32)
    return jnp.where(bits < 0, bits ^ 0x7FFFFFFF, bits)


def _index_weights(misc_ref):
    w = misc_ref[:, MISC_WI:MISC_WI + 4] * (IDX_HEADS ** -0.5 * HEAD_DIM ** -0.5)
    return jnp.concatenate([w[:, h:h + 1] for h in range(4)], axis=0)


def _chunk_loop(n_chunks, body, init):
    if isinstance(n_chunks, int):
        carry = init
        for c in range(n_chunks):
            carry = body(c, carry)
        return carry

    def several(g, carry):
        for u in range(COUNT_UNROLL):
            carry = body(g * COUNT_UNROLL + u, carry)
        return carry

    n_groups = n_chunks // COUNT_UNROLL
    carry = lax.fori_loop(0, n_groups, several, init)
    return lax.fori_loop(n_groups * COUNT_UNROLL, n_chunks, body, carry)


def _key_counter(key_ref, n_chunks):
    rows = key_ref.shape[1]

    def count(pred):
        def one(c, acc):
            for j in range(CK // LANES):
                k = key_ref[c, :, j * LANES:(j + 1) * LANES]
                acc = acc + jnp.where(pred(k, c * CK + j * LANES), 1.0, 0.0)
            return acc

        acc = _chunk_loop(n_chunks, one, jnp.zeros((rows, LANES), F32))
        return jnp.sum(acc, axis=-1, keepdims=True)

    return count


def _top_keys(count, n_keep):
    keep = float(n_keep)
    thr = jnp.where(count(lambda k, c: k >= 0) >= keep, 0, INT_MIN).astype(I32)

    def thr_bit(b, thr):
        cand = thr | jnp.left_shift(jnp.int32(1), 30 - b)
        return jnp.where(count(lambda k, c: k >= cand) >= keep, cand, thr)

    thr = lax.fori_loop(0, 31, thr_bit, thr)
    return thr, keep - count(lambda k, c: k > thr)


def _tie_cut(count, thr, need, rows, n_positions):
    n_pos_bits = (n_positions - 1).bit_length()

    def cut_bit(b, cut):
        cand = cut + jnp.left_shift(jnp.int32(1), n_pos_bits - 1 - b)
        n = count(lambda k, first: (k == thr) & (first + lax.broadcasted_iota(I32, (1, LANES), 1) < cand))
        return jnp.where(n < need, cand, cut)

    return lax.fori_loop(0, n_pos_bits, cut_bit, jnp.zeros((rows, 1), I32))


def _heads_to_lanes(o, tq):
    return jnp.concatenate([o[h * tq:(h + 1) * tq] for h in range(4)], axis=-1)


def _dsa_kernel(qd_ref, qi_ref, misc_ref, kdt_ref, vd_ref, kit_ref, o_ref, key_scr, *, tq, n_keep):
    start, last, qpos = _query_positions(tq, 0, 1)
    n_chunks = last // CK + 1
    qd = _heads_as_rows(qd_ref, HEAD_DIM ** -0.5)
    qi = _heads_as_rows(qi_ref, 1.0)
    wcol = _index_weights(misc_ref)

    def index_chunk(c, _):
        key_scr[c] = _index_keys(qi, wcol, kit_ref[c], _key_positions(c), qpos, tq)
        return 0

    lax.fori_loop(0, n_chunks, index_chunk, 0)
    thr, need = _top_keys(_key_counter(key_scr, n_chunks), n_keep)

    tri = jnp.where(lax.broadcasted_iota(I32, (LANES, LANES), 0) <= lax.broadcasted_iota(I32, (LANES, LANES), 1),
                    1.0, 0.0).astype(BF16)

    def attend_chunk(c, carry):
        state, seen = carry
        k = key_scr[c]
        ranks = []
        for j in range(CK // LANES):
            equal = jnp.where(k[:, j * LANES:(j + 1) * LANES] == thr, 1.0, 0.0)
            ranks.append(seen + _dot(equal.astype(BF16), tri))
            seen = seen + jnp.sum(equal, axis=-1, keepdims=True)
        rank = jnp.concatenate(ranks, axis=-1)
        taken = (k > thr) | ((k == thr) & (rank <= need))
        bias = _mask_bias(taken & (_key_positions(c) <= qpos))
        state = _softmax_step(_dot(qd, kdt_ref[c]), bias, _chunk_rows(vd_ref, c), state)
        return state, seen

    state, _ = lax.fori_loop(0, n_chunks, attend_chunk, (_softmax_init(tq), jnp.zeros((tq, 1), F32)))
    o_ref[...] = _heads_to_lanes(_softmax_done(state), tq).astype(o_ref.dtype)


def _dsa(qd, qi, misc, kd, vd, ki, *, nbatch, nq, tq, n_keep, n_chunks):
    nrows = qd.shape[1]
    head_spec = pl.BlockSpec((4, tq, HEAD_DIM), lambda b, i: (0, b * nq + i, 0))
    return pl.pallas_call(
        functools.partial(_dsa_kernel, tq=tq, n_keep=n_keep),
        out_shape=jax.ShapeDtypeStruct((nrows, 4 * HEAD_DIM), BF16),
        grid=(nbatch, nq),
        in_specs=[head_spec, head_spec, pl.BlockSpec((tq, 128), lambda b, i: (b * nq + i, 0)),
                  kd[1], vd[1], ki[1]],
        out_specs=pl.BlockSpec((tq, 4 * HEAD_DIM), lambda b, i: (b * nq + i, 0)),
        scratch_shapes=[pltpu.VMEM((n_chunks, tq, CK), I32)],
        compiler_params=_cparams(2),
        name="dsa",
    )(qd, qi, misc, kd[0], vd[0], ki[0])


def _dsa_step_index_kernel(qi_ref, misc_ref, kit_ref, key_ref, *, tq, pos, n_chunks):
    qi = _heads_as_rows(qi_ref, 1.0)
    wcol = _index_weights(misc_ref)
    for c in range(n_chunks):
        key_ref[c] = _index_keys(qi, wcol, kit_ref[c], _key_positions(c), pos, tq)[0:1, :]


def _dsa_step_top_kernel(key_ref, thr_ref, cut_ref, *, n_keep, n_chunks):
    rows = key_ref.shape[1]
    count = _key_counter(key_ref, n_chunks)
    thr, need = _top_keys(count, n_keep)
    thr_ref[...] = jnp.broadcast_to(thr, thr_ref.shape)
    cut_ref[...] = jnp.broadcast_to(_tie_cut(count, thr, need, rows, n_chunks * CK), cut_ref.shape)


def _dsa_step_attend_kernel(qd_ref, key_ref, thr_ref, cut_ref, kdt_ref, vd_ref, o_ref, *, tq, pos, n_chunks):
    qd = _heads_as_rows(qd_ref, HEAD_DIM ** -0.5)
    thr = thr_ref[:, 0:1]
    cut = cut_ref[:, 0:1]
    carry = _softmax_init(tq)
    for c in range(n_chunks):
        k = key_ref[c]
        kpos = _key_positions(c)
        bias = _mask_bias(((k > thr) | ((k == thr) & (kpos <= cut))) & (kpos <= pos))
        carry = _softmax_step(_dot(qd, kdt_ref[c]), jnp.broadcast_to(bias, (tq, CK)),
                              vd_ref[c * CK:(c + 1) * CK, :], carry)
    o_ref[...] = _heads_to_lanes(_softmax_done(carry), tq).astype(o_ref.dtype)


def _dsa_step(qd, qi, misc, kdt, vd, kit, *, nseq, tq, pos, n_keep, n_chunks):
    head_spec = pl.BlockSpec((4, tq, HEAD_DIM), lambda b: (0, b, 0))
    chunk_spec = pl.BlockSpec((None, n_chunks, HEAD_DIM, CK), lambda b: (b, 0, 0, 0))
    key_spec = pl.BlockSpec((n_chunks, None, 1, CK), lambda b: (0, b, 0, 0))
    row_spec = pl.BlockSpec((None, 1, LANES), lambda b: (b, 0, 0))
    keys = pl.pallas_call(
        functools.partial(_dsa_step_index_kernel, tq=tq, pos=pos, n_chunks=n_chunks),
        out_shape=jax.ShapeDtypeStruct((n_chunks, nseq, 1, CK), I32),
        grid=(nseq,),
        in_specs=[head_spec, pl.BlockSpec((tq, 128), lambda b: (b, 0)), chunk_spec],
        out_specs=key_spec,
        compiler_params=_cparams(1),
        name="dsa_step_index",
    )(qi, misc, kit)
    thr, cut = pl.pallas_call(
        functools.partial(_dsa_step_top_kernel, n_keep=n_keep, n_chunks=n_chunks),
        out_shape=[jax.ShapeDtypeStruct((nseq, LANES), I32)] * 2,
        compiler_params=pltpu.CompilerParams(vmem_limit_bytes=VMEM_LIMIT),
        name="dsa_step_top",
    )(keys.reshape(n_chunks, nseq, CK))
    return pl.pallas_call(
        functools.partial(_dsa_step_attend_kernel, tq=tq, pos=pos, n_chunks=n_chunks),
        out_shape=jax.ShapeDtypeStruct((nseq * tq, 4 * HEAD_DIM), BF16),
        grid=(nseq,),
        in_specs=[head_spec, key_spec, row_spec, row_spec, chunk_spec,
                  pl.BlockSpec((None, n_chunks * CK, HEAD_DIM), lambda b: (b, 0, 0))],
        out_specs=pl.BlockSpec((tq, 4 * HEAD_DIM), lambda b: (b, 0)),
        compiler_params=_cparams(1),
        name="dsa_step_attend",
    )(qd, keys, thr[:, None, :], cut[:, None, :], kdt, vd)


CONV_HALO = 32


def _conv_kernel(cur_ref, halo_ref, w_ref, b_ref, g_ref, beta_ref, o_ref, tail_ref, hx_scr, *, tc):
    i = pl.program_id(1)

    def glu(x):
        return x[:, :CONV_CH] * _sigmoid(x[:, CONV_CH:])

    hx_scr[0:CONV_HALO] = jnp.where(i > 0, glu(halo_ref[...]), 0.0)
    hx_scr[CONV_HALO:CONV_HALO + tc] = glu(cur_ref[...])
    first = CONV_HALO - (CONV_WIDTH - 1)
    y = jnp.broadcast_to(b_ref[...], (tc, CONV_CH))
    for k in range(CONV_WIDTH):
        y = y + hx_scr[first + k:first + k + tc] * w_ref[k:k + 1, :]
    y = _layernorm(y, g_ref[...], beta_ref[...])
    o_ref[...] = (y * _sigmoid(y)).astype(o_ref.dtype)
    tail_ref[...] = hx_scr[tc:tc + CONV_HALO]


def _conv(glu, w, b, g, beta, *, nbatch, seq, tc):
    n = glu.shape[0]
    nt = seq // tc
    per = tc // CONV_HALO
    vec = pl.BlockSpec((1, CONV_CH), lambda bb, i: (0, 0))
    return pl.pallas_call(
        functools.partial(_conv_kernel, tc=tc),
        out_shape=[jax.ShapeDtypeStruct((n, CONV_CH), BF16),
                   jax.ShapeDtypeStruct((nbatch, CONV_HALO, CONV_CH), F32)],
        grid=(nbatch, nt),
        in_specs=[pl.BlockSpec((tc, 2 * CONV_CH), lambda bb, i: (bb * nt + i, 0)),
                  pl.BlockSpec((CONV_HALO, 2 * CONV_CH),
                               lambda bb, i: (jnp.maximum((bb * nt + i) * per - 1, 0), 0)),
                  pl.BlockSpec((CONV_WIDTH, CONV_CH), lambda bb, i: (0, 0)), vec, vec, vec],
        out_specs=[pl.BlockSpec((tc, CONV_CH), lambda bb, i: (bb * nt + i, 0)),
                   pl.BlockSpec((None, CONV_HALO, CONV_CH), lambda bb, i: (bb, 0, 0))],
        scratch_shapes=[pltpu.VMEM((tc + CONV_HALO, CONV_CH), F32)],
        compiler_params=_cparams(2),
        name="conv",
    )(glu, glu, w, b, g, beta)


def _gelu(x):
    return 0.5 * x * (1.0 + lax.erf(x * (2.0 ** -0.5)))


def _sgu_kernel(uv_ref, g_ref, beta_ref, w_ref, bias_ref, o_ref, *, ts):
    a = _gelu(uv_ref[...])
    u = a[:, :SGU_CH]
    vn = _layernorm(a[:, SGU_CH:], g_ref[...], beta_ref[...]).astype(BF16)
    ri = lax.broadcasted_iota(I32, (SGU_CHUNK, SGU_CHUNK), 0)
    ci = lax.broadcasted_iota(I32, (SGU_CHUNK, SGU_CHUNK), 1)
    group = lax.broadcasted_iota(I32, (SGU_CHUNK, SGU_CH), 1) // (SGU_CH // SGU_GROUPS)
    ws = [jnp.where(ci <= ri, w_ref[gi], 0.0).astype(BF16) for gi in range(SGU_GROUPS)]
    for c in range(ts // SGU_CHUNK):
        rows = slice(c * SGU_CHUNK, (c + 1) * SGU_CHUNK)
        mixed = bias_ref[...]
        for gi in range(SGU_GROUPS):
            mixed = mixed + jnp.where(group == gi, _dot(ws[gi], vn[rows]), 0.0)
        o_ref[rows, :] = (u[rows] * mixed).astype(o_ref.dtype)


def _sgu(uv, g, beta, w, bias, *, ts):
    n = uv.shape[0]
    vec = pl.BlockSpec((1, SGU_CH), lambda i: (0, 0))
    return pl.pallas_call(
        functools.partial(_sgu_kernel, ts=ts),
        out_shape=jax.ShapeDtypeStruct((n, SGU_CH), BF16),
        grid=(n // ts,),
        in_specs=[pl.BlockSpec((ts, 2 * SGU_CH), lambda i: (i, 0)), vec, vec,
                  pl.BlockSpec((SGU_GROUPS, SGU_CHUNK, SGU_CHUNK), lambda i: (0, 0, 0)),
                  pl.BlockSpec((SGU_CHUNK, SGU_CH), lambda i: (0, 0))],
        out_specs=pl.BlockSpec((ts, SGU_CH), lambda i: (i, 0)),
        compiler_params=_cparams(1),
        name="sgu",
    )(uv, g, beta, w, bias)


def _step_mix_kernel(glu_ref, uv_ref, hist_ref, cw_ref, cb_ref, cg_ref, cbeta_ref, sg_ref, sbeta_ref,
                     sdiag_ref, sbias_ref, ob_ref, oc_ref, hx_ref, v_ref):
    x = glu_ref[...]
    hx = x[:, :CONV_CH] * _sigmoid(x[:, CONV_CH:])
    hx_ref[...] = hx
    y = cb_ref[...] + hx * cw_ref[CONV_WIDTH - 1:CONV_WIDTH, :]
    for k in range(CONV_WIDTH - 1):
        y = y + hist_ref[k] * cw_ref[k:k + 1, :]
    y = _layernorm(y, cg_ref[...], cbeta_ref[...])
    ob_ref[...] = (y * _sigmoid(y)).astype(ob_ref.dtype)
    a = _gelu(uv_ref[...])
    v = a[:, SGU_CH:]
    v_ref[...] = v
    vn = _layernorm(v, sg_ref[...], sbeta_ref[...])
    oc_ref[...] = (a[:, :SGU_CH] * (sdiag_ref[...] * vn + sbias_ref[...])).astype(oc_ref.dtype)


def _step_mix(glu, uv, hist, cw, cb, cg, cbeta, sg, sbeta, sdiag, sbias):
    n = glu.shape[0]
    sds = jax.ShapeDtypeStruct
    return pl.pallas_call(
        _step_mix_kernel,
        out_shape=[sds((n, CONV_CH), BF16), sds((n, SGU_CH), BF16), sds((n, CONV_CH), F32), sds((n, SGU_CH), F32)],
        compiler_params=pltpu.CompilerParams(vmem_limit_bytes=VMEM_LIMIT),
        name="step_mix",
    )(glu, uv, hist, cw, cb, cg, cbeta, sg, sbeta, sdiag, sbias)


def _merge_kernel(x_ref, g_ref, oa_ref, ob_ref, oc_ref, od_ref, wg_ref, wb_ref, wo_ref, y_ref):
    x = x_ref[...]
    d = x.shape[-1]
    hb = _rms(x, g_ref[...]).astype(BF16)
    acc = jnp.zeros(x.shape, F32)
    for k, o_ref in enumerate((oa_ref, ob_ref, oc_ref, od_ref)):
        gate = _sigmoid(_dot(hb, wg_ref[:, k * d:(k + 1) * d]))
        acc = acc + gate * _dot(o_ref[...], wb_ref[k])
    y_ref[...] = x + _dot(acc.astype(BF16), wo_ref[...])


def _merge(x, g, oa, ob, oc, od, wg, wb, wo, *, tm):
    n, d = x.shape
    row = pl.BlockSpec((tm, d), lambda i: (i, 0))
    br = pl.BlockSpec((tm, BRANCH_WIDTH), lambda i: (i, 0))
    once = pl.Buffered(1)
    return pl.pallas_call(
        _merge_kernel,
        out_shape=jax.ShapeDtypeStruct((n, d), F32),
        grid=(n // tm,),
        in_specs=[row, pl.BlockSpec((1, d), lambda i: (0, 0)), br, br, br, br,
                  pl.BlockSpec((d, N_BRANCH * d), lambda i: (0, 0), pipeline_mode=once),
                  pl.BlockSpec((N_BRANCH, BRANCH_WIDTH, d), lambda i: (0, 0, 0), pipeline_mode=once),
                  pl.BlockSpec((d, d), lambda i: (0, 0), pipeline_mode=once)],
        out_specs=row,
        compiler_params=_cparams(1),
        name="merge",
    )(x, g, oa, ob, oc, od, wg, wb, wo)


def _mlp_kernel(x_ref, g_ref, wu_ref, wd_ref, gf_ref, y_ref, *n_ref):
    x = x_ref[...]
    hb = _rms(x, g_ref[...]).astype(BF16)
    a = jnp.square(jnp.maximum(_dot(hb, wu_ref[...]), 0.0)).astype(BF16)
    y = x + _dot(a, wd_ref[...])
    y_ref[...] = y
    if n_ref:
        n_ref[0][...] = _rms(y, gf_ref[...])


def _mlp(x, g, wu, wd, gf, *, tm, final):
    n, d = x.shape
    row = pl.BlockSpec((tm, d), lambda i: (i, 0))
    vec = pl.BlockSpec((1, d), lambda i: (0, 0))
    once = pl.Buffered(1)
    out_shape = [jax.ShapeDtypeStruct((n, d), F32)] * (2 if final else 1)
    return pl.pallas_call(
        _mlp_kernel,
        out_shape=out_shape,
        grid=(n // tm,),
        in_specs=[row, vec, pl.BlockSpec(wu.shape, lambda i: (0, 0), pipeline_mode=once),
                  pl.BlockSpec(wd.shape, lambda i: (0, 0), pipeline_mode=once), vec],
        out_specs=[row] * (2 if final else 1),
        compiler_params=_cparams(1),
        name="mlp",
    )(x, g, wu, wd, gf)


def _select_rows(width, first):
    r = lax.broadcasted_iota(I32, (HEAD_DIM, width), 0)
    c = lax.broadcasted_iota(I32, (HEAD_DIM, width), 1)
    return jnp.where(c == r + first, 1.0, 0.0).astype(BF16)


def _new_row_chunk(row):
    first = lax.broadcasted_iota(I32, (CK, row.shape[-1]), 0) == 0
    return jnp.where(first, jnp.broadcast_to(row, (CK, row.shape[-1])), 0.0)


def _assemble_kernel(pt_ref, *refs, n_pages, page):
    nsa_pages = refs[:n_pages]
    dsa_pages = refs[n_pages:2 * n_pages]
    win_ref, nsa_new_ref, dsa_new_ref = refs[2 * n_pages:2 * n_pages + 3]
    cmp_ref, kst_ref, vs_ref, kwt_ref, vw_ref, kdt_ref, vd_ref, kit_ref = refs[2 * n_pages + 3:]
    per_chunk = CK // page
    pick_kslc = _select_rows(256, 128)
    pick_kd = _select_rows(192, 0)
    pick_ki = _select_rows(192, 128)
    pick_kwin = _select_rows(128, 0)

    def put_nsa(x, c, lanes, rows):
        kst_ref[c, :, lanes] = _dot_nt(pick_kslc, x.astype(BF16)).astype(BF16)
        vs_ref[rows, :] = x[:, 192:256].astype(BF16)

    def put_dsa(x, c, lanes, rows):
        xb = x.astype(BF16)
        kdt_ref[c, :, lanes] = _dot_nt(pick_kd, xb).astype(BF16)
        kit_ref[c, :, lanes] = _dot_nt(pick_ki, xb).astype(BF16)
        vd_ref[rows, :] = x[:, 64:128].astype(BF16)

    for p in range(n_pages):
        c, r = divmod(p, per_chunk)
        lanes = slice(r * page, (r + 1) * page)
        rows = slice(p * page, (p + 1) * page)
        x = nsa_pages[p][...]
        cmp_ref[rows, :] = x[:, 0:128]
        put_nsa(x, c, lanes, rows)
        put_dsa(dsa_pages[p][...], c, lanes, rows)

    c_new = n_pages // per_chunk
    new_rows = slice(c_new * CK, (c_new + 1) * CK)
    nsa_new = nsa_new_ref[...]
    put_nsa(_new_row_chunk(nsa_new[:, 0:256]), c_new, slice(0, CK), new_rows)
    put_dsa(_new_row_chunk(dsa_new_ref[...]), c_new, slice(0, CK), new_rows)

    def put_win(x, c, rows):
        kwt_ref[c] = _dot_nt(pick_kwin, x.astype(BF16)).astype(BF16)
        vw_ref[rows, :] = x[:, 64:128].astype(BF16)

    put_win(win_ref[...], 0, slice(0, CK))
    put_win(_new_row_chunk(nsa_new[:, 256:384]), 1, slice(CK, 2 * CK))


def _assemble(page_table, cache_nsa, cache_dsa, state_win, nsa_new, dsa_new, *, layer):
    nseq, n_pages = page_table.shape
    page = cache_nsa.shape[2]
    n_chunks = n_pages * page // CK + 1
    keys = n_chunks * CK
    sds = jax.ShapeDtypeStruct

    def page_spec(width, p):
        return pl.BlockSpec((None, None, page, width), lambda b, pt: (pt[b, p], layer, 0, 0))

    def per_seq(*shape):
        return pl.BlockSpec((None,) + shape, lambda b, pt: (b,) + (0,) * len(shape))

    grid_spec = pltpu.PrefetchScalarGridSpec(
        num_scalar_prefetch=1,
        grid=(nseq,),
        in_specs=([page_spec(256, p) for p in range(n_pages)] + [page_spec(192, p) for p in range(n_pages)]
                  + [pl.BlockSpec((None, None, CK, 128), lambda b, pt: (b, layer, 0, 0)),
                     per_seq(1, 384), per_seq(1, 192)]),
        out_specs=[per_seq(n_pages * page, 128), per_seq(n_chunks, HEAD_DIM, CK), per_seq(keys, HEAD_DIM),
                   per_seq(2, HEAD_DIM, CK), per_seq(2 * CK, HEAD_DIM),
                   per_seq(n_chunks, HEAD_DIM, CK), per_seq(keys, HEAD_DIM), per_seq(n_chunks, HEAD_DIM, CK)],
    )
    return pl.pallas_call(
        functools.partial(_assemble_kernel, n_pages=n_pages, page=page),
        out_shape=[sds((nseq, n_pages * page, 128), F32), sds((nseq, n_chunks, HEAD_DIM, CK), BF16),
                   sds((nseq, keys, HEAD_DIM), BF16), sds((nseq, 2, HEAD_DIM, CK), BF16),
                   sds((nseq, 2 * CK, HEAD_DIM), BF16), sds((nseq, n_chunks, HEAD_DIM, CK), BF16),
                   sds((nseq, keys, HEAD_DIM), BF16), sds((nseq, n_chunks, HEAD_DIM, CK), BF16)],
        grid_spec=grid_spec,
        compiler_params=_cparams(1),
        name="assemble",
    )(page_table, *([cache_nsa] * n_pages), *([cache_dsa] * n_pages), state_win, nsa_new, dsa_new)


def _prep_w_in(w_in):
    pts = np.cumsum(SPLIT_WIDTHS)[:-1].tolist()
    qa, kva, ga, glu, uv, qd, kvd, qi, ki, wi = jnp.split(w_in, pts, axis=-1)
    pad = jnp.zeros(w_in.shape[:2] + (128 - 16,), w_in.dtype)
    w = jnp.concatenate([qa, kva, glu, uv, qd, qi, wi, ga, pad, kvd, ki], axis=-1).astype(BF16)
    wt = jnp.concatenate([kva[..., 128:192], kva[..., 256:320], kvd[..., 0:64], ki], axis=-1)
    return w, jnp.swapaxes(wt, 1, 2).astype(BF16)


def _seq_spec(*shape):
    return pl.BlockSpec((None,) + shape, lambda b, i: (b,) + (0,) * len(shape))


def kernel(x_prompt, x_sample, cache_nsa, cache_dsa, state_win, state_conv, page_table, norm1_g, norm2_g,
           final_g, w_in, cmp_pos, cmp_w1, cmp_w2, conv_w, conv_b, conv_ln_g, conv_ln_b, sgu_ln_g, sgu_ln_b,
           sgu_w, sgu_b, w_branch, w_gate, w_out, w_up, w_down):
    nb_, seq, d = x_prompt.shape
    nseq, t_dec, _ = x_sample.shape
    depth = w_in.shape[0]
    n_pool, _, page, _, _ = cache_nsa.shape
    n_pages = page_table.shape[1]
    past = n_pages * page
    w_buf = state_win.shape[2]
    assert t_dec == 1 and seq % CK == 0 and past % CK == 0 and w_buf == CK == NSA_WINDOW
    nbp = seq // NSA_BLOCK
    assert nbp <= NB_PAD and past // NSA_BLOCK < NB_PAD
    n_tok = nb_ * seq
    sr = SAMPLE_ROWS

    w_proj, w_proj_t = _prep_w_in(w_in)
    wg, wb, wo = w_gate.astype(BF16), w_branch.astype(BF16), w_out.astype(BF16)
    wu, wd = w_up.astype(BF16), w_down.astype(BF16)
    def block_diag(a, b):
        z = jnp.zeros_like(a)
        return jnp.concatenate([jnp.concatenate([a, z], axis=-1), jnp.concatenate([z, b], axis=-1)], axis=-2)

    c_pos = jnp.concatenate([cmp_pos[:, 0], cmp_pos[:, 1]], axis=-1)
    c_w1 = cmp_w1.reshape(depth, 2, NSA_BLOCK, HEAD_DIM, HEAD_DIM).astype(BF16)
    c_w1 = block_diag(c_w1[:, 0], c_w1[:, 1])
    c_w2 = block_diag(cmp_w2[:, 0], cmp_w2[:, 1]).astype(BF16)
    sgu_bias = jnp.repeat(jnp.swapaxes(sgu_b, 1, 2), SGU_CH // SGU_GROUPS, axis=2)
    sgu_diag = jnp.repeat(sgu_w[:, :, 0, 0], SGU_CH // SGU_GROUPS, axis=1)[:, None, :]
    sgu_bias0 = sgu_bias[:, 0:1, :]
    vec = lambda a, l: a[l][None, :]
    cache_nsa4 = cache_nsa.reshape(n_pool, depth, page, 4 * HEAD_DIM)
    cache_dsa4 = cache_dsa.reshape(n_pool, depth, page, 3 * HEAD_DIM)
    state_win4 = state_win.reshape(nseq, depth, w_buf, 2 * HEAD_DIM)
    conv_hist = jnp.transpose(state_conv, (1, 2, 0, 3))

    n_chunks_p = seq // CK
    n_chunks_s = past // CK + 1
    nq = seq // Q_BLOCK

    def chunks_of(arr, slot, n_chunks):
        return arr, pl.BlockSpec((n_chunks, HEAD_DIM, CK), lambda b, i: (b, slot, 0))

    def rows_of(arr, slot, n_rows):
        return arr, pl.BlockSpec((None, n_rows, HEAD_DIM), lambda b, i: (slot, b, 0))

    xp = x_prompt.reshape(n_tok, d)
    xs = x_sample.reshape(nseq, d)
    outs = {k: [] for k in ("nsa_p", "nsa_s", "dsa_p", "dsa_s", "win_p", "win_s", "conv_p", "conv_s", "sgu_s")}
    yp = ys = None
    for l in range(depth):
        final = l == depth - 1
        qn, kva, glu, uv, qd, qi, misc, dsa_rows, kt4, v4 = _proj(
            xp, vec(norm1_g, l), w_proj[l], w_proj_t[l], tm=CK, emit_t=True)
        kva3 = kva.reshape(nb_, seq, 6, HEAD_DIM)
        kvc = _compress(kva, c_pos[l], c_w1[l], c_w2[l], rows_per_step=seq)
        kvc = jnp.pad(kvc.reshape(nb_, nbp, 2 * HEAD_DIM), ((0, 0), (0, NB_PAD - nbp), (0, 0)))
        o_a = _nsa(qn, misc, kvc, chunks_of(kt4, 0, n_chunks_p), rows_of(v4, 0, seq),
                   chunks_of(kt4, 1, n_chunks_p), rows_of(v4, 1, seq),
                   nbatch=nb_, nq=nq, tq=Q_BLOCK, pos0=0, pos_stride=1, win_base=0,
                   n_sel=min(NSA_N_SEL, seq // NSA_BLOCK))
        o_d = _dsa(qd, qi, misc, chunks_of(kt4, 2, n_chunks_p), rows_of(v4, 2, seq),
                   chunks_of(kt4, 3, n_chunks_p), nbatch=nb_, nq=nq, tq=Q_BLOCK,
                   n_keep=min(DSA_TOPK, seq // 4), n_chunks=n_chunks_p)
        o_b, conv_tail = _conv(glu, conv_w[l], vec(conv_b, l), vec(conv_ln_g, l), vec(conv_ln_b, l),
                               nbatch=nb_, seq=seq, tc=CK)
        o_c = _sgu(uv, vec(sgu_ln_g, l), vec(sgu_ln_b, l), sgu_w[l], sgu_bias[l], ts=CK)
        xp = _merge(xp, vec(norm1_g, l), o_a, o_b, o_c, o_d, wg[l], wb[l], wo[l], tm=256)
        res = _mlp(xp, vec(norm2_g, l), wu[l], wd[l], final_g[None, :], tm=256, final=final)
        xp = res[0]
        if final:
            yp = res[1]
        outs["nsa_p"].append(kva3[:, :, :4])
        outs["dsa_p"].append(dsa_rows.reshape(nb_, seq, 3, HEAD_DIM))
        outs["win_p"].append(kva3[:, seq - min(NSA_WINDOW, seq):, 4:])
        outs["conv_p"].append(conv_tail[:, CONV_HALO - (CONV_WIDTH - 1):])

        qn, kva, glu, uv, qd, qi, misc, dsa_rows = _proj(
            xs, vec(norm1_g, l), w_proj[l], w_proj_t[l], tm=nseq, emit_t=False)
        cmp_past, kst, vs, kwt, vw, kdt, vdd, kit = _assemble(
            page_table, cache_nsa4, cache_dsa4, state_win4, kva[:, None, :], dsa_rows[:, None, :], layer=l)
        nbs = past // NSA_BLOCK
        kvc = _compress(cmp_past.reshape(nseq * past, 2 * HEAD_DIM), c_pos[l], c_w1[l], c_w2[l],
                        rows_per_step=min(8, nseq) * past)
        kvc = jnp.pad(kvc.reshape(nseq, nbs, 2 * HEAD_DIM), ((0, 0), (0, NB_PAD - nbs), (0, 0)))

        def pad_rows(a):
            a = a[..., :, None, :]
            widths = [(0, 0)] * (a.ndim - 2) + [(0, sr - 1), (0, 0)]
            a = jnp.pad(a, widths)
            return a.reshape(a.shape[:-3] + (nseq * sr, a.shape[-1]))

        misc_r = pad_rows(misc)
        o_a = _nsa(pad_rows(qn), misc_r, kvc,
                   (kst, _seq_spec(n_chunks_s, HEAD_DIM, CK)), (vs, _seq_spec(n_chunks_s * CK, HEAD_DIM)),
                   (kwt, _seq_spec(2, HEAD_DIM, CK)), (vw, _seq_spec(2 * CK, HEAD_DIM)),
                   nbatch=nseq, nq=1, tq=sr, pos0=past, pos_stride=0, win_base=past // CK - 1,
                   n_sel=min(NSA_N_SEL, nbs + 1))
        o_d = _dsa_step(pad_rows(qd), pad_rows(qi), misc_r, kdt, vdd, kit, nseq=nseq, tq=sr, pos=past,
                        n_keep=min(DSA_TOPK, (past + 1) // 4), n_chunks=n_chunks_s)
        o_a = o_a.reshape(nseq, sr, -1)[:, 0]
        o_d = o_d.reshape(nseq, sr, -1)[:, 0]
        o_b, o_c, hx_new, v_rows = _step_mix(
            glu, uv, conv_hist[l], conv_w[l], vec(conv_b, l), vec(conv_ln_g, l), vec(conv_ln_b, l),
            vec(sgu_ln_g, l), vec(sgu_ln_b, l), sgu_diag[l], sgu_bias0[l])
        xs = _merge(xs, vec(norm1_g, l), o_a, o_b, o_c, o_d, wg[l], wb[l], wo[l], tm=nseq)
        res = _mlp(xs, vec(norm2_g, l), wu[l], wd[l], final_g[None, :], tm=nseq, final=final)
        xs = res[0]
        if final:
            ys = res[1]
        kva3 = kva.reshape(nseq, 1, 6, HEAD_DIM)
        outs["nsa_s"].append(kva3[:, :, :4])
        outs["dsa_s"].append(dsa_rows.reshape(nseq, 1, 3, HEAD_DIM))
        outs["win_s"].append(jnp.concatenate([state_win[:, l], kva3[:, :, 4:]], axis=1)[:, 1:])
        outs["conv_s"].append(jnp.concatenate([state_conv[:, l], hx_new[:, None, :]], axis=1)[:, 1:])
        outs["sgu_s"].append(v_rows[:, None, :])

    st = lambda k: jnp.stack(outs[k], axis=1)
    return (yp.reshape(nb_, seq, d), ys.reshape(nseq, 1, d), st("nsa_p"), st("nsa_s"), st("dsa_p"), st("dsa_s"),
            st("win_p"), st("win_s"), st("conv_p"), st("conv_s"), st("sgu_s"))
```

```python
import functools

import numpy as np
import jax
import jax.numpy as jnp
from jax import lax
from jax.experimental import pallas as pl
from jax.experimental.pallas import tpu as pltpu

F32, BF16, I32 = jnp.float32, jnp.bfloat16, jnp.int32

HEAD_DIM = 64
NSA_HEADS = 4
NSA_BLOCK = 64
NSA_N_SEL = 16
NSA_WINDOW = 512
FORCE_BONUS = 1.0e4
CONV_CH = 256
CONV_WIDTH = 31
SGU_CH = 256
SGU_GROUPS = 4
SGU_CHUNK = 128
DSA_HEADS = 4
IDX_HEADS = 4
DSA_TOPK = 256
N_BRANCH = 4
BRANCH_WIDTH = 256
Q_BLOCK = 128
EPS = 1e-6
NEG = -1e30

SPLIT_WIDTHS = (256, 384, 12, 512, 512, 256, 128, 256, 64, 4)

CK = 512
NB_PAD = 128
SAMPLE_ROWS = 16
VMEM_LIMIT = 56 * 1024 * 1024
INT_MIN = -2 ** 31

W_QA, W_KVA, W_GLU, W_UV, W_QD, W_QI, W_MISC, W_DSA = 0, 256, 640, 1152, 1664, 1920, 2176, 2304
W_TOTAL = 2496
MISC_WI, MISC_GA = 0, 4


def _cparams(n_axes):
    return pltpu.CompilerParams(dimension_semantics=("arbitrary",) * n_axes,
                                vmem_limit_bytes=VMEM_LIMIT)


def _dot(a, b):
    return jnp.dot(a, b, preferred_element_type=F32)


def _dot_nt(a, b):
    return lax.dot_general(a, b, (((1,), (1,)), ((), ())), preferred_element_type=F32)


def _rms(x, g):
    return x * lax.rsqrt(jnp.mean(x * x, axis=-1, keepdims=True) + EPS) * g


def _layernorm(x, g, b):
    mu = jnp.mean(x, axis=-1, keepdims=True)
    xc = x - mu
    var = jnp.mean(xc * xc, axis=-1, keepdims=True)
    return xc * lax.rsqrt(var + EPS) * g + b


def _sigmoid(x):
    return 1.0 / (1.0 + jnp.exp(-x))


def _tile4(a):
    return jnp.concatenate([a, a, a, a], axis=0)


def _proj_kernel(x_ref, g_ref, w_ref, wt_ref, qn_ref, kva_ref, glu_ref, uv_ref, qd_ref, qi_ref,
                 misc_ref, dsa_ref, *t_refs):
    hb = _rms(x_ref[...], g_ref[...]).astype(BF16)

    def slab(off, width):
        return _dot(hb, w_ref[:, off:off + width])

    qa = slab(W_QA, 256)
    qd = slab(W_QD, 256)
    qi = slab(W_QI, 256)
    for h in range(4):
        cols = slice(h * HEAD_DIM, (h + 1) * HEAD_DIM)
        qn_ref[h] = qa[:, cols].astype(BF16)
        qd_ref[h] = qd[:, cols].astype(BF16)
        qi_ref[h] = qi[:, cols].astype(BF16)
    kv = slab(W_KVA, 384)
    kva_ref[...] = kv
    glu_ref[...] = slab(W_GLU, 512)
    uv_ref[...] = slab(W_UV, 512)
    misc_ref[...] = slab(W_MISC, 128)
    dr = slab(W_DSA, 192)
    dsa_ref[...] = dr
    if t_refs:
        kt_ref, v_ref = t_refs
        kt_ref[0] = _dot_nt(wt_ref[...], hb).astype(BF16)
        v_ref[0] = kv[:, 192:256].astype(BF16)
        v_ref[1] = kv[:, 320:384].astype(BF16)
        v_ref[2] = dr[:, 64:128].astype(BF16)


def _proj(x, g, w, wt, *, tm, emit_t):
    n, d = x.shape
    sds = jax.ShapeDtypeStruct
    out_shape = [sds((4, n, HEAD_DIM), BF16), sds((n, 384), F32), sds((n, 512), F32), sds((n, 512), F32),
                 sds((4, n, HEAD_DIM), BF16), sds((4, n, HEAD_DIM), BF16), sds((n, 128), F32), sds((n, 192), F32)]
    head_spec = pl.BlockSpec((4, tm, HEAD_DIM), lambda i: (0, i, 0))

    def row_spec(width):
        return pl.BlockSpec((tm, width), lambda i: (i, 0))

    out_specs = [head_spec, row_spec(384), row_spec(512), row_spec(512), head_spec, head_spec,
                 row_spec(128), row_spec(192)]
    if emit_t:
        assert tm == CK
        out_shape += [sds((n // CK, 256, CK), BF16), sds((3, n, HEAD_DIM), BF16)]
        out_specs += [pl.BlockSpec((1, 256, CK), lambda i: (i, 0, 0)),
                      pl.BlockSpec((3, tm, HEAD_DIM), lambda i: (0, i, 0))]
    return pl.pallas_call(
        _proj_kernel,
        out_shape=out_shape,
        grid=(n // tm,),
        in_specs=[row_spec(d), pl.BlockSpec((1, d), lambda i: (0, 0)),
                  pl.BlockSpec((d, W_TOTAL), lambda i: (0, 0)),
                  pl.BlockSpec((256, d), lambda i: (0, 0))],
        out_specs=out_specs,
        compiler_params=_cparams(1),
        name="proj",
    )(x, g, w, wt)


def _compress_kernel(x_ref, pos_ref, w1_ref, w2_ref, o_ref):
    n_blocks = o_ref.shape[0]
    acc = jnp.zeros((n_blocks, 2 * HEAD_DIM), F32)
    for r in range(NSA_BLOCK):
        xr = x_ref[pl.ds(r, n_blocks, stride=NSA_BLOCK), :] + pos_ref[r:r + 1, :]
        acc = acc + _dot(xr.astype(BF16), w1_ref[r])
    a = acc * _sigmoid(acc)
    o_ref[...] = _dot(a.astype(BF16), w2_ref[...]).astype(o_ref.dtype)


def _compress(x, pos, w1, w2, *, rows_per_step):
    r = x.shape[0]
    width = 2 * HEAD_DIM
    return pl.pallas_call(
        _compress_kernel,
        out_shape=jax.ShapeDtypeStruct((r // NSA_BLOCK, width), BF16),
        grid=(r // rows_per_step,),
        in_specs=[pl.BlockSpec((rows_per_step, width), lambda i: (i, 0)),
                  pl.BlockSpec((NSA_BLOCK, width), lambda i: (0, 0)),
                  pl.BlockSpec((NSA_BLOCK, width, width), lambda i: (0, 0, 0)),
                  pl.BlockSpec((width, width), lambda i: (0, 0))],
        out_specs=pl.BlockSpec((rows_per_step // NSA_BLOCK, width), lambda i: (i, 0)),
        compiler_params=_cparams(1),
        name="compress",
    )(x, pos, w1, w2)


def _masked_probs(s, mask):
    s = jnp.where(mask, s, NEG)
    m = jnp.max(s, axis=-1, keepdims=True)
    p = jnp.where(mask, jnp.exp(s - m), 0.0)
    return p / jnp.maximum(jnp.sum(p, axis=-1, keepdims=True), 1e-30)


def _softmax_step(s, bias, v, carry, v_t=False):
    m, l, acc = carry
    tq = bias.shape[0]
    s = s.reshape(4, tq, CK) + bias[None]
    m_new = jnp.maximum(m, jnp.max(s, axis=-1, keepdims=True))
    alpha = jnp.exp(m - m_new)
    p = jnp.exp(s - m_new)
    l = alpha * l + jnp.sum(p, axis=-1, keepdims=True)
    pb = p.reshape(4 * tq, CK).astype(BF16)
    acc = alpha.reshape(4 * tq, 1) * acc + (_dot_nt(pb, v) if v_t else _dot(pb, v))
    return m_new, l, acc


def _softmax_init(tq):
    return (jnp.full((4, tq, 1), NEG, F32), jnp.zeros((4, tq, 1), F32), jnp.zeros((4 * tq, HEAD_DIM), F32))


def _softmax_done(carry):
    _, l, acc = carry
    return acc / jnp.maximum(l.reshape(acc.shape[0], 1), 1e-30)


def _mask_bias(allowed):
    return jnp.where(allowed, 0.0, NEG)


def _query_positions(tq, pos0, pos_stride):
    if pos_stride == 0:
        return pos0, pos0, jnp.full((tq, 1), pos0, I32)
    i = pl.program_id(1)
    start = pos0 + i * (tq * pos_stride)
    last = start + (tq - 1) * pos_stride
    t = lax.broadcasted_iota(I32, (tq, 1), 0)
    qpos = start + t * pos_stride
    return start, last, qpos


def _heads_as_rows(ref, scale):
    return jnp.concatenate([ref[h] for h in range(4)], axis=0) * scale


def _key_positions(c):
    return c * CK + lax.broadcasted_iota(I32, (1, CK), 1)


def _chunk_rows(ref, c):
    if isinstance(c, int):
        return ref[c * CK:(c + 1) * CK, :]
    return ref[pl.ds(pl.multiple_of(c * CK, CK), CK), :]


PICK_LANES = 128


def _nsa_kernel(q_ref, misc_ref, kvc_ref, kst_ref, vs_ref, kwt_ref, vw_ref, o_ref, *,
                tq, pos0, pos_stride, win_base, n_sel, v_t):
    values = (lambda ref, c: ref[c]) if v_t else _chunk_rows
    nb = kvc_ref.shape[0]
    start, last, qpos = _query_positions(tq, pos0, pos_stride)
    qpos4 = _tile4(qpos)
    q = _heads_as_rows(q_ref, HEAD_DIM ** -0.5)

    jb = lax.broadcasted_iota(I32, (1, nb), 1)
    mask_c = ((jb + 1) * NSA_BLOCK - 1) <= qpos4
    p_c = _masked_probs(_dot_nt(q, kvc_ref[:, 0:HEAD_DIM]), mask_c)
    o_c = _dot(p_c.astype(BF16), kvc_ref[:, HEAD_DIM:2 * HEAD_DIM])
    imp = p_c[0:tq] + p_c[tq:2 * tq] + p_c[2 * tq:3 * tq] + p_c[3 * tq:4 * tq]

    cur = qpos // NSA_BLOCK
    forced = (jb == 0) | (jb == cur) | (jb == cur - 1)
    score = jnp.where(jb <= cur, imp, -1.0) + jnp.where(forced, FORCE_BONUS, 0.0)
    if tq < PICK_LANES:
        score = jnp.concatenate([score, jnp.zeros((PICK_LANES - tq, nb), F32)], axis=0)
    jcol = lax.broadcasted_iota(I32, (nb, PICK_LANES), 0).astype(F32)

    def pick(_, carry):
        sc, sel = carry
        m = jnp.max(sc, axis=0, keepdims=True)
        first = jnp.min(jnp.where(sc == m, jcol, 1e9), axis=0, keepdims=True)
        hit = jcol == first
        return jnp.where(hit, -3e38, sc), jnp.where(hit, 1.0, sel)

    _, sel_t = lax.fori_loop(0, n_sel, pick, (score.T, jnp.zeros((nb, PICK_LANES), F32)))
    sel = sel_t.T[0:tq].astype(BF16)

    jrow = lax.broadcasted_iota(I32, (nb, CK), 0)
    kcol = lax.broadcasted_iota(I32, (nb, CK), 1)

    def sel_chunk(c, carry):
        expand = jnp.where(jrow == ((c * CK + kcol) // NSA_BLOCK), 1.0, 0.0).astype(BF16)
        bias = _mask_bias((_dot(sel, expand) > 0.5) & (_key_positions(c) <= qpos))
        return _softmax_step(_dot(q, kst_ref[c]), bias, values(vs_ref, c), carry, v_t)

    n_chunks = last // CK + 1
    if isinstance(n_chunks, int):
        carry = _softmax_init(tq)
        for c in range(n_chunks):
            carry = sel_chunk(c, carry)
    else:
        carry = lax.fori_loop(0, n_chunks, sel_chunk, _softmax_init(tq))
    o_s = _softmax_done(carry)

    carry = _softmax_init(tq)
    c1 = start // CK
    for back in (1, 0):
        c = c1 - back
        cc = max(c - win_base, 0) if isinstance(c, int) else jnp.maximum(c - win_base, 0)
        kpos = _key_positions(c)
        dpos = qpos - kpos
        bias = _mask_bias((dpos >= 0) & (dpos < NSA_WINDOW) & (kpos >= 0))
        carry = _softmax_step(_dot(q, kwt_ref[cc]), bias, values(vw_ref, cc), carry, v_t)
    o_w = _softmax_done(carry)

    g = _sigmoid(misc_ref[...])
    outs = []
    for h in range(4):
        rows = slice(h * tq, (h + 1) * tq)
        col = MISC_GA + 3 * h
        outs.append(g[:, col:col + 1] * o_c[rows] + g[:, col + 1:col + 2] * o_s[rows]
                    + g[:, col + 2:col + 3] * o_w[rows])
    o_ref[...] = jnp.concatenate(outs, axis=-1).astype(o_ref.dtype)


def _nsa(q, misc, kvc, ks, vs, kw, vw, *, nbatch, nq, tq, pos0, pos_stride, win_base, n_sel, v_t):
    nrows = q.shape[1]
    kern = functools.partial(_nsa_kernel, tq=tq, pos0=pos0, pos_stride=pos_stride, win_base=win_base,
                             n_sel=n_sel, v_t=v_t)
    return pl.pallas_call(
        kern,
        out_shape=jax.ShapeDtypeStruct((nrows, 4 * HEAD_DIM), BF16),
        grid=(nbatch, nq),
        in_specs=[pl.BlockSpec((4, tq, HEAD_DIM), lambda b, i: (0, b * nq + i, 0)),
                  pl.BlockSpec((tq, 128), lambda b, i: (b * nq + i, 0)),
                  pl.BlockSpec((None, NB_PAD, 2 * HEAD_DIM), lambda b, i: (b, 0, 0)),
                  ks[1], vs[1], kw[1], vw[1]],
        out_specs=pl.BlockSpec((tq, 4 * HEAD_DIM), lambda b, i: (b * nq + i, 0)),
        compiler_params=_cparams(2),
        name="nsa",
    )(q, misc, kvc, ks[0], vs[0], kw[0], vw[0])


LANES = 128
COUNT_UNROLL = 4


def _index_keys(qi, wcol, kit, kpos, qpos, tq):
    rel = jnp.maximum(_dot(qi, kit), 0.0) * wcol
    sc = rel[0:tq] + rel[tq:2 * tq] + rel[2 * tq:3 * tq] + rel[3 * tq:4 * tq]
    sc = jnp.where(sc == 0.0, 0.0, sc)
    sc = jnp.where(kpos <= qpos, sc, NEG)
    bits = lax.bitcast_convert_type(sc, I32)
    return jnp.where(bits < 0, bits ^ 0x7FFFFFFF, bits)


def _index_weights(misc_ref):
    w = misc_ref[:, MISC_WI:MISC_WI + 4] * (IDX_HEADS ** -0.5 * HEAD_DIM ** -0.5)
    return jnp.concatenate([w[:, h:h + 1] for h in range(4)], axis=0)


def _chunk_loop(n_chunks, body, init):
    if isinstance(n_chunks, int):
        carry = init
        for c in range(n_chunks):
            carry = body(c, carry)
        return carry

    def several(g, carry):
        for u in range(COUNT_UNROLL):
            carry = body(g * COUNT_UNROLL + u, carry)
        return carry

    n_groups = n_chunks // COUNT_UNROLL
    carry = lax.fori_loop(0, n_groups, several, init)
    return lax.fori_loop(n_groups * COUNT_UNROLL, n_chunks, body, carry)


def _key_counter(key_ref, n_chunks):
    rows = key_ref.shape[1]

    def count(pred):
        def one(c, acc):
            for j in range(CK // LANES):
                k = key_ref[c, :, j * LANES:(j + 1) * LANES]
                acc = acc + jnp.where(pred(k, c * CK + j * LANES), 1.0, 0.0)
            return acc

        acc = _chunk_loop(n_chunks, one, jnp.zeros((rows, LANES), F32))
        return jnp.sum(acc, axis=-1, keepdims=True)

    return count


def _top_keys(count, n_keep):
    keep = float(n_keep)
    thr = jnp.where(count(lambda k, c: k >= 0) >= keep, 0, INT_MIN).astype(I32)

    def thr_bit(b, thr):
        cand = thr | jnp.left_shift(jnp.int32(1), 30 - b)
        return jnp.where(count(lambda k, c: k >= cand) >= keep, cand, thr)

    thr = lax.fori_loop(0, 31, thr_bit, thr)
    return thr, keep - count(lambda k, c: k > thr)


def _tie_cut(count, thr, need, rows, n_positions):
    n_pos_bits = (n_positions - 1).bit_length()

    def cut_bit(b, cut):
        cand = cut + jnp.left_shift(jnp.int32(1), n_pos_bits - 1 - b)
        n = count(lambda k, first: (k == thr) & (first + lax.broadcasted_iota(I32, (1, LANES), 1) < cand))
        return jnp.where(n < need, cand, cut)

    return lax.fori_loop(0, n_pos_bits, cut_bit, jnp.zeros((rows, 1), I32))


def _heads_to_lanes(o, tq):
    return jnp.concatenate([o[h * tq:(h + 1) * tq] for h in range(4)], axis=-1)


def _dsa_kernel(qd_ref, qi_ref, misc_ref, kdt_ref, vd_ref, kit_ref, o_ref, key_scr, *, tq, n_keep):
    start, last, qpos = _query_positions(tq, 0, 1)
    n_chunks = last // CK + 1
    qd = _heads_as_rows(qd_ref, HEAD_DIM ** -0.5)
    qi = _heads_as_rows(qi_ref, 1.0)
    wcol = _index_weights(misc_ref)

    def index_chunk(c, _):
        key_scr[c] = _index_keys(qi, wcol, kit_ref[c], _key_positions(c), qpos, tq)
        return 0

    lax.fori_loop(0, n_chunks, index_chunk, 0)
    thr, need = _top_keys(_key_counter(key_scr, n_chunks), n_keep)

    tri = jnp.where(lax.broadcasted_iota(I32, (LANES, LANES), 0) <= lax.broadcasted_iota(I32, (LANES, LANES), 1),
                    1.0, 0.0).astype(BF16)

    def attend_chunk(c, carry):
        state, seen = carry
        k = key_scr[c]
        ranks = []
        for j in range(CK // LANES):
            equal = jnp.where(k[:, j * LANES:(j + 1) * LANES] == thr, 1.0, 0.0)
            ranks.append(seen + _dot(equal.astype(BF16), tri))
            seen = seen + jnp.sum(equal, axis=-1, keepdims=True)
        rank = jnp.concatenate(ranks, axis=-1)
        taken = (k > thr) | ((k == thr) & (rank <= need))
        bias = _mask_bias(taken & (_key_positions(c) <= qpos))
        state = _softmax_step(_dot(qd, kdt_ref[c]), bias, _chunk_rows(vd_ref, c), state)
        return state, seen

    state, _ = lax.fori_loop(0, n_chunks, attend_chunk, (_softmax_init(tq), jnp.zeros((tq, 1), F32)))
    o_ref[...] = _heads_to_lanes(_softmax_done(state), tq).astype(o_ref.dtype)


def _dsa(qd, qi, misc, kd, vd, ki, *, nbatch, nq, tq, n_keep, n_chunks):
    nrows = qd.shape[1]
    head_spec = pl.BlockSpec((4, tq, HEAD_DIM), lambda b, i: (0, b * nq + i, 0))
    return pl.pallas_call(
        functools.partial(_dsa_kernel, tq=tq, n_keep=n_keep),
        out_shape=jax.ShapeDtypeStruct((nrows, 4 * HEAD_DIM), BF16),
        grid=(nbatch, nq),
        in_specs=[head_spec, head_spec, pl.BlockSpec((tq, 128), lambda b, i: (b * nq + i, 0)),
                  kd[1], vd[1], ki[1]],
        out_specs=pl.BlockSpec((tq, 4 * HEAD_DIM), lambda b, i: (b * nq + i, 0)),
        scratch_shapes=[pltpu.VMEM((n_chunks, tq, CK), I32)],
        compiler_params=_cparams(2),
        name="dsa",
    )(qd, qi, misc, kd[0], vd[0], ki[0])


def _dsa_step_index_kernel(qi_ref, misc_ref, kit_ref, key_ref, *, tq, pos, n_chunks):
    qi = _heads_as_rows(qi_ref, 1.0)
    wcol = _index_weights(misc_ref)
    for c in range(n_chunks):
        key_ref[c] = _index_keys(qi, wcol, kit_ref[c], _key_positions(c), pos, tq)[0:1, :]


def _dsa_step_top_kernel(key_ref, thr_ref, cut_ref, *, n_keep, n_chunks):
    rows = key_ref.shape[1]
    count = _key_counter(key_ref, n_chunks)
    thr, need = _top_keys(count, n_keep)
    thr_ref[...] = jnp.broadcast_to(thr, thr_ref.shape)
    cut_ref[...] = jnp.broadcast_to(_tie_cut(count, thr, need, rows, n_chunks * CK), cut_ref.shape)


def _dsa_step_attend_kernel(qd_ref, key_ref, thr_ref, cut_ref, kdt_ref, vdt_ref, o_ref, *, tq, pos, n_chunks):
    qd = _heads_as_rows(qd_ref, HEAD_DIM ** -0.5)
    thr = thr_ref[:, 0:1]
    cut = cut_ref[:, 0:1]
    carry = _softmax_init(tq)
    for c in range(n_chunks):
        k = key_ref[c]
        kpos = _key_positions(c)
        bias = _mask_bias(((k > thr) | ((k == thr) & (kpos <= cut))) & (kpos <= pos))
        carry = _softmax_step(_dot(qd, kdt_ref[c]), jnp.broadcast_to(bias, (tq, CK)), vdt_ref[c], carry, True)
    o_ref[...] = _heads_to_lanes(_softmax_done(carry), tq).astype(o_ref.dtype)


def _dsa_step(qd, qi, misc, kdt, vd, kit, *, nseq, tq, pos, n_keep, n_chunks):
    head_spec = pl.BlockSpec((4, tq, HEAD_DIM), lambda b: (0, b, 0))
    chunk_spec = pl.BlockSpec((None, n_chunks, HEAD_DIM, CK), lambda b: (b, 0, 0, 0))
    key_spec = pl.BlockSpec((n_chunks, None, 1, CK), lambda b: (0, b, 0, 0))
    row_spec = pl.BlockSpec((None, 1, LANES), lambda b: (b, 0, 0))
    keys = pl.pallas_call(
        functools.partial(_dsa_step_index_kernel, tq=tq, pos=pos, n_chunks=n_chunks),
        out_shape=jax.ShapeDtypeStruct((n_chunks, nseq, 1, CK), I32),
        grid=(nseq,),
        in_specs=[head_spec, pl.BlockSpec((tq, 128), lambda b: (b, 0)), chunk_spec],
        out_specs=key_spec,
        compiler_params=_cparams(1),
        name="dsa_step_index",
    )(qi, misc, kit)
    thr, cut = pl.pallas_call(
        functools.partial(_dsa_step_top_kernel, n_keep=n_keep, n_chunks=n_chunks),
        out_shape=[jax.ShapeDtypeStruct((nseq, LANES), I32)] * 2,
        compiler_params=pltpu.CompilerParams(vmem_limit_bytes=VMEM_LIMIT),
        name="dsa_step_top",
    )(keys.reshape(n_chunks, nseq, CK))
    return pl.pallas_call(
        functools.partial(_dsa_step_attend_kernel, tq=tq, pos=pos, n_chunks=n_chunks),
        out_shape=jax.ShapeDtypeStruct((nseq * tq, 4 * HEAD_DIM), BF16),
        grid=(nseq,),
        in_specs=[head_spec, key_spec, row_spec, row_spec, chunk_spec, chunk_spec],
        out_specs=pl.BlockSpec((tq, 4 * HEAD_DIM), lambda b: (b, 0)),
        compiler_params=_cparams(1),
        name="dsa_step_attend",
    )(qd, keys, thr[:, None, :], cut[:, None, :], kdt, vd)


CONV_HALO = 32


def _conv_kernel(cur_ref, halo_ref, w_ref, b_ref, g_ref, beta_ref, o_ref, tail_ref, hx_scr, *, tc):
    i = pl.program_id(1)

    def glu(x):
        return x[:, :CONV_CH] * _sigmoid(x[:, CONV_CH:])

    hx_scr[0:CONV_HALO] = jnp.where(i > 0, glu(halo_ref[...]), 0.0)
    hx_scr[CONV_HALO:CONV_HALO + tc] = glu(cur_ref[...])
    first = CONV_HALO - (CONV_WIDTH - 1)
    y = jnp.broadcast_to(b_ref[...], (tc, CONV_CH))
    for k in range(CONV_WIDTH):
        y = y + hx_scr[first + k:first + k + tc] * w_ref[k:k + 1, :]
    y = _layernorm(y, g_ref[...], beta_ref[...])
    o_ref[...] = (y * _sigmoid(y)).astype(o_ref.dtype)
    tail_ref[...] = hx_scr[tc:tc + CONV_HALO]


def _conv(glu, w, b, g, beta, *, nbatch, seq, tc):
    n = glu.shape[0]
    nt = seq // tc
    per = tc // CONV_HALO
    vec = pl.BlockSpec((1, CONV_CH), lambda bb, i: (0, 0))
    return pl.pallas_call(
        functools.partial(_conv_kernel, tc=tc),
        out_shape=[jax.ShapeDtypeStruct((n, CONV_CH), BF16),
                   jax.ShapeDtypeStruct((nbatch, CONV_HALO, CONV_CH), F32)],
        grid=(nbatch, nt),
        in_specs=[pl.BlockSpec((tc, 2 * CONV_CH), lambda bb, i: (bb * nt + i, 0)),
                  pl.BlockSpec((CONV_HALO, 2 * CONV_CH),
                               lambda bb, i: (jnp.maximum((bb * nt + i) * per - 1, 0), 0)),
                  pl.BlockSpec((CONV_WIDTH, CONV_CH), lambda bb, i: (0, 0)), vec, vec, vec],
        out_specs=[pl.BlockSpec((tc, CONV_CH), lambda bb, i: (bb * nt + i, 0)),
                   pl.BlockSpec((None, CONV_HALO, CONV_CH), lambda bb, i: (bb, 0, 0))],
        scratch_shapes=[pltpu.VMEM((tc + CONV_HALO, CONV_CH), F32)],
        compiler_params=_cparams(2),
        name="conv",
    )(glu, glu, w, b, g, beta)


def _gelu(x):
    return 0.5 * x * (1.0 + lax.erf(x * (2.0 ** -0.5)))


def _sgu_kernel(uv_ref, g_ref, beta_ref, w_ref, bias_ref, o_ref, *, ts):
    a = _gelu(uv_ref[...])
    u = a[:, :SGU_CH]
    vn = _layernorm(a[:, SGU_CH:], g_ref[...], beta_ref[...]).astype(BF16)
    ri = lax.broadcasted_iota(I32, (SGU_CHUNK, SGU_CHUNK), 0)
    ci = lax.broadcasted_iota(I32, (SGU_CHUNK, SGU_CHUNK), 1)
    group = lax.broadcasted_iota(I32, (SGU_CHUNK, SGU_CH), 1) // (SGU_CH // SGU_GROUPS)
    ws = [jnp.where(ci <= ri, w_ref[gi], 0.0).astype(BF16) for gi in range(SGU_GROUPS)]
    for c in range(ts // SGU_CHUNK):
        rows = slice(c * SGU_CHUNK, (c + 1) * SGU_CHUNK)
        mixed = bias_ref[...]
        for gi in range(SGU_GROUPS):
            mixed = mixed + jnp.where(group == gi, _dot(ws[gi], vn[rows]), 0.0)
        o_ref[rows, :] = (u[rows] * mixed).astype(o_ref.dtype)


def _sgu(uv, g, beta, w, bias, *, ts):
    n = uv.shape[0]
    vec = pl.BlockSpec((1, SGU_CH), lambda i: (0, 0))
    return pl.pallas_call(
        functools.partial(_sgu_kernel, ts=ts),
        out_shape=jax.ShapeDtypeStruct((n, SGU_CH), BF16),
        grid=(n // ts,),
        in_specs=[pl.BlockSpec((ts, 2 * SGU_CH), lambda i: (i, 0)), vec, vec,
                  pl.BlockSpec((SGU_GROUPS, SGU_CHUNK, SGU_CHUNK), lambda i: (0, 0, 0)),
                  pl.BlockSpec((SGU_CHUNK, SGU_CH), lambda i: (0, 0))],
        out_specs=pl.BlockSpec((ts, SGU_CH), lambda i: (i, 0)),
        compiler_params=_cparams(1),
        name="sgu",
    )(uv, g, beta, w, bias)


def _step_mix_kernel(glu_ref, uv_ref, hist_ref, cw_ref, cb_ref, cg_ref, cbeta_ref, sg_ref, sbeta_ref,
                     sdiag_ref, sbias_ref, ob_ref, oc_ref, hx_ref, v_ref):
    x = glu_ref[...]
    hx = x[:, :CONV_CH] * _sigmoid(x[:, CONV_CH:])
    hx_ref[...] = hx
    y = cb_ref[...] + hx * cw_ref[CONV_WIDTH - 1:CONV_WIDTH, :]
    for k in range(CONV_WIDTH - 1):
        y = y + hist_ref[k] * cw_ref[k:k + 1, :]
    y = _layernorm(y, cg_ref[...], cbeta_ref[...])
    ob_ref[...] = (y * _sigmoid(y)).astype(ob_ref.dtype)
    a = _gelu(uv_ref[...])
    v = a[:, SGU_CH:]
    v_ref[...] = v
    vn = _layernorm(v, sg_ref[...], sbeta_ref[...])
    oc_ref[...] = (a[:, :SGU_CH] * (sdiag_ref[...] * vn + sbias_ref[...])).astype(oc_ref.dtype)


def _step_mix(glu, uv, hist, cw, cb, cg, cbeta, sg, sbeta, sdiag, sbias):
    n = glu.shape[0]
    sds = jax.ShapeDtypeStruct
    return pl.pallas_call(
        _step_mix_kernel,
        out_shape=[sds((n, CONV_CH), BF16), sds((n, SGU_CH), BF16), sds((n, CONV_CH), F32), sds((n, SGU_CH), F32)],
        compiler_params=pltpu.CompilerParams(vmem_limit_bytes=VMEM_LIMIT),
        name="step_mix",
    )(glu, uv, hist, cw, cb, cg, cbeta, sg, sbeta, sdiag, sbias)


def _merge_kernel(x_ref, g_ref, oa_ref, ob_ref, oc_ref, od_ref, wg_ref, wb_ref, wo_ref, y_ref):
    x = x_ref[...]
    d = x.shape[-1]
    hb = _rms(x, g_ref[...]).astype(BF16)
    acc = jnp.zeros(x.shape, F32)
    for k, o_ref in enumerate((oa_ref, ob_ref, oc_ref, od_ref)):
        gate = _sigmoid(_dot(hb, wg_ref[:, k * d:(k + 1) * d]))
        acc = acc + gate * _dot(o_ref[...], wb_ref[k])
    y_ref[...] = x + _dot(acc.astype(BF16), wo_ref[...])


def _merge(x, g, oa, ob, oc, od, wg, wb, wo, *, tm):
    n, d = x.shape
    row = pl.BlockSpec((tm, d), lambda i: (i, 0))
    br = pl.BlockSpec((tm, BRANCH_WIDTH), lambda i: (i, 0))
    once = pl.Buffered(1)
    return pl.pallas_call(
        _merge_kernel,
        out_shape=jax.ShapeDtypeStruct((n, d), F32),
        grid=(n // tm,),
        in_specs=[row, pl.BlockSpec((1, d), lambda i: (0, 0)), br, br, br, br,
                  pl.BlockSpec((d, N_BRANCH * d), lambda i: (0, 0), pipeline_mode=once),
                  pl.BlockSpec((N_BRANCH, BRANCH_WIDTH, d), lambda i: (0, 0, 0), pipeline_mode=once),
                  pl.BlockSpec((d, d), lambda i: (0, 0), pipeline_mode=once)],
        out_specs=row,
        compiler_params=_cparams(1),
        name="merge",
    )(x, g, oa, ob, oc, od, wg, wb, wo)


def _mlp_kernel(x_ref, g_ref, wu_ref, wd_ref, gf_ref, y_ref, *n_ref):
    x = x_ref[...]
    hb = _rms(x, g_ref[...]).astype(BF16)
    a = jnp.square(jnp.maximum(_dot(hb, wu_ref[...]), 0.0)).astype(BF16)
    y = x + _dot(a, wd_ref[...])
    y_ref[...] = y
    if n_ref:
        n_ref[0][...] = _rms(y, gf_ref[...])


def _mlp(x, g, wu, wd, gf, *, tm, final):
    n, d = x.shape
    row = pl.BlockSpec((tm, d), lambda i: (i, 0))
    vec = pl.BlockSpec((1, d), lambda i: (0, 0))
    once = pl.Buffered(1)
    out_shape = [jax.ShapeDtypeStruct((n, d), F32)] * (2 if final else 1)
    return pl.pallas_call(
        _mlp_kernel,
        out_shape=out_shape,
        grid=(n // tm,),
        in_specs=[row, vec, pl.BlockSpec(wu.shape, lambda i: (0, 0), pipeline_mode=once),
                  pl.BlockSpec(wd.shape, lambda i: (0, 0), pipeline_mode=once), vec],
        out_specs=[row] * (2 if final else 1),
        compiler_params=_cparams(1),
        name="mlp",
    )(x, g, wu, wd, gf)


def _assemble_kernel(pt_ref, *refs, n_pages, page):
    nsa_pages = refs[:n_pages]
    dsa_pages = refs[n_pages:2 * n_pages]
    win_ref, nsa_new_ref, dsa_new_ref = refs[2 * n_pages:2 * n_pages + 3]
    cmp_ref, kst_ref, vst_ref, kwt_ref, vwt_ref, kdt_ref, vdt_ref, kit_ref = refs[2 * n_pages + 3:]
    per_chunk = CK // page
    for p in range(n_pages):
        c, r = divmod(p, per_chunk)
        lanes = slice(r * page, (r + 1) * page)
        x = nsa_pages[p]
        cmp_ref[p * page:(p + 1) * page, :] = jnp.concatenate([x[0], x[1]], axis=0).T
        kst_ref[c, :, lanes] = x[2].astype(BF16)
        vst_ref[c, :, lanes] = x[3].astype(BF16)
        y = dsa_pages[p]
        kdt_ref[c, :, lanes] = y[0].astype(BF16)
        vdt_ref[c, :, lanes] = y[1].astype(BF16)
        kit_ref[c, :, lanes] = y[2].astype(BF16)

    first = lax.broadcasted_iota(I32, (HEAD_DIM, CK), 1) == 0

    def new_chunk(col):
        return jnp.where(first, jnp.broadcast_to(col, (HEAD_DIM, CK)), 0.0).astype(BF16)

    c_new = n_pages // per_chunk
    nsa_new = nsa_new_ref[...]
    dsa_new = dsa_new_ref[...]
    kst_ref[c_new] = new_chunk(nsa_new[2 * HEAD_DIM:3 * HEAD_DIM])
    vst_ref[c_new] = new_chunk(nsa_new[3 * HEAD_DIM:4 * HEAD_DIM])
    kdt_ref[c_new] = new_chunk(dsa_new[0:HEAD_DIM])
    vdt_ref[c_new] = new_chunk(dsa_new[HEAD_DIM:2 * HEAD_DIM])
    kit_ref[c_new] = new_chunk(dsa_new[2 * HEAD_DIM:3 * HEAD_DIM])
    kwt_ref[0] = win_ref[0].astype(BF16)
    vwt_ref[0] = win_ref[1].astype(BF16)
    kwt_ref[1] = new_chunk(nsa_new[4 * HEAD_DIM:5 * HEAD_DIM])
    vwt_ref[1] = new_chunk(nsa_new[5 * HEAD_DIM:6 * HEAD_DIM])


def _assemble(page_table, cache_nsa_t, cache_dsa_t, state_win_t, nsa_new, dsa_new, *, layer):
    nseq, n_pages = page_table.shape
    page = cache_nsa_t.shape[-1]
    n_chunks = n_pages * page // CK + 1
    sds = jax.ShapeDtypeStruct

    def page_spec(slots, p):
        return pl.BlockSpec((None, None, slots, HEAD_DIM, page), lambda b, pt: (pt[b, p], layer, 0, 0, 0))

    def per_seq(*shape):
        return pl.BlockSpec((None,) + shape, lambda b, pt: (b,) + (0,) * len(shape))

    chunks = per_seq(n_chunks, HEAD_DIM, CK)
    grid_spec = pltpu.PrefetchScalarGridSpec(
        num_scalar_prefetch=1,
        grid=(nseq,),
        in_specs=([page_spec(4, p) for p in range(n_pages)] + [page_spec(3, p) for p in range(n_pages)]
                  + [pl.BlockSpec((None, None, 2, HEAD_DIM, CK), lambda b, pt: (b, layer, 0, 0, 0)),
                     per_seq(6 * HEAD_DIM, 1), per_seq(3 * HEAD_DIM, 1)]),
        out_specs=[per_seq(n_pages * page, 2 * HEAD_DIM), chunks, chunks, per_seq(2, HEAD_DIM, CK),
                   per_seq(2, HEAD_DIM, CK), chunks, chunks, chunks],
    )
    chunk_shape = sds((nseq, n_chunks, HEAD_DIM, CK), BF16)
    win_shape = sds((nseq, 2, HEAD_DIM, CK), BF16)
    return pl.pallas_call(
        functools.partial(_assemble_kernel, n_pages=n_pages, page=page),
        out_shape=[sds((nseq, n_pages * page, 2 * HEAD_DIM), F32), chunk_shape, chunk_shape, win_shape, win_shape,
                   chunk_shape, chunk_shape, chunk_shape],
        grid_spec=grid_spec,
        compiler_params=_cparams(1),
        name="assemble",
    )(page_table, *([cache_nsa_t] * n_pages), *([cache_dsa_t] * n_pages), state_win_t, nsa_new, dsa_new)


def _prep_w_in(w_in):
    pts = np.cumsum(SPLIT_WIDTHS)[:-1].tolist()
    qa, kva, ga, glu, uv, qd, kvd, qi, ki, wi = jnp.split(w_in, pts, axis=-1)
    pad = jnp.zeros(w_in.shape[:2] + (128 - 16,), w_in.dtype)
    w = jnp.concatenate([qa, kva, glu, uv, qd, qi, wi, ga, pad, kvd, ki], axis=-1).astype(BF16)
    wt = jnp.concatenate([kva[..., 128:192], kva[..., 256:320], kvd[..., 0:64], ki], axis=-1)
    return w, jnp.swapaxes(wt, 1, 2).astype(BF16)


def _seq_spec(*shape):
    return pl.BlockSpec((None,) + shape, lambda b, i: (b,) + (0,) * len(shape))


def kernel(x_prompt, x_sample, cache_nsa, cache_dsa, state_win, state_conv, page_table, norm1_g, norm2_g,
           final_g, w_in, cmp_pos, cmp_w1, cmp_w2, conv_w, conv_b, conv_ln_g, conv_ln_b, sgu_ln_g, sgu_ln_b,
           sgu_w, sgu_b, w_branch, w_gate, w_out, w_up, w_down):
    nb_, seq, d = x_prompt.shape
    nseq, t_dec, _ = x_sample.shape
    depth = w_in.shape[0]
    n_pool, _, page, _, _ = cache_nsa.shape
    n_pages = page_table.shape[1]
    past = n_pages * page
    w_buf = state_win.shape[2]
    assert t_dec == 1 and seq % CK == 0 and past % CK == 0 and w_buf == CK == NSA_WINDOW
    nbp = seq // NSA_BLOCK
    assert nbp <= NB_PAD and past // NSA_BLOCK < NB_PAD
    n_tok = nb_ * seq
    sr = SAMPLE_ROWS

    w_proj, w_proj_t = _prep_w_in(w_in)
    wg, wb, wo = w_gate.astype(BF16), w_branch.astype(BF16), w_out.astype(BF16)
    wu, wd = w_up.astype(BF16), w_down.astype(BF16)
    def block_diag(a, b):
        z = jnp.zeros_like(a)
        return jnp.concatenate([jnp.concatenate([a, z], axis=-1), jnp.concatenate([z, b], axis=-1)], axis=-2)

    c_pos = jnp.concatenate([cmp_pos[:, 0], cmp_pos[:, 1]], axis=-1)
    c_w1 = cmp_w1.reshape(depth, 2, NSA_BLOCK, HEAD_DIM, HEAD_DIM).astype(BF16)
    c_w1 = block_diag(c_w1[:, 0], c_w1[:, 1])
    c_w2 = block_diag(cmp_w2[:, 0], cmp_w2[:, 1]).astype(BF16)
    sgu_bias = jnp.repeat(jnp.swapaxes(sgu_b, 1, 2), SGU_CH // SGU_GROUPS, axis=2)
    sgu_diag = jnp.repeat(sgu_w[:, :, 0, 0], SGU_CH // SGU_GROUPS, axis=1)[:, None, :]
    sgu_bias0 = sgu_bias[:, 0:1, :]
    vec = lambda a, l: a[l][None, :]
    cache_nsa_t = jnp.transpose(cache_nsa, (0, 1, 3, 4, 2))
    cache_dsa_t = jnp.transpose(cache_dsa, (0, 1, 3, 4, 2))
    state_win_t = jnp.transpose(state_win, (0, 1, 3, 4, 2))
    conv_hist = jnp.transpose(state_conv, (1, 2, 0, 3))

    n_chunks_p = seq // CK
    n_chunks_s = past // CK + 1
    nq = seq // Q_BLOCK

    def chunks_of(arr, slot, n_chunks):
        return arr, pl.BlockSpec((n_chunks, HEAD_DIM, CK), lambda b, i: (b, slot, 0))

    def rows_of(arr, slot, n_rows):
        return arr, pl.BlockSpec((None, n_rows, HEAD_DIM), lambda b, i: (slot, b, 0))

    xp = x_prompt.reshape(n_tok, d)
    xs = x_sample.reshape(nseq, d)
    outs = {k: [] for k in ("nsa_p", "nsa_s", "dsa_p", "dsa_s", "win_p", "win_s", "conv_p", "conv_s", "sgu_s")}
    yp = ys = None
    for l in range(depth):
        final = l == depth - 1
        qn, kva, glu, uv, qd, qi, misc, dsa_rows, kt4, v4 = _proj(
            xp, vec(norm1_g, l), w_proj[l], w_proj_t[l], tm=CK, emit_t=True)
        kva3 = kva.reshape(nb_, seq, 6, HEAD_DIM)
        kvc = _compress(kva, c_pos[l], c_w1[l], c_w2[l], rows_per_step=seq)
        kvc = jnp.pad(kvc.reshape(nb_, nbp, 2 * HEAD_DIM), ((0, 0), (0, NB_PAD - nbp), (0, 0)))
        o_a = _nsa(qn, misc, kvc, chunks_of(kt4, 0, n_chunks_p), rows_of(v4, 0, seq),
                   chunks_of(kt4, 1, n_chunks_p), rows_of(v4, 1, seq),
                   nbatch=nb_, nq=nq, tq=Q_BLOCK, pos0=0, pos_stride=1, win_base=0,
                   n_sel=min(NSA_N_SEL, seq // NSA_BLOCK), v_t=False)
        o_d = _dsa(qd, qi, misc, chunks_of(kt4, 2, n_chunks_p), rows_of(v4, 2, seq),
                   chunks_of(kt4, 3, n_chunks_p), nbatch=nb_, nq=nq, tq=Q_BLOCK,
                   n_keep=min(DSA_TOPK, seq // 4), n_chunks=n_chunks_p)
        o_b, conv_tail = _conv(glu, conv_w[l], vec(conv_b, l), vec(conv_ln_g, l), vec(conv_ln_b, l),
                               nbatch=nb_, seq=seq, tc=CK)
        o_c = _sgu(uv, vec(sgu_ln_g, l), vec(sgu_ln_b, l), sgu_w[l], sgu_bias[l], ts=CK)
        xp = _merge(xp, vec(norm1_g, l), o_a, o_b, o_c, o_d, wg[l], wb[l], wo[l], tm=256)
        res = _mlp(xp, vec(norm2_g, l), wu[l], wd[l], final_g[None, :], tm=256, final=final)
        xp = res[0]
        if final:
            yp = res[1]
        outs["nsa_p"].append(kva3[:, :, :4])
        outs["dsa_p"].append(dsa_rows.reshape(nb_, seq, 3, HEAD_DIM))
        outs["win_p"].append(kva3[:, seq - min(NSA_WINDOW, seq):, 4:])
        outs["conv_p"].append(conv_tail[:, CONV_HALO - (CONV_WIDTH - 1):])

        qn, kva, glu, uv, qd, qi, misc, dsa_rows = _proj(
            xs, vec(norm1_g, l), w_proj[l], w_proj_t[l], tm=nseq, emit_t=False)
        cmp_past, kst, vst, kwt, vwt, kdt, vdt, kit = _assemble(
            page_table, cache_nsa_t, cache_dsa_t, state_win_t, kva[:, :, None], dsa_rows[:, :, None], layer=l)
        nbs = past // NSA_BLOCK
        kvc = _compress(cmp_past.reshape(nseq * past, 2 * HEAD_DIM), c_pos[l], c_w1[l], c_w2[l],
                        rows_per_step=min(8, nseq) * past)
        kvc = jnp.pad(kvc.reshape(nseq, nbs, 2 * HEAD_DIM), ((0, 0), (0, NB_PAD - nbs), (0, 0)))

        def pad_rows(a):
            a = a[..., :, None, :]
            widths = [(0, 0)] * (a.ndim - 2) + [(0, sr - 1), (0, 0)]
            a = jnp.pad(a, widths)
            return a.reshape(a.shape[:-3] + (nseq * sr, a.shape[-1]))

        misc_r = pad_rows(misc)
        o_a = _nsa(pad_rows(qn), misc_r, kvc,
                   (kst, _seq_spec(n_chunks_s, HEAD_DIM, CK)), (vst, _seq_spec(n_chunks_s, HEAD_DIM, CK)),
                   (kwt, _seq_spec(2, HEAD_DIM, CK)), (vwt, _seq_spec(2, HEAD_DIM, CK)),
                   nbatch=nseq, nq=1, tq=sr, pos0=past, pos_stride=0, win_base=past // CK - 1,
                   n_sel=min(NSA_N_SEL, nbs + 1), v_t=True)
        o_d = _dsa_step(pad_rows(qd), pad_rows(qi), misc_r, kdt, vdt, kit, nseq=nseq, tq=sr, pos=past,
                        n_keep=min(DSA_TOPK, (past + 1) // 4), n_chunks=n_chunks_s)
        o_a = o_a.reshape(nseq, sr, -1)[:, 0]
        o_d = o_d.reshape(nseq, sr, -1)[:, 0]
        o_b, o_c, hx_new, v_rows = _step_mix(
            glu, uv, conv_hist[l], conv_w[l], vec(conv_b, l), vec(conv_ln_g, l), vec(conv_ln_b, l),
            vec(sgu_ln_g, l), vec(sgu_ln_b, l), sgu_diag[l], sgu_bias0[l])
        xs = _merge(xs, vec(norm1_g, l), o_a, o_b, o_c, o_d, wg[l], wb[l], wo[l], tm=nseq)
        res = _mlp(xs, vec(norm2_g, l), wu[l], wd[l], final_g[None, :], tm=nseq, final=final)
        xs = res[0]
        if final:
            ys = res[1]
        kva3 = kva.reshape(nseq, 1, 6, HEAD_DIM)
        outs["nsa_s"].append(kva3[:, :, :4])
        outs["dsa_s"].append(dsa_rows.reshape(nseq, 1, 3, HEAD_DIM))
        outs["win_s"].append(jnp.concatenate([state_win[:, l], kva3[:, :, 4:]], axis=1)[:, 1:])
        outs["conv_s"].append(jnp.concatenate([state_conv[:, l], hx_new[:, None, :]], axis=1)[:, 1:])
        outs["sgu_s"].append(v_rows[:, None, :])

    st = lambda k: jnp.stack(outs[k], axis=1)
    return (yp.reshape(nb_, seq, d), ys.reshape(nseq, 1, d), st("nsa_p"), st("nsa_s"), st("dsa_p"), st("dsa_s"),
            st("win_p"), st("win_s"), st("conv_p"), st("conv_s"), st("sgu_s"))
```

```python
import functools

import numpy as np
import jax
import jax.numpy as jnp
from jax import lax
from jax.experimental import pallas as pl
from jax.experimental.pallas import tpu as pltpu

F32, BF16, I32 = jnp.float32, jnp.bfloat16, jnp.int32

HEAD_DIM = 64
NSA_HEADS = 4
NSA_BLOCK = 64
NSA_N_SEL = 16
NSA_WINDOW = 512
FORCE_BONUS = 1.0e4
CONV_CH = 256
CONV_WIDTH = 31
SGU_CH = 256
SGU_GROUPS = 4
SGU_CHUNK = 128
DSA_HEADS = 4
IDX_HEADS = 4
DSA_TOPK = 256
N_BRANCH = 4
BRANCH_WIDTH = 256
Q_BLOCK = 128
EPS = 1e-6
NEG = -1e30

SPLIT_WIDTHS = (256, 384, 12, 512, 512, 256, 128, 256, 64, 4)

CK = 512
NB_PAD = 128
SAMPLE_ROWS = 16
VMEM_LIMIT = 56 * 1024 * 1024
INT_MIN = -2 ** 31

W_QA, W_KVA, W_GLU, W_UV, W_QD, W_QI, W_MISC, W_DSA = 0, 256, 640, 1152, 1664, 1920, 2176, 2304
W_TOTAL = 2496
MISC_WI, MISC_GA = 0, 4


def _cparams(n_axes):
    return pltpu.CompilerParams(dimension_semantics=("arbitrary",) * n_axes,
                                vmem_limit_bytes=VMEM_LIMIT)


def _dot(a, b):
    return jnp.dot(a, b, preferred_element_type=F32)


def _dot_nt(a, b):
    return lax.dot_general(a, b, (((1,), (1,)), ((), ())), preferred_element_type=F32)


def _rms(x, g):
    return x * lax.rsqrt(jnp.mean(x * x, axis=-1, keepdims=True) + EPS) * g


def _layernorm(x, g, b):
    mu = jnp.mean(x, axis=-1, keepdims=True)
    xc = x - mu
    var = jnp.mean(xc * xc, axis=-1, keepdims=True)
    return xc * lax.rsqrt(var + EPS) * g + b


def _sigmoid(x):
    return 1.0 / (1.0 + jnp.exp(-x))


def _tile4(a):
    return jnp.concatenate([a, a, a, a], axis=0)


def _proj_kernel(x_ref, g_ref, w_ref, wt_ref, qn_ref, kva_ref, glu_ref, uv_ref, qd_ref, qi_ref,
                 misc_ref, dsa_ref, *t_refs):
    hb = _rms(x_ref[...], g_ref[...]).astype(BF16)

    def slab(off, width):
        return _dot(hb, w_ref[:, off:off + width])

    qa = slab(W_QA, 256)
    qd = slab(W_QD, 256)
    qi = slab(W_QI, 256)
    for h in range(4):
        cols = slice(h * HEAD_DIM, (h + 1) * HEAD_DIM)
        qn_ref[h] = qa[:, cols].astype(BF16)
        qd_ref[h] = qd[:, cols].astype(BF16)
        qi_ref[h] = qi[:, cols].astype(BF16)
    kv = slab(W_KVA, 384)
    kva_ref[...] = kv
    glu_ref[...] = slab(W_GLU, 512)
    uv_ref[...] = slab(W_UV, 512)
    misc_ref[...] = slab(W_MISC, 128)
    dr = slab(W_DSA, 192)
    dsa_ref[...] = dr
    if t_refs:
        kt_ref, v_ref = t_refs
        kt_ref[0] = _dot_nt(wt_ref[...], hb).astype(BF16)
        v_ref[0] = kv[:, 192:256].astype(BF16)
        v_ref[1] = kv[:, 320:384].astype(BF16)
        v_ref[2] = dr[:, 64:128].astype(BF16)


def _proj(x, g, w, wt, *, tm, emit_t):
    n, d = x.shape
    sds = jax.ShapeDtypeStruct
    out_shape = [sds((4, n, HEAD_DIM), BF16), sds((n, 384), F32), sds((n, 512), F32), sds((n, 512), F32),
                 sds((4, n, HEAD_DIM), BF16), sds((4, n, HEAD_DIM), BF16), sds((n, 128), F32), sds((n, 192), F32)]
    head_spec = pl.BlockSpec((4, tm, HEAD_DIM), lambda i: (0, i, 0))

    def row_spec(width):
        return pl.BlockSpec((tm, width), lambda i: (i, 0))

    out_specs = [head_spec, row_spec(384), row_spec(512), row_spec(512), head_spec, head_spec,
                 row_spec(128), row_spec(192)]
    if emit_t:
        assert tm == CK
        out_shape += [sds((n // CK, 256, CK), BF16), sds((3, n, HEAD_DIM), BF16)]
        out_specs += [pl.BlockSpec((1, 256, CK), lambda i: (i, 0, 0)),
                      pl.BlockSpec((3, tm, HEAD_DIM), lambda i: (0, i, 0))]
    return pl.pallas_call(
        _proj_kernel,
        out_shape=out_shape,
        grid=(n // tm,),
        in_specs=[row_spec(d), pl.BlockSpec((1, d), lambda i: (0, 0)),
                  pl.BlockSpec((d, W_TOTAL), lambda i: (0, 0)),
                  pl.BlockSpec((256, d), lambda i: (0, 0))],
        out_specs=out_specs,
        compiler_params=_cparams(1),
        name="proj",
    )(x, g, w, wt)


def _compress_kernel(x_ref, pos_ref, w1_ref, w2_ref, o_ref):
    n_blocks = o_ref.shape[0]
    acc = jnp.zeros((n_blocks, 2 * HEAD_DIM), F32)
    for r in range(NSA_BLOCK):
        xr = x_ref[pl.ds(r, n_blocks, stride=NSA_BLOCK), :] + pos_ref[r:r + 1, :]
        acc = acc + _dot(xr.astype(BF16), w1_ref[r])
    a = acc * _sigmoid(acc)
    o_ref[...] = _dot(a.astype(BF16), w2_ref[...]).astype(o_ref.dtype)


def _compress(x, pos, w1, w2, *, rows_per_step):
    r = x.shape[0]
    width = 2 * HEAD_DIM
    return pl.pallas_call(
        _compress_kernel,
        out_shape=jax.ShapeDtypeStruct((r // NSA_BLOCK, width), BF16),
        grid=(r // rows_per_step,),
        in_specs=[pl.BlockSpec((rows_per_step, width), lambda i: (i, 0)),
                  pl.BlockSpec((NSA_BLOCK, width), lambda i: (0, 0)),
                  pl.BlockSpec((NSA_BLOCK, width, width), lambda i: (0, 0, 0)),
                  pl.BlockSpec((width, width), lambda i: (0, 0))],
        out_specs=pl.BlockSpec((rows_per_step // NSA_BLOCK, width), lambda i: (i, 0)),
        compiler_params=_cparams(1),
        name="compress",
    )(x, pos, w1, w2)


def _masked_probs(s, mask):
    s = jnp.where(mask, s, NEG)
    m = jnp.max(s, axis=-1, keepdims=True)
    p = jnp.where(mask, jnp.exp(s - m), 0.0)
    return p / jnp.maximum(jnp.sum(p, axis=-1, keepdims=True), 1e-30)


def _softmax_step(s, bias, v, carry, v_t=False):
    m, l, acc = carry
    tq = bias.shape[0]
    s = s.reshape(4, tq, CK) + bias[None]
    m_new = jnp.maximum(m, jnp.max(s, axis=-1, keepdims=True))
    alpha = jnp.exp(m - m_new)
    p = jnp.exp(s - m_new)
    l = alpha * l + jnp.sum(p, axis=-1, keepdims=True)
    pb = p.reshape(4 * tq, CK).astype(BF16)
    acc = alpha.reshape(4 * tq, 1) * acc + (_dot_nt(pb, v) if v_t else _dot(pb, v))
    return m_new, l, acc


def _softmax_init(tq):
    return (jnp.full((4, tq, 1), NEG, F32), jnp.zeros((4, tq, 1), F32), jnp.zeros((4 * tq, HEAD_DIM), F32))


def _softmax_done(carry):
    _, l, acc = carry
    return acc / jnp.maximum(l.reshape(acc.shape[0], 1), 1e-30)


def _mask_bias(allowed):
    return jnp.where(allowed, 0.0, NEG)


def _query_positions(tq, pos0, pos_stride):
    if pos_stride == 0:
        return pos0, pos0, jnp.full((tq, 1), pos0, I32)
    i = pl.program_id(1)
    start = pos0 + i * (tq * pos_stride)
    last = start + (tq - 1) * pos_stride
    t = lax.broadcasted_iota(I32, (tq, 1), 0)
    qpos = start + t * pos_stride
    return start, last, qpos


def _heads_as_rows(ref, scale):
    return jnp.concatenate([ref[h] for h in range(4)], axis=0) * scale


def _key_positions(c):
    return c * CK + lax.broadcasted_iota(I32, (1, CK), 1)


def _chunk_rows(ref, c):
    if isinstance(c, int):
        return ref[c * CK:(c + 1) * CK, :]
    return ref[pl.ds(pl.multiple_of(c * CK, CK), CK), :]


PICK_LANES = 128


def _nsa_kernel(q_ref, misc_ref, kvc_ref, kst_ref, vs_ref, kwt_ref, vw_ref, o_ref, *,
                tq, pos0, pos_stride, win_base, n_sel, v_t):
    values = (lambda ref, c: ref[c]) if v_t else _chunk_rows
    nb = kvc_ref.shape[0]
    start, last, qpos = _query_positions(tq, pos0, pos_stride)
    qpos4 = _tile4(qpos)
    q = _heads_as_rows(q_ref, HEAD_DIM ** -0.5)

    jb = lax.broadcasted_iota(I32, (1, nb), 1)
    mask_c = ((jb + 1) * NSA_BLOCK - 1) <= qpos4
    p_c = _masked_probs(_dot_nt(q, kvc_ref[:, 0:HEAD_DIM]), mask_c)
    o_c = _dot(p_c.astype(BF16), kvc_ref[:, HEAD_DIM:2 * HEAD_DIM])
    imp = p_c[0:tq] + p_c[tq:2 * tq] + p_c[2 * tq:3 * tq] + p_c[3 * tq:4 * tq]

    cur = qpos // NSA_BLOCK
    forced = (jb == 0) | (jb == cur) | (jb == cur - 1)
    score = jnp.where(jb <= cur, imp, -1.0) + jnp.where(forced, FORCE_BONUS, 0.0)
    if tq < PICK_LANES:
        score = jnp.concatenate([score, jnp.zeros((PICK_LANES - tq, nb), F32)], axis=0)
    jcol = lax.broadcasted_iota(I32, (nb, PICK_LANES), 0).astype(F32)

    def pick(_, carry):
        sc, sel = carry
        m = jnp.max(sc, axis=0, keepdims=True)
        first = jnp.min(jnp.where(sc == m, jcol, 1e9), axis=0, keepdims=True)
        hit = jcol == first
        return jnp.where(hit, -3e38, sc), jnp.where(hit, 1.0, sel)

    _, sel_t = lax.fori_loop(0, n_sel, pick, (score.T, jnp.zeros((nb, PICK_LANES), F32)))
    sel = sel_t.T[0:tq].astype(BF16)

    jrow = lax.broadcasted_iota(I32, (nb, CK), 0)
    kcol = lax.broadcasted_iota(I32, (nb, CK), 1)

    def sel_chunk(c, carry):
        expand = jnp.where(jrow == ((c * CK + kcol) // NSA_BLOCK), 1.0, 0.0).astype(BF16)
        bias = _mask_bias((_dot(sel, expand) > 0.5) & (_key_positions(c) <= qpos))
        return _softmax_step(_dot(q, kst_ref[c]), bias, values(vs_ref, c), carry, v_t)

    o_s = _softmax_done(_chunk_loop(last // CK + 1, sel_chunk, _softmax_init(tq), unroll=ATTEND_UNROLL))

    carry = _softmax_init(tq)
    c1 = start // CK
    for back in (1, 0):
        c = c1 - back
        cc = max(c - win_base, 0) if isinstance(c, int) else jnp.maximum(c - win_base, 0)
        kpos = _key_positions(c)
        dpos = qpos - kpos
        bias = _mask_bias((dpos >= 0) & (dpos < NSA_WINDOW) & (kpos >= 0))
        carry = _softmax_step(_dot(q, kwt_ref[cc]), bias, values(vw_ref, cc), carry, v_t)
    o_w = _softmax_done(carry)

    g = _sigmoid(misc_ref[...])
    outs = []
    for h in range(4):
        rows = slice(h * tq, (h + 1) * tq)
        col = MISC_GA + 3 * h
        outs.append(g[:, col:col + 1] * o_c[rows] + g[:, col + 1:col + 2] * o_s[rows]
                    + g[:, col + 2:col + 3] * o_w[rows])
    o_ref[...] = jnp.concatenate(outs, axis=-1).astype(o_ref.dtype)


def _nsa(q, misc, kvc, ks, vs, kw, vw, *, nbatch, nq, tq, pos0, pos_stride, win_base, n_sel, v_t):
    nrows = q.shape[1]
    kern = functools.partial(_nsa_kernel, tq=tq, pos0=pos0, pos_stride=pos_stride, win_base=win_base,
                             n_sel=n_sel, v_t=v_t)
    return pl.pallas_call(
        kern,
        out_shape=jax.ShapeDtypeStruct((nrows, 4 * HEAD_DIM), BF16),
        grid=(nbatch, nq),
        in_specs=[pl.BlockSpec((4, tq, HEAD_DIM), lambda b, i: (0, b * nq + i, 0)),
                  pl.BlockSpec((tq, 128), lambda b, i: (b * nq + i, 0)),
                  pl.BlockSpec((None, NB_PAD, 2 * HEAD_DIM), lambda b, i: (b, 0, 0)),
                  ks[1], vs[1], kw[1], vw[1]],
        out_specs=pl.BlockSpec((tq, 4 * HEAD_DIM), lambda b, i: (b * nq + i, 0)),
        compiler_params=_cparams(2),
        name="nsa",
    )(q, misc, kvc, ks[0], vs[0], kw[0], vw[0])


LANES = 128
COUNT_UNROLL = 4
ATTEND_UNROLL = 4
INDEX_UNROLL = 4


def _index_keys(qi, wcol, kit, kpos, qpos, tq):
    rel = jnp.maximum(_dot(qi, kit), 0.0) * wcol
    sc = rel[0:tq] + rel[tq:2 * tq] + rel[2 * tq:3 * tq] + rel[3 * tq:4 * tq]
    sc = jnp.where(sc == 0.0, 0.0, sc)
    sc = jnp.where(kpos <= qpos, sc, NEG)
    bits = lax.bitcast_convert_type(sc, I32)
    return jnp.where(bits < 0, bits ^ 0x7FFFFFFF, bits)


def _index_weights(misc_ref):
    w = misc_ref[:, MISC_WI:MISC_WI + 4] * (IDX_HEADS ** -0.5 * HEAD_DIM ** -0.5)
    return jnp.concatenate([w[:, h:h + 1] for h in range(4)], axis=0)


def _chunk_loop(n_chunks, body, init, unroll=COUNT_UNROLL):
    if isinstance(n_chunks, int):
        carry = init
        for c in range(n_chunks):
            carry = body(c, carry)
        return carry

    def several(g, carry):
        for u in range(unroll):
            carry = body(g * unroll + u, carry)
        return carry

    n_groups = n_chunks // unroll
    carry = lax.fori_loop(0, n_groups, several, init)
    return lax.fori_loop(n_groups * unroll, n_chunks, body, carry)


def _key_counter(key_ref, n_chunks):
    rows = key_ref.shape[1]

    def count(pred):
        def one(c, acc):
            for j in range(CK // LANES):
                k = key_ref[c, :, j * LANES:(j + 1) * LANES]
                acc = acc + jnp.where(pred(k, c * CK + j * LANES), 1.0, 0.0)
            return acc

        acc = _chunk_loop(n_chunks, one, jnp.zeros((rows, LANES), F32))
        return jnp.sum(acc, axis=-1, keepdims=True)

    return count


def _top_keys(count, n_keep):
    keep = float(n_keep)
    thr = jnp.where(count(lambda k, c: k >= 0) >= keep, 0, INT_MIN).astype(I32)

    def thr_bit(b, thr):
        cand = thr | jnp.left_shift(jnp.int32(1), 30 - b)
        return jnp.where(count(lambda k, c: k >= cand) >= keep, cand, thr)

    thr = lax.fori_loop(0, 31, thr_bit, thr)
    return thr, keep - count(lambda k, c: k > thr)


def _tie_cut(count, thr, need, rows, n_positions):
    n_pos_bits = (n_positions - 1).bit_length()

    def cut_bit(b, cut):
        cand = cut + jnp.left_shift(jnp.int32(1), n_pos_bits - 1 - b)
        n = count(lambda k, first: (k == thr) & (first + lax.broadcasted_iota(I32, (1, LANES), 1) < cand))
        return jnp.where(n < need, cand, cut)

    return lax.fori_loop(0, n_pos_bits, cut_bit, jnp.zeros((rows, 1), I32))


def _heads_to_lanes(o, tq):
    return jnp.concatenate([o[h * tq:(h + 1) * tq] for h in range(4)], axis=-1)


def _dsa_kernel(qd_ref, qi_ref, misc_ref, kdt_ref, vd_ref, kit_ref, o_ref, key_scr, *, tq, n_keep):
    start, last, qpos = _query_positions(tq, 0, 1)
    n_chunks = last // CK + 1
    qd = _heads_as_rows(qd_ref, HEAD_DIM ** -0.5)
    qi = _heads_as_rows(qi_ref, 1.0)
    wcol = _index_weights(misc_ref)

    def index_chunk(c, _):
        key_scr[c] = _index_keys(qi, wcol, kit_ref[c], _key_positions(c), qpos, tq)
        return 0

    _chunk_loop(n_chunks, index_chunk, 0, unroll=INDEX_UNROLL)
    thr, need = _top_keys(_key_counter(key_scr, n_chunks), n_keep)

    tri = jnp.where(lax.broadcasted_iota(I32, (LANES, LANES), 0) <= lax.broadcasted_iota(I32, (LANES, LANES), 1),
                    1.0, 0.0).astype(BF16)

    def attend_chunk(c, carry):
        state, seen = carry
        k = key_scr[c]
        ranks = []
        for j in range(CK // LANES):
            equal = jnp.where(k[:, j * LANES:(j + 1) * LANES] == thr, 1.0, 0.0)
            ranks.append(seen + _dot(equal.astype(BF16), tri))
            seen = seen + jnp.sum(equal, axis=-1, keepdims=True)
        rank = jnp.concatenate(ranks, axis=-1)
        taken = (k > thr) | ((k == thr) & (rank <= need))
        bias = _mask_bias(taken & (_key_positions(c) <= qpos))
        state = _softmax_step(_dot(qd, kdt_ref[c]), bias, _chunk_rows(vd_ref, c), state)
        return state, seen

    state, _ = _chunk_loop(n_chunks, attend_chunk, (_softmax_init(tq), jnp.zeros((tq, 1), F32)),
                           unroll=ATTEND_UNROLL)
    o_ref[...] = _heads_to_lanes(_softmax_done(state), tq).astype(o_ref.dtype)


def _dsa(qd, qi, misc, kd, vd, ki, *, nbatch, nq, tq, n_keep, n_chunks):
    nrows = qd.shape[1]
    head_spec = pl.BlockSpec((4, tq, HEAD_DIM), lambda b, i: (0, b * nq + i, 0))
    return pl.pallas_call(
        functools.partial(_dsa_kernel, tq=tq, n_keep=n_keep),
        out_shape=jax.ShapeDtypeStruct((nrows, 4 * HEAD_DIM), BF16),
        grid=(nbatch, nq),
        in_specs=[head_spec, head_spec, pl.BlockSpec((tq, 128), lambda b, i: (b * nq + i, 0)),
                  kd[1], vd[1], ki[1]],
        out_specs=pl.BlockSpec((tq, 4 * HEAD_DIM), lambda b, i: (b * nq + i, 0)),
        scratch_shapes=[pltpu.VMEM((n_chunks, tq, CK), I32)],
        compiler_params=_cparams(2),
        name="dsa",
    )(qd, qi, misc, kd[0], vd[0], ki[0])


def _dsa_step_index_kernel(qi_ref, misc_ref, kit_ref, key_ref, *, tq, pos, n_chunks):
    qi = _heads_as_rows(qi_ref, 1.0)
    wcol = _index_weights(misc_ref)
    for c in range(n_chunks):
        key_ref[c] = _index_keys(qi, wcol, kit_ref[c], _key_positions(c), pos, tq)[0:1, :]


def _dsa_step_top_kernel(key_ref, thr_ref, cut_ref, *, n_keep, n_chunks):
    rows = key_ref.shape[1]
    count = _key_counter(key_ref, n_chunks)
    thr, need = _top_keys(count, n_keep)
    thr_ref[...] = jnp.broadcast_to(thr, thr_ref.shape)
    cut_ref[...] = jnp.broadcast_to(_tie_cut(count, thr, need, rows, n_chunks * CK), cut_ref.shape)


def _dsa_step_attend_kernel(qd_ref, key_ref, thr_ref, cut_ref, kdt_ref, vdt_ref, o_ref, *, tq, pos, n_chunks):
    qd = _heads_as_rows(qd_ref, HEAD_DIM ** -0.5)
    thr = thr_ref[:, 0:1]
    cut = cut_ref[:, 0:1]
    carry = _softmax_init(tq)
    for c in range(n_chunks):
        k = key_ref[c]
        kpos = _key_positions(c)
        bias = _mask_bias(((k > thr) | ((k == thr) & (kpos <= cut))) & (kpos <= pos))
        carry = _softmax_step(_dot(qd, kdt_ref[c]), jnp.broadcast_to(bias, (tq, CK)), vdt_ref[c], carry, True)
    o_ref[...] = _heads_to_lanes(_softmax_done(carry), tq).astype(o_ref.dtype)


def _dsa_step(qd, qi, misc, kdt, vd, kit, *, nseq, tq, pos, n_keep, n_chunks):
    head_spec = pl.BlockSpec((4, tq, HEAD_DIM), lambda b: (0, b, 0))
    chunk_spec = pl.BlockSpec((None, n_chunks, HEAD_DIM, CK), lambda b: (b, 0, 0, 0))
    key_spec = pl.BlockSpec((n_chunks, None, 1, CK), lambda b: (0, b, 0, 0))
    row_spec = pl.BlockSpec((None, 1, LANES), lambda b: (b, 0, 0))
    keys = pl.pallas_call(
        functools.partial(_dsa_step_index_kernel, tq=tq, pos=pos, n_chunks=n_chunks),
        out_shape=jax.ShapeDtypeStruct((n_chunks, nseq, 1, CK), I32),
        grid=(nseq,),
        in_specs=[head_spec, pl.BlockSpec((tq, 128), lambda b: (b, 0)), chunk_spec],
        out_specs=key_spec,
        compiler_params=_cparams(1),
        name="dsa_step_index",
    )(qi, misc, kit)
    thr, cut = pl.pallas_call(
        functools.partial(_dsa_step_top_kernel, n_keep=n_keep, n_chunks=n_chunks),
        out_shape=[jax.ShapeDtypeStruct((nseq, LANES), I32)] * 2,
        compiler_params=pltpu.CompilerParams(vmem_limit_bytes=VMEM_LIMIT),
        name="dsa_step_top",
    )(keys.reshape(n_chunks, nseq, CK))
    return pl.pallas_call(
        functools.partial(_dsa_step_attend_kernel, tq=tq, pos=pos, n_chunks=n_chunks),
        out_shape=jax.ShapeDtypeStruct((nseq * tq, 4 * HEAD_DIM), BF16),
        grid=(nseq,),
        in_specs=[head_spec, key_spec, row_spec, row_spec, chunk_spec, chunk_spec],
        out_specs=pl.BlockSpec((tq, 4 * HEAD_DIM), lambda b: (b, 0)),
        compiler_params=_cparams(1),
        name="dsa_step_attend",
    )(qd, keys, thr[:, None, :], cut[:, None, :], kdt, vd)


CONV_HALO = 32


def _conv_kernel(cur_ref, halo_ref, w_ref, b_ref, g_ref, beta_ref, o_ref, tail_ref, hx_scr, *, tc):
    i = pl.program_id(1)

    def glu(x):
        return x[:, :CONV_CH] * _sigmoid(x[:, CONV_CH:])

    hx_scr[0:CONV_HALO] = jnp.where(i > 0, glu(halo_ref[...]), 0.0)
    hx_scr[CONV_HALO:CONV_HALO + tc] = glu(cur_ref[...])
    first = CONV_HALO - (CONV_WIDTH - 1)
    y = jnp.broadcast_to(b_ref[...], (tc, CONV_CH))
    for k in range(CONV_WIDTH):
        y = y + hx_scr[first + k:first + k + tc] * w_ref[k:k + 1, :]
    y = _layernorm(y, g_ref[...], beta_ref[...])
    o_ref[...] = (y * _sigmoid(y)).astype(o_ref.dtype)
    tail_ref[...] = hx_scr[tc:tc + CONV_HALO]


def _conv(glu, w, b, g, beta, *, nbatch, seq, tc):
    n = glu.shape[0]
    nt = seq // tc
    per = tc // CONV_HALO
    vec = pl.BlockSpec((1, CONV_CH), lambda bb, i: (0, 0))
    return pl.pallas_call(
        functools.partial(_conv_kernel, tc=tc),
        out_shape=[jax.ShapeDtypeStruct((n, CONV_CH), BF16),
                   jax.ShapeDtypeStruct((nbatch, CONV_HALO, CONV_CH), F32)],
        grid=(nbatch, nt),
        in_specs=[pl.BlockSpec((tc, 2 * CONV_CH), lambda bb, i: (bb * nt + i, 0)),
                  pl.BlockSpec((CONV_HALO, 2 * CONV_CH),
                               lambda bb, i: (jnp.maximum((bb * nt + i) * per - 1, 0), 0)),
                  pl.BlockSpec((CONV_WIDTH, CONV_CH), lambda bb, i: (0, 0)), vec, vec, vec],
        out_specs=[pl.BlockSpec((tc, CONV_CH), lambda bb, i: (bb * nt + i, 0)),
                   pl.BlockSpec((None, CONV_HALO, CONV_CH), lambda bb, i: (bb, 0, 0))],
        scratch_shapes=[pltpu.VMEM((tc + CONV_HALO, CONV_CH), F32)],
        compiler_params=_cparams(2),
        name="conv",
    )(glu, glu, w, b, g, beta)


def _gelu(x):
    return 0.5 * x * (1.0 + lax.erf(x * (2.0 ** -0.5)))


def _sgu_kernel(uv_ref, g_ref, beta_ref, w_ref, bias_ref, o_ref, *, ts):
    a = _gelu(uv_ref[...])
    u = a[:, :SGU_CH]
    vn = _layernorm(a[:, SGU_CH:], g_ref[...], beta_ref[...]).astype(BF16)
    ri = lax.broadcasted_iota(I32, (SGU_CHUNK, SGU_CHUNK), 0)
    ci = lax.broadcasted_iota(I32, (SGU_CHUNK, SGU_CHUNK), 1)
    group = lax.broadcasted_iota(I32, (SGU_CHUNK, SGU_CH), 1) // (SGU_CH // SGU_GROUPS)
    ws = [jnp.where(ci <= ri, w_ref[gi], 0.0).astype(BF16) for gi in range(SGU_GROUPS)]
    for c in range(ts // SGU_CHUNK):
        rows = slice(c * SGU_CHUNK, (c + 1) * SGU_CHUNK)
        mixed = bias_ref[...]
        for gi in range(SGU_GROUPS):
            mixed = mixed + jnp.where(group == gi, _dot(ws[gi], vn[rows]), 0.0)
        o_ref[rows, :] = (u[rows] * mixed).astype(o_ref.dtype)


def _sgu(uv, g, beta, w, bias, *, ts):
    n = uv.shape[0]
    vec = pl.BlockSpec((1, SGU_CH), lambda i: (0, 0))
    return pl.pallas_call(
        functools.partial(_sgu_kernel, ts=ts),
        out_shape=jax.ShapeDtypeStruct((n, SGU_CH), BF16),
        grid=(n // ts,),
        in_specs=[pl.BlockSpec((ts, 2 * SGU_CH), lambda i: (i, 0)), vec, vec,
                  pl.BlockSpec((SGU_GROUPS, SGU_CHUNK, SGU_CHUNK), lambda i: (0, 0, 0)),
                  pl.BlockSpec((SGU_CHUNK, SGU_CH), lambda i: (0, 0))],
        out_specs=pl.BlockSpec((ts, SGU_CH), lambda i: (i, 0)),
        compiler_params=_cparams(1),
        name="sgu",
    )(uv, g, beta, w, bias)


def _step_mix_kernel(glu_ref, uv_ref, hist_ref, cw_ref, cb_ref, cg_ref, cbeta_ref, sg_ref, sbeta_ref,
                     sdiag_ref, sbias_ref, ob_ref, oc_ref, hx_ref, v_ref):
    x = glu_ref[...]
    hx = x[:, :CONV_CH] * _sigmoid(x[:, CONV_CH:])
    hx_ref[...] = hx
    y = cb_ref[...] + hx * cw_ref[CONV_WIDTH - 1:CONV_WIDTH, :]
    for k in range(CONV_WIDTH - 1):
        y = y + hist_ref[k] * cw_ref[k:k + 1, :]
    y = _layernorm(y, cg_ref[...], cbeta_ref[...])
    ob_ref[...] = (y * _sigmoid(y)).astype(ob_ref.dtype)
    a = _gelu(uv_ref[...])
    v = a[:, SGU_CH:]
    v_ref[...] = v
    vn = _layernorm(v, sg_ref[...], sbeta_ref[...])
    oc_ref[...] = (a[:, :SGU_CH] * (sdiag_ref[...] * vn + sbias_ref[...])).astype(oc_ref.dtype)


def _step_mix(glu, uv, hist, cw, cb, cg, cbeta, sg, sbeta, sdiag, sbias):
    n = glu.shape[0]
    sds = jax.ShapeDtypeStruct
    return pl.pallas_call(
        _step_mix_kernel,
        out_shape=[sds((n, CONV_CH), BF16), sds((n, SGU_CH), BF16), sds((n, CONV_CH), F32), sds((n, SGU_CH), F32)],
        compiler_params=pltpu.CompilerParams(vmem_limit_bytes=VMEM_LIMIT),
        name="step_mix",
    )(glu, uv, hist, cw, cb, cg, cbeta, sg, sbeta, sdiag, sbias)


def _merge_kernel(x_ref, g_ref, oa_ref, ob_ref, oc_ref, od_ref, wg_ref, wb_ref, wo_ref, y_ref):
    x = x_ref[...]
    d = x.shape[-1]
    hb = _rms(x, g_ref[...]).astype(BF16)
    acc = jnp.zeros(x.shape, F32)
    for k, o_ref in enumerate((oa_ref, ob_ref, oc_ref, od_ref)):
        gate = _sigmoid(_dot(hb, wg_ref[:, k * d:(k + 1) * d]))
        acc = acc + gate * _dot(o_ref[...], wb_ref[k])
    y_ref[...] = x + _dot(acc.astype(BF16), wo_ref[...])


def _merge(x, g, oa, ob, oc, od, wg, wb, wo, *, tm):
    n, d = x.shape
    row = pl.BlockSpec((tm, d), lambda i: (i, 0))
    br = pl.BlockSpec((tm, BRANCH_WIDTH), lambda i: (i, 0))
    once = pl.Buffered(1)
    return pl.pallas_call(
        _merge_kernel,
        out_shape=jax.ShapeDtypeStruct((n, d), F32),
        grid=(n // tm,),
        in_specs=[row, pl.BlockSpec((1, d), lambda i: (0, 0)), br, br, br, br,
                  pl.BlockSpec((d, N_BRANCH * d), lambda i: (0, 0), pipeline_mode=once),
                  pl.BlockSpec((N_BRANCH, BRANCH_WIDTH, d), lambda i: (0, 0, 0), pipeline_mode=once),
                  pl.BlockSpec((d, d), lambda i: (0, 0), pipeline_mode=once)],
        out_specs=row,
        compiler_params=_cparams(1),
        name="merge",
    )(x, g, oa, ob, oc, od, wg, wb, wo)


def _mlp_kernel(x_ref, g_ref, wu_ref, wd_ref, gf_ref, y_ref, *n_ref):
    x = x_ref[...]
    hb = _rms(x, g_ref[...]).astype(BF16)
    a = jnp.square(jnp.maximum(_dot(hb, wu_ref[...]), 0.0)).astype(BF16)
    y = x + _dot(a, wd_ref[...])
    y_ref[...] = y
    if n_ref:
        n_ref[0][...] = _rms(y, gf_ref[...])


def _mlp(x, g, wu, wd, gf, *, tm, final):
    n, d = x.shape
    row = pl.BlockSpec((tm, d), lambda i: (i, 0))
    vec = pl.BlockSpec((1, d), lambda i: (0, 0))
    once = pl.Buffered(1)
    out_shape = [jax.ShapeDtypeStruct((n, d), F32)] * (2 if final else 1)
    return pl.pallas_call(
        _mlp_kernel,
        out_shape=out_shape,
        grid=(n // tm,),
        in_specs=[row, vec, pl.BlockSpec(wu.shape, lambda i: (0, 0), pipeline_mode=once),
                  pl.BlockSpec(wd.shape, lambda i: (0, 0), pipeline_mode=once), vec],
        out_specs=[row] * (2 if final else 1),
        compiler_params=_cparams(1),
        name="mlp",
    )(x, g, wu, wd, gf)


def _assemble_kernel(pt_ref, *refs, n_pages, page):
    nsa_pages = refs[:n_pages]
    dsa_pages = refs[n_pages:2 * n_pages]
    win_ref, nsa_new_ref, dsa_new_ref = refs[2 * n_pages:2 * n_pages + 3]
    cmp_ref, kst_ref, vst_ref, kwt_ref, vwt_ref, kdt_ref, vdt_ref, kit_ref = refs[2 * n_pages + 3:]
    per_chunk = CK // page
    for p in range(n_pages):
        c, r = divmod(p, per_chunk)
        lanes = slice(r * page, (r + 1) * page)
        x = nsa_pages[p]
        cmp_ref[p * page:(p + 1) * page, :] = jnp.concatenate([x[0], x[1]], axis=0).T
        kst_ref[c, :, lanes] = x[2].astype(BF16)
        vst_ref[c, :, lanes] = x[3].astype(BF16)
        y = dsa_pages[p]
        kdt_ref[c, :, lanes] = y[0].astype(BF16)
        vdt_ref[c, :, lanes] = y[1].astype(BF16)
        kit_ref[c, :, lanes] = y[2].astype(BF16)

    first = lax.broadcasted_iota(I32, (HEAD_DIM, CK), 1) == 0

    def new_chunk(col):
        return jnp.where(first, jnp.broadcast_to(col, (HEAD_DIM, CK)), 0.0).astype(BF16)

    c_new = n_pages // per_chunk
    nsa_new = nsa_new_ref[...]
    dsa_new = dsa_new_ref[...]
    kst_ref[c_new] = new_chunk(nsa_new[2 * HEAD_DIM:3 * HEAD_DIM])
    vst_ref[c_new] = new_chunk(nsa_new[3 * HEAD_DIM:4 * HEAD_DIM])
    kdt_ref[c_new] = new_chunk(dsa_new[0:HEAD_DIM])
    vdt_ref[c_new] = new_chunk(dsa_new[HEAD_DIM:2 * HEAD_DIM])
    kit_ref[c_new] = new_chunk(dsa_new[2 * HEAD_DIM:3 * HEAD_DIM])
    kwt_ref[0] = win_ref[0].astype(BF16)
    vwt_ref[0] = win_ref[1].astype(BF16)
    kwt_ref[1] = new_chunk(nsa_new[4 * HEAD_DIM:5 * HEAD_DIM])
    vwt_ref[1] = new_chunk(nsa_new[5 * HEAD_DIM:6 * HEAD_DIM])


def _assemble(page_table, cache_nsa_t, cache_dsa_t, state_win_t, nsa_new, dsa_new, *, layer):
    nseq, n_pages = page_table.shape
    page = cache_nsa_t.shape[-1]
    n_chunks = n_pages * page // CK + 1
    sds = jax.ShapeDtypeStruct

    def page_spec(slots, p):
        return pl.BlockSpec((None, None, slots, HEAD_DIM, page), lambda b, pt: (pt[b, p], layer, 0, 0, 0))

    def per_seq(*shape):
        return pl.BlockSpec((None,) + shape, lambda b, pt: (b,) + (0,) * len(shape))

    chunks = per_seq(n_chunks, HEAD_DIM, CK)
    grid_spec = pltpu.PrefetchScalarGridSpec(
        num_scalar_prefetch=1,
        grid=(nseq,),
        in_specs=([page_spec(4, p) for p in range(n_pages)] + [page_spec(3, p) for p in range(n_pages)]
                  + [pl.BlockSpec((None, None, 2, HEAD_DIM, CK), lambda b, pt: (b, layer, 0, 0, 0)),
                     per_seq(6 * HEAD_DIM, 1), per_seq(3 * HEAD_DIM, 1)]),
        out_specs=[per_seq(n_pages * page, 2 * HEAD_DIM), chunks, chunks, per_seq(2, HEAD_DIM, CK),
                   per_seq(2, HEAD_DIM, CK), chunks, chunks, chunks],
    )
    chunk_shape = sds((nseq, n_chunks, HEAD_DIM, CK), BF16)
    win_shape = sds((nseq, 2, HEAD_DIM, CK), BF16)
    return pl.pallas_call(
        functools.partial(_assemble_kernel, n_pages=n_pages, page=page),
        out_shape=[sds((nseq, n_pages * page, 2 * HEAD_DIM), F32), chunk_shape, chunk_shape, win_shape, win_shape,
                   chunk_shape, chunk_shape, chunk_shape],
        grid_spec=grid_spec,
        compiler_params=_cparams(1),
        name="assemble",
    )(page_table, *([cache_nsa_t] * n_pages), *([cache_dsa_t] * n_pages), state_win_t, nsa_new, dsa_new)


def _prep_w_in(w_in):
    pts = np.cumsum(SPLIT_WIDTHS)[:-1].tolist()
    qa, kva, ga, glu, uv, qd, kvd, qi, ki, wi = jnp.split(w_in, pts, axis=-1)
    pad = jnp.zeros(w_in.shape[:2] + (128 - 16,), w_in.dtype)
    w = jnp.concatenate([qa, kva, glu, uv, qd, qi, wi, ga, pad, kvd, ki], axis=-1).astype(BF16)
    wt = jnp.concatenate([kva[..., 128:192], kva[..., 256:320], kvd[..., 0:64], ki], axis=-1)
    return w, jnp.swapaxes(wt, 1, 2).astype(BF16)


def _seq_spec(*shape):
    return pl.BlockSpec((None,) + shape, lambda b, i: (b,) + (0,) * len(shape))


def kernel(x_prompt, x_sample, cache_nsa, cache_dsa, state_win, state_conv, page_table, norm1_g, norm2_g,
           final_g, w_in, cmp_pos, cmp_w1, cmp_w2, conv_w, conv_b, conv_ln_g, conv_ln_b, sgu_ln_g, sgu_ln_b,
           sgu_w, sgu_b, w_branch, w_gate, w_out, w_up, w_down):
    nb_, seq, d = x_prompt.shape
    nseq, t_dec, _ = x_sample.shape
    depth = w_in.shape[0]
    n_pool, _, page, _, _ = cache_nsa.shape
    n_pages = page_table.shape[1]
    past = n_pages * page
    w_buf = state_win.shape[2]
    assert t_dec == 1 and seq % CK == 0 and past % CK == 0 and w_buf == CK == NSA_WINDOW
    nbp = seq // NSA_BLOCK
    assert nbp <= NB_PAD and past // NSA_BLOCK < NB_PAD
    n_tok = nb_ * seq
    sr = SAMPLE_ROWS

    w_proj, w_proj_t = _prep_w_in(w_in)
    wg, wb, wo = w_gate.astype(BF16), w_branch.astype(BF16), w_out.astype(BF16)
    wu, wd = w_up.astype(BF16), w_down.astype(BF16)
    def block_diag(a, b):
        z = jnp.zeros_like(a)
        return jnp.concatenate([jnp.concatenate([a, z], axis=-1), jnp.concatenate([z, b], axis=-1)], axis=-2)

    c_pos = jnp.concatenate([cmp_pos[:, 0], cmp_pos[:, 1]], axis=-1)
    c_w1 = cmp_w1.reshape(depth, 2, NSA_BLOCK, HEAD_DIM, HEAD_DIM).astype(BF16)
    c_w1 = block_diag(c_w1[:, 0], c_w1[:, 1])
    c_w2 = block_diag(cmp_w2[:, 0], cmp_w2[:, 1]).astype(BF16)
    sgu_bias = jnp.repeat(jnp.swapaxes(sgu_b, 1, 2), SGU_CH // SGU_GROUPS, axis=2)
    sgu_diag = jnp.repeat(sgu_w[:, :, 0, 0], SGU_CH // SGU_GROUPS, axis=1)[:, None, :]
    sgu_bias0 = sgu_bias[:, 0:1, :]
    vec = lambda a, l: a[l][None, :]
    cache_nsa_t = jnp.transpose(cache_nsa, (0, 1, 3, 4, 2))
    cache_dsa_t = jnp.transpose(cache_dsa, (0, 1, 3, 4, 2))
    state_win_t = jnp.transpose(state_win, (0, 1, 3, 4, 2))
    conv_hist = jnp.transpose(state_conv, (1, 2, 0, 3))

    n_chunks_p = seq // CK
    n_chunks_s = past // CK + 1
    nq = seq // Q_BLOCK

    def chunks_of(arr, slot, n_chunks):
        return arr, pl.BlockSpec((n_chunks, HEAD_DIM, CK), lambda b, i: (b, slot, 0))

    def rows_of(arr, slot, n_rows):
        return arr, pl.BlockSpec((None, n_rows, HEAD_DIM), lambda b, i: (slot, b, 0))

    xp = x_prompt.reshape(n_tok, d)
    xs = x_sample.reshape(nseq, d)
    outs = {k: [] for k in ("nsa_p", "nsa_s", "dsa_p", "dsa_s", "win_p", "win_s", "conv_p", "conv_s", "sgu_s")}
    yp = ys = None
    for l in range(depth):
        final = l == depth - 1
        qn, kva, glu, uv, qd, qi, misc, dsa_rows, kt4, v4 = _proj(
            xp, vec(norm1_g, l), w_proj[l], w_proj_t[l], tm=CK, emit_t=True)
        kva3 = kva.reshape(nb_, seq, 6, HEAD_DIM)
        kvc = _compress(kva, c_pos[l], c_w1[l], c_w2[l], rows_per_step=seq)
        kvc = jnp.pad(kvc.reshape(nb_, nbp, 2 * HEAD_DIM), ((0, 0), (0, NB_PAD - nbp), (0, 0)))
        o_a = _nsa(qn, misc, kvc, chunks_of(kt4, 0, n_chunks_p), rows_of(v4, 0, seq),
                   chunks_of(kt4, 1, n_chunks_p), rows_of(v4, 1, seq),
                   nbatch=nb_, nq=nq, tq=Q_BLOCK, pos0=0, pos_stride=1, win_base=0,
                   n_sel=min(NSA_N_SEL, seq // NSA_BLOCK), v_t=False)
        o_d = _dsa(qd, qi, misc, chunks_of(kt4, 2, n_chunks_p), rows_of(v4, 2, seq),
                   chunks_of(kt4, 3, n_chunks_p), nbatch=nb_, nq=nq, tq=Q_BLOCK,
                   n_keep=min(DSA_TOPK, seq // 4), n_chunks=n_chunks_p)
        o_b, conv_tail = _conv(glu, conv_w[l], vec(conv_b, l), vec(conv_ln_g, l), vec(conv_ln_b, l),
                               nbatch=nb_, seq=seq, tc=CK)
        o_c = _sgu(uv, vec(sgu_ln_g, l), vec(sgu_ln_b, l), sgu_w[l], sgu_bias[l], ts=CK)
        xp = _merge(xp, vec(norm1_g, l), o_a, o_b, o_c, o_d, wg[l], wb[l], wo[l], tm=256)
        res = _mlp(xp, vec(norm2_g, l), wu[l], wd[l], final_g[None, :], tm=256, final=final)
        xp = res[0]
        if final:
            yp = res[1]
        outs["nsa_p"].append(kva3[:, :, :4])
        outs["dsa_p"].append(dsa_rows.reshape(nb_, seq, 3, HEAD_DIM))
        outs["win_p"].append(kva3[:, seq - min(NSA_WINDOW, seq):, 4:])
        outs["conv_p"].append(conv_tail[:, CONV_HALO - (CONV_WIDTH - 1):])

        qn, kva, glu, uv, qd, qi, misc, dsa_rows = _proj(
            xs, vec(norm1_g, l), w_proj[l], w_proj_t[l], tm=nseq, emit_t=False)
        cmp_past, kst, vst, kwt, vwt, kdt, vdt, kit = _assemble(
            page_table, cache_nsa_t, cache_dsa_t, state_win_t, kva[:, :, None], dsa_rows[:, :, None], layer=l)
        nbs = past // NSA_BLOCK
        kvc = _compress(cmp_past.reshape(nseq * past, 2 * HEAD_DIM), c_pos[l], c_w1[l], c_w2[l],
                        rows_per_step=min(8, nseq) * past)
        kvc = jnp.pad(kvc.reshape(nseq, nbs, 2 * HEAD_DIM), ((0, 0), (0, NB_PAD - nbs), (0, 0)))

        def pad_rows(a):
            a = a[..., :, None, :]
            widths = [(0, 0)] * (a.ndim - 2) + [(0, sr - 1), (0, 0)]
            a = jnp.pad(a, widths)
            return a.reshape(a.shape[:-3] + (nseq * sr, a.shape[-1]))

        misc_r = pad_rows(misc)
        o_a = _nsa(pad_rows(qn), misc_r, kvc,
                   (kst, _seq_spec(n_chunks_s, HEAD_DIM, CK)), (vst, _seq_spec(n_chunks_s, HEAD_DIM, CK)),
                   (kwt, _seq_spec(2, HEAD_DIM, CK)), (vwt, _seq_spec(2, HEAD_DIM, CK)),
                   nbatch=nseq, nq=1, tq=sr, pos0=past, pos_stride=0, win_base=past // CK - 1,
                   n_sel=min(NSA_N_SEL, nbs + 1), v_t=True)
        o_d = _dsa_step(pad_rows(qd), pad_rows(qi), misc_r, kdt, vdt, kit, nseq=nseq, tq=sr, pos=past,
                        n_keep=min(DSA_TOPK, (past + 1) // 4), n_chunks=n_chunks_s)
        o_a = o_a.reshape(nseq, sr, -1)[:, 0]
        o_d = o_d.reshape(nseq, sr, -1)[:, 0]
        o_b, o_c, hx_new, v_rows = _step_mix(
            glu, uv, conv_hist[l], conv_w[l], vec(conv_b, l), vec(conv_ln_g, l), vec(conv_ln_b, l),
            vec(sgu_ln_g, l), vec(sgu_ln_b, l), sgu_diag[l], sgu_bias0[l])
        xs = _merge(xs, vec(norm1_g, l), o_a, o_b, o_c, o_d, wg[l], wb[l], wo[l], tm=nseq)
        res = _mlp(xs, vec(norm2_g, l), wu[l], wd[l], final_g[None, :], tm=nseq, final=final)
        xs = res[0]
        if final:
            ys = res[1]
        kva3 = kva.reshape(nseq, 1, 6, HEAD_DIM)
        outs["nsa_s"].append(kva3[:, :, :4])
        outs["dsa_s"].append(dsa_rows.reshape(nseq, 1, 3, HEAD_DIM))
        outs["win_s"].append(jnp.concatenate([state_win[:, l], kva3[:, :, 4:]], axis=1)[:, 1:])
        outs["conv_s"].append(jnp.concatenate([state_conv[:, l], hx_new[:, None, :]], axis=1)[:, 1:])
        outs["sgu_s"].append(v_rows[:, None, :])

    st = lambda k: jnp.stack(outs[k], axis=1)
    return (yp.reshape(nb_, seq, d), ys.reshape(nseq, 1, d), st("nsa_p"), st("nsa_s"), st("dsa_p"), st("dsa_s"),
            st("win_p"), st("win_s"), st("conv_p"), st("conv_s"), st("sgu_s"))
```

```python
import functools

import numpy as np
import jax
import jax.numpy as jnp
from jax import lax
from jax.experimental import pallas as pl
from jax.experimental.pallas import tpu as pltpu

F32, BF16, I32 = jnp.float32, jnp.bfloat16, jnp.int32

HEAD_DIM = 64
NSA_HEADS = 4
NSA_BLOCK = 64
NSA_N_SEL = 16
NSA_WINDOW = 512
FORCE_BONUS = 1.0e4
CONV_CH = 256
CONV_WIDTH = 31
SGU_CH = 256
SGU_GROUPS = 4
SGU_CHUNK = 128
DSA_HEADS = 4
IDX_HEADS = 4
DSA_TOPK = 256
N_BRANCH = 4
BRANCH_WIDTH = 256
Q_BLOCK = 128
EPS = 1e-6
NEG = -1e30

SPLIT_WIDTHS = (256, 384, 12, 512, 512, 256, 128, 256, 64, 4)

CK = 512
NB_PAD = 128
SAMPLE_ROWS = 16
VMEM_LIMIT = 56 * 1024 * 1024
INT_MIN = -2 ** 31

W_QA, W_KVA, W_GLU, W_UV, W_QD, W_QI, W_MISC, W_DSA = 0, 256, 640, 1152, 1664, 1920, 2176, 2304
W_TOTAL = 2496
MISC_WI, MISC_GA = 0, 4


def _cparams(n_axes):
    return pltpu.CompilerParams(dimension_semantics=("arbitrary",) * n_axes,
                                vmem_limit_bytes=VMEM_LIMIT)


def _dot(a, b):
    return jnp.dot(a, b, preferred_element_type=F32)


def _dot_nt(a, b):
    return lax.dot_general(a, b, (((1,), (1,)), ((), ())), preferred_element_type=F32)


def _rms(x, g):
    return x * lax.rsqrt(jnp.mean(x * x, axis=-1, keepdims=True) + EPS) * g


def _layernorm(x, g, b):
    mu = jnp.mean(x, axis=-1, keepdims=True)
    xc = x - mu
    var = jnp.mean(xc * xc, axis=-1, keepdims=True)
    return xc * lax.rsqrt(var + EPS) * g + b


def _sigmoid(x):
    return 1.0 / (1.0 + jnp.exp(-x))


def _tile4(a):
    return jnp.concatenate([a, a, a, a], axis=0)


def _proj_kernel(x_ref, g_ref, w_ref, wt_ref, qn_ref, kva_ref, glu_ref, uv_ref, qd_ref, qi_ref,
                 misc_ref, dsa_ref, *t_refs):
    hb = _rms(x_ref[...], g_ref[...]).astype(BF16)

    def slab(off, width):
        return _dot(hb, w_ref[:, off:off + width])

    qa = slab(W_QA, 256)
    qd = slab(W_QD, 256)
    qi = slab(W_QI, 256)
    for h in range(4):
        cols = slice(h * HEAD_DIM, (h + 1) * HEAD_DIM)
        qn_ref[h] = qa[:, cols].astype(BF16)
        qd_ref[h] = qd[:, cols].astype(BF16)
        qi_ref[h] = qi[:, cols].astype(BF16)
    kv = slab(W_KVA, 384)
    kva_ref[...] = kv
    glu_ref[...] = slab(W_GLU, 512)
    uv_ref[...] = slab(W_UV, 512)
    misc_ref[...] = slab(W_MISC, 128)
    dr = slab(W_DSA, 192)
    dsa_ref[...] = dr
    if t_refs:
        kt_ref, v_ref = t_refs
        kt_ref[0] = _dot_nt(wt_ref[...], hb).astype(BF16)
        v_ref[0] = kv[:, 192:256].astype(BF16)
        v_ref[1] = kv[:, 320:384].astype(BF16)
        v_ref[2] = dr[:, 64:128].astype(BF16)


def _proj(x, g, w, wt, *, tm, emit_t):
    n, d = x.shape
    sds = jax.ShapeDtypeStruct
    out_shape = [sds((4, n, HEAD_DIM), BF16), sds((n, 384), F32), sds((n, 512), F32), sds((n, 512), F32),
                 sds((4, n, HEAD_DIM), BF16), sds((4, n, HEAD_DIM), BF16), sds((n, 128), F32), sds((n, 192), F32)]
    head_spec = pl.BlockSpec((4, tm, HEAD_DIM), lambda i: (0, i, 0))

    def row_spec(width):
        return pl.BlockSpec((tm, width), lambda i: (i, 0))

    out_specs = [head_spec, row_spec(384), row_spec(512), row_spec(512), head_spec, head_spec,
                 row_spec(128), row_spec(192)]
    if emit_t:
        assert tm == CK
        out_shape += [sds((n // CK, 256, CK), BF16), sds((3, n, HEAD_DIM), BF16)]
        out_specs += [pl.BlockSpec((1, 256, CK), lambda i: (i, 0, 0)),
                      pl.BlockSpec((3, tm, HEAD_DIM), lambda i: (0, i, 0))]
    return pl.pallas_call(
        _proj_kernel,
        out_shape=out_shape,
        grid=(n // tm,),
        in_specs=[row_spec(d), pl.BlockSpec((1, d), lambda i: (0, 0)),
                  pl.BlockSpec((d, W_TOTAL), lambda i: (0, 0)),
                  pl.BlockSpec((256, d), lambda i: (0, 0))],
        out_specs=out_specs,
        compiler_params=_cparams(1),
        name="proj",
    )(x, g, w, wt)


def _compress_kernel(x_ref, pos_ref, w1_ref, w2_ref, o_ref):
    n_blocks = o_ref.shape[0]
    acc = jnp.zeros((n_blocks, 2 * HEAD_DIM), F32)
    for r in range(NSA_BLOCK):
        xr = x_ref[pl.ds(r, n_blocks, stride=NSA_BLOCK), :] + pos_ref[r:r + 1, :]
        acc = acc + _dot(xr.astype(BF16), w1_ref[r])
    a = acc * _sigmoid(acc)
    o_ref[...] = _dot(a.astype(BF16), w2_ref[...]).astype(o_ref.dtype)


def _compress(x, pos, w1, w2, *, rows_per_step):
    r = x.shape[0]
    width = 2 * HEAD_DIM
    return pl.pallas_call(
        _compress_kernel,
        out_shape=jax.ShapeDtypeStruct((r // NSA_BLOCK, width), BF16),
        grid=(r // rows_per_step,),
        in_specs=[pl.BlockSpec((rows_per_step, width), lambda i: (i, 0)),
                  pl.BlockSpec((NSA_BLOCK, width), lambda i: (0, 0)),
                  pl.BlockSpec((NSA_BLOCK, width, width), lambda i: (0, 0, 0)),
                  pl.BlockSpec((width, width), lambda i: (0, 0))],
        out_specs=pl.BlockSpec((rows_per_step // NSA_BLOCK, width), lambda i: (i, 0)),
        compiler_params=_cparams(1),
        name="compress",
    )(x, pos, w1, w2)


def _masked_probs(s, mask):
    s = jnp.where(mask, s, NEG)
    m = jnp.max(s, axis=-1, keepdims=True)
    p = jnp.where(mask, jnp.exp(s - m), 0.0)
    return p / jnp.maximum(jnp.sum(p, axis=-1, keepdims=True), 1e-30)


def _softmax_step(s, bias, v, carry, v_t=False):
    m, l, acc = carry
    tq = bias.shape[0]
    s = s.reshape(4, tq, CK) + bias[None]
    m_new = jnp.maximum(m, jnp.max(s, axis=-1, keepdims=True))
    alpha = jnp.exp(m - m_new)
    p = jnp.exp(s - m_new)
    l = alpha * l + jnp.sum(p, axis=-1, keepdims=True)
    pb = p.reshape(4 * tq, CK).astype(BF16)
    acc = alpha.reshape(4 * tq, 1) * acc + (_dot_nt(pb, v) if v_t else _dot(pb, v))
    return m_new, l, acc


def _softmax_init(tq):
    return (jnp.full((4, tq, 1), NEG, F32), jnp.zeros((4, tq, 1), F32), jnp.zeros((4 * tq, HEAD_DIM), F32))


def _softmax_done(carry):
    _, l, acc = carry
    return acc / jnp.maximum(l.reshape(acc.shape[0], 1), 1e-30)


def _softmax_all(chunks, tq, v_t):
    ss = [s.reshape(4, tq, CK) + bias[None] for s, bias, _ in chunks]
    m = functools.reduce(jnp.maximum, [jnp.max(s, axis=-1, keepdims=True) for s in ss])
    ps = [jnp.exp(s - m) for s in ss]
    l = functools.reduce(jnp.add, [jnp.sum(p, axis=-1, keepdims=True) for p in ps])
    pv = _dot_nt if v_t else _dot
    acc = functools.reduce(jnp.add, [pv(p.reshape(4 * tq, CK).astype(BF16), v) for p, (_, _, v) in zip(ps, chunks)])
    return acc / jnp.maximum(l.reshape(4 * tq, 1), 1e-30)


def _mask_bias(allowed):
    return jnp.where(allowed, 0.0, NEG)


def _query_positions(tq, pos0, pos_stride):
    if pos_stride == 0:
        return pos0, pos0, jnp.full((tq, 1), pos0, I32)
    i = pl.program_id(1)
    start = pos0 + i * (tq * pos_stride)
    last = start + (tq - 1) * pos_stride
    t = lax.broadcasted_iota(I32, (tq, 1), 0)
    qpos = start + t * pos_stride
    return start, last, qpos


def _heads_as_rows(ref, scale):
    return jnp.concatenate([ref[h] for h in range(4)], axis=0) * scale


def _key_positions(c):
    return c * CK + lax.broadcasted_iota(I32, (1, CK), 1)


def _chunk_rows(ref, c):
    if isinstance(c, int):
        return ref[c * CK:(c + 1) * CK, :]
    return ref[pl.ds(pl.multiple_of(c * CK, CK), CK), :]


PICK_LANES = 128


def _nsa_kernel(q_ref, misc_ref, kvc_ref, kst_ref, vs_ref, kwt_ref, vw_ref, o_ref, *,
                tq, pos0, pos_stride, win_base, n_sel, v_t):
    values = (lambda ref, c: ref[c]) if v_t else _chunk_rows
    nb = kvc_ref.shape[0]
    start, last, qpos = _query_positions(tq, pos0, pos_stride)
    qpos4 = _tile4(qpos)
    q = _heads_as_rows(q_ref, HEAD_DIM ** -0.5)

    jb = lax.broadcasted_iota(I32, (1, nb), 1)
    mask_c = ((jb + 1) * NSA_BLOCK - 1) <= qpos4
    p_c = _masked_probs(_dot_nt(q, kvc_ref[:, 0:HEAD_DIM]), mask_c)
    o_c = _dot(p_c.astype(BF16), kvc_ref[:, HEAD_DIM:2 * HEAD_DIM])
    imp = p_c[0:tq] + p_c[tq:2 * tq] + p_c[2 * tq:3 * tq] + p_c[3 * tq:4 * tq]

    cur = qpos // NSA_BLOCK
    forced = (jb == 0) | (jb == cur) | (jb == cur - 1)
    score = jnp.where(jb <= cur, imp, -1.0) + jnp.where(forced, FORCE_BONUS, 0.0)
    if tq < PICK_LANES:
        score = jnp.concatenate([score, jnp.zeros((PICK_LANES - tq, nb), F32)], axis=0)
    jcol = lax.broadcasted_iota(I32, (nb, PICK_LANES), 0).astype(F32)

    def pick(_, carry):
        sc, sel = carry
        m = jnp.max(sc, axis=0, keepdims=True)
        first = jnp.min(jnp.where(sc == m, jcol, 1e9), axis=0, keepdims=True)
        hit = jcol == first
        return jnp.where(hit, -3e38, sc), jnp.where(hit, 1.0, sel)

    _, sel_t = lax.fori_loop(0, n_sel, pick, (score.T, jnp.zeros((nb, PICK_LANES), F32)))
    sel = sel_t.T[0:tq].astype(BF16)

    jrow = lax.broadcasted_iota(I32, (nb, CK), 0)
    kcol = lax.broadcasted_iota(I32, (nb, CK), 1)

    def sel_parts(c):
        expand = jnp.where(jrow == ((c * CK + kcol) // NSA_BLOCK), 1.0, 0.0).astype(BF16)
        bias = _mask_bias((_dot(sel, expand) > 0.5) & (_key_positions(c) <= qpos))
        return _dot(q, kst_ref[c]), bias, values(vs_ref, c)

    def win_parts(c):
        cc = max(c - win_base, 0) if isinstance(c, int) else jnp.maximum(c - win_base, 0)
        kpos = _key_positions(c)
        dpos = qpos - kpos
        bias = _mask_bias((dpos >= 0) & (dpos < NSA_WINDOW) & (kpos >= 0))
        return _dot(q, kwt_ref[cc]), bias, values(vw_ref, cc)

    n_chunks = last // CK + 1
    c1 = start // CK
    if isinstance(n_chunks, int):
        o_s = _softmax_all([sel_parts(c) for c in range(n_chunks)], tq, v_t)
        o_w = _softmax_all([win_parts(c1 - 1), win_parts(c1)], tq, v_t)
    else:
        def sel_chunk(c, carry):
            return _softmax_step(*sel_parts(c), carry, v_t)

        o_s = _softmax_done(_chunk_loop(n_chunks, sel_chunk, _softmax_init(tq), unroll=ATTEND_UNROLL))
        carry = _softmax_init(tq)
        for c in (c1 - 1, c1):
            carry = _softmax_step(*win_parts(c), carry, v_t)
        o_w = _softmax_done(carry)

    g = _sigmoid(misc_ref[...])
    outs = []
    for h in range(4):
        rows = slice(h * tq, (h + 1) * tq)
        col = MISC_GA + 3 * h
        outs.append(g[:, col:col + 1] * o_c[rows] + g[:, col + 1:col + 2] * o_s[rows]
                    + g[:, col + 2:col + 3] * o_w[rows])
    o_ref[...] = jnp.concatenate(outs, axis=-1).astype(o_ref.dtype)


def _nsa(q, misc, kvc, ks, vs, kw, vw, *, nbatch, nq, tq, pos0, pos_stride, win_base, n_sel, v_t):
    nrows = q.shape[1]
    kern = functools.partial(_nsa_kernel, tq=tq, pos0=pos0, pos_stride=pos_stride, win_base=win_base,
                             n_sel=n_sel, v_t=v_t)
    return pl.pallas_call(
        kern,
        out_shape=jax.ShapeDtypeStruct((nrows, 4 * HEAD_DIM), BF16),
        grid=(nbatch, nq),
        in_specs=[pl.BlockSpec((4, tq, HEAD_DIM), lambda b, i: (0, b * nq + i, 0)),
                  pl.BlockSpec((tq, 128), lambda b, i: (b * nq + i, 0)),
                  pl.BlockSpec((None, NB_PAD, 2 * HEAD_DIM), lambda b, i: (b, 0, 0)),
                  ks[1], vs[1], kw[1], vw[1]],
        out_specs=pl.BlockSpec((tq, 4 * HEAD_DIM), lambda b, i: (b * nq + i, 0)),
        compiler_params=_cparams(2),
        name="nsa",
    )(q, misc, kvc, ks[0], vs[0], kw[0], vw[0])


LANES = 128
COUNT_UNROLL = 4
ATTEND_UNROLL = 4
INDEX_UNROLL = 4


def _index_keys(qi, wcol, kit, kpos, qpos, tq):
    rel = jnp.maximum(_dot(qi, kit), 0.0) * wcol
    sc = rel[0:tq] + rel[tq:2 * tq] + rel[2 * tq:3 * tq] + rel[3 * tq:4 * tq]
    sc = jnp.where(sc == 0.0, 0.0, sc)
    sc = jnp.where(kpos <= qpos, sc, NEG)
    bits = lax.bitcast_convert_type(sc, I32)
    return jnp.where(bits < 0, bits ^ 0x7FFFFFFF, bits)


def _index_weights(misc_ref):
    w = misc_ref[:, MISC_WI:MISC_WI + 4] * (IDX_HEADS ** -0.5 * HEAD_DIM ** -0.5)
    return jnp.concatenate([w[:, h:h + 1] for h in range(4)], axis=0)


def _chunk_loop(n_chunks, body, init, unroll=COUNT_UNROLL):
    if isinstance(n_chunks, int):
        carry = init
        for c in range(n_chunks):
            carry = body(c, carry)
        return carry

    def several(g, carry):
        for u in range(unroll):
            carry = body(g * unroll + u, carry)
        return carry

    n_groups = n_chunks // unroll
    carry = lax.fori_loop(0, n_groups, several, init)
    return lax.fori_loop(n_groups * unroll, n_chunks, body, carry)


def _key_counter(key_ref, n_chunks):
    rows = key_ref.shape[1]

    def count(pred):
        def one(c, acc):
            for j in range(CK // LANES):
                k = key_ref[c, :, j * LANES:(j + 1) * LANES]
                acc = acc + jnp.where(pred(k, c * CK + j * LANES), 1.0, 0.0)
            return acc

        acc = _chunk_loop(n_chunks, one, jnp.zeros((rows, LANES), F32))
        return jnp.sum(acc, axis=-1, keepdims=True)

    return count


def _top_keys(count, n_keep):
    keep = float(n_keep)
    thr = jnp.where(count(lambda k, c: k >= 0) >= keep, 0, INT_MIN).astype(I32)

    def thr_bit(b, thr):
        cand = thr | jnp.left_shift(jnp.int32(1), 30 - b)
        return jnp.where(count(lambda k, c: k >= cand) >= keep, cand, thr)

    thr = lax.fori_loop(0, 31, thr_bit, thr)
    return thr, keep - count(lambda k, c: k > thr)


def _tie_cut(count, thr, need, rows, n_positions):
    n_pos_bits = (n_positions - 1).bit_length()

    def cut_bit(b, cut):
        cand = cut + jnp.left_shift(jnp.int32(1), n_pos_bits - 1 - b)
        n = count(lambda k, first: (k == thr) & (first + lax.broadcasted_iota(I32, (1, LANES), 1) < cand))
        return jnp.where(n < need, cand, cut)

    return lax.fori_loop(0, n_pos_bits, cut_bit, jnp.zeros((rows, 1), I32))


def _heads_to_lanes(o, tq):
    return jnp.concatenate([o[h * tq:(h + 1) * tq] for h in range(4)], axis=-1)


def _dsa_kernel(qd_ref, qi_ref, misc_ref, kdt_ref, vd_ref, kit_ref, o_ref, key_scr, *, tq, n_keep):
    start, last, qpos = _query_positions(tq, 0, 1)
    n_chunks = last // CK + 1
    qd = _heads_as_rows(qd_ref, HEAD_DIM ** -0.5)
    qi = _heads_as_rows(qi_ref, 1.0)
    wcol = _index_weights(misc_ref)

    def index_chunk(c, _):
        key_scr[c] = _index_keys(qi, wcol, kit_ref[c], _key_positions(c), qpos, tq)
        return 0

    _chunk_loop(n_chunks, index_chunk, 0, unroll=INDEX_UNROLL)
    thr, need = _top_keys(_key_counter(key_scr, n_chunks), n_keep)

    tri = jnp.where(lax.broadcasted_iota(I32, (LANES, LANES), 0) <= lax.broadcasted_iota(I32, (LANES, LANES), 1),
                    1.0, 0.0).astype(BF16)

    def attend_chunk(c, carry):
        state, seen = carry
        k = key_scr[c]
        ranks = []
        for j in range(CK // LANES):
            equal = jnp.where(k[:, j * LANES:(j + 1) * LANES] == thr, 1.0, 0.0)
            ranks.append(seen + _dot(equal.astype(BF16), tri))
            seen = seen + jnp.sum(equal, axis=-1, keepdims=True)
        rank = jnp.concatenate(ranks, axis=-1)
        taken = (k > thr) | ((k == thr) & (rank <= need))
        bias = _mask_bias(taken & (_key_positions(c) <= qpos))
        state = _softmax_step(_dot(qd, kdt_ref[c]), bias, _chunk_rows(vd_ref, c), state)
        return state, seen

    state, _ = _chunk_loop(n_chunks, attend_chunk, (_softmax_init(tq), jnp.zeros((tq, 1), F32)),
                           unroll=ATTEND_UNROLL)
    o_ref[...] = _heads_to_lanes(_softmax_done(state), tq).astype(o_ref.dtype)


def _dsa(qd, qi, misc, kd, vd, ki, *, nbatch, nq, tq, n_keep, n_chunks):
    nrows = qd.shape[1]
    head_spec = pl.BlockSpec((4, tq, HEAD_DIM), lambda b, i: (0, b * nq + i, 0))
    return pl.pallas_call(
        functools.partial(_dsa_kernel, tq=tq, n_keep=n_keep),
        out_shape=jax.ShapeDtypeStruct((nrows, 4 * HEAD_DIM), BF16),
        grid=(nbatch, nq),
        in_specs=[head_spec, head_spec, pl.BlockSpec((tq, 128), lambda b, i: (b * nq + i, 0)),
                  kd[1], vd[1], ki[1]],
        out_specs=pl.BlockSpec((tq, 4 * HEAD_DIM), lambda b, i: (b * nq + i, 0)),
        scratch_shapes=[pltpu.VMEM((n_chunks, tq, CK), I32)],
        compiler_params=_cparams(2),
        name="dsa",
    )(qd, qi, misc, kd[0], vd[0], ki[0])


def _dsa_step_index_kernel(qi_ref, misc_ref, kit_ref, key_ref, *, tq, pos, n_chunks):
    qi = _heads_as_rows(qi_ref, 1.0)
    wcol = _index_weights(misc_ref)
    for c in range(n_chunks):
        key_ref[c] = _index_keys(qi, wcol, kit_ref[c], _key_positions(c), pos, tq)[0:1, :]


def _dsa_step_top_kernel(key_ref, thr_ref, cut_ref, *, n_keep, n_chunks):
    rows = key_ref.shape[1]
    count = _key_counter(key_ref, n_chunks)
    thr, need = _top_keys(count, n_keep)
    thr_ref[...] = jnp.broadcast_to(thr, thr_ref.shape)
    cut_ref[...] = jnp.broadcast_to(_tie_cut(count, thr, need, rows, n_chunks * CK), cut_ref.shape)


def _dsa_step_attend_kernel(qd_ref, key_ref, thr_ref, cut_ref, kdt_ref, vdt_ref, o_ref, *, tq, pos, n_chunks):
    qd = _heads_as_rows(qd_ref, HEAD_DIM ** -0.5)
    thr = thr_ref[:, 0:1]
    cut = cut_ref[:, 0:1]
    chunks = []
    for c in range(n_chunks):
        k = key_ref[c]
        kpos = _key_positions(c)
        bias = _mask_bias(((k > thr) | ((k == thr) & (kpos <= cut))) & (kpos <= pos))
        chunks.append((_dot(qd, kdt_ref[c]), jnp.broadcast_to(bias, (tq, CK)), vdt_ref[c]))
    o_ref[...] = _heads_to_lanes(_softmax_all(chunks, tq, True), tq).astype(o_ref.dtype)


def _dsa_step(qd, qi, misc, kdt, vd, kit, *, nseq, tq, pos, n_keep, n_chunks):
    head_spec = pl.BlockSpec((4, tq, HEAD_DIM), lambda b: (0, b, 0))
    chunk_spec = pl.BlockSpec((None, n_chunks, HEAD_DIM, CK), lambda b: (b, 0, 0, 0))
    key_spec = pl.BlockSpec((n_chunks, None, 1, CK), lambda b: (0, b, 0, 0))
    row_spec = pl.BlockSpec((None, 1, LANES), lambda b: (b, 0, 0))
    keys = pl.pallas_call(
        functools.partial(_dsa_step_index_kernel, tq=tq, pos=pos, n_chunks=n_chunks),
        out_shape=jax.ShapeDtypeStruct((n_chunks, nseq, 1, CK), I32),
        grid=(nseq,),
        in_specs=[head_spec, pl.BlockSpec((tq, 128), lambda b: (b, 0)), chunk_spec],
        out_specs=key_spec,
        compiler_params=_cparams(1),
        name="dsa_step_index",
    )(qi, misc, kit)
    thr, cut = pl.pallas_call(
        functools.partial(_dsa_step_top_kernel, n_keep=n_keep, n_chunks=n_chunks),
        out_shape=[jax.ShapeDtypeStruct((nseq, LANES), I32)] * 2,
        compiler_params=pltpu.CompilerParams(vmem_limit_bytes=VMEM_LIMIT),
        name="dsa_step_top",
    )(keys.reshape(n_chunks, nseq, CK))
    return pl.pallas_call(
        functools.partial(_dsa_step_attend_kernel, tq=tq, pos=pos, n_chunks=n_chunks),
        out_shape=jax.ShapeDtypeStruct((nseq * tq, 4 * HEAD_DIM), BF16),
        grid=(nseq,),
        in_specs=[head_spec, key_spec, row_spec, row_spec, chunk_spec, chunk_spec],
        out_specs=pl.BlockSpec((tq, 4 * HEAD_DIM), lambda b: (b, 0)),
        compiler_params=_cparams(1),
        name="dsa_step_attend",
    )(qd, keys, thr[:, None, :], cut[:, None, :], kdt, vd)


CONV_HALO = 32


def _conv_kernel(cur_ref, halo_ref, w_ref, b_ref, g_ref, beta_ref, o_ref, tail_ref, hx_scr, *, tc):
    i = pl.program_id(1)

    def glu(x):
        return x[:, :CONV_CH] * _sigmoid(x[:, CONV_CH:])

    hx_scr[0:CONV_HALO] = jnp.where(i > 0, glu(halo_ref[...]), 0.0)
    hx_scr[CONV_HALO:CONV_HALO + tc] = glu(cur_ref[...])
    first = CONV_HALO - (CONV_WIDTH - 1)
    y = jnp.broadcast_to(b_ref[...], (tc, CONV_CH))
    for k in range(CONV_WIDTH):
        y = y + hx_scr[first + k:first + k + tc] * w_ref[k:k + 1, :]
    y = _layernorm(y, g_ref[...], beta_ref[...])
    o_ref[...] = (y * _sigmoid(y)).astype(o_ref.dtype)
    tail_ref[...] = hx_scr[tc:tc + CONV_HALO]


def _conv(glu, w, b, g, beta, *, nbatch, seq, tc):
    n = glu.shape[0]
    nt = seq // tc
    per = tc // CONV_HALO
    vec = pl.BlockSpec((1, CONV_CH), lambda bb, i: (0, 0))
    return pl.pallas_call(
        functools.partial(_conv_kernel, tc=tc),
        out_shape=[jax.ShapeDtypeStruct((n, CONV_CH), BF16),
                   jax.ShapeDtypeStruct((nbatch, CONV_HALO, CONV_CH), F32)],
        grid=(nbatch, nt),
        in_specs=[pl.BlockSpec((tc, 2 * CONV_CH), lambda bb, i: (bb * nt + i, 0)),
                  pl.BlockSpec((CONV_HALO, 2 * CONV_CH),
                               lambda bb, i: (jnp.maximum((bb * nt + i) * per - 1, 0), 0)),
                  pl.BlockSpec((CONV_WIDTH, CONV_CH), lambda bb, i: (0, 0)), vec, vec, vec],
        out_specs=[pl.BlockSpec((tc, CONV_CH), lambda bb, i: (bb * nt + i, 0)),
                   pl.BlockSpec((None, CONV_HALO, CONV_CH), lambda bb, i: (bb, 0, 0))],
        scratch_shapes=[pltpu.VMEM((tc + CONV_HALO, CONV_CH), F32)],
        compiler_params=_cparams(2),
        name="conv",
    )(glu, glu, w, b, g, beta)


def _gelu(x):
    return 0.5 * x * (1.0 + lax.erf(x * (2.0 ** -0.5)))


def _sgu_kernel(uv_ref, g_ref, beta_ref, w_ref, bias_ref, o_ref, *, ts):
    a = _gelu(uv_ref[...])
    u = a[:, :SGU_CH]
    vn = _layernorm(a[:, SGU_CH:], g_ref[...], beta_ref[...]).astype(BF16)
    ri = lax.broadcasted_iota(I32, (SGU_CHUNK, SGU_CHUNK), 0)
    ci = lax.broadcasted_iota(I32, (SGU_CHUNK, SGU_CHUNK), 1)
    group = lax.broadcasted_iota(I32, (SGU_CHUNK, SGU_CH), 1) // (SGU_CH // SGU_GROUPS)
    ws = [jnp.where(ci <= ri, w_ref[gi], 0.0).astype(BF16) for gi in range(SGU_GROUPS)]
    for c in range(ts // SGU_CHUNK):
        rows = slice(c * SGU_CHUNK, (c + 1) * SGU_CHUNK)
        mixed = bias_ref[...]
        for gi in range(SGU_GROUPS):
            mixed = mixed + jnp.where(group == gi, _dot(ws[gi], vn[rows]), 0.0)
        o_ref[rows, :] = (u[rows] * mixed).astype(o_ref.dtype)


def _sgu(uv, g, beta, w, bias, *, ts):
    n = uv.shape[0]
    vec = pl.BlockSpec((1, SGU_CH), lambda i: (0, 0))
    return pl.pallas_call(
        functools.partial(_sgu_kernel, ts=ts),
        out_shape=jax.ShapeDtypeStruct((n, SGU_CH), BF16),
        grid=(n // ts,),
        in_specs=[pl.BlockSpec((ts, 2 * SGU_CH), lambda i: (i, 0)), vec, vec,
                  pl.BlockSpec((SGU_GROUPS, SGU_CHUNK, SGU_CHUNK), lambda i: (0, 0, 0)),
                  pl.BlockSpec((SGU_CHUNK, SGU_CH), lambda i: (0, 0))],
        out_specs=pl.BlockSpec((ts, SGU_CH), lambda i: (i, 0)),
        compiler_params=_cparams(1),
        name="sgu",
    )(uv, g, beta, w, bias)


def _step_mix_kernel(glu_ref, uv_ref, hist_ref, cw_ref, cb_ref, cg_ref, cbeta_ref, sg_ref, sbeta_ref,
                     sdiag_ref, sbias_ref, ob_ref, oc_ref, hx_ref, v_ref):
    x = glu_ref[...]
    hx = x[:, :CONV_CH] * _sigmoid(x[:, CONV_CH:])
    hx_ref[...] = hx
    y = cb_ref[...] + hx * cw_ref[CONV_WIDTH - 1:CONV_WIDTH, :]
    for k in range(CONV_WIDTH - 1):
        y = y + hist_ref[k] * cw_ref[k:k + 1, :]
    y = _layernorm(y, cg_ref[...], cbeta_ref[...])
    ob_ref[...] = (y * _sigmoid(y)).astype(ob_ref.dtype)
    a = _gelu(uv_ref[...])
    v = a[:, SGU_CH:]
    v_ref[...] = v
    vn = _layernorm(v, sg_ref[...], sbeta_ref[...])
    oc_ref[...] = (a[:, :SGU_CH] * (sdiag_ref[...] * vn + sbias_ref[...])).astype(oc_ref.dtype)


def _step_mix(glu, uv, hist, cw, cb, cg, cbeta, sg, sbeta, sdiag, sbias):
    n = glu.shape[0]
    sds = jax.ShapeDtypeStruct
    return pl.pallas_call(
        _step_mix_kernel,
        out_shape=[sds((n, CONV_CH), BF16), sds((n, SGU_CH), BF16), sds((n, CONV_CH), F32), sds((n, SGU_CH), F32)],
        compiler_params=pltpu.CompilerParams(vmem_limit_bytes=VMEM_LIMIT),
        name="step_mix",
    )(glu, uv, hist, cw, cb, cg, cbeta, sg, sbeta, sdiag, sbias)


def _merge_kernel(x_ref, g_ref, oa_ref, ob_ref, oc_ref, od_ref, wg_ref, wb_ref, wo_ref, y_ref):
    x = x_ref[...]
    d = x.shape[-1]
    hb = _rms(x, g_ref[...]).astype(BF16)
    acc = jnp.zeros(x.shape, F32)
    for k, o_ref in enumerate((oa_ref, ob_ref, oc_ref, od_ref)):
        gate = _sigmoid(_dot(hb, wg_ref[:, k * d:(k + 1) * d]))
        acc = acc + gate * _dot(o_ref[...], wb_ref[k])
    y_ref[...] = x + _dot(acc.astype(BF16), wo_ref[...])


def _merge(x, g, oa, ob, oc, od, wg, wb, wo, *, tm):
    n, d = x.shape
    row = pl.BlockSpec((tm, d), lambda i: (i, 0))
    br = pl.BlockSpec((tm, BRANCH_WIDTH), lambda i: (i, 0))
    once = pl.Buffered(1)
    return pl.pallas_call(
        _merge_kernel,
        out_shape=jax.ShapeDtypeStruct((n, d), F32),
        grid=(n // tm,),
        in_specs=[row, pl.BlockSpec((1, d), lambda i: (0, 0)), br, br, br, br,
                  pl.BlockSpec((d, N_BRANCH * d), lambda i: (0, 0), pipeline_mode=once),
                  pl.BlockSpec((N_BRANCH, BRANCH_WIDTH, d), lambda i: (0, 0, 0), pipeline_mode=once),
                  pl.BlockSpec((d, d), lambda i: (0, 0), pipeline_mode=once)],
        out_specs=row,
        compiler_params=_cparams(1),
        name="merge",
    )(x, g, oa, ob, oc, od, wg, wb, wo)


def _mlp_kernel(x_ref, g_ref, wu_ref, wd_ref, gf_ref, y_ref, *n_ref):
    x = x_ref[...]
    hb = _rms(x, g_ref[...]).astype(BF16)
    a = jnp.square(jnp.maximum(_dot(hb, wu_ref[...]), 0.0)).astype(BF16)
    y = x + _dot(a, wd_ref[...])
    y_ref[...] = y
    if n_ref:
        n_ref[0][...] = _rms(y, gf_ref[...])


def _mlp(x, g, wu, wd, gf, *, tm, final):
    n, d = x.shape
    row = pl.BlockSpec((tm, d), lambda i: (i, 0))
    vec = pl.BlockSpec((1, d), lambda i: (0, 0))
    once = pl.Buffered(1)
    out_shape = [jax.ShapeDtypeStruct((n, d), F32)] * (2 if final else 1)
    return pl.pallas_call(
        _mlp_kernel,
        out_shape=out_shape,
        grid=(n // tm,),
        in_specs=[row, vec, pl.BlockSpec(wu.shape, lambda i: (0, 0), pipeline_mode=once),
                  pl.BlockSpec(wd.shape, lambda i: (0, 0), pipeline_mode=once), vec],
        out_specs=[row] * (2 if final else 1),
        compiler_params=_cparams(1),
        name="mlp",
    )(x, g, wu, wd, gf)


def _assemble_kernel(pt_ref, *refs, n_pages, page):
    nsa_pages = refs[:n_pages]
    dsa_pages = refs[n_pages:2 * n_pages]
    win_ref, nsa_new_ref, dsa_new_ref = refs[2 * n_pages:2 * n_pages + 3]
    cmp_ref, kst_ref, vst_ref, kwt_ref, vwt_ref, kdt_ref, vdt_ref, kit_ref = refs[2 * n_pages + 3:]
    per_chunk = CK // page
    for p in range(n_pages):
        c, r = divmod(p, per_chunk)
        lanes = slice(r * page, (r + 1) * page)
        x = nsa_pages[p]
        cmp_ref[p * page:(p + 1) * page, :] = jnp.concatenate([x[0], x[1]], axis=0).T
        kst_ref[c, :, lanes] = x[2].astype(BF16)
        vst_ref[c, :, lanes] = x[3].astype(BF16)
        y = dsa_pages[p]
        kdt_ref[c, :, lanes] = y[0].astype(BF16)
        vdt_ref[c, :, lanes] = y[1].astype(BF16)
        kit_ref[c, :, lanes] = y[2].astype(BF16)

    first = lax.broadcasted_iota(I32, (HEAD_DIM, CK), 1) == 0

    def new_chunk(col):
        return jnp.where(first, jnp.broadcast_to(col, (HEAD_DIM, CK)), 0.0).astype(BF16)

    c_new = n_pages // per_chunk
    nsa_new = nsa_new_ref[...]
    dsa_new = dsa_new_ref[...]
    kst_ref[c_new] = new_chunk(nsa_new[2 * HEAD_DIM:3 * HEAD_DIM])
    vst_ref[c_new] = new_chunk(nsa_new[3 * HEAD_DIM:4 * HEAD_DIM])
    kdt_ref[c_new] = new_chunk(dsa_new[0:HEAD_DIM])
    vdt_ref[c_new] = new_chunk(dsa_new[HEAD_DIM:2 * HEAD_DIM])
    kit_ref[c_new] = new_chunk(dsa_new[2 * HEAD_DIM:3 * HEAD_DIM])
    kwt_ref[0] = win_ref[0].astype(BF16)
    vwt_ref[0] = win_ref[1].astype(BF16)
    kwt_ref[1] = new_chunk(nsa_new[4 * HEAD_DIM:5 * HEAD_DIM])
    vwt_ref[1] = new_chunk(nsa_new[5 * HEAD_DIM:6 * HEAD_DIM])


def _assemble(page_table, cache_nsa_t, cache_dsa_t, state_win_t, nsa_new, dsa_new, *, layer):
    nseq, n_pages = page_table.shape
    page = cache_nsa_t.shape[-1]
    n_chunks = n_pages * page // CK + 1
    sds = jax.ShapeDtypeStruct

    def page_spec(slots, p):
        return pl.BlockSpec((None, None, slots, HEAD_DIM, page), lambda b, pt: (pt[b, p], layer, 0, 0, 0))

    def per_seq(*shape):
        return pl.BlockSpec((None,) + shape, lambda b, pt: (b,) + (0,) * len(shape))

    chunks = per_seq(n_chunks, HEAD_DIM, CK)
    grid_spec = pltpu.PrefetchScalarGridSpec(
        num_scalar_prefetch=1,
        grid=(nseq,),
        in_specs=([page_spec(4, p) for p in range(n_pages)] + [page_spec(3, p) for p in range(n_pages)]
                  + [pl.BlockSpec((None, None, 2, HEAD_DIM, CK), lambda b, pt: (b, layer, 0, 0, 0)),
                     per_seq(6 * HEAD_DIM, 1), per_seq(3 * HEAD_DIM, 1)]),
        out_specs=[per_seq(n_pages * page, 2 * HEAD_DIM), chunks, chunks, per_seq(2, HEAD_DIM, CK),
                   per_seq(2, HEAD_DIM, CK), chunks, chunks, chunks],
    )
    chunk_shape = sds((nseq, n_chunks, HEAD_DIM, CK), BF16)
    win_shape = sds((nseq, 2, HEAD_DIM, CK), BF16)
    return pl.pallas_call(
        functools.partial(_assemble_kernel, n_pages=n_pages, page=page),
        out_shape=[sds((nseq, n_pages * page, 2 * HEAD_DIM), F32), chunk_shape, chunk_shape, win_shape, win_shape,
                   chunk_shape, chunk_shape, chunk_shape],
        grid_spec=grid_spec,
        compiler_params=_cparams(1),
        name="assemble",
    )(page_table, *([cache_nsa_t] * n_pages), *([cache_dsa_t] * n_pages), state_win_t, nsa_new, dsa_new)


def _prep_w_in(w_in):
    pts = np.cumsum(SPLIT_WIDTHS)[:-1].tolist()
    qa, kva, ga, glu, uv, qd, kvd, qi, ki, wi = jnp.split(w_in, pts, axis=-1)
    pad = jnp.zeros(w_in.shape[:2] + (128 - 16,), w_in.dtype)
    w = jnp.concatenate([qa, kva, glu, uv, qd, qi, wi, ga, pad, kvd, ki], axis=-1).astype(BF16)
    wt = jnp.concatenate([kva[..., 128:192], kva[..., 256:320], kvd[..., 0:64], ki], axis=-1)
    return w, jnp.swapaxes(wt, 1, 2).astype(BF16)


def _seq_spec(*shape):
    return pl.BlockSpec((None,) + shape, lambda b, i: (b,) + (0,) * len(shape))


def kernel(x_prompt, x_sample, cache_nsa, cache_dsa, state_win, state_conv, page_table, norm1_g, norm2_g,
           final_g, w_in, cmp_pos, cmp_w1, cmp_w2, conv_w, conv_b, conv_ln_g, conv_ln_b, sgu_ln_g, sgu_ln_b,
           sgu_w, sgu_b, w_branch, w_gate, w_out, w_up, w_down):
    nb_, seq, d = x_prompt.shape
    nseq, t_dec, _ = x_sample.shape
    depth = w_in.shape[0]
    n_pool, _, page, _, _ = cache_nsa.shape
    n_pages = page_table.shape[1]
    past = n_pages * page
    w_buf = state_win.shape[2]
    assert t_dec == 1 and seq % CK == 0 and past % CK == 0 and w_buf == CK == NSA_WINDOW
    nbp = seq // NSA_BLOCK
    assert nbp <= NB_PAD and past // NSA_BLOCK < NB_PAD
    n_tok = nb_ * seq
    sr = SAMPLE_ROWS

    w_proj, w_proj_t = _prep_w_in(w_in)
    wg, wb, wo = w_gate.astype(BF16), w_branch.astype(BF16), w_out.astype(BF16)
    wu, wd = w_up.astype(BF16), w_down.astype(BF16)
    def block_diag(a, b):
        z = jnp.zeros_like(a)
        return jnp.concatenate([jnp.concatenate([a, z], axis=-1), jnp.concatenate([z, b], axis=-1)], axis=-2)

    c_pos = jnp.concatenate([cmp_pos[:, 0], cmp_pos[:, 1]], axis=-1)
    c_w1 = cmp_w1.reshape(depth, 2, NSA_BLOCK, HEAD_DIM, HEAD_DIM).astype(BF16)
    c_w1 = block_diag(c_w1[:, 0], c_w1[:, 1])
    c_w2 = block_diag(cmp_w2[:, 0], cmp_w2[:, 1]).astype(BF16)
    sgu_bias = jnp.repeat(jnp.swapaxes(sgu_b, 1, 2), SGU_CH // SGU_GROUPS, axis=2)
    sgu_diag = jnp.repeat(sgu_w[:, :, 0, 0], SGU_CH // SGU_GROUPS, axis=1)[:, None, :]
    sgu_bias0 = sgu_bias[:, 0:1, :]
    vec = lambda a, l: a[l][None, :]
    cache_nsa_t = jnp.transpose(cache_nsa, (0, 1, 3, 4, 2))
    cache_dsa_t = jnp.transpose(cache_dsa, (0, 1, 3, 4, 2))
    state_win_t = jnp.transpose(state_win, (0, 1, 3, 4, 2))
    conv_hist = jnp.transpose(state_conv, (1, 2, 0, 3))

    n_chunks_p = seq // CK
    n_chunks_s = past // CK + 1
    nq = seq // Q_BLOCK

    def chunks_of(arr, slot, n_chunks):
        return arr, pl.BlockSpec((n_chunks, HEAD_DIM, CK), lambda b, i: (b, slot, 0))

    def rows_of(arr, slot, n_rows):
        return arr, pl.BlockSpec((None, n_rows, HEAD_DIM), lambda b, i: (slot, b, 0))

    xp = x_prompt.reshape(n_tok, d)
    xs = x_sample.reshape(nseq, d)
    outs = {k: [] for k in ("nsa_p", "nsa_s", "dsa_p", "dsa_s", "win_p", "win_s", "conv_p", "conv_s", "sgu_s")}
    yp = ys = None
    for l in range(depth):
        final = l == depth - 1
        qn, kva, glu, uv, qd, qi, misc, dsa_rows, kt4, v4 = _proj(
            xp, vec(norm1_g, l), w_proj[l], w_proj_t[l], tm=CK, emit_t=True)
        kva3 = kva.reshape(nb_, seq, 6, HEAD_DIM)
        kvc = _compress(kva, c_pos[l], c_w1[l], c_w2[l], rows_per_step=seq)
        kvc = jnp.pad(kvc.reshape(nb_, nbp, 2 * HEAD_DIM), ((0, 0), (0, NB_PAD - nbp), (0, 0)))
        o_a = _nsa(qn, misc, kvc, chunks_of(kt4, 0, n_chunks_p), rows_of(v4, 0, seq),
                   chunks_of(kt4, 1, n_chunks_p), rows_of(v4, 1, seq),
                   nbatch=nb_, nq=nq, tq=Q_BLOCK, pos0=0, pos_stride=1, win_base=0,
                   n_sel=min(NSA_N_SEL, seq // NSA_BLOCK), v_t=False)
        o_d = _dsa(qd, qi, misc, chunks_of(kt4, 2, n_chunks_p), rows_of(v4, 2, seq),
                   chunks_of(kt4, 3, n_chunks_p), nbatch=nb_, nq=nq, tq=Q_BLOCK,
                   n_keep=min(DSA_TOPK, seq // 4), n_chunks=n_chunks_p)
        o_b, conv_tail = _conv(glu, conv_w[l], vec(conv_b, l), vec(conv_ln_g, l), vec(conv_ln_b, l),
                               nbatch=nb_, seq=seq, tc=CK)
        o_c = _sgu(uv, vec(sgu_ln_g, l), vec(sgu_ln_b, l), sgu_w[l], sgu_bias[l], ts=CK)
        xp = _merge(xp, vec(norm1_g, l), o_a, o_b, o_c, o_d, wg[l], wb[l], wo[l], tm=256)
        res = _mlp(xp, vec(norm2_g, l), wu[l], wd[l], final_g[None, :], tm=256, final=final)
        xp = res[0]
        if final:
            yp = res[1]
        outs["nsa_p"].append(kva3[:, :, :4])
        outs["dsa_p"].append(dsa_rows.reshape(nb_, seq, 3, HEAD_DIM))
        outs["win_p"].append(kva3[:, seq - min(NSA_WINDOW, seq):, 4:])
        outs["conv_p"].append(conv_tail[:, CONV_HALO - (CONV_WIDTH - 1):])

        qn, kva, glu, uv, qd, qi, misc, dsa_rows = _proj(
            xs, vec(norm1_g, l), w_proj[l], w_proj_t[l], tm=nseq, emit_t=False)
        cmp_past, kst, vst, kwt, vwt, kdt, vdt, kit = _assemble(
            page_table, cache_nsa_t, cache_dsa_t, state_win_t, kva[:, :, None], dsa_rows[:, :, None], layer=l)
        nbs = past // NSA_BLOCK
        kvc = _compress(cmp_past.reshape(nseq * past, 2 * HEAD_DIM), c_pos[l], c_w1[l], c_w2[l],
                        rows_per_step=min(8, nseq) * past)
        kvc = jnp.pad(kvc.reshape(nseq, nbs, 2 * HEAD_DIM), ((0, 0), (0, NB_PAD - nbs), (0, 0)))

        def pad_rows(a):
            a = a[..., :, None, :]
            widths = [(0, 0)] * (a.ndim - 2) + [(0, sr - 1), (0, 0)]
            a = jnp.pad(a, widths)
            return a.reshape(a.shape[:-3] + (nseq * sr, a.shape[-1]))

        misc_r = pad_rows(misc)
        o_a = _nsa(pad_rows(qn), misc_r, kvc,
                   (kst, _seq_spec(n_chunks_s, HEAD_DIM, CK)), (vst, _seq_spec(n_chunks_s, HEAD_DIM, CK)),
                   (kwt, _seq_spec(2, HEAD_DIM, CK)), (vwt, _seq_spec(2, HEAD_DIM, CK)),
                   nbatch=nseq, nq=1, tq=sr, pos0=past, pos_stride=0, win_base=past // CK - 1,
                   n_sel=min(NSA_N_SEL, nbs + 1), v_t=True)
        o_d = _dsa_step(pad_rows(qd), pad_rows(qi), misc_r, kdt, vdt, kit, nseq=nseq, tq=sr, pos=past,
                        n_keep=min(DSA_TOPK, (past + 1) // 4), n_chunks=n_chunks_s)
        o_a = o_a.reshape(nseq, sr, -1)[:, 0]
        o_d = o_d.reshape(nseq, sr, -1)[:, 0]
        o_b, o_c, hx_new, v_rows = _step_mix(
            glu, uv, conv_hist[l], conv_w[l], vec(conv_b, l), vec(conv_ln_g, l), vec(conv_ln_b, l),
            vec(sgu_ln_g, l), vec(sgu_ln_b, l), sgu_diag[l], sgu_bias0[l])
        xs = _merge(xs, vec(norm1_g, l), o_a, o_b, o_c, o_d, wg[l], wb[l], wo[l], tm=nseq)
        res = _mlp(xs, vec(norm2_g, l), wu[l], wd[l], final_g[None, :], tm=nseq, final=final)
        xs = res[0]
        if final:
            ys = res[1]
        kva3 = kva.reshape(nseq, 1, 6, HEAD_DIM)
        outs["nsa_s"].append(kva3[:, :, :4])
        outs["dsa_s"].append(dsa_rows.reshape(nseq, 1, 3, HEAD_DIM))
        outs["win_s"].append(jnp.concatenate([state_win[:, l], kva3[:, :, 4:]], axis=1)[:, 1:])
        outs["conv_s"].append(jnp.concatenate([state_conv[:, l], hx_new[:, None, :]], axis=1)[:, 1:])
        outs["sgu_s"].append(v_rows[:, None, :])

    st = lambda k: jnp.stack(outs[k], axis=1)
    return (yp.reshape(nb_, seq, d), ys.reshape(nseq, 1, d), st("nsa_p"), st("nsa_s"), st("dsa_p"), st("dsa_s"),
            st("win_p"), st("win_s"), st("conv_p"), st("conv_s"), st("sgu_s"))
```

```python
import functools

import numpy as np
import jax
import jax.numpy as jnp
from jax import lax
from jax.experimental import pallas as pl
from jax.experimental.pallas import tpu as pltpu

F32, BF16, I32 = jnp.float32, jnp.bfloat16, jnp.int32

HEAD_DIM = 64
NSA_HEADS = 4
NSA_BLOCK = 64
NSA_N_SEL = 16
NSA_WINDOW = 512
FORCE_BONUS = 1.0e4
CONV_CH = 256
CONV_WIDTH = 31
SGU_CH = 256
SGU_GROUPS = 4
SGU_CHUNK = 128
DSA_HEADS = 4
IDX_HEADS = 4
DSA_TOPK = 256
N_BRANCH = 4
BRANCH_WIDTH = 256
Q_BLOCK = 128
EPS = 1e-6
NEG = -1e30

SPLIT_WIDTHS = (256, 384, 12, 512, 512, 256, 128, 256, 64, 4)

CK = 512
NB_PAD = 128
SAMPLE_ROWS = 16
VMEM_LIMIT = 56 * 1024 * 1024
INT_MIN = -2 ** 31

W_QA, W_KVA, W_GLU, W_UV, W_QD, W_QI, W_MISC, W_DSA = 0, 256, 640, 1152, 1664, 1920, 2176, 2304
W_TOTAL = 2496
MISC_WI, MISC_GA = 0, 4


def _cparams(n_axes):
    return pltpu.CompilerParams(dimension_semantics=("arbitrary",) * n_axes,
                                vmem_limit_bytes=VMEM_LIMIT)


def _dot(a, b):
    return jnp.dot(a, b, preferred_element_type=F32)


def _dot_nt(a, b):
    return lax.dot_general(a, b, (((1,), (1,)), ((), ())), preferred_element_type=F32)


def _rms(x, g):
    return x * lax.rsqrt(jnp.mean(x * x, axis=-1, keepdims=True) + EPS) * g


def _layernorm(x, g, b):
    mu = jnp.mean(x, axis=-1, keepdims=True)
    xc = x - mu
    var = jnp.mean(xc * xc, axis=-1, keepdims=True)
    return xc * lax.rsqrt(var + EPS) * g + b


def _sigmoid(x):
    return 1.0 / (1.0 + jnp.exp(-x))


def _tile4(a):
    return jnp.concatenate([a, a, a, a], axis=0)


def _proj_kernel(x_ref, g_ref, w_ref, wt_ref, qn_ref, kva_ref, glu_ref, uv_ref, qd_ref, qi_ref,
                 misc_ref, dsa_ref, *t_refs):
    hb = _rms(x_ref[...], g_ref[...]).astype(BF16)

    def slab(off, width):
        return _dot(hb, w_ref[:, off:off + width])

    qa = slab(W_QA, 256)
    qd = slab(W_QD, 256)
    qi = slab(W_QI, 256)
    for h in range(4):
        cols = slice(h * HEAD_DIM, (h + 1) * HEAD_DIM)
        qn_ref[h] = qa[:, cols].astype(BF16)
        qd_ref[h] = qd[:, cols].astype(BF16)
        qi_ref[h] = qi[:, cols].astype(BF16)
    kv = slab(W_KVA, 384)
    kva_ref[...] = kv
    glu_ref[...] = slab(W_GLU, 512)
    uv_ref[...] = slab(W_UV, 512)
    misc_ref[...] = slab(W_MISC, 128)
    dr = slab(W_DSA, 192)
    dsa_ref[...] = dr
    if t_refs:
        kt_ref, v_ref = t_refs
        kt_ref[0] = _dot_nt(wt_ref[...], hb).astype(BF16)
        ones = jnp.ones((kv.shape[0], HEAD_DIM), F32)
        v_ref[0] = jnp.concatenate([kv[:, 192:256], ones], axis=-1).astype(BF16)
        v_ref[1] = jnp.concatenate([kv[:, 320:384], ones], axis=-1).astype(BF16)
        v_ref[2] = jnp.concatenate([dr[:, 64:128], ones], axis=-1).astype(BF16)


def _proj(x, g, w, wt, *, tm, emit_t):
    n, d = x.shape
    sds = jax.ShapeDtypeStruct
    out_shape = [sds((4, n, HEAD_DIM), BF16), sds((n, 384), F32), sds((n, 512), F32), sds((n, 512), F32),
                 sds((4, n, HEAD_DIM), BF16), sds((4, n, HEAD_DIM), BF16), sds((n, 128), F32), sds((n, 192), F32)]
    head_spec = pl.BlockSpec((4, tm, HEAD_DIM), lambda i: (0, i, 0))

    def row_spec(width):
        return pl.BlockSpec((tm, width), lambda i: (i, 0))

    out_specs = [head_spec, row_spec(384), row_spec(512), row_spec(512), head_spec, head_spec,
                 row_spec(128), row_spec(192)]
    if emit_t:
        assert tm == CK
        out_shape += [sds((n // CK, 256, CK), BF16), sds((3, n, 2 * HEAD_DIM), BF16)]
        out_specs += [pl.BlockSpec((1, 256, CK), lambda i: (i, 0, 0)),
                      pl.BlockSpec((3, tm, 2 * HEAD_DIM), lambda i: (0, i, 0))]
    return pl.pallas_call(
        _proj_kernel,
        out_shape=out_shape,
        grid=(n // tm,),
        in_specs=[row_spec(d), pl.BlockSpec((1, d), lambda i: (0, 0)),
                  pl.BlockSpec((d, W_TOTAL), lambda i: (0, 0)),
                  pl.BlockSpec((256, d), lambda i: (0, 0))],
        out_specs=out_specs,
        compiler_params=_cparams(1),
        name="proj",
    )(x, g, w, wt)


def _compress_kernel(x_ref, pos_ref, w1_ref, w2_ref, o_ref):
    n_blocks = o_ref.shape[0]
    acc = jnp.zeros((n_blocks, 2 * HEAD_DIM), F32)
    for r in range(NSA_BLOCK):
        xr = x_ref[pl.ds(r, n_blocks, stride=NSA_BLOCK), :] + pos_ref[r:r + 1, :]
        acc = acc + _dot(xr.astype(BF16), w1_ref[r])
    a = acc * _sigmoid(acc)
    o_ref[...] = _dot(a.astype(BF16), w2_ref[...]).astype(o_ref.dtype)


def _compress(x, pos, w1, w2, *, rows_per_step):
    r = x.shape[0]
    width = 2 * HEAD_DIM
    return pl.pallas_call(
        _compress_kernel,
        out_shape=jax.ShapeDtypeStruct((r // NSA_BLOCK, width), BF16),
        grid=(r // rows_per_step,),
        in_specs=[pl.BlockSpec((rows_per_step, width), lambda i: (i, 0)),
                  pl.BlockSpec((NSA_BLOCK, width), lambda i: (0, 0)),
                  pl.BlockSpec((NSA_BLOCK, width, width), lambda i: (0, 0, 0)),
                  pl.BlockSpec((width, width), lambda i: (0, 0))],
        out_specs=pl.BlockSpec((rows_per_step // NSA_BLOCK, width), lambda i: (i, 0)),
        compiler_params=_cparams(1),
        name="compress",
    )(x, pos, w1, w2)


def _masked_probs(s, mask):
    s = jnp.where(mask, s, NEG)
    m = jnp.max(s, axis=-1, keepdims=True)
    p = jnp.where(mask, jnp.exp(s - m), 0.0)
    return p / jnp.maximum(jnp.sum(p, axis=-1, keepdims=True), 1e-30)


def _softmax_step(s, bias, v1, carry):
    m, acc = carry
    tq = bias.shape[0]
    s = s.reshape(4, tq, CK) + bias[None]
    m_new = jnp.maximum(m, jnp.max(s, axis=-1, keepdims=True))
    alpha = jnp.exp(m - m_new)
    p = jnp.exp(s - m_new)
    acc = alpha.reshape(4 * tq, 1) * acc + _dot(p.reshape(4 * tq, CK).astype(BF16), v1)
    return m_new, acc


def _softmax_init(tq):
    return jnp.full((4, tq, 1), NEG, F32), jnp.zeros((4 * tq, 2 * HEAD_DIM), F32)


def _softmax_done(carry):
    _, acc = carry
    return acc[:, 0:HEAD_DIM] / jnp.maximum(acc[:, HEAD_DIM:HEAD_DIM + 1], 1e-30)


def _softmax_all(chunks, tq, v_t):
    ss = [s.reshape(4, tq, CK) + bias[None] for s, bias, _ in chunks]
    m = functools.reduce(jnp.maximum, [jnp.max(s, axis=-1, keepdims=True) for s in ss])
    ps = [jnp.exp(s - m) for s in ss]
    l = functools.reduce(jnp.add, [jnp.sum(p, axis=-1, keepdims=True) for p in ps])
    pv = _dot_nt if v_t else _dot
    acc = functools.reduce(jnp.add, [pv(p.reshape(4 * tq, CK).astype(BF16), v) for p, (_, _, v) in zip(ps, chunks)])
    return acc[:, 0:HEAD_DIM] / jnp.maximum(l.reshape(4 * tq, 1), 1e-30)


def _mask_bias(allowed):
    return jnp.where(allowed, 0.0, NEG)


def _query_positions(tq, pos0, pos_stride):
    if pos_stride == 0:
        return pos0, pos0, jnp.full((tq, 1), pos0, I32)
    i = pl.program_id(1)
    start = pos0 + i * (tq * pos_stride)
    last = start + (tq - 1) * pos_stride
    t = lax.broadcasted_iota(I32, (tq, 1), 0)
    qpos = start + t * pos_stride
    return start, last, qpos


def _heads_as_rows(ref, scale):
    return jnp.concatenate([ref[h] for h in range(4)], axis=0) * scale


def _key_positions(c):
    return c * CK + lax.broadcasted_iota(I32, (1, CK), 1)


def _chunk_rows(ref, c):
    if isinstance(c, int):
        return ref[c * CK:(c + 1) * CK, :]
    return ref[pl.ds(pl.multiple_of(c * CK, CK), CK), :]


PICK_LANES = 128


def _nsa_kernel(q_ref, misc_ref, kvc_ref, kst_ref, vs_ref, kwt_ref, vw_ref, o_ref, *,
                tq, pos0, pos_stride, win_base, n_sel, v_t):
    values = (lambda ref, c: ref[c]) if v_t else _chunk_rows
    nb = kvc_ref.shape[0]
    start, last, qpos = _query_positions(tq, pos0, pos_stride)
    qpos4 = _tile4(qpos)
    q = _heads_as_rows(q_ref, HEAD_DIM ** -0.5)

    jb = lax.broadcasted_iota(I32, (1, nb), 1)
    mask_c = ((jb + 1) * NSA_BLOCK - 1) <= qpos4
    p_c = _masked_probs(_dot_nt(q, kvc_ref[:, 0:HEAD_DIM]), mask_c)
    o_c = _dot(p_c.astype(BF16), kvc_ref[:, HEAD_DIM:2 * HEAD_DIM])
    imp = p_c[0:tq] + p_c[tq:2 * tq] + p_c[2 * tq:3 * tq] + p_c[3 * tq:4 * tq]

    cur = qpos // NSA_BLOCK
    forced = (jb == 0) | (jb == cur) | (jb == cur - 1)
    score = jnp.where(jb <= cur, imp, -1.0) + jnp.where(forced, FORCE_BONUS, 0.0)
    if tq < PICK_LANES:
        score = jnp.concatenate([score, jnp.zeros((PICK_LANES - tq, nb), F32)], axis=0)
    jcol = lax.broadcasted_iota(I32, (nb, PICK_LANES), 0).astype(F32)

    def pick(_, carry):
        sc, sel = carry
        m = jnp.max(sc, axis=0, keepdims=True)
        first = jnp.min(jnp.where(sc == m, jcol, 1e9), axis=0, keepdims=True)
        hit = jcol == first
        return jnp.where(hit, -3e38, sc), jnp.where(hit, 1.0, sel)

    _, sel_t = lax.fori_loop(0, n_sel, pick, (score.T, jnp.zeros((nb, PICK_LANES), F32)))
    sel = sel_t.T[0:tq].astype(BF16)

    jrow = lax.broadcasted_iota(I32, (nb, CK), 0)
    kcol = lax.broadcasted_iota(I32, (nb, CK), 1)

    def sel_parts(c):
        expand = jnp.where(jrow == ((c * CK + kcol) // NSA_BLOCK), 1.0, 0.0).astype(BF16)
        bias = _mask_bias((_dot(sel, expand) > 0.5) & (_key_positions(c) <= qpos))
        return _dot(q, kst_ref[c]), bias, values(vs_ref, c)

    def win_parts(c):
        cc = max(c - win_base, 0) if isinstance(c, int) else jnp.maximum(c - win_base, 0)
        kpos = _key_positions(c)
        dpos = qpos - kpos
        bias = _mask_bias((dpos >= 0) & (dpos < NSA_WINDOW) & (kpos >= 0))
        return _dot(q, kwt_ref[cc]), bias, values(vw_ref, cc)

    n_chunks = last // CK + 1
    c1 = start // CK
    if isinstance(n_chunks, int):
        o_s = _softmax_all([sel_parts(c) for c in range(n_chunks)], tq, v_t)
        o_w = _softmax_all([win_parts(c1 - 1), win_parts(c1)], tq, v_t)
    else:
        assert not v_t

        def sel_chunk(c, carry):
            return _softmax_step(*sel_parts(c), carry)

        o_s = _softmax_done(_chunk_loop(n_chunks, sel_chunk, _softmax_init(tq), unroll=ATTEND_UNROLL))
        carry = _softmax_init(tq)
        for c in (c1 - 1, c1):
            carry = _softmax_step(*win_parts(c), carry)
        o_w = _softmax_done(carry)

    g = _sigmoid(misc_ref[...])
    outs = []
    for h in range(4):
        rows = slice(h * tq, (h + 1) * tq)
        col = MISC_GA + 3 * h
        outs.append(g[:, col:col + 1] * o_c[rows] + g[:, col + 1:col + 2] * o_s[rows]
                    + g[:, col + 2:col + 3] * o_w[rows])
    o_ref[...] = jnp.concatenate(outs, axis=-1).astype(o_ref.dtype)


def _nsa(q, misc, kvc, ks, vs, kw, vw, *, nbatch, nq, tq, pos0, pos_stride, win_base, n_sel, v_t):
    nrows = q.shape[1]
    kern = functools.partial(_nsa_kernel, tq=tq, pos0=pos0, pos_stride=pos_stride, win_base=win_base,
                             n_sel=n_sel, v_t=v_t)
    return pl.pallas_call(
        kern,
        out_shape=jax.ShapeDtypeStruct((nrows, 4 * HEAD_DIM), BF16),
        grid=(nbatch, nq),
        in_specs=[pl.BlockSpec((4, tq, HEAD_DIM), lambda b, i: (0, b * nq + i, 0)),
                  pl.BlockSpec((tq, 128), lambda b, i: (b * nq + i, 0)),
                  pl.BlockSpec((None, NB_PAD, 2 * HEAD_DIM), lambda b, i: (b, 0, 0)),
                  ks[1], vs[1], kw[1], vw[1]],
        out_specs=pl.BlockSpec((tq, 4 * HEAD_DIM), lambda b, i: (b * nq + i, 0)),
        compiler_params=_cparams(2),
        name="nsa",
    )(q, misc, kvc, ks[0], vs[0], kw[0], vw[0])


LANES = 128
COUNT_UNROLL = 4
ATTEND_UNROLL = 4
INDEX_UNROLL = 4


def _index_keys(qi, wcol, kit, kpos, qpos, tq):
    rel = jnp.maximum(_dot(qi, kit), 0.0) * wcol
    sc = rel[0:tq] + rel[tq:2 * tq] + rel[2 * tq:3 * tq] + rel[3 * tq:4 * tq]
    sc = jnp.where(sc == 0.0, 0.0, sc)
    sc = jnp.where(kpos <= qpos, sc, NEG)
    bits = lax.bitcast_convert_type(sc, I32)
    return jnp.where(bits < 0, bits ^ 0x7FFFFFFF, bits)


def _index_weights(misc_ref):
    w = misc_ref[:, MISC_WI:MISC_WI + 4] * (IDX_HEADS ** -0.5 * HEAD_DIM ** -0.5)
    return jnp.concatenate([w[:, h:h + 1] for h in range(4)], axis=0)


def _chunk_loop(n_chunks, body, init, unroll=COUNT_UNROLL):
    if isinstance(n_chunks, int):
        carry = init
        for c in range(n_chunks):
            carry = body(c, carry)
        return carry

    def several(g, carry):
        for u in range(unroll):
            carry = body(g * unroll + u, carry)
        return carry

    n_groups = n_chunks // unroll
    carry = lax.fori_loop(0, n_groups, several, init)
    return lax.fori_loop(n_groups * unroll, n_chunks, body, carry)


def _key_counter(key_ref, n_chunks):
    rows = key_ref.shape[1]

    def count(pred):
        def one(c, acc):
            for j in range(CK // LANES):
                k = key_ref[c, :, j * LANES:(j + 1) * LANES]
                acc = acc + jnp.where(pred(k, c * CK + j * LANES), 1.0, 0.0)
            return acc

        acc = _chunk_loop(n_chunks, one, jnp.zeros((rows, LANES), F32))
        return jnp.sum(acc, axis=-1, keepdims=True)

    return count


def _top_keys(count, n_keep):
    keep = float(n_keep)
    thr = jnp.where(count(lambda k, c: k >= 0) >= keep, 0, INT_MIN).astype(I32)

    def thr_bit(b, thr):
        cand = thr | jnp.left_shift(jnp.int32(1), 30 - b)
        return jnp.where(count(lambda k, c: k >= cand) >= keep, cand, thr)

    thr = lax.fori_loop(0, 31, thr_bit, thr)
    return thr, keep - count(lambda k, c: k > thr)


def _tie_cut(count, thr, need, rows, n_positions):
    n_pos_bits = (n_positions - 1).bit_length()

    def cut_bit(b, cut):
        cand = cut + jnp.left_shift(jnp.int32(1), n_pos_bits - 1 - b)
        n = count(lambda k, first: (k == thr) & (first + lax.broadcasted_iota(I32, (1, LANES), 1) < cand))
        return jnp.where(n < need, cand, cut)

    return lax.fori_loop(0, n_pos_bits, cut_bit, jnp.zeros((rows, 1), I32))


def _heads_to_lanes(o, tq):
    return jnp.concatenate([o[h * tq:(h + 1) * tq] for h in range(4)], axis=-1)


def _dsa_kernel(qd_ref, qi_ref, misc_ref, kdt_ref, vd_ref, kit_ref, o_ref, key_scr, *, tq, n_keep):
    start, last, qpos = _query_positions(tq, 0, 1)
    n_chunks = last // CK + 1
    qd = _heads_as_rows(qd_ref, HEAD_DIM ** -0.5)
    qi = _heads_as_rows(qi_ref, 1.0)
    wcol = _index_weights(misc_ref)

    def index_chunk(c, _):
        key_scr[c] = _index_keys(qi, wcol, kit_ref[c], _key_positions(c), qpos, tq)
        return 0

    _chunk_loop(n_chunks, index_chunk, 0, unroll=INDEX_UNROLL)
    thr, need = _top_keys(_key_counter(key_scr, n_chunks), n_keep)

    tri = jnp.where(lax.broadcasted_iota(I32, (LANES, LANES), 0) <= lax.broadcasted_iota(I32, (LANES, LANES), 1),
                    1.0, 0.0).astype(BF16)

    def attend_chunk(c, carry):
        state, seen = carry
        k = key_scr[c]
        ranks = []
        for j in range(CK // LANES):
            equal = jnp.where(k[:, j * LANES:(j + 1) * LANES] == thr, 1.0, 0.0)
            ranks.append(seen + _dot(equal.astype(BF16), tri))
            seen = seen + jnp.sum(equal, axis=-1, keepdims=True)
        rank = jnp.concatenate(ranks, axis=-1)
        taken = (k > thr) | ((k == thr) & (rank <= need))
        bias = _mask_bias(taken & (_key_positions(c) <= qpos))
        state = _softmax_step(_dot(qd, kdt_ref[c]), bias, _chunk_rows(vd_ref, c), state)
        return state, seen

    state, _ = _chunk_loop(n_chunks, attend_chunk, (_softmax_init(tq), jnp.zeros((tq, 1), F32)),
                           unroll=ATTEND_UNROLL)
    o_ref[...] = _heads_to_lanes(_softmax_done(state), tq).astype(o_ref.dtype)


def _dsa(qd, qi, misc, kd, vd, ki, *, nbatch, nq, tq, n_keep, n_chunks):
    nrows = qd.shape[1]
    head_spec = pl.BlockSpec((4, tq, HEAD_DIM), lambda b, i: (0, b * nq + i, 0))
    return pl.pallas_call(
        functools.partial(_dsa_kernel, tq=tq, n_keep=n_keep),
        out_shape=jax.ShapeDtypeStruct((nrows, 4 * HEAD_DIM), BF16),
        grid=(nbatch, nq),
        in_specs=[head_spec, head_spec, pl.BlockSpec((tq, 128), lambda b, i: (b * nq + i, 0)),
                  kd[1], vd[1], ki[1]],
        out_specs=pl.BlockSpec((tq, 4 * HEAD_DIM), lambda b, i: (b * nq + i, 0)),
        scratch_shapes=[pltpu.VMEM((n_chunks, tq, CK), I32)],
        compiler_params=_cparams(2),
        name="dsa",
    )(qd, qi, misc, kd[0], vd[0], ki[0])


def _dsa_step_index_kernel(qi_ref, misc_ref, kit_ref, key_ref, *, tq, pos, n_chunks):
    qi = _heads_as_rows(qi_ref, 1.0)
    wcol = _index_weights(misc_ref)
    for c in range(n_chunks):
        key_ref[c] = _index_keys(qi, wcol, kit_ref[c], _key_positions(c), pos, tq)[0:1, :]


def _dsa_step_top_kernel(key_ref, thr_ref, cut_ref, *, n_keep, n_chunks):
    rows = key_ref.shape[1]
    count = _key_counter(key_ref, n_chunks)
    thr, need = _top_keys(count, n_keep)
    thr_ref[...] = jnp.broadcast_to(thr, thr_ref.shape)
    cut_ref[...] = jnp.broadcast_to(_tie_cut(count, thr, need, rows, n_chunks * CK), cut_ref.shape)


def _dsa_step_attend_kernel(qd_ref, key_ref, thr_ref, cut_ref, kdt_ref, vdt_ref, o_ref, *, tq, pos, n_chunks):
    qd = _heads_as_rows(qd_ref, HEAD_DIM ** -0.5)
    thr = thr_ref[:, 0:1]
    cut = cut_ref[:, 0:1]
    chunks = []
    for c in range(n_chunks):
        k = key_ref[c]
        kpos = _key_positions(c)
        bias = _mask_bias(((k > thr) | ((k == thr) & (kpos <= cut))) & (kpos <= pos))
        chunks.append((_dot(qd, kdt_ref[c]), jnp.broadcast_to(bias, (tq, CK)), vdt_ref[c]))
    o_ref[...] = _heads_to_lanes(_softmax_all(chunks, tq, True), tq).astype(o_ref.dtype)


def _dsa_step(qd, qi, misc, kdt, vd, kit, *, nseq, tq, pos, n_keep, n_chunks):
    head_spec = pl.BlockSpec((4, tq, HEAD_DIM), lambda b: (0, b, 0))
    chunk_spec = pl.BlockSpec((None, n_chunks, HEAD_DIM, CK), lambda b: (b, 0, 0, 0))
    key_spec = pl.BlockSpec((n_chunks, None, 1, CK), lambda b: (0, b, 0, 0))
    row_spec = pl.BlockSpec((None, 1, LANES), lambda b: (b, 0, 0))
    keys = pl.pallas_call(
        functools.partial(_dsa_step_index_kernel, tq=tq, pos=pos, n_chunks=n_chunks),
        out_shape=jax.ShapeDtypeStruct((n_chunks, nseq, 1, CK), I32),
        grid=(nseq,),
        in_specs=[head_spec, pl.BlockSpec((tq, 128), lambda b: (b, 0)), chunk_spec],
        out_specs=key_spec,
        compiler_params=_cparams(1),
        name="dsa_step_index",
    )(qi, misc, kit)
    thr, cut = pl.pallas_call(
        functools.partial(_dsa_step_top_kernel, n_keep=n_keep, n_chunks=n_chunks),
        out_shape=[jax.ShapeDtypeStruct((nseq, LANES), I32)] * 2,
        compiler_params=pltpu.CompilerParams(vmem_limit_bytes=VMEM_LIMIT),
        name="dsa_step_top",
    )(keys.reshape(n_chunks, nseq, CK))
    return pl.pallas_call(
        functools.partial(_dsa_step_attend_kernel, tq=tq, pos=pos, n_chunks=n_chunks),
        out_shape=jax.ShapeDtypeStruct((nseq * tq, 4 * HEAD_DIM), BF16),
        grid=(nseq,),
        in_specs=[head_spec, key_spec, row_spec, row_spec, chunk_spec, chunk_spec],
        out_specs=pl.BlockSpec((tq, 4 * HEAD_DIM), lambda b: (b, 0)),
        compiler_params=_cparams(1),
        name="dsa_step_attend",
    )(qd, keys, thr[:, None, :], cut[:, None, :], kdt, vd)


CONV_HALO = 32


def _conv_kernel(cur_ref, halo_ref, w_ref, b_ref, g_ref, beta_ref, o_ref, tail_ref, hx_scr, *, tc):
    i = pl.program_id(1)

    def glu(x):
        return x[:, :CONV_CH] * _sigmoid(x[:, CONV_CH:])

    hx_scr[0:CONV_HALO] = jnp.where(i > 0, glu(halo_ref[...]), 0.0)
    hx_scr[CONV_HALO:CONV_HALO + tc] = glu(cur_ref[...])
    first = CONV_HALO - (CONV_WIDTH - 1)
    y = jnp.broadcast_to(b_ref[...], (tc, CONV_CH))
    for k in range(CONV_WIDTH):
        y = y + hx_scr[first + k:first + k + tc] * w_ref[k:k + 1, :]
    y = _layernorm(y, g_ref[...], beta_ref[...])
    o_ref[...] = (y * _sigmoid(y)).astype(o_ref.dtype)
    tail_ref[...] = hx_scr[tc:tc + CONV_HALO]


def _conv(glu, w, b, g, beta, *, nbatch, seq, tc):
    n = glu.shape[0]
    nt = seq // tc
    per = tc // CONV_HALO
    vec = pl.BlockSpec((1, CONV_CH), lambda bb, i: (0, 0))
    return pl.pallas_call(
        functools.partial(_conv_kernel, tc=tc),
        out_shape=[jax.ShapeDtypeStruct((n, CONV_CH), BF16),
                   jax.ShapeDtypeStruct((nbatch, CONV_HALO, CONV_CH), F32)],
        grid=(nbatch, nt),
        in_specs=[pl.BlockSpec((tc, 2 * CONV_CH), lambda bb, i: (bb * nt + i, 0)),
                  pl.BlockSpec((CONV_HALO, 2 * CONV_CH),
                               lambda bb, i: (jnp.maximum((bb * nt + i) * per - 1, 0), 0)),
                  pl.BlockSpec((CONV_WIDTH, CONV_CH), lambda bb, i: (0, 0)), vec, vec, vec],
        out_specs=[pl.BlockSpec((tc, CONV_CH), lambda bb, i: (bb * nt + i, 0)),
                   pl.BlockSpec((None, CONV_HALO, CONV_CH), lambda bb, i: (bb, 0, 0))],
        scratch_shapes=[pltpu.VMEM((tc + CONV_HALO, CONV_CH), F32)],
        compiler_params=_cparams(2),
        name="conv",
    )(glu, glu, w, b, g, beta)


def _gelu(x):
    return 0.5 * x * (1.0 + lax.erf(x * (2.0 ** -0.5)))


def _sgu_kernel(uv_ref, g_ref, beta_ref, w_ref, bias_ref, o_ref, *, ts):
    a = _gelu(uv_ref[...])
    u = a[:, :SGU_CH]
    vn = _layernorm(a[:, SGU_CH:], g_ref[...], beta_ref[...]).astype(BF16)
    ri = lax.broadcasted_iota(I32, (SGU_CHUNK, SGU_CHUNK), 0)
    ci = lax.broadcasted_iota(I32, (SGU_CHUNK, SGU_CHUNK), 1)
    group = lax.broadcasted_iota(I32, (SGU_CHUNK, SGU_CH), 1) // (SGU_CH // SGU_GROUPS)
    ws = [jnp.where(ci <= ri, w_ref[gi], 0.0).astype(BF16) for gi in range(SGU_GROUPS)]
    for c in range(ts // SGU_CHUNK):
        rows = slice(c * SGU_CHUNK, (c + 1) * SGU_CHUNK)
        mixed = bias_ref[...]
        for gi in range(SGU_GROUPS):
            mixed = mixed + jnp.where(group == gi, _dot(ws[gi], vn[rows]), 0.0)
        o_ref[rows, :] = (u[rows] * mixed).astype(o_ref.dtype)


def _sgu(uv, g, beta, w, bias, *, ts):
    n = uv.shape[0]
    vec = pl.BlockSpec((1, SGU_CH), lambda i: (0, 0))
    return pl.pallas_call(
        functools.partial(_sgu_kernel, ts=ts),
        out_shape=jax.ShapeDtypeStruct((n, SGU_CH), BF16),
        grid=(n // ts,),
        in_specs=[pl.BlockSpec((ts, 2 * SGU_CH), lambda i: (i, 0)), vec, vec,
                  pl.BlockSpec((SGU_GROUPS, SGU_CHUNK, SGU_CHUNK), lambda i: (0, 0, 0)),
                  pl.BlockSpec((SGU_CHUNK, SGU_CH), lambda i: (0, 0))],
        out_specs=pl.BlockSpec((ts, SGU_CH), lambda i: (i, 0)),
        compiler_params=_cparams(1),
        name="sgu",
    )(uv, g, beta, w, bias)


def _step_mix_kernel(glu_ref, uv_ref, hist_ref, cw_ref, cb_ref, cg_ref, cbeta_ref, sg_ref, sbeta_ref,
                     sdiag_ref, sbias_ref, ob_ref, oc_ref, hx_ref, v_ref):
    x = glu_ref[...]
    hx = x[:, :CONV_CH] * _sigmoid(x[:, CONV_CH:])
    hx_ref[...] = hx
    y = cb_ref[...] + hx * cw_ref[CONV_WIDTH - 1:CONV_WIDTH, :]
    for k in range(CONV_WIDTH - 1):
        y = y + hist_ref[k] * cw_ref[k:k + 1, :]
    y = _layernorm(y, cg_ref[...], cbeta_ref[...])
    ob_ref[...] = (y * _sigmoid(y)).astype(ob_ref.dtype)
    a = _gelu(uv_ref[...])
    v = a[:, SGU_CH:]
    v_ref[...] = v
    vn = _layernorm(v, sg_ref[...], sbeta_ref[...])
    oc_ref[...] = (a[:, :SGU_CH] * (sdiag_ref[...] * vn + sbias_ref[...])).astype(oc_ref.dtype)


def _step_mix(glu, uv, hist, cw, cb, cg, cbeta, sg, sbeta, sdiag, sbias):
    n = glu.shape[0]
    sds = jax.ShapeDtypeStruct
    return pl.pallas_call(
        _step_mix_kernel,
        out_shape=[sds((n, CONV_CH), BF16), sds((n, SGU_CH), BF16), sds((n, CONV_CH), F32), sds((n, SGU_CH), F32)],
        compiler_params=pltpu.CompilerParams(vmem_limit_bytes=VMEM_LIMIT),
        name="step_mix",
    )(glu, uv, hist, cw, cb, cg, cbeta, sg, sbeta, sdiag, sbias)


def _merge_kernel(x_ref, g_ref, oa_ref, ob_ref, oc_ref, od_ref, wg_ref, wb_ref, wo_ref, y_ref):
    x = x_ref[...]
    d = x.shape[-1]
    hb = _rms(x, g_ref[...]).astype(BF16)
    acc = jnp.zeros(x.shape, F32)
    for k, o_ref in enumerate((oa_ref, ob_ref, oc_ref, od_ref)):
        gate = _sigmoid(_dot(hb, wg_ref[:, k * d:(k + 1) * d]))
        acc = acc + gate * _dot(o_ref[...], wb_ref[k])
    y_ref[...] = x + _dot(acc.astype(BF16), wo_ref[...])


def _merge(x, g, oa, ob, oc, od, wg, wb, wo, *, tm):
    n, d = x.shape
    row = pl.BlockSpec((tm, d), lambda i: (i, 0))
    br = pl.BlockSpec((tm, BRANCH_WIDTH), lambda i: (i, 0))
    once = pl.Buffered(1)
    return pl.pallas_call(
        _merge_kernel,
        out_shape=jax.ShapeDtypeStruct((n, d), F32),
        grid=(n // tm,),
        in_specs=[row, pl.BlockSpec((1, d), lambda i: (0, 0)), br, br, br, br,
                  pl.BlockSpec((d, N_BRANCH * d), lambda i: (0, 0), pipeline_mode=once),
                  pl.BlockSpec((N_BRANCH, BRANCH_WIDTH, d), lambda i: (0, 0, 0), pipeline_mode=once),
                  pl.BlockSpec((d, d), lambda i: (0, 0), pipeline_mode=once)],
        out_specs=row,
        compiler_params=_cparams(1),
        name="merge",
    )(x, g, oa, ob, oc, od, wg, wb, wo)


def _mlp_kernel(x_ref, g_ref, wu_ref, wd_ref, gf_ref, y_ref, *n_ref):
    x = x_ref[...]
    hb = _rms(x, g_ref[...]).astype(BF16)
    a = jnp.square(jnp.maximum(_dot(hb, wu_ref[...]), 0.0)).astype(BF16)
    y = x + _dot(a, wd_ref[...])
    y_ref[...] = y
    if n_ref:
        n_ref[0][...] = _rms(y, gf_ref[...])


def _mlp(x, g, wu, wd, gf, *, tm, final):
    n, d = x.shape
    row = pl.BlockSpec((tm, d), lambda i: (i, 0))
    vec = pl.BlockSpec((1, d), lambda i: (0, 0))
    once = pl.Buffered(1)
    out_shape = [jax.ShapeDtypeStruct((n, d), F32)] * (2 if final else 1)
    return pl.pallas_call(
        _mlp_kernel,
        out_shape=out_shape,
        grid=(n // tm,),
        in_specs=[row, vec, pl.BlockSpec(wu.shape, lambda i: (0, 0), pipeline_mode=once),
                  pl.BlockSpec(wd.shape, lambda i: (0, 0), pipeline_mode=once), vec],
        out_specs=[row] * (2 if final else 1),
        compiler_params=_cparams(1),
        name="mlp",
    )(x, g, wu, wd, gf)


def _assemble_kernel(pt_ref, *refs, n_pages, page):
    nsa_pages = refs[:n_pages]
    dsa_pages = refs[n_pages:2 * n_pages]
    win_ref, nsa_new_ref, dsa_new_ref = refs[2 * n_pages:2 * n_pages + 3]
    cmp_ref, kst_ref, vst_ref, kwt_ref, vwt_ref, kdt_ref, vdt_ref, kit_ref = refs[2 * n_pages + 3:]
    per_chunk = CK // page
    for p in range(n_pages):
        c, r = divmod(p, per_chunk)
        lanes = slice(r * page, (r + 1) * page)
        x = nsa_pages[p]
        cmp_ref[p * page:(p + 1) * page, :] = jnp.concatenate([x[0], x[1]], axis=0).T
        kst_ref[c, :, lanes] = x[2].astype(BF16)
        vst_ref[c, :, lanes] = x[3].astype(BF16)
        y = dsa_pages[p]
        kdt_ref[c, :, lanes] = y[0].astype(BF16)
        vdt_ref[c, :, lanes] = y[1].astype(BF16)
        kit_ref[c, :, lanes] = y[2].astype(BF16)

    first = lax.broadcasted_iota(I32, (HEAD_DIM, CK), 1) == 0

    def new_chunk(col):
        return jnp.where(first, jnp.broadcast_to(col, (HEAD_DIM, CK)), 0.0).astype(BF16)

    c_new = n_pages // per_chunk
    nsa_new = nsa_new_ref[...]
    dsa_new = dsa_new_ref[...]
    kst_ref[c_new] = new_chunk(nsa_new[2 * HEAD_DIM:3 * HEAD_DIM])
    vst_ref[c_new] = new_chunk(nsa_new[3 * HEAD_DIM:4 * HEAD_DIM])
    kdt_ref[c_new] = new_chunk(dsa_new[0:HEAD_DIM])
    vdt_ref[c_new] = new_chunk(dsa_new[HEAD_DIM:2 * HEAD_DIM])
    kit_ref[c_new] = new_chunk(dsa_new[2 * HEAD_DIM:3 * HEAD_DIM])
    kwt_ref[0] = win_ref[0].astype(BF16)
    vwt_ref[0] = win_ref[1].astype(BF16)
    kwt_ref[1] = new_chunk(nsa_new[4 * HEAD_DIM:5 * HEAD_DIM])
    vwt_ref[1] = new_chunk(nsa_new[5 * HEAD_DIM:6 * HEAD_DIM])


def _assemble(page_table, cache_nsa_t, cache_dsa_t, state_win_t, nsa_new, dsa_new, *, layer):
    nseq, n_pages = page_table.shape
    page = cache_nsa_t.shape[-1]
    n_chunks = n_pages * page // CK + 1
    sds = jax.ShapeDtypeStruct

    def page_spec(slots, p):
        return pl.BlockSpec((None, None, slots, HEAD_DIM, page), lambda b, pt: (pt[b, p], layer, 0, 0, 0))

    def per_seq(*shape):
        return pl.BlockSpec((None,) + shape, lambda b, pt: (b,) + (0,) * len(shape))

    chunks = per_seq(n_chunks, HEAD_DIM, CK)
    grid_spec = pltpu.PrefetchScalarGridSpec(
        num_scalar_prefetch=1,
        grid=(nseq,),
        in_specs=([page_spec(4, p) for p in range(n_pages)] + [page_spec(3, p) for p in range(n_pages)]
                  + [pl.BlockSpec((None, None, 2, HEAD_DIM, CK), lambda b, pt: (b, layer, 0, 0, 0)),
                     per_seq(6 * HEAD_DIM, 1), per_seq(3 * HEAD_DIM, 1)]),
        out_specs=[per_seq(n_pages * page, 2 * HEAD_DIM), chunks, chunks, per_seq(2, HEAD_DIM, CK),
                   per_seq(2, HEAD_DIM, CK), chunks, chunks, chunks],
    )
    chunk_shape = sds((nseq, n_chunks, HEAD_DIM, CK), BF16)
    win_shape = sds((nseq, 2, HEAD_DIM, CK), BF16)
    return pl.pallas_call(
        functools.partial(_assemble_kernel, n_pages=n_pages, page=page),
        out_shape=[sds((nseq, n_pages * page, 2 * HEAD_DIM), F32), chunk_shape, chunk_shape, win_shape, win_shape,
                   chunk_shape, chunk_shape, chunk_shape],
        grid_spec=grid_spec,
        compiler_params=_cparams(1),
        name="assemble",
    )(page_table, *([cache_nsa_t] * n_pages), *([cache_dsa_t] * n_pages), state_win_t, nsa_new, dsa_new)


def _prep_w_in(w_in):
    pts = np.cumsum(SPLIT_WIDTHS)[:-1].tolist()
    qa, kva, ga, glu, uv, qd, kvd, qi, ki, wi = jnp.split(w_in, pts, axis=-1)
    pad = jnp.zeros(w_in.shape[:2] + (128 - 16,), w_in.dtype)
    w = jnp.concatenate([qa, kva, glu, uv, qd, qi, wi, ga, pad, kvd, ki], axis=-1).astype(BF16)
    wt = jnp.concatenate([kva[..., 128:192], kva[..., 256:320], kvd[..., 0:64], ki], axis=-1)
    return w, jnp.swapaxes(wt, 1, 2).astype(BF16)


def _seq_spec(*shape):
    return pl.BlockSpec((None,) + shape, lambda b, i: (b,) + (0,) * len(shape))


def kernel(x_prompt, x_sample, cache_nsa, cache_dsa, state_win, state_conv, page_table, norm1_g, norm2_g,
           final_g, w_in, cmp_pos, cmp_w1, cmp_w2, conv_w, conv_b, conv_ln_g, conv_ln_b, sgu_ln_g, sgu_ln_b,
           sgu_w, sgu_b, w_branch, w_gate, w_out, w_up, w_down):
    nb_, seq, d = x_prompt.shape
    nseq, t_dec, _ = x_sample.shape
    depth = w_in.shape[0]
    n_pool, _, page, _, _ = cache_nsa.shape
    n_pages = page_table.shape[1]
    past = n_pages * page
    w_buf = state_win.shape[2]
    assert t_dec == 1 and seq % CK == 0 and past % CK == 0 and w_buf == CK == NSA_WINDOW
    nbp = seq // NSA_BLOCK
    assert nbp <= NB_PAD and past // NSA_BLOCK < NB_PAD
    n_tok = nb_ * seq
    sr = SAMPLE_ROWS

    w_proj, w_proj_t = _prep_w_in(w_in)
    wg, wb, wo = w_gate.astype(BF16), w_branch.astype(BF16), w_out.astype(BF16)
    wu, wd = w_up.astype(BF16), w_down.astype(BF16)
    def block_diag(a, b):
        z = jnp.zeros_like(a)
        return jnp.concatenate([jnp.concatenate([a, z], axis=-1), jnp.concatenate([z, b], axis=-1)], axis=-2)

    c_pos = jnp.concatenate([cmp_pos[:, 0], cmp_pos[:, 1]], axis=-1)
    c_w1 = cmp_w1.reshape(depth, 2, NSA_BLOCK, HEAD_DIM, HEAD_DIM).astype(BF16)
    c_w1 = block_diag(c_w1[:, 0], c_w1[:, 1])
    c_w2 = block_diag(cmp_w2[:, 0], cmp_w2[:, 1]).astype(BF16)
    sgu_bias = jnp.repeat(jnp.swapaxes(sgu_b, 1, 2), SGU_CH // SGU_GROUPS, axis=2)
    sgu_diag = jnp.repeat(sgu_w[:, :, 0, 0], SGU_CH // SGU_GROUPS, axis=1)[:, None, :]
    sgu_bias0 = sgu_bias[:, 0:1, :]
    vec = lambda a, l: a[l][None, :]
    cache_nsa_t = jnp.transpose(cache_nsa, (0, 1, 3, 4, 2))
    cache_dsa_t = jnp.transpose(cache_dsa, (0, 1, 3, 4, 2))
    state_win_t = jnp.transpose(state_win, (0, 1, 3, 4, 2))
    conv_hist = jnp.transpose(state_conv, (1, 2, 0, 3))

    n_chunks_p = seq // CK
    n_chunks_s = past // CK + 1
    nq = seq // Q_BLOCK

    def chunks_of(arr, slot, n_chunks):
        return arr, pl.BlockSpec((n_chunks, HEAD_DIM, CK), lambda b, i: (b, slot, 0))

    def rows_of(arr, slot, n_rows):
        return arr, pl.BlockSpec((None, n_rows, 2 * HEAD_DIM), lambda b, i: (slot, b, 0))

    xp = x_prompt.reshape(n_tok, d)
    xs = x_sample.reshape(nseq, d)
    outs = {k: [] for k in ("nsa_p", "nsa_s", "dsa_p", "dsa_s", "win_p", "win_s", "conv_p", "conv_s", "sgu_s")}
    yp = ys = None
    for l in range(depth):
        final = l == depth - 1
        qn, kva, glu, uv, qd, qi, misc, dsa_rows, kt4, v4 = _proj(
            xp, vec(norm1_g, l), w_proj[l], w_proj_t[l], tm=CK, emit_t=True)
        kva3 = kva.reshape(nb_, seq, 6, HEAD_DIM)
        kvc = _compress(kva, c_pos[l], c_w1[l], c_w2[l], rows_per_step=seq)
        kvc = jnp.pad(kvc.reshape(nb_, nbp, 2 * HEAD_DIM), ((0, 0), (0, NB_PAD - nbp), (0, 0)))
        o_a = _nsa(qn, misc, kvc, chunks_of(kt4, 0, n_chunks_p), rows_of(v4, 0, seq),
                   chunks_of(kt4, 1, n_chunks_p), rows_of(v4, 1, seq),
                   nbatch=nb_, nq=nq, tq=Q_BLOCK, pos0=0, pos_stride=1, win_base=0,
                   n_sel=min(NSA_N_SEL, seq // NSA_BLOCK), v_t=False)
        o_d = _dsa(qd, qi, misc, chunks_of(kt4, 2, n_chunks_p), rows_of(v4, 2, seq),
                   chunks_of(kt4, 3, n_chunks_p), nbatch=nb_, nq=nq, tq=Q_BLOCK,
                   n_keep=min(DSA_TOPK, seq // 4), n_chunks=n_chunks_p)
        o_b, conv_tail = _conv(glu, conv_w[l], vec(conv_b, l), vec(conv_ln_g, l), vec(conv_ln_b, l),
                               nbatch=nb_, seq=seq, tc=CK)
        o_c = _sgu(uv, vec(sgu_ln_g, l), vec(sgu_ln_b, l), sgu_w[l], sgu_bias[l], ts=CK)
        xp = _merge(xp, vec(norm1_g, l), o_a, o_b, o_c, o_d, wg[l], wb[l], wo[l], tm=256)
        res = _mlp(xp, vec(norm2_g, l), wu[l], wd[l], final_g[None, :], tm=256, final=final)
        xp = res[0]
        if final:
            yp = res[1]
        outs["nsa_p"].append(kva3[:, :, :4])
        outs["dsa_p"].append(dsa_rows.reshape(nb_, seq, 3, HEAD_DIM))
        outs["win_p"].append(kva3[:, seq - min(NSA_WINDOW, seq):, 4:])
        outs["conv_p"].append(conv_tail[:, CONV_HALO - (CONV_WIDTH - 1):])

        qn, kva, glu, uv, qd, qi, misc, dsa_rows = _proj(
            xs, vec(norm1_g, l), w_proj[l], w_proj_t[l], tm=nseq, emit_t=False)
        cmp_past, kst, vst, kwt, vwt, kdt, vdt, kit = _assemble(
            page_table, cache_nsa_t, cache_dsa_t, state_win_t, kva[:, :, None], dsa_rows[:, :, None], layer=l)
        nbs = past // NSA_BLOCK
        kvc = _compress(cmp_past.reshape(nseq * past, 2 * HEAD_DIM), c_pos[l], c_w1[l], c_w2[l],
                        rows_per_step=min(8, nseq) * past)
        kvc = jnp.pad(kvc.reshape(nseq, nbs, 2 * HEAD_DIM), ((0, 0), (0, NB_PAD - nbs), (0, 0)))

        def pad_rows(a):
            a = a[..., :, None, :]
            widths = [(0, 0)] * (a.ndim - 2) + [(0, sr - 1), (0, 0)]
            a = jnp.pad(a, widths)
            return a.reshape(a.shape[:-3] + (nseq * sr, a.shape[-1]))

        misc_r = pad_rows(misc)
        o_a = _nsa(pad_rows(qn), misc_r, kvc,
                   (kst, _seq_spec(n_chunks_s, HEAD_DIM, CK)), (vst, _seq_spec(n_chunks_s, HEAD_DIM, CK)),
                   (kwt, _seq_spec(2, HEAD_DIM, CK)), (vwt, _seq_spec(2, HEAD_DIM, CK)),
                   nbatch=nseq, nq=1, tq=sr, pos0=past, pos_stride=0, win_base=past // CK - 1,
                   n_sel=min(NSA_N_SEL, nbs + 1), v_t=True)
        o_d = _dsa_step(pad_rows(qd), pad_rows(qi), misc_r, kdt, vdt, kit, nseq=nseq, tq=sr, pos=past,
                        n_keep=min(DSA_TOPK, (past + 1) // 4), n_chunks=n_chunks_s)
        o_a = o_a.reshape(nseq, sr, -1)[:, 0]
        o_d = o_d.reshape(nseq, sr, -1)[:, 0]
        o_b, o_c, hx_new, v_rows = _step_mix(
            glu, uv, conv_hist[l], conv_w[l], vec(conv_b, l), vec(conv_ln_g, l), vec(conv_ln_b, l),
            vec(sgu_ln_g, l), vec(sgu_ln_b, l), sgu_diag[l], sgu_bias0[l])
        xs = _merge(xs, vec(norm1_g, l), o_a, o_b, o_c, o_d, wg[l], wb[l], wo[l], tm=nseq)
        res = _mlp(xs, vec(norm2_g, l), wu[l], wd[l], final_g[None, :], tm=nseq, final=final)
        xs = res[0]
        if final:
            ys = res[1]
        kva3 = kva.reshape(nseq, 1, 6, HEAD_DIM)
        outs["nsa_s"].append(kva3[:, :, :4])
        outs["dsa_s"].append(dsa_rows.reshape(nseq, 1, 3, HEAD_DIM))
        outs["win_s"].append(jnp.concatenate([state_win[:, l], kva3[:, :, 4:]], axis=1)[:, 1:])
        outs["conv_s"].append(jnp.concatenate([state_conv[:, l], hx_new[:, None, :]], axis=1)[:, 1:])
        outs["sgu_s"].append(v_rows[:, None, :])

    st = lambda k: jnp.stack(outs[k], axis=1)
    return (yp.reshape(nb_, seq, d), ys.reshape(nseq, 1, d), st("nsa_p"), st("nsa_s"), st("dsa_p"), st("dsa_s"),
            st("win_p"), st("win_s"), st("conv_p"), st("conv_s"), st("sgu_s"))
```

```python
import functools

import numpy as np
import jax
import jax.numpy as jnp
from jax import lax
from jax.experimental import pallas as pl
from jax.experimental.pallas import tpu as pltpu

F32, BF16, I32 = jnp.float32, jnp.bfloat16, jnp.int32

HEAD_DIM = 64
NSA_HEADS = 4
NSA_BLOCK = 64
NSA_N_SEL = 16
NSA_WINDOW = 512
FORCE_BONUS = 1.0e4
CONV_CH = 256
CONV_WIDTH = 31
SGU_CH = 256
SGU_GROUPS = 4
SGU_CHUNK = 128
DSA_HEADS = 4
IDX_HEADS = 4
DSA_TOPK = 256
N_BRANCH = 4
BRANCH_WIDTH = 256
Q_BLOCK = 128
EPS = 1e-6
NEG = -1e30

SPLIT_WIDTHS = (256, 384, 12, 512, 512, 256, 128, 256, 64, 4)

CK = 512
NB_PAD = 128
SAMPLE_ROWS = 16
VMEM_LIMIT = 56 * 1024 * 1024
INT_MIN = -2 ** 31

W_QA, W_KVA, W_GLU, W_UV, W_QD, W_QI, W_MISC, W_DSA = 0, 256, 640, 1152, 1664, 1920, 2176, 2304
W_TOTAL = 2496
MISC_WI, MISC_GA = 0, 4


def _cparams(n_axes):
    return pltpu.CompilerParams(dimension_semantics=("arbitrary",) * n_axes,
                                vmem_limit_bytes=VMEM_LIMIT)


def _dot(a, b):
    return jnp.dot(a, b, preferred_element_type=F32)


def _dot_nt(a, b):
    return lax.dot_general(a, b, (((1,), (1,)), ((), ())), preferred_element_type=F32)


def _rms(x, g):
    return x * lax.rsqrt(jnp.mean(x * x, axis=-1, keepdims=True) + EPS) * g


def _layernorm(x, g, b):
    mu = jnp.mean(x, axis=-1, keepdims=True)
    xc = x - mu
    var = jnp.mean(xc * xc, axis=-1, keepdims=True)
    return xc * lax.rsqrt(var + EPS) * g + b


def _sigmoid(x):
    return 1.0 / (1.0 + jnp.exp(-x))


def _tile4(a):
    return jnp.concatenate([a, a, a, a], axis=0)


def _proj_kernel(x_ref, g_ref, w_ref, wt_ref, qn_ref, kva_ref, glu_ref, uv_ref, qd_ref, qi_ref,
                 misc_ref, dsa_ref, *t_refs):
    hb = _rms(x_ref[...], g_ref[...]).astype(BF16)

    def slab(off, width):
        return _dot(hb, w_ref[:, off:off + width])

    qa = slab(W_QA, 256)
    qd = slab(W_QD, 256)
    qi = slab(W_QI, 256)
    for h in range(4):
        cols = slice(h * HEAD_DIM, (h + 1) * HEAD_DIM)
        qn_ref[h] = qa[:, cols].astype(BF16)
        qd_ref[h] = qd[:, cols].astype(BF16)
        qi_ref[h] = qi[:, cols].astype(BF16)
    kv = slab(W_KVA, 384)
    kva_ref[...] = kv
    glu_ref[...] = slab(W_GLU, 512)
    uv_ref[...] = slab(W_UV, 512)
    misc_ref[...] = slab(W_MISC, 128)
    dr = slab(W_DSA, 192)
    dsa_ref[...] = dr
    if t_refs:
        kt_ref, v_ref = t_refs
        kt_ref[0] = _dot_nt(wt_ref[...], hb).astype(BF16)
        ones = jnp.ones((kv.shape[0], HEAD_DIM), F32)
        v_ref[0] = jnp.concatenate([kv[:, 192:256], ones], axis=-1).astype(BF16)
        v_ref[1] = jnp.concatenate([kv[:, 320:384], ones], axis=-1).astype(BF16)
        v_ref[2] = jnp.concatenate([dr[:, 64:128], ones], axis=-1).astype(BF16)


def _proj(x, g, w, wt, *, tm, emit_t):
    n, d = x.shape
    sds = jax.ShapeDtypeStruct
    out_shape = [sds((4, n, HEAD_DIM), BF16), sds((n, 384), F32), sds((n, 512), F32), sds((n, 512), F32),
                 sds((4, n, HEAD_DIM), BF16), sds((4, n, HEAD_DIM), BF16), sds((n, 128), F32), sds((n, 192), F32)]
    head_spec = pl.BlockSpec((4, tm, HEAD_DIM), lambda i: (0, i, 0))

    def row_spec(width):
        return pl.BlockSpec((tm, width), lambda i: (i, 0))

    out_specs = [head_spec, row_spec(384), row_spec(512), row_spec(512), head_spec, head_spec,
                 row_spec(128), row_spec(192)]
    if emit_t:
        assert tm == CK
        out_shape += [sds((n // CK, 256, CK), BF16), sds((3, n, 2 * HEAD_DIM), BF16)]
        out_specs += [pl.BlockSpec((1, 256, CK), lambda i: (i, 0, 0)),
                      pl.BlockSpec((3, tm, 2 * HEAD_DIM), lambda i: (0, i, 0))]
    return pl.pallas_call(
        _proj_kernel,
        out_shape=out_shape,
        grid=(n // tm,),
        in_specs=[row_spec(d), pl.BlockSpec((1, d), lambda i: (0, 0)),
                  pl.BlockSpec((d, W_TOTAL), lambda i: (0, 0)),
                  pl.BlockSpec((256, d), lambda i: (0, 0))],
        out_specs=out_specs,
        compiler_params=_cparams(1),
        name="proj",
    )(x, g, w, wt)


def _compress_kernel(x_ref, pos_ref, w1_ref, w2_ref, o_ref):
    n_blocks = o_ref.shape[0]
    acc = jnp.zeros((n_blocks, 2 * HEAD_DIM), F32)
    for r in range(NSA_BLOCK):
        xr = x_ref[pl.ds(r, n_blocks, stride=NSA_BLOCK), :] + pos_ref[r:r + 1, :]
        acc = acc + _dot(xr.astype(BF16), w1_ref[r])
    a = acc * _sigmoid(acc)
    o_ref[...] = _dot(a.astype(BF16), w2_ref[...]).astype(o_ref.dtype)


def _compress(x, pos, w1, w2, *, rows_per_step):
    r = x.shape[0]
    width = 2 * HEAD_DIM
    return pl.pallas_call(
        _compress_kernel,
        out_shape=jax.ShapeDtypeStruct((r // NSA_BLOCK, width), BF16),
        grid=(r // rows_per_step,),
        in_specs=[pl.BlockSpec((rows_per_step, width), lambda i: (i, 0)),
                  pl.BlockSpec((NSA_BLOCK, width), lambda i: (0, 0)),
                  pl.BlockSpec((NSA_BLOCK, width, width), lambda i: (0, 0, 0)),
                  pl.BlockSpec((width, width), lambda i: (0, 0))],
        out_specs=pl.BlockSpec((rows_per_step // NSA_BLOCK, width), lambda i: (i, 0)),
        compiler_params=_cparams(1),
        name="compress",
    )(x, pos, w1, w2)


def _masked_probs(s, mask):
    s = jnp.where(mask, s, NEG)
    m = jnp.max(s, axis=-1, keepdims=True)
    p = jnp.where(mask, jnp.exp(s - m), 0.0)
    return p / jnp.maximum(jnp.sum(p, axis=-1, keepdims=True), 1e-30)


def _softmax_step(s, bias, v1, carry):
    m, acc = carry
    tq = bias.shape[0]
    s = s.reshape(4, tq, CK) + bias[None]
    m_new = jnp.maximum(m, jnp.max(s, axis=-1, keepdims=True))
    alpha = jnp.exp(m - m_new)
    p = jnp.exp(s - m_new)
    acc = alpha.reshape(4 * tq, 1) * acc + _dot(p.reshape(4 * tq, CK).astype(BF16), v1)
    return m_new, acc


def _softmax_init(tq):
    return jnp.full((4, tq, 1), NEG, F32), jnp.zeros((4 * tq, 2 * HEAD_DIM), F32)


def _softmax_done(carry):
    _, acc = carry
    return acc[:, 0:HEAD_DIM] / jnp.maximum(acc[:, HEAD_DIM:HEAD_DIM + 1], 1e-30)


def _softmax_all(chunks, tq, v_t):
    ss = [s.reshape(4, tq, CK) + bias[None] for s, bias, _ in chunks]
    m = functools.reduce(jnp.maximum, [jnp.max(s, axis=-1, keepdims=True) for s in ss])
    ps = [jnp.exp(s - m) for s in ss]
    l = functools.reduce(jnp.add, [jnp.sum(p, axis=-1, keepdims=True) for p in ps])
    pv = _dot_nt if v_t else _dot
    acc = functools.reduce(jnp.add, [pv(p.reshape(4 * tq, CK).astype(BF16), v) for p, (_, _, v) in zip(ps, chunks)])
    return acc[:, 0:HEAD_DIM] / jnp.maximum(l.reshape(4 * tq, 1), 1e-30)


def _mask_bias(allowed):
    return jnp.where(allowed, 0.0, NEG)


def _query_positions(tq, pos0, pos_stride):
    if pos_stride == 0:
        return pos0, pos0, jnp.full((tq, 1), pos0, I32)
    i = pl.program_id(1)
    start = pos0 + i * (tq * pos_stride)
    last = start + (tq - 1) * pos_stride
    t = lax.broadcasted_iota(I32, (tq, 1), 0)
    qpos = start + t * pos_stride
    return start, last, qpos


def _heads_as_rows(ref, scale):
    return jnp.concatenate([ref[h] for h in range(4)], axis=0) * scale


def _key_positions(c):
    return c * CK + lax.broadcasted_iota(I32, (1, CK), 1)


def _chunk_rows(ref, c):
    if isinstance(c, int):
        return ref[c * CK:(c + 1) * CK, :]
    return ref[pl.ds(pl.multiple_of(c * CK, CK), CK), :]


PICK_LANES = 128


def _nsa_kernel(q_ref, misc_ref, kvc_ref, kst_ref, vs_ref, kwt_ref, vw_ref, o_ref, *,
                tq, pos0, pos_stride, win_base, n_sel, v_t):
    values = (lambda ref, c: ref[c]) if v_t else _chunk_rows
    nb = kvc_ref.shape[0]
    start, last, qpos = _query_positions(tq, pos0, pos_stride)
    qpos4 = _tile4(qpos)
    q = _heads_as_rows(q_ref, HEAD_DIM ** -0.5)

    jb = lax.broadcasted_iota(I32, (1, nb), 1)
    mask_c = ((jb + 1) * NSA_BLOCK - 1) <= qpos4
    p_c = _masked_probs(_dot_nt(q, kvc_ref[:, 0:HEAD_DIM]), mask_c)
    o_c = _dot(p_c.astype(BF16), kvc_ref[:, HEAD_DIM:2 * HEAD_DIM])
    imp = p_c[0:tq] + p_c[tq:2 * tq] + p_c[2 * tq:3 * tq] + p_c[3 * tq:4 * tq]

    cur = qpos // NSA_BLOCK
    forced = (jb == 0) | (jb == cur) | (jb == cur - 1)
    score = jnp.where(jb <= cur, imp, -1.0) + jnp.where(forced, FORCE_BONUS, 0.0)
    if tq < PICK_LANES:
        score = jnp.concatenate([score, jnp.zeros((PICK_LANES - tq, nb), F32)], axis=0)
    jcol = lax.broadcasted_iota(I32, (nb, PICK_LANES), 0).astype(F32)

    def pick(_, carry):
        sc, sel = carry
        m = jnp.max(sc, axis=0, keepdims=True)
        first = jnp.min(jnp.where(sc == m, jcol, 1e9), axis=0, keepdims=True)
        hit = jcol == first
        return jnp.where(hit, -3e38, sc), jnp.where(hit, 1.0, sel)

    _, sel_t = lax.fori_loop(0, n_sel, pick, (score.T, jnp.zeros((nb, PICK_LANES), F32)))
    sel = sel_t.T[0:tq].astype(BF16)

    jrow = lax.broadcasted_iota(I32, (nb, CK), 0)
    kcol = lax.broadcasted_iota(I32, (nb, CK), 1)

    def sel_parts(c):
        expand = jnp.where(jrow == ((c * CK + kcol) // NSA_BLOCK), 1.0, 0.0).astype(BF16)
        bias = _mask_bias((_dot(sel, expand) > 0.5) & (_key_positions(c) <= qpos))
        return _dot(q, kst_ref[c]), bias, values(vs_ref, c)

    def win_parts(c):
        cc = max(c - win_base, 0) if isinstance(c, int) else jnp.maximum(c - win_base, 0)
        kpos = _key_positions(c)
        dpos = qpos - kpos
        bias = _mask_bias((dpos >= 0) & (dpos < NSA_WINDOW) & (kpos >= 0))
        return _dot(q, kwt_ref[cc]), bias, values(vw_ref, cc)

    n_chunks = last // CK + 1
    c1 = start // CK
    if isinstance(n_chunks, int):
        o_s = _softmax_all([sel_parts(c) for c in range(n_chunks)], tq, v_t)
        o_w = _softmax_all([win_parts(c1 - 1), win_parts(c1)], tq, v_t)
    else:
        assert not v_t

        def sel_chunk(c, carry):
            return _softmax_step(*sel_parts(c), carry)

        o_s = _softmax_done(_chunk_loop(n_chunks, sel_chunk, _softmax_init(tq), unroll=ATTEND_UNROLL))
        carry = _softmax_init(tq)
        for c in (c1 - 1, c1):
            carry = _softmax_step(*win_parts(c), carry)
        o_w = _softmax_done(carry)

    g = _sigmoid(misc_ref[...])
    outs = []
    for h in range(4):
        rows = slice(h * tq, (h + 1) * tq)
        col = MISC_GA + 3 * h
        outs.append(g[:, col:col + 1] * o_c[rows] + g[:, col + 1:col + 2] * o_s[rows]
                    + g[:, col + 2:col + 3] * o_w[rows])
    o_ref[...] = jnp.concatenate(outs, axis=-1).astype(o_ref.dtype)


def _nsa(q, misc, kvc, ks, vs, kw, vw, *, nbatch, nq, tq, pos0, pos_stride, win_base, n_sel, v_t):
    nrows = q.shape[1]
    kern = functools.partial(_nsa_kernel, tq=tq, pos0=pos0, pos_stride=pos_stride, win_base=win_base,
                             n_sel=n_sel, v_t=v_t)
    return pl.pallas_call(
        kern,
        out_shape=jax.ShapeDtypeStruct((nrows, 4 * HEAD_DIM), BF16),
        grid=(nbatch, nq),
        in_specs=[pl.BlockSpec((4, tq, HEAD_DIM), lambda b, i: (0, b * nq + i, 0)),
                  pl.BlockSpec((tq, 128), lambda b, i: (b * nq + i, 0)),
                  pl.BlockSpec((None, NB_PAD, 2 * HEAD_DIM), lambda b, i: (b, 0, 0)),
                  ks[1], vs[1], kw[1], vw[1]],
        out_specs=pl.BlockSpec((tq, 4 * HEAD_DIM), lambda b, i: (b * nq + i, 0)),
        compiler_params=_cparams(2),
        name="nsa",
    )(q, misc, kvc, ks[0], vs[0], kw[0], vw[0])


LANES = 128
COUNT_UNROLL = 4
ATTEND_UNROLL = 4
INDEX_UNROLL = 4


def _index_keys(qi, wcol, kit, kpos, qpos, tq):
    rel = jnp.maximum(_dot(qi, kit), 0.0) * wcol
    sc = rel[0:tq] + rel[tq:2 * tq] + rel[2 * tq:3 * tq] + rel[3 * tq:4 * tq]
    sc = jnp.where(sc == 0.0, 0.0, sc)
    sc = jnp.where(kpos <= qpos, sc, NEG)
    bits = lax.bitcast_convert_type(sc, I32)
    return jnp.where(bits < 0, bits ^ 0x7FFFFFFF, bits)


def _index_weights(misc_ref):
    w = misc_ref[:, MISC_WI:MISC_WI + 4] * (IDX_HEADS ** -0.5 * HEAD_DIM ** -0.5)
    return jnp.concatenate([w[:, h:h + 1] for h in range(4)], axis=0)


def _chunk_loop(n_chunks, body, init, unroll=COUNT_UNROLL):
    if isinstance(n_chunks, int):
        carry = init
        for c in range(n_chunks):
            carry = body(c, carry)
        return carry

    carry, done, width = init, 0, unroll
    while width >= 1:
        def several(g, carry, done=done, width=width):
            for u in range(width):
                carry = body(done + g * width + u, carry)
            return carry

        n_trips = (n_chunks - done) // width
        carry = lax.fori_loop(0, n_trips, several, carry)
        done = done + n_trips * width
        width //= 2
    return carry


def _key_counter(key_ref, n_chunks):
    rows = key_ref.shape[1]

    def count(pred):
        def one(c, acc):
            for j in range(CK // LANES):
                k = key_ref[c, :, j * LANES:(j + 1) * LANES]
                acc = acc + jnp.where(pred(k, c * CK + j * LANES), 1.0, 0.0)
            return acc

        acc = _chunk_loop(n_chunks, one, jnp.zeros((rows, LANES), F32))
        return jnp.sum(acc, axis=-1, keepdims=True)

    return count


def _top_keys(count, n_keep):
    keep = float(n_keep)
    n0 = count(lambda k, c: k >= 0)
    thr = jnp.where(n0 >= keep, 0, INT_MIN).astype(I32)
    above = jnp.where(n0 >= keep, 0.0, n0)

    def thr_bit(b, state):
        thr, above = state
        cand = thr | jnp.left_shift(jnp.int32(1), 30 - b)
        n = count(lambda k, c: k >= cand)
        return jnp.where(n >= keep, cand, thr), jnp.where(n >= keep, above, n)

    thr, above = lax.fori_loop(0, 31, thr_bit, (thr, above))
    return thr, keep - above


def _tie_cut(count, thr, need, rows, n_positions):
    n_pos_bits = (n_positions - 1).bit_length()

    def cut_bit(b, cut):
        cand = cut + jnp.left_shift(jnp.int32(1), n_pos_bits - 1 - b)
        n = count(lambda k, first: (k == thr) & (first + lax.broadcasted_iota(I32, (1, LANES), 1) < cand))
        return jnp.where(n < need, cand, cut)

    return lax.fori_loop(0, n_pos_bits, cut_bit, jnp.zeros((rows, 1), I32))


def _heads_to_lanes(o, tq):
    return jnp.concatenate([o[h * tq:(h + 1) * tq] for h in range(4)], axis=-1)


def _dsa_kernel(qd_ref, qi_ref, misc_ref, kdt_ref, vd_ref, kit_ref, o_ref, key_scr, *, tq, n_keep):
    start, last, qpos = _query_positions(tq, 0, 1)
    n_chunks = last // CK + 1
    qd = _heads_as_rows(qd_ref, HEAD_DIM ** -0.5)
    qi = _heads_as_rows(qi_ref, 1.0)
    wcol = _index_weights(misc_ref)

    def index_chunk(c, _):
        key_scr[c] = _index_keys(qi, wcol, kit_ref[c], _key_positions(c), qpos, tq)
        return 0

    _chunk_loop(n_chunks, index_chunk, 0, unroll=INDEX_UNROLL)
    thr, need = _top_keys(_key_counter(key_scr, n_chunks), n_keep)

    tri = jnp.where(lax.broadcasted_iota(I32, (LANES, LANES), 0) <= lax.broadcasted_iota(I32, (LANES, LANES), 1),
                    1.0, 0.0).astype(BF16)

    def attend_chunk(c, carry):
        state, seen = carry
        k = key_scr[c]
        ranks = []
        for j in range(CK // LANES):
            equal = jnp.where(k[:, j * LANES:(j + 1) * LANES] == thr, 1.0, 0.0)
            ranks.append(seen + _dot(equal.astype(BF16), tri))
            seen = seen + jnp.sum(equal, axis=-1, keepdims=True)
        rank = jnp.concatenate(ranks, axis=-1)
        taken = (k > thr) | ((k == thr) & (rank <= need))
        bias = _mask_bias(taken & (_key_positions(c) <= qpos))
        state = _softmax_step(_dot(qd, kdt_ref[c]), bias, _chunk_rows(vd_ref, c), state)
        return state, seen

    state, _ = _chunk_loop(n_chunks, attend_chunk, (_softmax_init(tq), jnp.zeros((tq, 1), F32)),
                           unroll=ATTEND_UNROLL)
    o_ref[...] = _heads_to_lanes(_softmax_done(state), tq).astype(o_ref.dtype)


def _dsa(qd, qi, misc, kd, vd, ki, *, nbatch, nq, tq, n_keep, n_chunks):
    nrows = qd.shape[1]
    head_spec = pl.BlockSpec((4, tq, HEAD_DIM), lambda b, i: (0, b * nq + i, 0))
    return pl.pallas_call(
        functools.partial(_dsa_kernel, tq=tq, n_keep=n_keep),
        out_shape=jax.ShapeDtypeStruct((nrows, 4 * HEAD_DIM), BF16),
        grid=(nbatch, nq),
        in_specs=[head_spec, head_spec, pl.BlockSpec((tq, 128), lambda b, i: (b * nq + i, 0)),
                  kd[1], vd[1], ki[1]],
        out_specs=pl.BlockSpec((tq, 4 * HEAD_DIM), lambda b, i: (b * nq + i, 0)),
        scratch_shapes=[pltpu.VMEM((n_chunks, tq, CK), I32)],
        compiler_params=_cparams(2),
        name="dsa",
    )(qd, qi, misc, kd[0], vd[0], ki[0])


def _dsa_step_index_kernel(qi_ref, misc_ref, kit_ref, key_ref, *, tq, pos, n_chunks):
    qi = _heads_as_rows(qi_ref, 1.0)
    wcol = _index_weights(misc_ref)
    for c in range(n_chunks):
        key_ref[c] = _index_keys(qi, wcol, kit_ref[c], _key_positions(c), pos, tq)[0:1, :]


def _dsa_step_top_kernel(key_ref, thr_ref, cut_ref, *, n_keep, n_chunks):
    rows = key_ref.shape[1]
    count = _key_counter(key_ref, n_chunks)
    thr, need = _top_keys(count, n_keep)
    thr_ref[...] = jnp.broadcast_to(thr, thr_ref.shape)
    cut_ref[...] = jnp.broadcast_to(_tie_cut(count, thr, need, rows, n_chunks * CK), cut_ref.shape)


def _dsa_step_attend_kernel(qd_ref, key_ref, thr_ref, cut_ref, kdt_ref, vdt_ref, o_ref, *, tq, pos, n_chunks):
    qd = _heads_as_rows(qd_ref, HEAD_DIM ** -0.5)
    thr = thr_ref[:, 0:1]
    cut = cut_ref[:, 0:1]
    chunks = []
    for c in range(n_chunks):
        k = key_ref[c]
        kpos = _key_positions(c)
        bias = _mask_bias(((k > thr) | ((k == thr) & (kpos <= cut))) & (kpos <= pos))
        chunks.append((_dot(qd, kdt_ref[c]), jnp.broadcast_to(bias, (tq, CK)), vdt_ref[c]))
    o_ref[...] = _heads_to_lanes(_softmax_all(chunks, tq, True), tq).astype(o_ref.dtype)


def _dsa_step(qd, qi, misc, kdt, vd, kit, *, nseq, tq, pos, n_keep, n_chunks):
    head_spec = pl.BlockSpec((4, tq, HEAD_DIM), lambda b: (0, b, 0))
    chunk_spec = pl.BlockSpec((None, n_chunks, HEAD_DIM, CK), lambda b: (b, 0, 0, 0))
    key_spec = pl.BlockSpec((n_chunks, None, 1, CK), lambda b: (0, b, 0, 0))
    row_spec = pl.BlockSpec((None, 1, LANES), lambda b: (b, 0, 0))
    keys = pl.pallas_call(
        functools.partial(_dsa_step_index_kernel, tq=tq, pos=pos, n_chunks=n_chunks),
        out_shape=jax.ShapeDtypeStruct((n_chunks, nseq, 1, CK), I32),
        grid=(nseq,),
        in_specs=[head_spec, pl.BlockSpec((tq, 128), lambda b: (b, 0)), chunk_spec],
        out_specs=key_spec,
        compiler_params=_cparams(1),
        name="dsa_step_index",
    )(qi, misc, kit)
    thr, cut = pl.pallas_call(
        functools.partial(_dsa_step_top_kernel, n_keep=n_keep, n_chunks=n_chunks),
        out_shape=[jax.ShapeDtypeStruct((nseq, LANES), I32)] * 2,
        compiler_params=pltpu.CompilerParams(vmem_limit_bytes=VMEM_LIMIT),
        name="dsa_step_top",
    )(keys.reshape(n_chunks, nseq, CK))
    return pl.pallas_call(
        functools.partial(_dsa_step_attend_kernel, tq=tq, pos=pos, n_chunks=n_chunks),
        out_shape=jax.ShapeDtypeStruct((nseq * tq, 4 * HEAD_DIM), BF16),
        grid=(nseq,),
        in_specs=[head_spec, key_spec, row_spec, row_spec, chunk_spec, chunk_spec],
        out_specs=pl.BlockSpec((tq, 4 * HEAD_DIM), lambda b: (b, 0)),
        compiler_params=_cparams(1),
        name="dsa_step_attend",
    )(qd, keys, thr[:, None, :], cut[:, None, :], kdt, vd)


CONV_HALO = 32


def _conv_kernel(cur_ref, halo_ref, w_ref, b_ref, g_ref, beta_ref, o_ref, tail_ref, hx_scr, *, tc):
    i = pl.program_id(1)

    def glu(x):
        return x[:, :CONV_CH] * _sigmoid(x[:, CONV_CH:])

    hx_scr[0:CONV_HALO] = jnp.where(i > 0, glu(halo_ref[...]), 0.0)
    hx_scr[CONV_HALO:CONV_HALO + tc] = glu(cur_ref[...])
    first = CONV_HALO - (CONV_WIDTH - 1)
    y = jnp.broadcast_to(b_ref[...], (tc, CONV_CH))
    for k in range(CONV_WIDTH):
        y = y + hx_scr[first + k:first + k + tc] * w_ref[k:k + 1, :]
    y = _layernorm(y, g_ref[...], beta_ref[...])
    o_ref[...] = (y * _sigmoid(y)).astype(o_ref.dtype)
    tail_ref[...] = hx_scr[tc:tc + CONV_HALO]


def _conv(glu, w, b, g, beta, *, nbatch, seq, tc):
    n = glu.shape[0]
    nt = seq // tc
    per = tc // CONV_HALO
    vec = pl.BlockSpec((1, CONV_CH), lambda bb, i: (0, 0))
    return pl.pallas_call(
        functools.partial(_conv_kernel, tc=tc),
        out_shape=[jax.ShapeDtypeStruct((n, CONV_CH), BF16),
                   jax.ShapeDtypeStruct((nbatch, CONV_HALO, CONV_CH), F32)],
        grid=(nbatch, nt),
        in_specs=[pl.BlockSpec((tc, 2 * CONV_CH), lambda bb, i: (bb * nt + i, 0)),
                  pl.BlockSpec((CONV_HALO, 2 * CONV_CH),
                               lambda bb, i: (jnp.maximum((bb * nt + i) * per - 1, 0), 0)),
                  pl.BlockSpec((CONV_WIDTH, CONV_CH), lambda bb, i: (0, 0)), vec, vec, vec],
        out_specs=[pl.BlockSpec((tc, CONV_CH), lambda bb, i: (bb * nt + i, 0)),
                   pl.BlockSpec((None, CONV_HALO, CONV_CH), lambda bb, i: (bb, 0, 0))],
        scratch_shapes=[pltpu.VMEM((tc + CONV_HALO, CONV_CH), F32)],
        compiler_params=_cparams(2),
        name="conv",
    )(glu, glu, w, b, g, beta)


def _gelu(x):
    return 0.5 * x * (1.0 + lax.erf(x * (2.0 ** -0.5)))


def _sgu_kernel(uv_ref, g_ref, beta_ref, w_ref, bias_ref, o_ref, *, ts):
    a = _gelu(uv_ref[...])
    u = a[:, :SGU_CH]
    vn = _layernorm(a[:, SGU_CH:], g_ref[...], beta_ref[...]).astype(BF16)
    ri = lax.broadcasted_iota(I32, (SGU_CHUNK, SGU_CHUNK), 0)
    ci = lax.broadcasted_iota(I32, (SGU_CHUNK, SGU_CHUNK), 1)
    group = lax.broadcasted_iota(I32, (SGU_CHUNK, SGU_CH), 1) // (SGU_CH // SGU_GROUPS)
    ws = [jnp.where(ci <= ri, w_ref[gi], 0.0).astype(BF16) for gi in range(SGU_GROUPS)]
    for c in range(ts // SGU_CHUNK):
        rows = slice(c * SGU_CHUNK, (c + 1) * SGU_CHUNK)
        mixed = bias_ref[...]
        for gi in range(SGU_GROUPS):
            mixed = mixed + jnp.where(group == gi, _dot(ws[gi], vn[rows]), 0.0)
        o_ref[rows, :] = (u[rows] * mixed).astype(o_ref.dtype)


def _sgu(uv, g, beta, w, bias, *, ts):
    n = uv.shape[0]
    vec = pl.BlockSpec((1, SGU_CH), lambda i: (0, 0))
    return pl.pallas_call(
        functools.partial(_sgu_kernel, ts=ts),
        out_shape=jax.ShapeDtypeStruct((n, SGU_CH), BF16),
        grid=(n // ts,),
        in_specs=[pl.BlockSpec((ts, 2 * SGU_CH), lambda i: (i, 0)), vec, vec,
                  pl.BlockSpec((SGU_GROUPS, SGU_CHUNK, SGU_CHUNK), lambda i: (0, 0, 0)),
                  pl.BlockSpec((SGU_CHUNK, SGU_CH), lambda i: (0, 0))],
        out_specs=pl.BlockSpec((ts, SGU_CH), lambda i: (i, 0)),
        compiler_params=_cparams(1),
        name="sgu",
    )(uv, g, beta, w, bias)


def _step_mix_kernel(glu_ref, uv_ref, hist_ref, cw_ref, cb_ref, cg_ref, cbeta_ref, sg_ref, sbeta_ref,
                     sdiag_ref, sbias_ref, ob_ref, oc_ref, hx_ref, v_ref):
    x = glu_ref[...]
    hx = x[:, :CONV_CH] * _sigmoid(x[:, CONV_CH:])
    hx_ref[...] = hx
    y = cb_ref[...] + hx * cw_ref[CONV_WIDTH - 1:CONV_WIDTH, :]
    for k in range(CONV_WIDTH - 1):
        y = y + hist_ref[k] * cw_ref[k:k + 1, :]
    y = _layernorm(y, cg_ref[...], cbeta_ref[...])
    ob_ref[...] = (y * _sigmoid(y)).astype(ob_ref.dtype)
    a = _gelu(uv_ref[...])
    v = a[:, SGU_CH:]
    v_ref[...] = v
    vn = _layernorm(v, sg_ref[...], sbeta_ref[...])
    oc_ref[...] = (a[:, :SGU_CH] * (sdiag_ref[...] * vn + sbias_ref[...])).astype(oc_ref.dtype)


def _step_mix(glu, uv, hist, cw, cb, cg, cbeta, sg, sbeta, sdiag, sbias):
    n = glu.shape[0]
    sds = jax.ShapeDtypeStruct
    return pl.pallas_call(
        _step_mix_kernel,
        out_shape=[sds((n, CONV_CH), BF16), sds((n, SGU_CH), BF16), sds((n, CONV_CH), F32), sds((n, SGU_CH), F32)],
        compiler_params=pltpu.CompilerParams(vmem_limit_bytes=VMEM_LIMIT),
        name="step_mix",
    )(glu, uv, hist, cw, cb, cg, cbeta, sg, sbeta, sdiag, sbias)


def _merge_kernel(x_ref, g_ref, oa_ref, ob_ref, oc_ref, od_ref, wg_ref, wb_ref, wo_ref, y_ref):
    x = x_ref[...]
    d = x.shape[-1]
    hb = _rms(x, g_ref[...]).astype(BF16)
    acc = jnp.zeros(x.shape, F32)
    for k, o_ref in enumerate((oa_ref, ob_ref, oc_ref, od_ref)):
        gate = _sigmoid(_dot(hb, wg_ref[:, k * d:(k + 1) * d]))
        acc = acc + gate * _dot(o_ref[...], wb_ref[k])
    y_ref[...] = x + _dot(acc.astype(BF16), wo_ref[...])


def _merge(x, g, oa, ob, oc, od, wg, wb, wo, *, tm):
    n, d = x.shape
    row = pl.BlockSpec((tm, d), lambda i: (i, 0))
    br = pl.BlockSpec((tm, BRANCH_WIDTH), lambda i: (i, 0))
    once = pl.Buffered(1)
    return pl.pallas_call(
        _merge_kernel,
        out_shape=jax.ShapeDtypeStruct((n, d), F32),
        grid=(n // tm,),
        in_specs=[row, pl.BlockSpec((1, d), lambda i: (0, 0)), br, br, br, br,
                  pl.BlockSpec((d, N_BRANCH * d), lambda i: (0, 0), pipeline_mode=once),
                  pl.BlockSpec((N_BRANCH, BRANCH_WIDTH, d), lambda i: (0, 0, 0), pipeline_mode=once),
                  pl.BlockSpec((d, d), lambda i: (0, 0), pipeline_mode=once)],
        out_specs=row,
        compiler_params=_cparams(1),
        name="merge",
    )(x, g, oa, ob, oc, od, wg, wb, wo)


def _mlp_kernel(x_ref, g_ref, wu_ref, wd_ref, gf_ref, y_ref, *n_ref):
    x = x_ref[...]
    hb = _rms(x, g_ref[...]).astype(BF16)
    a = jnp.square(jnp.maximum(_dot(hb, wu_ref[...]), 0.0)).astype(BF16)
    y = x + _dot(a, wd_ref[...])
    y_ref[...] = y
    if n_ref:
        n_ref[0][...] = _rms(y, gf_ref[...])


def _mlp(x, g, wu, wd, gf, *, tm, final):
    n, d = x.shape
    row = pl.BlockSpec((tm, d), lambda i: (i, 0))
    vec = pl.BlockSpec((1, d), lambda i: (0, 0))
    once = pl.Buffered(1)
    out_shape = [jax.ShapeDtypeStruct((n, d), F32)] * (2 if final else 1)
    return pl.pallas_call(
        _mlp_kernel,
        out_shape=out_shape,
        grid=(n // tm,),
        in_specs=[row, vec, pl.BlockSpec(wu.shape, lambda i: (0, 0), pipeline_mode=once),
                  pl.BlockSpec(wd.shape, lambda i: (0, 0), pipeline_mode=once), vec],
        out_specs=[row] * (2 if final else 1),
        compiler_params=_cparams(1),
        name="mlp",
    )(x, g, wu, wd, gf)


def _assemble_kernel(pt_ref, *refs, n_pages, page):
    nsa_pages = refs[:n_pages]
    dsa_pages = refs[n_pages:2 * n_pages]
    win_ref, nsa_new_ref, dsa_new_ref = refs[2 * n_pages:2 * n_pages + 3]
    cmp_ref, kst_ref, vst_ref, kwt_ref, vwt_ref, kdt_ref, vdt_ref, kit_ref = refs[2 * n_pages + 3:]
    per_chunk = CK // page
    for p in range(n_pages):
        c, r = divmod(p, per_chunk)
        lanes = slice(r * page, (r + 1) * page)
        x = nsa_pages[p]
        cmp_ref[p * page:(p + 1) * page, :] = jnp.concatenate([x[0], x[1]], axis=0).T
        kst_ref[c, :, lanes] = x[2].astype(BF16)
        vst_ref[c, :, lanes] = x[3].astype(BF16)
        y = dsa_pages[p]
        kdt_ref[c, :, lanes] = y[0].astype(BF16)
        vdt_ref[c, :, lanes] = y[1].astype(BF16)
        kit_ref[c, :, lanes] = y[2].astype(BF16)

    first = lax.broadcasted_iota(I32, (HEAD_DIM, CK), 1) == 0

    def new_chunk(col):
        return jnp.where(first, jnp.broadcast_to(col, (HEAD_DIM, CK)), 0.0).astype(BF16)

    c_new = n_pages // per_chunk
    nsa_new = nsa_new_ref[...]
    dsa_new = dsa_new_ref[...]
    kst_ref[c_new] = new_chunk(nsa_new[2 * HEAD_DIM:3 * HEAD_DIM])
    vst_ref[c_new] = new_chunk(nsa_new[3 * HEAD_DIM:4 * HEAD_DIM])
    kdt_ref[c_new] = new_chunk(dsa_new[0:HEAD_DIM])
    vdt_ref[c_new] = new_chunk(dsa_new[HEAD_DIM:2 * HEAD_DIM])
    kit_ref[c_new] = new_chunk(dsa_new[2 * HEAD_DIM:3 * HEAD_DIM])
    kwt_ref[0] = win_ref[0].astype(BF16)
    vwt_ref[0] = win_ref[1].astype(BF16)
    kwt_ref[1] = new_chunk(nsa_new[4 * HEAD_DIM:5 * HEAD_DIM])
    vwt_ref[1] = new_chunk(nsa_new[5 * HEAD_DIM:6 * HEAD_DIM])


def _assemble(page_table, cache_nsa_t, cache_dsa_t, state_win_t, nsa_new, dsa_new, *, layer):
    nseq, n_pages = page_table.shape
    page = cache_nsa_t.shape[-1]
    n_chunks = n_pages * page // CK + 1
    sds = jax.ShapeDtypeStruct

    def page_spec(slots, p):
        return pl.BlockSpec((None, None, slots, HEAD_DIM, page), lambda b, pt: (pt[b, p], layer, 0, 0, 0))

    def per_seq(*shape):
        return pl.BlockSpec((None,) + shape, lambda b, pt: (b,) + (0,) * len(shape))

    chunks = per_seq(n_chunks, HEAD_DIM, CK)
    grid_spec = pltpu.PrefetchScalarGridSpec(
        num_scalar_prefetch=1,
        grid=(nseq,),
        in_specs=([page_spec(4, p) for p in range(n_pages)] + [page_spec(3, p) for p in range(n_pages)]
                  + [pl.BlockSpec((None, None, 2, HEAD_DIM, CK), lambda b, pt: (b, layer, 0, 0, 0)),
                     per_seq(6 * HEAD_DIM, 1), per_seq(3 * HEAD_DIM, 1)]),
        out_specs=[per_seq(n_pages * page, 2 * HEAD_DIM), chunks, chunks, per_seq(2, HEAD_DIM, CK),
                   per_seq(2, HEAD_DIM, CK), chunks, chunks, chunks],
    )
    chunk_shape = sds((nseq, n_chunks, HEAD_DIM, CK), BF16)
    win_shape = sds((nseq, 2, HEAD_DIM, CK), BF16)
    return pl.pallas_call(
        functools.partial(_assemble_kernel, n_pages=n_pages, page=page),
        out_shape=[sds((nseq, n_pages * page, 2 * HEAD_DIM), F32), chunk_shape, chunk_shape, win_shape, win_shape,
                   chunk_shape, chunk_shape, chunk_shape],
        grid_spec=grid_spec,
        compiler_params=_cparams(1),
        name="assemble",
    )(page_table, *([cache_nsa_t] * n_pages), *([cache_dsa_t] * n_pages), state_win_t, nsa_new, dsa_new)


def _prep_w_in(w_in):
    pts = np.cumsum(SPLIT_WIDTHS)[:-1].tolist()
    qa, kva, ga, glu, uv, qd, kvd, qi, ki, wi = jnp.split(w_in, pts, axis=-1)
    pad = jnp.zeros(w_in.shape[:2] + (128 - 16,), w_in.dtype)
    w = jnp.concatenate([qa, kva, glu, uv, qd, qi, wi, ga, pad, kvd, ki], axis=-1).astype(BF16)
    wt = jnp.concatenate([kva[..., 128:192], kva[..., 256:320], kvd[..., 0:64], ki], axis=-1)
    return w, jnp.swapaxes(wt, 1, 2).astype(BF16)


def _seq_spec(*shape):
    return pl.BlockSpec((None,) + shape, lambda b, i: (b,) + (0,) * len(shape))


def kernel(x_prompt, x_sample, cache_nsa, cache_dsa, state_win, state_conv, page_table, norm1_g, norm2_g,
           final_g, w_in, cmp_pos, cmp_w1, cmp_w2, conv_w, conv_b, conv_ln_g, conv_ln_b, sgu_ln_g, sgu_ln_b,
           sgu_w, sgu_b, w_branch, w_gate, w_out, w_up, w_down):
    nb_, seq, d = x_prompt.shape
    nseq, t_dec, _ = x_sample.shape
    depth = w_in.shape[0]
    n_pool, _, page, _, _ = cache_nsa.shape
    n_pages = page_table.shape[1]
    past = n_pages * page
    w_buf = state_win.shape[2]
    assert t_dec == 1 and seq % CK == 0 and past % CK == 0 and w_buf == CK == NSA_WINDOW
    nbp = seq // NSA_BLOCK
    assert nbp <= NB_PAD and past // NSA_BLOCK < NB_PAD
    n_tok = nb_ * seq
    sr = SAMPLE_ROWS

    w_proj, w_proj_t = _prep_w_in(w_in)
    wg, wb, wo = w_gate.astype(BF16), w_branch.astype(BF16), w_out.astype(BF16)
    wu, wd = w_up.astype(BF16), w_down.astype(BF16)
    def block_diag(a, b):
        z = jnp.zeros_like(a)
        return jnp.concatenate([jnp.concatenate([a, z], axis=-1), jnp.concatenate([z, b], axis=-1)], axis=-2)

    c_pos = jnp.concatenate([cmp_pos[:, 0], cmp_pos[:, 1]], axis=-1)
    c_w1 = cmp_w1.reshape(depth, 2, NSA_BLOCK, HEAD_DIM, HEAD_DIM).astype(BF16)
    c_w1 = block_diag(c_w1[:, 0], c_w1[:, 1])
    c_w2 = block_diag(cmp_w2[:, 0], cmp_w2[:, 1]).astype(BF16)
    sgu_bias = jnp.repeat(jnp.swapaxes(sgu_b, 1, 2), SGU_CH // SGU_GROUPS, axis=2)
    sgu_diag = jnp.repeat(sgu_w[:, :, 0, 0], SGU_CH // SGU_GROUPS, axis=1)[:, None, :]
    sgu_bias0 = sgu_bias[:, 0:1, :]
    vec = lambda a, l: a[l][None, :]
    cache_nsa_t = jnp.transpose(cache_nsa, (0, 1, 3, 4, 2))
    cache_dsa_t = jnp.transpose(cache_dsa, (0, 1, 3, 4, 2))
    state_win_t = jnp.transpose(state_win, (0, 1, 3, 4, 2))
    conv_hist = jnp.transpose(state_conv, (1, 2, 0, 3))

    n_chunks_p = seq // CK
    n_chunks_s = past // CK + 1
    nq = seq // Q_BLOCK

    def chunks_of(arr, slot, n_chunks):
        return arr, pl.BlockSpec((n_chunks, HEAD_DIM, CK), lambda b, i: (b, slot, 0))

    def rows_of(arr, slot, n_rows):
        return arr, pl.BlockSpec((None, n_rows, 2 * HEAD_DIM), lambda b, i: (slot, b, 0))

    xp = x_prompt.reshape(n_tok, d)
    xs = x_sample.reshape(nseq, d)
    outs = {k: [] for k in ("nsa_p", "nsa_s", "dsa_p", "dsa_s", "win_p", "win_s", "conv_p", "conv_s", "sgu_s")}
    yp = ys = None
    for l in range(depth):
        final = l == depth - 1
        qn, kva, glu, uv, qd, qi, misc, dsa_rows, kt4, v4 = _proj(
            xp, vec(norm1_g, l), w_proj[l], w_proj_t[l], tm=CK, emit_t=True)
        kva3 = kva.reshape(nb_, seq, 6, HEAD_DIM)
        kvc = _compress(kva, c_pos[l], c_w1[l], c_w2[l], rows_per_step=seq)
        kvc = jnp.pad(kvc.reshape(nb_, nbp, 2 * HEAD_DIM), ((0, 0), (0, NB_PAD - nbp), (0, 0)))
        o_a = _nsa(qn, misc, kvc, chunks_of(kt4, 0, n_chunks_p), rows_of(v4, 0, seq),
                   chunks_of(kt4, 1, n_chunks_p), rows_of(v4, 1, seq),
                   nbatch=nb_, nq=nq, tq=Q_BLOCK, pos0=0, pos_stride=1, win_base=0,
                   n_sel=min(NSA_N_SEL, seq // NSA_BLOCK), v_t=False)
        o_d = _dsa(qd, qi, misc, chunks_of(kt4, 2, n_chunks_p), rows_of(v4, 2, seq),
                   chunks_of(kt4, 3, n_chunks_p), nbatch=nb_, nq=nq, tq=Q_BLOCK,
                   n_keep=min(DSA_TOPK, seq // 4), n_chunks=n_chunks_p)
        o_b, conv_tail = _conv(glu, conv_w[l], vec(conv_b, l), vec(conv_ln_g, l), vec(conv_ln_b, l),
                               nbatch=nb_, seq=seq, tc=CK)
        o_c = _sgu(uv, vec(sgu_ln_g, l), vec(sgu_ln_b, l), sgu_w[l], sgu_bias[l], ts=CK)
        xp = _merge(xp, vec(norm1_g, l), o_a, o_b, o_c, o_d, wg[l], wb[l], wo[l], tm=256)
        res = _mlp(xp, vec(norm2_g, l), wu[l], wd[l], final_g[None, :], tm=256, final=final)
        xp = res[0]
        if final:
            yp = res[1]
        outs["nsa_p"].append(kva3[:, :, :4])
        outs["dsa_p"].append(dsa_rows.reshape(nb_, seq, 3, HEAD_DIM))
        outs["win_p"].append(kva3[:, seq - min(NSA_WINDOW, seq):, 4:])
        outs["conv_p"].append(conv_tail[:, CONV_HALO - (CONV_WIDTH - 1):])

        qn, kva, glu, uv, qd, qi, misc, dsa_rows = _proj(
            xs, vec(norm1_g, l), w_proj[l], w_proj_t[l], tm=nseq, emit_t=False)
        cmp_past, kst, vst, kwt, vwt, kdt, vdt, kit = _assemble(
            page_table, cache_nsa_t, cache_dsa_t, state_win_t, kva[:, :, None], dsa_rows[:, :, None], layer=l)
        nbs = past // NSA_BLOCK
        kvc = _compress(cmp_past.reshape(nseq * past, 2 * HEAD_DIM), c_pos[l], c_w1[l], c_w2[l],
                        rows_per_step=min(8, nseq) * past)
        kvc = jnp.pad(kvc.reshape(nseq, nbs, 2 * HEAD_DIM), ((0, 0), (0, NB_PAD - nbs), (0, 0)))

        def pad_rows(a):
            a = a[..., :, None, :]
            widths = [(0, 0)] * (a.ndim - 2) + [(0, sr - 1), (0, 0)]
            a = jnp.pad(a, widths)
            return a.reshape(a.shape[:-3] + (nseq * sr, a.shape[-1]))

        misc_r = pad_rows(misc)
        o_a = _nsa(pad_rows(qn), misc_r, kvc,
                   (kst, _seq_spec(n_chunks_s, HEAD_DIM, CK)), (vst, _seq_spec(n_chunks_s, HEAD_DIM, CK)),
                   (kwt, _seq_spec(2, HEAD_DIM, CK)), (vwt, _seq_spec(2, HEAD_DIM, CK)),
                   nbatch=nseq, nq=1, tq=sr, pos0=past, pos_stride=0, win_base=past // CK - 1,
                   n_sel=min(NSA_N_SEL, nbs + 1), v_t=True)
        o_d = _dsa_step(pad_rows(qd), pad_rows(qi), misc_r, kdt, vdt, kit, nseq=nseq, tq=sr, pos=past,
                        n_keep=min(DSA_TOPK, (past + 1) // 4), n_chunks=n_chunks_s)
        o_a = o_a.reshape(nseq, sr, -1)[:, 0]
        o_d = o_d.reshape(nseq, sr, -1)[:, 0]
        o_b, o_c, hx_new, v_rows = _step_mix(
            glu, uv, conv_hist[l], conv_w[l], vec(conv_b, l), vec(conv_ln_g, l), vec(conv_ln_b, l),
            vec(sgu_ln_g, l), vec(sgu_ln_b, l), sgu_diag[l], sgu_bias0[l])
        xs = _merge(xs, vec(norm1_g, l), o_a, o_b, o_c, o_d, wg[l], wb[l], wo[l], tm=nseq)
        res = _mlp(xs, vec(norm2_g, l), wu[l], wd[l], final_g[None, :], tm=nseq, final=final)
        xs = res[0]
        if final:
            ys = res[1]
        kva3 = kva.reshape(nseq, 1, 6, HEAD_DIM)
        outs["nsa_s"].append(kva3[:, :, :4])
        outs["dsa_s"].append(dsa_rows.reshape(nseq, 1, 3, HEAD_DIM))
        outs["win_s"].append(jnp.concatenate([state_win[:, l], kva3[:, :, 4:]], axis=1)[:, 1:])
        outs["conv_s"].append(jnp.concatenate([state_conv[:, l], hx_new[:, None, :]], axis=1)[:, 1:])
        outs["sgu_s"].append(v_rows[:, None, :])

    st = lambda k: jnp.stack(outs[k], axis=1)
    return (yp.reshape(nb_, seq, d), ys.reshape(nseq, 1, d), st("nsa_p"), st("nsa_s"), st("dsa_p"), st("dsa_s"),
            st("win_p"), st("win_s"), st("conv_p"), st("conv_s"), st("sgu_s"))
```

```python
import functools

import numpy as np
import jax
import jax.numpy as jnp
from jax import lax
from jax.experimental import pallas as pl
from jax.experimental.pallas import tpu as pltpu

F32, BF16, I32 = jnp.float32, jnp.bfloat16, jnp.int32

HEAD_DIM = 64
NSA_HEADS = 4
NSA_BLOCK = 64
NSA_N_SEL = 16
NSA_WINDOW = 512
FORCE_BONUS = 1.0e4
CONV_CH = 256
CONV_WIDTH = 31
SGU_CH = 256
SGU_GROUPS = 4
SGU_CHUNK = 128
DSA_HEADS = 4
IDX_HEADS = 4
DSA_TOPK = 256
N_BRANCH = 4
BRANCH_WIDTH = 256
Q_BLOCK = 128
EPS = 1e-6
NEG = -1e30

SPLIT_WIDTHS = (256, 384, 12, 512, 512, 256, 128, 256, 64, 4)

CK = 512
NB_PAD = 128
SAMPLE_ROWS = 16
VMEM_LIMIT = 56 * 1024 * 1024
INT_MIN = -2 ** 31

W_QA, W_KVA, W_GLU, W_UV, W_QD, W_QI, W_MISC, W_DSA = 0, 256, 640, 1152, 1664, 1920, 2176, 2304
W_TOTAL = 2496
MISC_WI, MISC_GA = 0, 4


def _cparams(n_axes):
    return pltpu.CompilerParams(dimension_semantics=("arbitrary",) * n_axes,
                                vmem_limit_bytes=VMEM_LIMIT)


def _dot(a, b):
    return jnp.dot(a, b, preferred_element_type=F32)


def _dot_nt(a, b):
    return lax.dot_general(a, b, (((1,), (1,)), ((), ())), preferred_element_type=F32)


def _rms(x, g):
    return x * lax.rsqrt(jnp.mean(x * x, axis=-1, keepdims=True) + EPS) * g


def _layernorm(x, g, b):
    mu = jnp.mean(x, axis=-1, keepdims=True)
    xc = x - mu
    var = jnp.mean(xc * xc, axis=-1, keepdims=True)
    return xc * lax.rsqrt(var + EPS) * g + b


def _sigmoid(x):
    return 1.0 / (1.0 + jnp.exp(-x))


def _tile4(a):
    return jnp.concatenate([a, a, a, a], axis=0)


def _proj_kernel(x_ref, g_ref, w_ref, wt_ref, qn_ref, kva_ref, glu_ref, uv_ref, qd_ref, qi_ref,
                 misc_ref, dsa_ref, *t_refs):
    hb = _rms(x_ref[...], g_ref[...]).astype(BF16)

    def slab(off, width):
        return _dot(hb, w_ref[:, off:off + width])

    qa = slab(W_QA, 256)
    qd = slab(W_QD, 256)
    qi = slab(W_QI, 256)
    for h in range(4):
        cols = slice(h * HEAD_DIM, (h + 1) * HEAD_DIM)
        qn_ref[h] = qa[:, cols].astype(BF16)
        qd_ref[h] = qd[:, cols].astype(BF16)
        qi_ref[h] = qi[:, cols].astype(BF16)
    kv = slab(W_KVA, 384)
    kva_ref[...] = kv
    glu_ref[...] = slab(W_GLU, 512)
    uv_ref[...] = slab(W_UV, 512)
    misc_ref[...] = slab(W_MISC, 128)
    dr = slab(W_DSA, 192)
    dsa_ref[...] = dr
    if t_refs:
        kt_ref, v_ref = t_refs
        kt_ref[0] = _dot_nt(wt_ref[...], hb).astype(BF16)
        ones = jnp.ones((kv.shape[0], HEAD_DIM), F32)
        v_ref[0] = jnp.concatenate([kv[:, 192:256], ones], axis=-1).astype(BF16)
        v_ref[1] = jnp.concatenate([kv[:, 320:384], ones], axis=-1).astype(BF16)
        v_ref[2] = jnp.concatenate([dr[:, 64:128], ones], axis=-1).astype(BF16)


def _proj(x, g, w, wt, *, tm, emit_t):
    n, d = x.shape
    sds = jax.ShapeDtypeStruct
    out_shape = [sds((4, n, HEAD_DIM), BF16), sds((n, 384), F32), sds((n, 512), F32), sds((n, 512), F32),
                 sds((4, n, HEAD_DIM), BF16), sds((4, n, HEAD_DIM), BF16), sds((n, 128), F32), sds((n, 192), F32)]
    head_spec = pl.BlockSpec((4, tm, HEAD_DIM), lambda i: (0, i, 0))

    def row_spec(width):
        return pl.BlockSpec((tm, width), lambda i: (i, 0))

    out_specs = [head_spec, row_spec(384), row_spec(512), row_spec(512), head_spec, head_spec,
                 row_spec(128), row_spec(192)]
    if emit_t:
        assert tm == CK
        out_shape += [sds((n // CK, 256, CK), BF16), sds((3, n, 2 * HEAD_DIM), BF16)]
        out_specs += [pl.BlockSpec((1, 256, CK), lambda i: (i, 0, 0)),
                      pl.BlockSpec((3, tm, 2 * HEAD_DIM), lambda i: (0, i, 0))]
    return pl.pallas_call(
        _proj_kernel,
        out_shape=out_shape,
        grid=(n // tm,),
        in_specs=[row_spec(d), pl.BlockSpec((1, d), lambda i: (0, 0)),
                  pl.BlockSpec((d, W_TOTAL), lambda i: (0, 0)),
                  pl.BlockSpec((256, d), lambda i: (0, 0))],
        out_specs=out_specs,
        compiler_params=_cparams(1),
        name="proj",
    )(x, g, w, wt)


def _compress_kernel(x_ref, pos_ref, w1_ref, w2_ref, o_ref):
    n_blocks = o_ref.shape[0]
    acc = jnp.zeros((n_blocks, 2 * HEAD_DIM), F32)
    for r in range(NSA_BLOCK):
        xr = x_ref[pl.ds(r, n_blocks, stride=NSA_BLOCK), :] + pos_ref[r:r + 1, :]
        acc = acc + _dot(xr.astype(BF16), w1_ref[r])
    a = acc * _sigmoid(acc)
    o_ref[...] = _dot(a.astype(BF16), w2_ref[...]).astype(o_ref.dtype)


def _compress(x, pos, w1, w2, *, rows_per_step):
    r = x.shape[0]
    width = 2 * HEAD_DIM
    return pl.pallas_call(
        _compress_kernel,
        out_shape=jax.ShapeDtypeStruct((r // NSA_BLOCK, width), BF16),
        grid=(r // rows_per_step,),
        in_specs=[pl.BlockSpec((rows_per_step, width), lambda i: (i, 0)),
                  pl.BlockSpec((NSA_BLOCK, width), lambda i: (0, 0)),
                  pl.BlockSpec((NSA_BLOCK, width, width), lambda i: (0, 0, 0)),
                  pl.BlockSpec((width, width), lambda i: (0, 0))],
        out_specs=pl.BlockSpec((rows_per_step // NSA_BLOCK, width), lambda i: (i, 0)),
        compiler_params=_cparams(1),
        name="compress",
    )(x, pos, w1, w2)


def _masked_probs(s, mask):
    s = jnp.where(mask, s, NEG)
    m = jnp.max(s, axis=-1, keepdims=True)
    p = jnp.where(mask, jnp.exp(s - m), 0.0)
    return p / jnp.maximum(jnp.sum(p, axis=-1, keepdims=True), 1e-30)


def _softmax_step(s, bias, v1, carry):
    m, acc = carry
    tq = bias.shape[0]
    s = s.reshape(4, tq, CK) + bias[None]
    m_new = jnp.maximum(m, jnp.max(s, axis=-1, keepdims=True))
    alpha = jnp.exp(m - m_new)
    p = jnp.exp(s - m_new)
    acc = alpha.reshape(4 * tq, 1) * acc + _dot(p.reshape(4 * tq, CK).astype(BF16), v1)
    return m_new, acc


def _softmax_init(tq):
    return jnp.full((4, tq, 1), NEG, F32), jnp.zeros((4 * tq, 2 * HEAD_DIM), F32)


def _softmax_done(carry):
    _, acc = carry
    return acc[:, 0:HEAD_DIM] / jnp.maximum(acc[:, HEAD_DIM:HEAD_DIM + 1], 1e-30)


def _softmax_all(chunks, tq, v_t):
    ss = [s.reshape(4, tq, CK) + bias[None] for s, bias, _ in chunks]
    m = functools.reduce(jnp.maximum, [jnp.max(s, axis=-1, keepdims=True) for s in ss])
    ps = [jnp.exp(s - m) for s in ss]
    l = functools.reduce(jnp.add, [jnp.sum(p, axis=-1, keepdims=True) for p in ps])
    pv = _dot_nt if v_t else _dot
    acc = functools.reduce(jnp.add, [pv(p.reshape(4 * tq, CK).astype(BF16), v) for p, (_, _, v) in zip(ps, chunks)])
    return acc[:, 0:HEAD_DIM] / jnp.maximum(l.reshape(4 * tq, 1), 1e-30)


def _mask_bias(allowed):
    return jnp.where(allowed, 0.0, NEG)


def _query_positions(tq, pos0, pos_stride):
    if pos_stride == 0:
        return pos0, pos0, jnp.full((tq, 1), pos0, I32)
    i = pl.program_id(1)
    start = pos0 + i * (tq * pos_stride)
    last = start + (tq - 1) * pos_stride
    t = lax.broadcasted_iota(I32, (tq, 1), 0)
    qpos = start + t * pos_stride
    return start, last, qpos


def _heads_as_rows(ref, scale):
    return jnp.concatenate([ref[h] for h in range(4)], axis=0) * scale


def _key_positions(c):
    return c * CK + lax.broadcasted_iota(I32, (1, CK), 1)


def _chunk_rows(ref, c):
    if isinstance(c, int):
        return ref[c * CK:(c + 1) * CK, :]
    return ref[pl.ds(pl.multiple_of(c * CK, CK), CK), :]


PICK_LANES = 128


def _nsa_kernel(q_ref, misc_ref, kvc_ref, kst_ref, vs_ref, kwt_ref, vw_ref, o_ref, *,
                tq, pos0, pos_stride, win_base, n_sel, v_t):
    values = (lambda ref, c: ref[c]) if v_t else _chunk_rows
    nb = kvc_ref.shape[0]
    start, last, qpos = _query_positions(tq, pos0, pos_stride)
    qpos4 = _tile4(qpos)
    q = _heads_as_rows(q_ref, HEAD_DIM ** -0.5)

    jb = lax.broadcasted_iota(I32, (1, nb), 1)
    mask_c = ((jb + 1) * NSA_BLOCK - 1) <= qpos4
    p_c = _masked_probs(_dot_nt(q, kvc_ref[:, 0:HEAD_DIM]), mask_c)
    o_c = _dot(p_c.astype(BF16), kvc_ref[:, HEAD_DIM:2 * HEAD_DIM])
    imp = p_c[0:tq] + p_c[tq:2 * tq] + p_c[2 * tq:3 * tq] + p_c[3 * tq:4 * tq]

    cur = qpos // NSA_BLOCK
    forced = (jb == 0) | (jb == cur) | (jb == cur - 1)
    score = jnp.where(jb <= cur, imp, -1.0) + jnp.where(forced, FORCE_BONUS, 0.0)
    if tq < PICK_LANES:
        score = jnp.concatenate([score, jnp.zeros((PICK_LANES - tq, nb), F32)], axis=0)
    jcol = lax.broadcasted_iota(I32, (nb, PICK_LANES), 0).astype(F32)

    def pick(_, carry):
        sc, sel = carry
        m = jnp.max(sc, axis=0, keepdims=True)
        first = jnp.min(jnp.where(sc == m, jcol, 1e9), axis=0, keepdims=True)
        hit = jcol == first
        return jnp.where(hit, -3e38, sc), jnp.where(hit, 1.0, sel)

    _, sel_t = lax.fori_loop(0, n_sel, pick, (score.T, jnp.zeros((nb, PICK_LANES), F32)))
    sel = sel_t.T[0:tq].astype(BF16)

    jrow = lax.broadcasted_iota(I32, (nb, CK), 0)
    kcol = lax.broadcasted_iota(I32, (nb, CK), 1)

    def sel_parts(c):
        expand = jnp.where(jrow == ((c * CK + kcol) // NSA_BLOCK), 1.0, 0.0).astype(BF16)
        bias = _mask_bias((_dot(sel, expand) > 0.5) & (_key_positions(c) <= qpos))
        return _dot(q, kst_ref[c]), bias, values(vs_ref, c)

    def win_parts(c):
        cc = max(c - win_base, 0) if isinstance(c, int) else jnp.maximum(c - win_base, 0)
        kpos = _key_positions(c)
        dpos = qpos - kpos
        bias = _mask_bias((dpos >= 0) & (dpos < NSA_WINDOW) & (kpos >= 0))
        return _dot(q, kwt_ref[cc]), bias, values(vw_ref, cc)

    n_chunks = last // CK + 1
    c1 = start // CK
    if isinstance(n_chunks, int):
        o_s = _softmax_all([sel_parts(c) for c in range(n_chunks)], tq, v_t)
        o_w = _softmax_all([win_parts(c1 - 1), win_parts(c1)], tq, v_t)
    else:
        assert not v_t

        def sel_chunk(c, carry):
            return _softmax_step(*sel_parts(c), carry)

        o_s = _softmax_done(_chunk_loop(n_chunks, sel_chunk, _softmax_init(tq), unroll=ATTEND_UNROLL))
        carry = _softmax_init(tq)
        for c in (c1 - 1, c1):
            carry = _softmax_step(*win_parts(c), carry)
        o_w = _softmax_done(carry)

    g = _sigmoid(misc_ref[...])
    outs = []
    for h in range(4):
        rows = slice(h * tq, (h + 1) * tq)
        col = MISC_GA + 3 * h
        outs.append(g[:, col:col + 1] * o_c[rows] + g[:, col + 1:col + 2] * o_s[rows]
                    + g[:, col + 2:col + 3] * o_w[rows])
    o_ref[...] = jnp.concatenate(outs, axis=-1).astype(o_ref.dtype)


def _nsa(q, misc, kvc, ks, vs, kw, vw, *, nbatch, nq, tq, pos0, pos_stride, win_base, n_sel, v_t):
    nrows = q.shape[1]
    kern = functools.partial(_nsa_kernel, tq=tq, pos0=pos0, pos_stride=pos_stride, win_base=win_base,
                             n_sel=n_sel, v_t=v_t)
    return pl.pallas_call(
        kern,
        out_shape=jax.ShapeDtypeStruct((nrows, 4 * HEAD_DIM), BF16),
        grid=(nbatch, nq),
        in_specs=[pl.BlockSpec((4, tq, HEAD_DIM), lambda b, i: (0, b * nq + i, 0)),
                  pl.BlockSpec((tq, 128), lambda b, i: (b * nq + i, 0)),
                  pl.BlockSpec((None, NB_PAD, 2 * HEAD_DIM), lambda b, i: (b, 0, 0)),
                  ks[1], vs[1], kw[1], vw[1]],
        out_specs=pl.BlockSpec((tq, 4 * HEAD_DIM), lambda b, i: (b * nq + i, 0)),
        compiler_params=_cparams(2),
        name="nsa",
    )(q, misc, kvc, ks[0], vs[0], kw[0], vw[0])


LANES = 128
COUNT_UNROLL = 4
ATTEND_UNROLL = 4
INDEX_UNROLL = 4


def _index_keys(qi, wcol, kit, kpos, qpos, tq):
    rel = jnp.maximum(_dot(qi, kit), 0.0) * wcol
    sc = rel[0:tq] + rel[tq:2 * tq] + rel[2 * tq:3 * tq] + rel[3 * tq:4 * tq]
    sc = jnp.where(sc == 0.0, 0.0, sc)
    sc = jnp.where(kpos <= qpos, sc, NEG)
    bits = lax.bitcast_convert_type(sc, I32)
    return jnp.where(bits < 0, bits ^ 0x7FFFFFFF, bits)


def _index_weights(misc_ref):
    w = misc_ref[:, MISC_WI:MISC_WI + 4] * (IDX_HEADS ** -0.5 * HEAD_DIM ** -0.5)
    return jnp.concatenate([w[:, h:h + 1] for h in range(4)], axis=0)


def _chunk_loop(n_chunks, body, init, unroll=COUNT_UNROLL):
    if isinstance(n_chunks, int):
        carry = init
        for c in range(n_chunks):
            carry = body(c, carry)
        return carry

    carry, done, width = init, 0, unroll
    while width >= 1:
        def several(g, carry, done=done, width=width):
            for u in range(width):
                carry = body(done + g * width + u, carry)
            return carry

        n_trips = (n_chunks - done) // width
        carry = lax.fori_loop(0, n_trips, several, carry)
        done = done + n_trips * width
        width //= 2
    return carry


def _key_counter(key_ref, n_chunks):
    rows = key_ref.shape[1]

    def count(pred):
        def one(c, acc):
            for j in range(CK // LANES):
                k = key_ref[c, :, j * LANES:(j + 1) * LANES]
                acc = acc + jnp.where(pred(k, c * CK + j * LANES), 1.0, 0.0)
            return acc

        acc = _chunk_loop(n_chunks, one, jnp.zeros((rows, LANES), F32))
        return jnp.sum(acc, axis=-1, keepdims=True)

    return count


LATE_BITS = 4


def _top_keys(count, n_keep, n_all, late=None):
    keep = float(n_keep)
    n0 = count(lambda k, c: k >= 0)
    thr = jnp.where(n0 >= keep, 0, INT_MIN).astype(I32)
    above = jnp.where(n0 >= keep, 0.0, n0)
    at = jnp.where(n0 >= keep, n0, n_all)

    def thr_bit(b, state):
        thr, above, at = state
        cand = thr | jnp.left_shift(jnp.int32(1), 30 - b)
        n = count(lambda k, c: k >= cand)
        take = n >= keep
        return jnp.where(take, cand, thr), jnp.where(take, above, n), jnp.where(take, n, at)

    if late is None:
        thr, above, _ = lax.fori_loop(0, 31, thr_bit, (thr, above, at))
        return thr, keep - above

    thr, above, at = lax.fori_loop(0, 31 - LATE_BITS, thr_bit, (thr, above, at))
    thr_scr, above_scr = late
    n_above = count(lambda k, c: k > thr)
    settled = (at == keep) | (n_above < keep)
    thr_scr[...] = jnp.broadcast_to(thr, thr_scr.shape)
    above_scr[...] = jnp.broadcast_to(n_above, above_scr.shape)

    @pl.when(jnp.min(jnp.where(settled, 1.0, 0.0)) < 0.5)
    def _():
        t, a, _ = lax.fori_loop(31 - LATE_BITS, 31, thr_bit, (thr, above, at))
        thr_scr[...] = jnp.broadcast_to(t, thr_scr.shape)
        above_scr[...] = jnp.broadcast_to(a, above_scr.shape)

    return thr_scr[:, 0:1], keep - above_scr[:, 0:1]


def _tie_cut(count, thr, need, rows, n_positions):
    n_pos_bits = (n_positions - 1).bit_length()

    def cut_bit(b, cut):
        cand = cut + jnp.left_shift(jnp.int32(1), n_pos_bits - 1 - b)
        n = count(lambda k, first: (k == thr) & (first + lax.broadcasted_iota(I32, (1, LANES), 1) < cand))
        return jnp.where(n < need, cand, cut)

    return lax.fori_loop(0, n_pos_bits, cut_bit, jnp.zeros((rows, 1), I32))


def _heads_to_lanes(o, tq):
    return jnp.concatenate([o[h * tq:(h + 1) * tq] for h in range(4)], axis=-1)


def _dsa_kernel(qd_ref, qi_ref, misc_ref, kdt_ref, vd_ref, kit_ref, o_ref, key_scr, thr_scr, above_scr, *,
                tq, n_keep):
    start, last, qpos = _query_positions(tq, 0, 1)
    n_chunks = last // CK + 1
    qd = _heads_as_rows(qd_ref, HEAD_DIM ** -0.5)
    qi = _heads_as_rows(qi_ref, 1.0)
    wcol = _index_weights(misc_ref)

    def index_chunk(c, _):
        key_scr[c] = _index_keys(qi, wcol, kit_ref[c], _key_positions(c), qpos, tq)
        return 0

    _chunk_loop(n_chunks, index_chunk, 0, unroll=INDEX_UNROLL)
    thr, need = _top_keys(_key_counter(key_scr, n_chunks), n_keep, (n_chunks * CK).astype(F32),
                          late=(thr_scr, above_scr))

    tri = jnp.where(lax.broadcasted_iota(I32, (LANES, LANES), 0) <= lax.broadcasted_iota(I32, (LANES, LANES), 1),
                    1.0, 0.0).astype(BF16)

    def attend_chunk(c, carry):
        state, seen = carry
        k = key_scr[c]
        ranks = []
        for j in range(CK // LANES):
            equal = jnp.where(k[:, j * LANES:(j + 1) * LANES] == thr, 1.0, 0.0)
            ranks.append(seen + _dot(equal.astype(BF16), tri))
            seen = seen + jnp.sum(equal, axis=-1, keepdims=True)
        rank = jnp.concatenate(ranks, axis=-1)
        taken = (k > thr) | ((k == thr) & (rank <= need))
        bias = _mask_bias(taken & (_key_positions(c) <= qpos))
        state = _softmax_step(_dot(qd, kdt_ref[c]), bias, _chunk_rows(vd_ref, c), state)
        return state, seen

    state, _ = _chunk_loop(n_chunks, attend_chunk, (_softmax_init(tq), jnp.zeros((tq, 1), F32)),
                           unroll=ATTEND_UNROLL)
    o_ref[...] = _heads_to_lanes(_softmax_done(state), tq).astype(o_ref.dtype)


def _dsa(qd, qi, misc, kd, vd, ki, *, nbatch, nq, tq, n_keep, n_chunks):
    nrows = qd.shape[1]
    head_spec = pl.BlockSpec((4, tq, HEAD_DIM), lambda b, i: (0, b * nq + i, 0))
    return pl.pallas_call(
        functools.partial(_dsa_kernel, tq=tq, n_keep=n_keep),
        out_shape=jax.ShapeDtypeStruct((nrows, 4 * HEAD_DIM), BF16),
        grid=(nbatch, nq),
        in_specs=[head_spec, head_spec, pl.BlockSpec((tq, 128), lambda b, i: (b * nq + i, 0)),
                  kd[1], vd[1], ki[1]],
        out_specs=pl.BlockSpec((tq, 4 * HEAD_DIM), lambda b, i: (b * nq + i, 0)),
        scratch_shapes=[pltpu.VMEM((n_chunks, tq, CK), I32), pltpu.VMEM((tq, LANES), I32),
                        pltpu.VMEM((tq, LANES), F32)],
        compiler_params=_cparams(2),
        name="dsa",
    )(qd, qi, misc, kd[0], vd[0], ki[0])


def _dsa_step_index_kernel(qi_ref, misc_ref, kit_ref, key_ref, *, tq, pos, n_chunks):
    qi = _heads_as_rows(qi_ref, 1.0)
    wcol = _index_weights(misc_ref)
    for c in range(n_chunks):
        key_ref[c] = _index_keys(qi, wcol, kit_ref[c], _key_positions(c), pos, tq)[0:1, :]


def _dsa_step_top_kernel(key_ref, thr_ref, cut_ref, *, n_keep, n_chunks):
    rows = key_ref.shape[1]
    count = _key_counter(key_ref, n_chunks)
    thr, need = _top_keys(count, n_keep, float(n_chunks * CK))
    thr_ref[...] = jnp.broadcast_to(thr, thr_ref.shape)
    cut_ref[...] = jnp.broadcast_to(_tie_cut(count, thr, need, rows, n_chunks * CK), cut_ref.shape)


def _dsa_step_attend_kernel(qd_ref, key_ref, thr_ref, cut_ref, kdt_ref, vdt_ref, o_ref, *, tq, pos, n_chunks):
    qd = _heads_as_rows(qd_ref, HEAD_DIM ** -0.5)
    thr = thr_ref[:, 0:1]
    cut = cut_ref[:, 0:1]
    chunks = []
    for c in range(n_chunks):
        k = key_ref[c]
        kpos = _key_positions(c)
        bias = _mask_bias(((k > thr) | ((k == thr) & (kpos <= cut))) & (kpos <= pos))
        chunks.append((_dot(qd, kdt_ref[c]), jnp.broadcast_to(bias, (tq, CK)), vdt_ref[c]))
    o_ref[...] = _heads_to_lanes(_softmax_all(chunks, tq, True), tq).astype(o_ref.dtype)


def _dsa_step(qd, qi, misc, kdt, vd, kit, *, nseq, tq, pos, n_keep, n_chunks):
    head_spec = pl.BlockSpec((4, tq, HEAD_DIM), lambda b: (0, b, 0))
    chunk_spec = pl.BlockSpec((None, n_chunks, HEAD_DIM, CK), lambda b: (b, 0, 0, 0))
    key_spec = pl.BlockSpec((n_chunks, None, 1, CK), lambda b: (0, b, 0, 0))
    row_spec = pl.BlockSpec((None, 1, LANES), lambda b: (b, 0, 0))
    keys = pl.pallas_call(
        functools.partial(_dsa_step_index_kernel, tq=tq, pos=pos, n_chunks=n_chunks),
        out_shape=jax.ShapeDtypeStruct((n_chunks, nseq, 1, CK), I32),
        grid=(nseq,),
        in_specs=[head_spec, pl.BlockSpec((tq, 128), lambda b: (b, 0)), chunk_spec],
        out_specs=key_spec,
        compiler_params=_cparams(1),
        name="dsa_step_index",
    )(qi, misc, kit)
    thr, cut = pl.pallas_call(
        functools.partial(_dsa_step_top_kernel, n_keep=n_keep, n_chunks=n_chunks),
        out_shape=[jax.ShapeDtypeStruct((nseq, LANES), I32)] * 2,
        compiler_params=pltpu.CompilerParams(vmem_limit_bytes=VMEM_LIMIT),
        name="dsa_step_top",
    )(keys.reshape(n_chunks, nseq, CK))
    return pl.pallas_call(
        functools.partial(_dsa_step_attend_kernel, tq=tq, pos=pos, n_chunks=n_chunks),
        out_shape=jax.ShapeDtypeStruct((nseq * tq, 4 * HEAD_DIM), BF16),
        grid=(nseq,),
        in_specs=[head_spec, key_spec, row_spec, row_spec, chunk_spec, chunk_spec],
        out_specs=pl.BlockSpec((tq, 4 * HEAD_DIM), lambda b: (b, 0)),
        compiler_params=_cparams(1),
        name="dsa_step_attend",
    )(qd, keys, thr[:, None, :], cut[:, None, :], kdt, vd)


CONV_HALO = 32


def _conv_kernel(cur_ref, halo_ref, w_ref, b_ref, g_ref, beta_ref, o_ref, tail_ref, hx_scr, *, tc):
    i = pl.program_id(1)

    def glu(x):
        return x[:, :CONV_CH] * _sigmoid(x[:, CONV_CH:])

    hx_scr[0:CONV_HALO] = jnp.where(i > 0, glu(halo_ref[...]), 0.0)
    hx_scr[CONV_HALO:CONV_HALO + tc] = glu(cur_ref[...])
    first = CONV_HALO - (CONV_WIDTH - 1)
    y = jnp.broadcast_to(b_ref[...], (tc, CONV_CH))
    for k in range(CONV_WIDTH):
        y = y + hx_scr[first + k:first + k + tc] * w_ref[k:k + 1, :]
    y = _layernorm(y, g_ref[...], beta_ref[...])
    o_ref[...] = (y * _sigmoid(y)).astype(o_ref.dtype)
    tail_ref[...] = hx_scr[tc:tc + CONV_HALO]


def _conv(glu, w, b, g, beta, *, nbatch, seq, tc):
    n = glu.shape[0]
    nt = seq // tc
    per = tc // CONV_HALO
    vec = pl.BlockSpec((1, CONV_CH), lambda bb, i: (0, 0))
    return pl.pallas_call(
        functools.partial(_conv_kernel, tc=tc),
        out_shape=[jax.ShapeDtypeStruct((n, CONV_CH), BF16),
                   jax.ShapeDtypeStruct((nbatch, CONV_HALO, CONV_CH), F32)],
        grid=(nbatch, nt),
        in_specs=[pl.BlockSpec((tc, 2 * CONV_CH), lambda bb, i: (bb * nt + i, 0)),
                  pl.BlockSpec((CONV_HALO, 2 * CONV_CH),
                               lambda bb, i: (jnp.maximum((bb * nt + i) * per - 1, 0), 0)),
                  pl.BlockSpec((CONV_WIDTH, CONV_CH), lambda bb, i: (0, 0)), vec, vec, vec],
        out_specs=[pl.BlockSpec((tc, CONV_CH), lambda bb, i: (bb * nt + i, 0)),
                   pl.BlockSpec((None, CONV_HALO, CONV_CH), lambda bb, i: (bb, 0, 0))],
        scratch_shapes=[pltpu.VMEM((tc + CONV_HALO, CONV_CH), F32)],
        compiler_params=_cparams(2),
        name="conv",
    )(glu, glu, w, b, g, beta)


def _gelu(x):
    return 0.5 * x * (1.0 + lax.erf(x * (2.0 ** -0.5)))


def _sgu_kernel(uv_ref, g_ref, beta_ref, w_ref, bias_ref, o_ref, *, ts):
    a = _gelu(uv_ref[...])
    u = a[:, :SGU_CH]
    vn = _layernorm(a[:, SGU_CH:], g_ref[...], beta_ref[...]).astype(BF16)
    ri = lax.broadcasted_iota(I32, (SGU_CHUNK, SGU_CHUNK), 0)
    ci = lax.broadcasted_iota(I32, (SGU_CHUNK, SGU_CHUNK), 1)
    group = lax.broadcasted_iota(I32, (SGU_CHUNK, SGU_CH), 1) // (SGU_CH // SGU_GROUPS)
    ws = [jnp.where(ci <= ri, w_ref[gi], 0.0).astype(BF16) for gi in range(SGU_GROUPS)]
    for c in range(ts // SGU_CHUNK):
        rows = slice(c * SGU_CHUNK, (c + 1) * SGU_CHUNK)
        mixed = bias_ref[...]
        for gi in range(SGU_GROUPS):
            mixed = mixed + jnp.where(group == gi, _dot(ws[gi], vn[rows]), 0.0)
        o_ref[rows, :] = (u[rows] * mixed).astype(o_ref.dtype)


def _sgu(uv, g, beta, w, bias, *, ts):
    n = uv.shape[0]
    vec = pl.BlockSpec((1, SGU_CH), lambda i: (0, 0))
    return pl.pallas_call(
        functools.partial(_sgu_kernel, ts=ts),
        out_shape=jax.ShapeDtypeStruct((n, SGU_CH), BF16),
        grid=(n // ts,),
        in_specs=[pl.BlockSpec((ts, 2 * SGU_CH), lambda i: (i, 0)), vec, vec,
                  pl.BlockSpec((SGU_GROUPS, SGU_CHUNK, SGU_CHUNK), lambda i: (0, 0, 0)),
                  pl.BlockSpec((SGU_CHUNK, SGU_CH), lambda i: (0, 0))],
        out_specs=pl.BlockSpec((ts, SGU_CH), lambda i: (i, 0)),
        compiler_params=_cparams(1),
        name="sgu",
    )(uv, g, beta, w, bias)


def _step_mix_kernel(glu_ref, uv_ref, hist_ref, cw_ref, cb_ref, cg_ref, cbeta_ref, sg_ref, sbeta_ref,
                     sdiag_ref, sbias_ref, ob_ref, oc_ref, hx_ref, v_ref):
    x = glu_ref[...]
    hx = x[:, :CONV_CH] * _sigmoid(x[:, CONV_CH:])
    hx_ref[...] = hx
    y = cb_ref[...] + hx * cw_ref[CONV_WIDTH - 1:CONV_WIDTH, :]
    for k in range(CONV_WIDTH - 1):
        y = y + hist_ref[k] * cw_ref[k:k + 1, :]
    y = _layernorm(y, cg_ref[...], cbeta_ref[...])
    ob_ref[...] = (y * _sigmoid(y)).astype(ob_ref.dtype)
    a = _gelu(uv_ref[...])
    v = a[:, SGU_CH:]
    v_ref[...] = v
    vn = _layernorm(v, sg_ref[...], sbeta_ref[...])
    oc_ref[...] = (a[:, :SGU_CH] * (sdiag_ref[...] * vn + sbias_ref[...])).astype(oc_ref.dtype)


def _step_mix(glu, uv, hist, cw, cb, cg, cbeta, sg, sbeta, sdiag, sbias):
    n = glu.shape[0]
    sds = jax.ShapeDtypeStruct
    return pl.pallas_call(
        _step_mix_kernel,
        out_shape=[sds((n, CONV_CH), BF16), sds((n, SGU_CH), BF16), sds((n, CONV_CH), F32), sds((n, SGU_CH), F32)],
        compiler_params=pltpu.CompilerParams(vmem_limit_bytes=VMEM_LIMIT),
        name="step_mix",
    )(glu, uv, hist, cw, cb, cg, cbeta, sg, sbeta, sdiag, sbias)


def _merge_kernel(x_ref, g_ref, oa_ref, ob_ref, oc_ref, od_ref, wg_ref, wb_ref, wo_ref, y_ref):
    x = x_ref[...]
    d = x.shape[-1]
    hb = _rms(x, g_ref[...]).astype(BF16)
    acc = jnp.zeros(x.shape, F32)
    for k, o_ref in enumerate((oa_ref, ob_ref, oc_ref, od_ref)):
        gate = _sigmoid(_dot(hb, wg_ref[:, k * d:(k + 1) * d]))
        acc = acc + gate * _dot(o_ref[...], wb_ref[k])
    y_ref[...] = x + _dot(acc.astype(BF16), wo_ref[...])


def _merge(x, g, oa, ob, oc, od, wg, wb, wo, *, tm):
    n, d = x.shape
    row = pl.BlockSpec((tm, d), lambda i: (i, 0))
    br = pl.BlockSpec((tm, BRANCH_WIDTH), lambda i: (i, 0))
    once = pl.Buffered(1)
    return pl.pallas_call(
        _merge_kernel,
        out_shape=jax.ShapeDtypeStruct((n, d), F32),
        grid=(n // tm,),
        in_specs=[row, pl.BlockSpec((1, d), lambda i: (0, 0)), br, br, br, br,
                  pl.BlockSpec((d, N_BRANCH * d), lambda i: (0, 0), pipeline_mode=once),
                  pl.BlockSpec((N_BRANCH, BRANCH_WIDTH, d), lambda i: (0, 0, 0), pipeline_mode=once),
                  pl.BlockSpec((d, d), lambda i: (0, 0), pipeline_mode=once)],
        out_specs=row,
        compiler_params=_cparams(1),
        name="merge",
    )(x, g, oa, ob, oc, od, wg, wb, wo)


def _mlp_kernel(x_ref, g_ref, wu_ref, wd_ref, gf_ref, y_ref, *n_ref):
    x = x_ref[...]
    hb = _rms(x, g_ref[...]).astype(BF16)
    a = jnp.square(jnp.maximum(_dot(hb, wu_ref[...]), 0.0)).astype(BF16)
    y = x + _dot(a, wd_ref[...])
    y_ref[...] = y
    if n_ref:
        n_ref[0][...] = _rms(y, gf_ref[...])


def _mlp(x, g, wu, wd, gf, *, tm, final):
    n, d = x.shape
    row = pl.BlockSpec((tm, d), lambda i: (i, 0))
    vec = pl.BlockSpec((1, d), lambda i: (0, 0))
    once = pl.Buffered(1)
    out_shape = [jax.ShapeDtypeStruct((n, d), F32)] * (2 if final else 1)
    return pl.pallas_call(
        _mlp_kernel,
        out_shape=out_shape,
        grid=(n // tm,),
        in_specs=[row, vec, pl.BlockSpec(wu.shape, lambda i: (0, 0), pipeline_mode=once),
                  pl.BlockSpec(wd.shape, lambda i: (0, 0), pipeline_mode=once), vec],
        out_specs=[row] * (2 if final else 1),
        compiler_params=_cparams(1),
        name="mlp",
    )(x, g, wu, wd, gf)


def _assemble_kernel(pt_ref, *refs, n_pages, page):
    nsa_pages = refs[:n_pages]
    dsa_pages = refs[n_pages:2 * n_pages]
    win_ref, nsa_new_ref, dsa_new_ref = refs[2 * n_pages:2 * n_pages + 3]
    cmp_ref, kst_ref, vst_ref, kwt_ref, vwt_ref, kdt_ref, vdt_ref, kit_ref = refs[2 * n_pages + 3:]
    per_chunk = CK // page
    for p in range(n_pages):
        c, r = divmod(p, per_chunk)
        lanes = slice(r * page, (r + 1) * page)
        x = nsa_pages[p]
        cmp_ref[p * page:(p + 1) * page, :] = jnp.concatenate([x[0], x[1]], axis=0).T
        kst_ref[c, :, lanes] = x[2].astype(BF16)
        vst_ref[c, :, lanes] = x[3].astype(BF16)
        y = dsa_pages[p]
        kdt_ref[c, :, lanes] = y[0].astype(BF16)
        vdt_ref[c, :, lanes] = y[1].astype(BF16)
        kit_ref[c, :, lanes] = y[2].astype(BF16)

    first = lax.broadcasted_iota(I32, (HEAD_DIM, CK), 1) == 0

    def new_chunk(col):
        return jnp.where(first, jnp.broadcast_to(col, (HEAD_DIM, CK)), 0.0).astype(BF16)

    c_new = n_pages // per_chunk
    nsa_new = nsa_new_ref[...]
    dsa_new = dsa_new_ref[...]
    kst_ref[c_new] = new_chunk(nsa_new[2 * HEAD_DIM:3 * HEAD_DIM])
    vst_ref[c_new] = new_chunk(nsa_new[3 * HEAD_DIM:4 * HEAD_DIM])
    kdt_ref[c_new] = new_chunk(dsa_new[0:HEAD_DIM])
    vdt_ref[c_new] = new_chunk(dsa_new[HEAD_DIM:2 * HEAD_DIM])
    kit_ref[c_new] = new_chunk(dsa_new[2 * HEAD_DIM:3 * HEAD_DIM])
    kwt_ref[0] = win_ref[0].astype(BF16)
    vwt_ref[0] = win_ref[1].astype(BF16)
    kwt_ref[1] = new_chunk(nsa_new[4 * HEAD_DIM:5 * HEAD_DIM])
    vwt_ref[1] = new_chunk(nsa_new[5 * HEAD_DIM:6 * HEAD_DIM])


def _assemble(page_table, cache_nsa_t, cache_dsa_t, state_win_t, nsa_new, dsa_new, *, layer):
    nseq, n_pages = page_table.shape
    page = cache_nsa_t.shape[-1]
    n_chunks = n_pages * page // CK + 1
    sds = jax.ShapeDtypeStruct

    def page_spec(slots, p):
        return pl.BlockSpec((None, None, slots, HEAD_DIM, page), lambda b, pt: (pt[b, p], layer, 0, 0, 0))

    def per_seq(*shape):
        return pl.BlockSpec((None,) + shape, lambda b, pt: (b,) + (0,) * len(shape))

    chunks = per_seq(n_chunks, HEAD_DIM, CK)
    grid_spec = pltpu.PrefetchScalarGridSpec(
        num_scalar_prefetch=1,
        grid=(nseq,),
        in_specs=([page_spec(4, p) for p in range(n_pages)] + [page_spec(3, p) for p in range(n_pages)]
                  + [pl.BlockSpec((None, None, 2, HEAD_DIM, CK), lambda b, pt: (b, layer, 0, 0, 0)),
                     per_seq(6 * HEAD_DIM, 1), per_seq(3 * HEAD_DIM, 1)]),
        out_specs=[per_seq(n_pages * page, 2 * HEAD_DIM), chunks, chunks, per_seq(2, HEAD_DIM, CK),
                   per_seq(2, HEAD_DIM, CK), chunks, chunks, chunks],
    )
    chunk_shape = sds((nseq, n_chunks, HEAD_DIM, CK), BF16)
    win_shape = sds((nseq, 2, HEAD_DIM, CK), BF16)
    return pl.pallas_call(
        functools.partial(_assemble_kernel, n_pages=n_pages, page=page),
        out_shape=[sds((nseq, n_pages * page, 2 * HEAD_DIM), F32), chunk_shape, chunk_shape, win_shape, win_shape,
                   chunk_shape, chunk_shape, chunk_shape],
        grid_spec=grid_spec,
        compiler_params=_cparams(1),
        name="assemble",
    )(page_table, *([cache_nsa_t] * n_pages), *([cache_dsa_t] * n_pages), state_win_t, nsa_new, dsa_new)


def _prep_w_in(w_in):
    pts = np.cumsum(SPLIT_WIDTHS)[:-1].tolist()
    qa, kva, ga, glu, uv, qd, kvd, qi, ki, wi = jnp.split(w_in, pts, axis=-1)
    pad = jnp.zeros(w_in.shape[:2] + (128 - 16,), w_in.dtype)
    w = jnp.concatenate([qa, kva, glu, uv, qd, qi, wi, ga, pad, kvd, ki], axis=-1).astype(BF16)
    wt = jnp.concatenate([kva[..., 128:192], kva[..., 256:320], kvd[..., 0:64], ki], axis=-1)
    return w, jnp.swapaxes(wt, 1, 2).astype(BF16)


def _seq_spec(*shape):
    return pl.BlockSpec((None,) + shape, lambda b, i: (b,) + (0,) * len(shape))


def kernel(x_prompt, x_sample, cache_nsa, cache_dsa, state_win, state_conv, page_table, norm1_g, norm2_g,
           final_g, w_in, cmp_pos, cmp_w1, cmp_w2, conv_w, conv_b, conv_ln_g, conv_ln_b, sgu_ln_g, sgu_ln_b,
           sgu_w, sgu_b, w_branch, w_gate, w_out, w_up, w_down):
    nb_, seq, d = x_prompt.shape
    nseq, t_dec, _ = x_sample.shape
    depth = w_in.shape[0]
    n_pool, _, page, _, _ = cache_nsa.shape
    n_pages = page_table.shape[1]
    past = n_pages * page
    w_buf = state_win.shape[2]
    assert t_dec == 1 and seq % CK == 0 and past % CK == 0 and w_buf == CK == NSA_WINDOW
    nbp = seq // NSA_BLOCK
    assert nbp <= NB_PAD and past // NSA_BLOCK < NB_PAD
    n_tok = nb_ * seq
    sr = SAMPLE_ROWS

    w_proj, w_proj_t = _prep_w_in(w_in)
    wg, wb, wo = w_gate.astype(BF16), w_branch.astype(BF16), w_out.astype(BF16)
    wu, wd = w_up.astype(BF16), w_down.astype(BF16)
    def block_diag(a, b):
        z = jnp.zeros_like(a)
        return jnp.concatenate([jnp.concatenate([a, z], axis=-1), jnp.concatenate([z, b], axis=-1)], axis=-2)

    c_pos = jnp.concatenate([cmp_pos[:, 0], cmp_pos[:, 1]], axis=-1)
    c_w1 = cmp_w1.reshape(depth, 2, NSA_BLOCK, HEAD_DIM, HEAD_DIM).astype(BF16)
    c_w1 = block_diag(c_w1[:, 0], c_w1[:, 1])
    c_w2 = block_diag(cmp_w2[:, 0], cmp_w2[:, 1]).astype(BF16)
    sgu_bias = jnp.repeat(jnp.swapaxes(sgu_b, 1, 2), SGU_CH // SGU_GROUPS, axis=2)
    sgu_diag = jnp.repeat(sgu_w[:, :, 0, 0], SGU_CH // SGU_GROUPS, axis=1)[:, None, :]
    sgu_bias0 = sgu_bias[:, 0:1, :]
    vec = lambda a, l: a[l][None, :]
    cache_nsa_t = jnp.transpose(cache_nsa, (0, 1, 3, 4, 2))
    cache_dsa_t = jnp.transpose(cache_dsa, (0, 1, 3, 4, 2))
    state_win_t = jnp.transpose(state_win, (0, 1, 3, 4, 2))
    conv_hist = jnp.transpose(state_conv, (1, 2, 0, 3))

    n_chunks_p = seq // CK
    n_chunks_s = past // CK + 1
    nq = seq // Q_BLOCK

    def chunks_of(arr, slot, n_chunks):
        return arr, pl.BlockSpec((n_chunks, HEAD_DIM, CK), lambda b, i: (b, slot, 0))

    def rows_of(arr, slot, n_rows):
        return arr, pl.BlockSpec((None, n_rows, 2 * HEAD_DIM), lambda b, i: (slot, b, 0))

    xp = x_prompt.reshape(n_tok, d)
    xs = x_sample.reshape(nseq, d)
    outs = {k: [] for k in ("nsa_p", "nsa_s", "dsa_p", "dsa_s", "win_p", "win_s", "conv_p", "conv_s", "sgu_s")}
    yp = ys = None
    for l in range(depth):
        final = l == depth - 1
        qn, kva, glu, uv, qd, qi, misc, dsa_rows, kt4, v4 = _proj(
            xp, vec(norm1_g, l), w_proj[l], w_proj_t[l], tm=CK, emit_t=True)
        kva3 = kva.reshape(nb_, seq, 6, HEAD_DIM)
        kvc = _compress(kva, c_pos[l], c_w1[l], c_w2[l], rows_per_step=seq)
        kvc = jnp.pad(kvc.reshape(nb_, nbp, 2 * HEAD_DIM), ((0, 0), (0, NB_PAD - nbp), (0, 0)))
        o_a = _nsa(qn, misc, kvc, chunks_of(kt4, 0, n_chunks_p), rows_of(v4, 0, seq),
                   chunks_of(kt4, 1, n_chunks_p), rows_of(v4, 1, seq),
                   nbatch=nb_, nq=nq, tq=Q_BLOCK, pos0=0, pos_stride=1, win_base=0,
                   n_sel=min(NSA_N_SEL, seq // NSA_BLOCK), v_t=False)
        o_d = _dsa(qd, qi, misc, chunks_of(kt4, 2, n_chunks_p), rows_of(v4, 2, seq),
                   chunks_of(kt4, 3, n_chunks_p), nbatch=nb_, nq=nq, tq=Q_BLOCK,
                   n_keep=min(DSA_TOPK, seq // 4), n_chunks=n_chunks_p)
        o_b, conv_tail = _conv(glu, conv_w[l], vec(conv_b, l), vec(conv_ln_g, l), vec(conv_ln_b, l),
                               nbatch=nb_, seq=seq, tc=CK)
        o_c = _sgu(uv, vec(sgu_ln_g, l), vec(sgu_ln_b, l), sgu_w[l], sgu_bias[l], ts=CK)
        xp = _merge(xp, vec(norm1_g, l), o_a, o_b, o_c, o_d, wg[l], wb[l], wo[l], tm=256)
        res = _mlp(xp, vec(norm2_g, l), wu[l], wd[l], final_g[None, :], tm=256, final=final)
        xp = res[0]
        if final:
            yp = res[1]
        outs["nsa_p"].append(kva3[:, :, :4])
        outs["dsa_p"].append(dsa_rows.reshape(nb_, seq, 3, HEAD_DIM))
        outs["win_p"].append(kva3[:, seq - min(NSA_WINDOW, seq):, 4:])
        outs["conv_p"].append(conv_tail[:, CONV_HALO - (CONV_WIDTH - 1):])

        qn, kva, glu, uv, qd, qi, misc, dsa_rows = _proj(
            xs, vec(norm1_g, l), w_proj[l], w_proj_t[l], tm=nseq, emit_t=False)
        cmp_past, kst, vst, kwt, vwt, kdt, vdt, kit = _assemble(
            page_table, cache_nsa_t, cache_dsa_t, state_win_t, kva[:, :, None], dsa_rows[:, :, None], layer=l)
        nbs = past // NSA_BLOCK
        kvc = _compress(cmp_past.reshape(nseq * past, 2 * HEAD_DIM), c_pos[l], c_w1[l], c_w2[l],
                        rows_per_step=min(8, nseq) * past)
        kvc = jnp.pad(kvc.reshape(nseq, nbs, 2 * HEAD_DIM), ((0, 0), (0, NB_PAD - nbs), (0, 0)))

        def pad_rows(a):
            a = a[..., :, None, :]
            widths = [(0, 0)] * (a.ndim - 2) + [(0, sr - 1), (0, 0)]
            a = jnp.pad(a, widths)
            return a.reshape(a.shape[:-3] + (nseq * sr, a.shape[-1]))

        misc_r = pad_rows(misc)
        o_a = _nsa(pad_rows(qn), misc_r, kvc,
                   (kst, _seq_spec(n_chunks_s, HEAD_DIM, CK)), (vst, _seq_spec(n_chunks_s, HEAD_DIM, CK)),
                   (kwt, _seq_spec(2, HEAD_DIM, CK)), (vwt, _seq_spec(2, HEAD_DIM, CK)),
                   nbatch=nseq, nq=1, tq=sr, pos0=past, pos_stride=0, win_base=past // CK - 1,
                   n_sel=min(NSA_N_SEL, nbs + 1), v_t=True)
        o_d = _dsa_step(pad_rows(qd), pad_rows(qi), misc_r, kdt, vdt, kit, nseq=nseq, tq=sr, pos=past,
                        n_keep=min(DSA_TOPK, (past + 1) // 4), n_chunks=n_chunks_s)
        o_a = o_a.reshape(nseq, sr, -1)[:, 0]
        o_d = o_d.reshape(nseq, sr, -1)[:, 0]
        o_b, o_c, hx_new, v_rows = _step_mix(
            glu, uv, conv_hist[l], conv_w[l], vec(conv_b, l), vec(conv_ln_g, l), vec(conv_ln_b, l),
            vec(sgu_ln_g, l), vec(sgu_ln_b, l), sgu_diag[l], sgu_bias0[l])
        xs = _merge(xs, vec(norm1_g, l), o_a, o_b, o_c, o_d, wg[l], wb[l], wo[l], tm=nseq)
        res = _mlp(xs, vec(norm2_g, l), wu[l], wd[l], final_g[None, :], tm=nseq, final=final)
        xs = res[0]
        if final:
            ys = res[1]
        kva3 = kva.reshape(nseq, 1, 6, HEAD_DIM)
        outs["nsa_s"].append(kva3[:, :, :4])
        outs["dsa_s"].append(dsa_rows.reshape(nseq, 1, 3, HEAD_DIM))
        outs["win_s"].append(jnp.concatenate([state_win[:, l], kva3[:, :, 4:]], axis=1)[:, 1:])
        outs["conv_s"].append(jnp.concatenate([state_conv[:, l], hx_new[:, None, :]], axis=1)[:, 1:])
        outs["sgu_s"].append(v_rows[:, None, :])

    st = lambda k: jnp.stack(outs[k], axis=1)
    return (yp.reshape(nb_, seq, d), ys.reshape(nseq, 1, d), st("nsa_p"), st("nsa_s"), st("dsa_p"), st("dsa_s"),
            st("win_p"), st("win_s"), st("conv_p"), st("conv_s"), st("sgu_s"))
```

```python
import functools

import numpy as np
import jax
import jax.numpy as jnp
from jax import lax
from jax.experimental import pallas as pl
from jax.experimental.pallas import tpu as pltpu

F32, BF16, I32 = jnp.float32, jnp.bfloat16, jnp.int32

HEAD_DIM = 64
NSA_HEADS = 4
NSA_BLOCK = 64
NSA_N_SEL = 16
NSA_WINDOW = 512
FORCE_BONUS = 1.0e4
CONV_CH = 256
CONV_WIDTH = 31
SGU_CH = 256
SGU_GROUPS = 4
SGU_CHUNK = 128
DSA_HEADS = 4
IDX_HEADS = 4
DSA_TOPK = 256
N_BRANCH = 4
BRANCH_WIDTH = 256
Q_BLOCK = 128
EPS = 1e-6
NEG = -1e30

SPLIT_WIDTHS = (256, 384, 12, 512, 512, 256, 128, 256, 64, 4)

CK = 512
NB_PAD = 128
SAMPLE_ROWS = 16
VMEM_LIMIT = 56 * 1024 * 1024
INT_MIN = -2 ** 31

W_QA, W_KVA, W_GLU, W_UV, W_QD, W_QI, W_MISC, W_DSA = 0, 256, 640, 1152, 1664, 1920, 2176, 2304
W_TOTAL = 2496
MISC_WI, MISC_GA = 0, 4


def _cparams(n_axes):
    return pltpu.CompilerParams(dimension_semantics=("arbitrary",) * n_axes,
                                vmem_limit_bytes=VMEM_LIMIT)


def _dot(a, b):
    return jnp.dot(a, b, preferred_element_type=F32)


def _dot_nt(a, b):
    return lax.dot_general(a, b, (((1,), (1,)), ((), ())), preferred_element_type=F32)


def _rms(x, g):
    return x * lax.rsqrt(jnp.mean(x * x, axis=-1, keepdims=True) + EPS) * g


def _layernorm(x, g, b):
    mu = jnp.mean(x, axis=-1, keepdims=True)
    xc = x - mu
    var = jnp.mean(xc * xc, axis=-1, keepdims=True)
    return xc * lax.rsqrt(var + EPS) * g + b


def _sigmoid(x):
    return 1.0 / (1.0 + jnp.exp(-x))


def _tile4(a):
    return jnp.concatenate([a, a, a, a], axis=0)


def _proj_kernel(x_ref, g_ref, w_ref, wt_ref, qn_ref, kva_ref, glu_ref, uv_ref, qd_ref, qi_ref,
                 misc_ref, dsa_ref, *t_refs):
    hb = _rms(x_ref[...], g_ref[...]).astype(BF16)

    def slab(off, width):
        return _dot(hb, w_ref[:, off:off + width])

    qa = slab(W_QA, 256)
    qd = slab(W_QD, 256)
    qi = slab(W_QI, 256)
    for h in range(4):
        cols = slice(h * HEAD_DIM, (h + 1) * HEAD_DIM)
        qn_ref[h] = qa[:, cols].astype(BF16)
        qd_ref[h] = qd[:, cols].astype(BF16)
        qi_ref[h] = qi[:, cols].astype(BF16)
    kv = slab(W_KVA, 384)
    kva_ref[...] = kv
    glu_ref[...] = slab(W_GLU, 512)
    uv_ref[...] = slab(W_UV, 512)
    misc_ref[...] = slab(W_MISC, 128)
    dr = slab(W_DSA, 192)
    dsa_ref[...] = dr
    if t_refs:
        kt_ref, v_ref = t_refs
        kt_ref[0] = _dot_nt(wt_ref[...], hb).astype(BF16)
        ones = jnp.ones((kv.shape[0], HEAD_DIM), F32)
        v_ref[0] = jnp.concatenate([kv[:, 192:256], ones], axis=-1).astype(BF16)
        v_ref[1] = jnp.concatenate([kv[:, 320:384], ones], axis=-1).astype(BF16)
        v_ref[2] = jnp.concatenate([dr[:, 64:128], ones], axis=-1).astype(BF16)


def _proj(x, g, w, wt, *, tm, emit_t):
    n, d = x.shape
    sds = jax.ShapeDtypeStruct
    out_shape = [sds((4, n, HEAD_DIM), BF16), sds((n, 384), F32), sds((n, 512), F32), sds((n, 512), F32),
                 sds((4, n, HEAD_DIM), BF16), sds((4, n, HEAD_DIM), BF16), sds((n, 128), F32), sds((n, 192), F32)]
    head_spec = pl.BlockSpec((4, tm, HEAD_DIM), lambda i: (0, i, 0))

    def row_spec(width):
        return pl.BlockSpec((tm, width), lambda i: (i, 0))

    out_specs = [head_spec, row_spec(384), row_spec(512), row_spec(512), head_spec, head_spec,
                 row_spec(128), row_spec(192)]
    if emit_t:
        assert tm == CK
        out_shape += [sds((n // CK, 256, CK), BF16), sds((3, n, 2 * HEAD_DIM), BF16)]
        out_specs += [pl.BlockSpec((1, 256, CK), lambda i: (i, 0, 0)),
                      pl.BlockSpec((3, tm, 2 * HEAD_DIM), lambda i: (0, i, 0))]
    return pl.pallas_call(
        _proj_kernel,
        out_shape=out_shape,
        grid=(n // tm,),
        in_specs=[row_spec(d), pl.BlockSpec((1, d), lambda i: (0, 0)),
                  pl.BlockSpec((d, W_TOTAL), lambda i: (0, 0)),
                  pl.BlockSpec((256, d), lambda i: (0, 0))],
        out_specs=out_specs,
        compiler_params=_cparams(1),
        name="proj",
    )(x, g, w, wt)


def _compress_kernel(x_ref, pos_ref, w1_ref, w2_ref, o_ref):
    n_blocks = o_ref.shape[0]
    acc = jnp.zeros((n_blocks, 2 * HEAD_DIM), F32)
    for r in range(NSA_BLOCK):
        xr = x_ref[pl.ds(r, n_blocks, stride=NSA_BLOCK), :] + pos_ref[r:r + 1, :]
        acc = acc + _dot(xr.astype(BF16), w1_ref[r])
    a = acc * _sigmoid(acc)
    o_ref[...] = _dot(a.astype(BF16), w2_ref[...]).astype(o_ref.dtype)


def _compress(x, pos, w1, w2, *, rows_per_step):
    r = x.shape[0]
    width = 2 * HEAD_DIM
    return pl.pallas_call(
        _compress_kernel,
        out_shape=jax.ShapeDtypeStruct((r // NSA_BLOCK, width), BF16),
        grid=(r // rows_per_step,),
        in_specs=[pl.BlockSpec((rows_per_step, width), lambda i: (i, 0)),
                  pl.BlockSpec((NSA_BLOCK, width), lambda i: (0, 0)),
                  pl.BlockSpec((NSA_BLOCK, width, width), lambda i: (0, 0, 0)),
                  pl.BlockSpec((width, width), lambda i: (0, 0))],
        out_specs=pl.BlockSpec((rows_per_step // NSA_BLOCK, width), lambda i: (i, 0)),
        compiler_params=_cparams(1),
        name="compress",
    )(x, pos, w1, w2)


def _masked_probs(s, mask):
    s = jnp.where(mask, s, NEG)
    m = jnp.max(s, axis=-1, keepdims=True)
    p = jnp.where(mask, jnp.exp(s - m), 0.0)
    return p / jnp.maximum(jnp.sum(p, axis=-1, keepdims=True), 1e-30)


def _softmax_step(s, bias, v1, carry):
    m, acc = carry
    tq = bias.shape[0]
    s = s.reshape(4, tq, CK) + bias[None]
    m_new = jnp.maximum(m, jnp.max(s, axis=-1, keepdims=True))
    alpha = jnp.exp(m - m_new)
    p = jnp.exp(s - m_new)
    acc = alpha.reshape(4 * tq, 1) * acc + _dot(p.reshape(4 * tq, CK).astype(BF16), v1)
    return m_new, acc


def _softmax_init(tq):
    return jnp.full((4, tq, 1), NEG, F32), jnp.zeros((4 * tq, 2 * HEAD_DIM), F32)


def _softmax_done(carry):
    _, acc = carry
    return acc[:, 0:HEAD_DIM] / jnp.maximum(acc[:, HEAD_DIM:HEAD_DIM + 1], 1e-30)


def _softmax_all(chunks, tq, v_t):
    ss = [s.reshape(4, tq, CK) + bias[None] for s, bias, _ in chunks]
    m = functools.reduce(jnp.maximum, [jnp.max(s, axis=-1, keepdims=True) for s in ss])
    ps = [jnp.exp(s - m) for s in ss]
    l = functools.reduce(jnp.add, [jnp.sum(p, axis=-1, keepdims=True) for p in ps])
    pv = _dot_nt if v_t else _dot
    acc = functools.reduce(jnp.add, [pv(p.reshape(4 * tq, CK).astype(BF16), v) for p, (_, _, v) in zip(ps, chunks)])
    return acc[:, 0:HEAD_DIM] / jnp.maximum(l.reshape(4 * tq, 1), 1e-30)


def _mask_bias(allowed):
    return jnp.where(allowed, 0.0, NEG)


def _query_positions(tq, pos0, pos_stride):
    if pos_stride == 0:
        return pos0, pos0, jnp.full((tq, 1), pos0, I32)
    i = pl.program_id(1)
    start = pos0 + i * (tq * pos_stride)
    last = start + (tq - 1) * pos_stride
    t = lax.broadcasted_iota(I32, (tq, 1), 0)
    qpos = start + t * pos_stride
    return start, last, qpos


def _heads_as_rows(ref, scale):
    return jnp.concatenate([ref[h] for h in range(4)], axis=0) * scale


def _key_positions(c):
    return c * CK + lax.broadcasted_iota(I32, (1, CK), 1)


def _chunk_rows(ref, c):
    if isinstance(c, int):
        return ref[c * CK:(c + 1) * CK, :]
    return ref[pl.ds(pl.multiple_of(c * CK, CK), CK), :]


PICK_LANES = 128


def _nsa_kernel(q_ref, misc_ref, kvc_ref, kst_ref, vs_ref, kwt_ref, vw_ref, o_ref, *,
                tq, pos0, pos_stride, win_base, n_sel, v_t):
    values = (lambda ref, c: ref[c]) if v_t else _chunk_rows
    nb = kvc_ref.shape[0]
    start, last, qpos = _query_positions(tq, pos0, pos_stride)
    qpos4 = _tile4(qpos)
    q = _heads_as_rows(q_ref, HEAD_DIM ** -0.5)

    jb = lax.broadcasted_iota(I32, (1, nb), 1)
    mask_c = ((jb + 1) * NSA_BLOCK - 1) <= qpos4
    p_c = _masked_probs(_dot_nt(q, kvc_ref[:, 0:HEAD_DIM]), mask_c)
    o_c = _dot(p_c.astype(BF16), kvc_ref[:, HEAD_DIM:2 * HEAD_DIM])
    imp = p_c[0:tq] + p_c[tq:2 * tq] + p_c[2 * tq:3 * tq] + p_c[3 * tq:4 * tq]

    cur = qpos // NSA_BLOCK
    forced = (jb == 0) | (jb == cur) | (jb == cur - 1)
    score = jnp.where(jb <= cur, imp, -1.0) + jnp.where(forced, FORCE_BONUS, 0.0)
    if tq < PICK_LANES:
        score = jnp.concatenate([score, jnp.zeros((PICK_LANES - tq, nb), F32)], axis=0)
    jcol = lax.broadcasted_iota(I32, (nb, PICK_LANES), 0).astype(F32)

    def pick(_, carry):
        sc, sel = carry
        m = jnp.max(sc, axis=0, keepdims=True)
        first = jnp.min(jnp.where(sc == m, jcol, 1e9), axis=0, keepdims=True)
        hit = jcol == first
        return jnp.where(hit, -3e38, sc), jnp.where(hit, 1.0, sel)

    _, sel_t = lax.fori_loop(0, n_sel, pick, (score.T, jnp.zeros((nb, PICK_LANES), F32)))
    sel = sel_t.T[0:tq].astype(BF16)

    jrow = lax.broadcasted_iota(I32, (nb, CK), 0)
    kcol = lax.broadcasted_iota(I32, (nb, CK), 1)

    def sel_parts(c):
        expand = jnp.where(jrow == ((c * CK + kcol) // NSA_BLOCK), 1.0, 0.0).astype(BF16)
        bias = _mask_bias((_dot(sel, expand) > 0.5) & (_key_positions(c) <= qpos))
        return _dot(q, kst_ref[c]), bias, values(vs_ref, c)

    def win_parts(c):
        cc = max(c - win_base, 0) if isinstance(c, int) else jnp.maximum(c - win_base, 0)
        kpos = _key_positions(c)
        dpos = qpos - kpos
        bias = _mask_bias((dpos >= 0) & (dpos < NSA_WINDOW) & (kpos >= 0))
        return _dot(q, kwt_ref[cc]), bias, values(vw_ref, cc)

    n_chunks = last // CK + 1
    c1 = start // CK
    if isinstance(n_chunks, int):
        o_s = _softmax_all([sel_parts(c) for c in range(n_chunks)], tq, v_t)
        o_w = _softmax_all([win_parts(c1 - 1), win_parts(c1)], tq, v_t)
    else:
        assert not v_t

        def sel_chunk(c, carry):
            return _softmax_step(*sel_parts(c), carry)

        o_s = _softmax_done(_chunk_loop(n_chunks, sel_chunk, _softmax_init(tq), unroll=ATTEND_UNROLL))
        carry = _softmax_init(tq)
        for c in (c1 - 1, c1):
            carry = _softmax_step(*win_parts(c), carry)
        o_w = _softmax_done(carry)

    g = _sigmoid(misc_ref[...])
    outs = []
    for h in range(4):
        rows = slice(h * tq, (h + 1) * tq)
        col = MISC_GA + 3 * h
        outs.append(g[:, col:col + 1] * o_c[rows] + g[:, col + 1:col + 2] * o_s[rows]
                    + g[:, col + 2:col + 3] * o_w[rows])
    o_ref[...] = jnp.concatenate(outs, axis=-1).astype(o_ref.dtype)


def _nsa(q, misc, kvc, ks, vs, kw, vw, *, nbatch, nq, tq, pos0, pos_stride, win_base, n_sel, v_t):
    nrows = q.shape[1]
    kern = functools.partial(_nsa_kernel, tq=tq, pos0=pos0, pos_stride=pos_stride, win_base=win_base,
                             n_sel=n_sel, v_t=v_t)
    return pl.pallas_call(
        kern,
        out_shape=jax.ShapeDtypeStruct((nrows, 4 * HEAD_DIM), BF16),
        grid=(nbatch, nq),
        in_specs=[pl.BlockSpec((4, tq, HEAD_DIM), lambda b, i: (0, b * nq + i, 0)),
                  pl.BlockSpec((tq, 128), lambda b, i: (b * nq + i, 0)),
                  pl.BlockSpec((None, NB_PAD, 2 * HEAD_DIM), lambda b, i: (b, 0, 0)),
                  ks[1], vs[1], kw[1], vw[1]],
        out_specs=pl.BlockSpec((tq, 4 * HEAD_DIM), lambda b, i: (b * nq + i, 0)),
        compiler_params=_cparams(2),
        name="nsa",
    )(q, misc, kvc, ks[0], vs[0], kw[0], vw[0])


LANES = 128
COUNT_UNROLL = 4
ATTEND_UNROLL = 4
INDEX_UNROLL = 4


def _index_keys(qi, wcol, kit, kpos, qpos, tq):
    rel = jnp.maximum(_dot(qi, kit), 0.0) * wcol
    sc = rel[0:tq] + rel[tq:2 * tq] + rel[2 * tq:3 * tq] + rel[3 * tq:4 * tq]
    sc = jnp.where(sc == 0.0, 0.0, sc)
    sc = jnp.where(kpos <= qpos, sc, NEG)
    bits = lax.bitcast_convert_type(sc, I32)
    return jnp.where(bits < 0, bits ^ 0x7FFFFFFF, bits)


def _index_weights(misc_ref):
    w = misc_ref[:, MISC_WI:MISC_WI + 4] * (IDX_HEADS ** -0.5 * HEAD_DIM ** -0.5)
    return jnp.concatenate([w[:, h:h + 1] for h in range(4)], axis=0)


def _chunk_loop(n_chunks, body, init, unroll=COUNT_UNROLL):
    if isinstance(n_chunks, int):
        carry = init
        for c in range(n_chunks):
            carry = body(c, carry)
        return carry

    carry, done, width = init, 0, unroll
    while width >= 1:
        def several(g, carry, done=done, width=width):
            for u in range(width):
                carry = body(done + g * width + u, carry)
            return carry

        n_trips = (n_chunks - done) // width
        carry = lax.fori_loop(0, n_trips, several, carry)
        done = done + n_trips * width
        width //= 2
    return carry


def _key_counter(key_ref, n_chunks):
    rows = key_ref.shape[1]

    def count(pred):
        def one(c, acc):
            for j in range(CK // LANES):
                k = key_ref[c, :, j * LANES:(j + 1) * LANES]
                acc = acc + jnp.where(pred(k, c * CK + j * LANES), 1.0, 0.0)
            return acc

        acc = _chunk_loop(n_chunks, one, jnp.zeros((rows, LANES), F32))
        return jnp.sum(acc, axis=-1, keepdims=True)

    return count


LATE_BITS = 5


def _top_keys(count, n_keep, n_all, late=None):
    keep = float(n_keep)
    n0 = count(lambda k, c: k >= 0)
    thr = jnp.where(n0 >= keep, 0, INT_MIN).astype(I32)
    above = jnp.where(n0 >= keep, 0.0, n0)
    at = jnp.where(n0 >= keep, n0, n_all)

    def thr_bit(b, state):
        thr, above, at = state
        cand = thr | jnp.left_shift(jnp.int32(1), 30 - b)
        n = count(lambda k, c: k >= cand)
        take = n >= keep
        return jnp.where(take, cand, thr), jnp.where(take, above, n), jnp.where(take, n, at)

    if late is None:
        thr, above, _ = lax.fori_loop(0, 31, thr_bit, (thr, above, at))
        return thr, keep - above

    thr, above, at = lax.fori_loop(0, 31 - LATE_BITS, thr_bit, (thr, above, at))
    thr_scr, above_scr = late
    n_above = count(lambda k, c: k > thr)
    settled = (at == keep) | (n_above < keep)
    thr_scr[...] = jnp.broadcast_to(thr, thr_scr.shape)
    above_scr[...] = jnp.broadcast_to(n_above, above_scr.shape)

    @pl.when(jnp.min(jnp.where(settled, 1.0, 0.0)) < 0.5)
    def _():
        t, a, _ = lax.fori_loop(31 - LATE_BITS, 31, thr_bit, (thr, above, at))
        thr_scr[...] = jnp.broadcast_to(t, thr_scr.shape)
        above_scr[...] = jnp.broadcast_to(a, above_scr.shape)

    return thr_scr[:, 0:1], keep - above_scr[:, 0:1]


def _tie_cut(count, thr, need, rows, n_positions):
    n_pos_bits = (n_positions - 1).bit_length()

    def cut_bit(b, cut):
        cand = cut + jnp.left_shift(jnp.int32(1), n_pos_bits - 1 - b)
        n = count(lambda k, first: (k == thr) & (first + lax.broadcasted_iota(I32, (1, LANES), 1) < cand))
        return jnp.where(n < need, cand, cut)

    return lax.fori_loop(0, n_pos_bits, cut_bit, jnp.zeros((rows, 1), I32))


def _heads_to_lanes(o, tq):
    return jnp.concatenate([o[h * tq:(h + 1) * tq] for h in range(4)], axis=-1)


def _dsa_kernel(qd_ref, qi_ref, misc_ref, kdt_ref, vd_ref, kit_ref, o_ref, key_scr, thr_scr, above_scr, *,
                tq, n_keep):
    start, last, qpos = _query_positions(tq, 0, 1)
    n_chunks = last // CK + 1
    qd = _heads_as_rows(qd_ref, HEAD_DIM ** -0.5)
    qi = _heads_as_rows(qi_ref, 1.0)
    wcol = _index_weights(misc_ref)

    def index_chunk(c, _):
        key_scr[c] = _index_keys(qi, wcol, kit_ref[c], _key_positions(c), qpos, tq)
        return 0

    _chunk_loop(n_chunks, index_chunk, 0, unroll=INDEX_UNROLL)
    thr, need = _top_keys(_key_counter(key_scr, n_chunks), n_keep, (n_chunks * CK).astype(F32),
                          late=(thr_scr, above_scr))

    tri = jnp.where(lax.broadcasted_iota(I32, (LANES, LANES), 0) <= lax.broadcasted_iota(I32, (LANES, LANES), 1),
                    1.0, 0.0).astype(BF16)

    def attend_chunk(c, carry):
        state, seen = carry
        k = key_scr[c]
        ranks = []
        for j in range(CK // LANES):
            equal = jnp.where(k[:, j * LANES:(j + 1) * LANES] == thr, 1.0, 0.0)
            ranks.append(seen + _dot(equal.astype(BF16), tri))
            seen = seen + jnp.sum(equal, axis=-1, keepdims=True)
        rank = jnp.concatenate(ranks, axis=-1)
        taken = (k > thr) | ((k == thr) & (rank <= need))
        bias = _mask_bias(taken & (_key_positions(c) <= qpos))
        state = _softmax_step(_dot(qd, kdt_ref[c]), bias, _chunk_rows(vd_ref, c), state)
        return state, seen

    state, _ = _chunk_loop(n_chunks, attend_chunk, (_softmax_init(tq), jnp.zeros((tq, 1), F32)),
                           unroll=ATTEND_UNROLL)
    o_ref[...] = _heads_to_lanes(_softmax_done(state), tq).astype(o_ref.dtype)


def _dsa(qd, qi, misc, kd, vd, ki, *, nbatch, nq, tq, n_keep, n_chunks):
    nrows = qd.shape[1]
    head_spec = pl.BlockSpec((4, tq, HEAD_DIM), lambda b, i: (0, b * nq + i, 0))
    return pl.pallas_call(
        functools.partial(_dsa_kernel, tq=tq, n_keep=n_keep),
        out_shape=jax.ShapeDtypeStruct((nrows, 4 * HEAD_DIM), BF16),
        grid=(nbatch, nq),
        in_specs=[head_spec, head_spec, pl.BlockSpec((tq, 128), lambda b, i: (b * nq + i, 0)),
                  kd[1], vd[1], ki[1]],
        out_specs=pl.BlockSpec((tq, 4 * HEAD_DIM), lambda b, i: (b * nq + i, 0)),
        scratch_shapes=[pltpu.VMEM((n_chunks, tq, CK), I32), pltpu.VMEM((tq, LANES), I32),
                        pltpu.VMEM((tq, LANES), F32)],
        compiler_params=_cparams(2),
        name="dsa",
    )(qd, qi, misc, kd[0], vd[0], ki[0])


def _dsa_step_index_kernel(qi_ref, misc_ref, kit_ref, key_ref, *, tq, pos, n_chunks):
    qi = _heads_as_rows(qi_ref, 1.0)
    wcol = _index_weights(misc_ref)
    for c in range(n_chunks):
        key_ref[c] = _index_keys(qi, wcol, kit_ref[c], _key_positions(c), pos, tq)[0:1, :]


def _dsa_step_top_kernel(key_ref, thr_ref, cut_ref, *, n_keep, n_chunks):
    rows = key_ref.shape[1]
    count = _key_counter(key_ref, n_chunks)
    thr, need = _top_keys(count, n_keep, float(n_chunks * CK))
    thr_ref[...] = jnp.broadcast_to(thr, thr_ref.shape)
    cut_ref[...] = jnp.broadcast_to(_tie_cut(count, thr, need, rows, n_chunks * CK), cut_ref.shape)


def _dsa_step_attend_kernel(qd_ref, key_ref, thr_ref, cut_ref, kdt_ref, vdt_ref, o_ref, *, tq, pos, n_chunks):
    qd = _heads_as_rows(qd_ref, HEAD_DIM ** -0.5)
    thr = thr_ref[:, 0:1]
    cut = cut_ref[:, 0:1]
    chunks = []
    for c in range(n_chunks):
        k = key_ref[c]
        kpos = _key_positions(c)
        bias = _mask_bias(((k > thr) | ((k == thr) & (kpos <= cut))) & (kpos <= pos))
        chunks.append((_dot(qd, kdt_ref[c]), jnp.broadcast_to(bias, (tq, CK)), vdt_ref[c]))
    o_ref[...] = _heads_to_lanes(_softmax_all(chunks, tq, True), tq).astype(o_ref.dtype)


def _dsa_step(qd, qi, misc, kdt, vd, kit, *, nseq, tq, pos, n_keep, n_chunks):
    head_spec = pl.BlockSpec((4, tq, HEAD_DIM), lambda b: (0, b, 0))
    chunk_spec = pl.BlockSpec((None, n_chunks, HEAD_DIM, CK), lambda b: (b, 0, 0, 0))
    key_spec = pl.BlockSpec((n_chunks, None, 1, CK), lambda b: (0, b, 0, 0))
    row_spec = pl.BlockSpec((None, 1, LANES), lambda b: (b, 0, 0))
    keys = pl.pallas_call(
        functools.partial(_dsa_step_index_kernel, tq=tq, pos=pos, n_chunks=n_chunks),
        out_shape=jax.ShapeDtypeStruct((n_chunks, nseq, 1, CK), I32),
        grid=(nseq,),
        in_specs=[head_spec, pl.BlockSpec((tq, 128), lambda b: (b, 0)), chunk_spec],
        out_specs=key_spec,
        compiler_params=_cparams(1),
        name="dsa_step_index",
    )(qi, misc, kit)
    thr, cut = pl.pallas_call(
        functools.partial(_dsa_step_top_kernel, n_keep=n_keep, n_chunks=n_chunks),
        out_shape=[jax.ShapeDtypeStruct((nseq, LANES), I32)] * 2,
        compiler_params=pltpu.CompilerParams(vmem_limit_bytes=VMEM_LIMIT),
        name="dsa_step_top",
    )(keys.reshape(n_chunks, nseq, CK))
    return pl.pallas_call(
        functools.partial(_dsa_step_attend_kernel, tq=tq, pos=pos, n_chunks=n_chunks),
        out_shape=jax.ShapeDtypeStruct((nseq * tq, 4 * HEAD_DIM), BF16),
        grid=(nseq,),
        in_specs=[head_spec, key_spec, row_spec, row_spec, chunk_spec, chunk_spec],
        out_specs=pl.BlockSpec((tq, 4 * HEAD_DIM), lambda b: (b, 0)),
        compiler_params=_cparams(1),
        name="dsa_step_attend",
    )(qd, keys, thr[:, None, :], cut[:, None, :], kdt, vd)


CONV_HALO = 32


def _conv_kernel(cur_ref, halo_ref, w_ref, b_ref, g_ref, beta_ref, o_ref, tail_ref, hx_scr, *, tc):
    i = pl.program_id(1)

    def glu(x):
        return x[:, :CONV_CH] * _sigmoid(x[:, CONV_CH:])

    hx_scr[0:CONV_HALO] = jnp.where(i > 0, glu(halo_ref[...]), 0.0)
    hx_scr[CONV_HALO:CONV_HALO + tc] = glu(cur_ref[...])
    first = CONV_HALO - (CONV_WIDTH - 1)
    y = jnp.broadcast_to(b_ref[...], (tc, CONV_CH))
    for k in range(CONV_WIDTH):
        y = y + hx_scr[first + k:first + k + tc] * w_ref[k:k + 1, :]
    y = _layernorm(y, g_ref[...], beta_ref[...])
    o_ref[...] = (y * _sigmoid(y)).astype(o_ref.dtype)
    tail_ref[...] = hx_scr[tc:tc + CONV_HALO]


def _conv(glu, w, b, g, beta, *, nbatch, seq, tc):
    n = glu.shape[0]
    nt = seq // tc
    per = tc // CONV_HALO
    vec = pl.BlockSpec((1, CONV_CH), lambda bb, i: (0, 0))
    return pl.pallas_call(
        functools.partial(_conv_kernel, tc=tc),
        out_shape=[jax.ShapeDtypeStruct((n, CONV_CH), BF16),
                   jax.ShapeDtypeStruct((nbatch, CONV_HALO, CONV_CH), F32)],
        grid=(nbatch, nt),
        in_specs=[pl.BlockSpec((tc, 2 * CONV_CH), lambda bb, i: (bb * nt + i, 0)),
                  pl.BlockSpec((CONV_HALO, 2 * CONV_CH),
                               lambda bb, i: (jnp.maximum((bb * nt + i) * per - 1, 0), 0)),
                  pl.BlockSpec((CONV_WIDTH, CONV_CH), lambda bb, i: (0, 0)), vec, vec, vec],
        out_specs=[pl.BlockSpec((tc, CONV_CH), lambda bb, i: (bb * nt + i, 0)),
                   pl.BlockSpec((None, CONV_HALO, CONV_CH), lambda bb, i: (bb, 0, 0))],
        scratch_shapes=[pltpu.VMEM((tc + CONV_HALO, CONV_CH), F32)],
        compiler_params=_cparams(2),
        name="conv",
    )(glu, glu, w, b, g, beta)


def _gelu(x):
    return 0.5 * x * (1.0 + lax.erf(x * (2.0 ** -0.5)))


def _sgu_kernel(uv_ref, g_ref, beta_ref, w_ref, bias_ref, o_ref, *, ts):
    a = _gelu(uv_ref[...])
    u = a[:, :SGU_CH]
    vn = _layernorm(a[:, SGU_CH:], g_ref[...], beta_ref[...]).astype(BF16)
    ri = lax.broadcasted_iota(I32, (SGU_CHUNK, SGU_CHUNK), 0)
    ci = lax.broadcasted_iota(I32, (SGU_CHUNK, SGU_CHUNK), 1)
    group = lax.broadcasted_iota(I32, (SGU_CHUNK, SGU_CH), 1) // (SGU_CH // SGU_GROUPS)
    ws = [jnp.where(ci <= ri, w_ref[gi], 0.0).astype(BF16) for gi in range(SGU_GROUPS)]
    for c in range(ts // SGU_CHUNK):
        rows = slice(c * SGU_CHUNK, (c + 1) * SGU_CHUNK)
        mixed = bias_ref[...]
        for gi in range(SGU_GROUPS):
            mixed = mixed + jnp.where(group == gi, _dot(ws[gi], vn[rows]), 0.0)
        o_ref[rows, :] = (u[rows] * mixed).astype(o_ref.dtype)


def _sgu(uv, g, beta, w, bias, *, ts):
    n = uv.shape[0]
    vec = pl.BlockSpec((1, SGU_CH), lambda i: (0, 0))
    return pl.pallas_call(
        functools.partial(_sgu_kernel, ts=ts),
        out_shape=jax.ShapeDtypeStruct((n, SGU_CH), BF16),
        grid=(n // ts,),
        in_specs=[pl.BlockSpec((ts, 2 * SGU_CH), lambda i: (i, 0)), vec, vec,
                  pl.BlockSpec((SGU_GROUPS, SGU_CHUNK, SGU_CHUNK), lambda i: (0, 0, 0)),
                  pl.BlockSpec((SGU_CHUNK, SGU_CH), lambda i: (0, 0))],
        out_specs=pl.BlockSpec((ts, SGU_CH), lambda i: (i, 0)),
        compiler_params=_cparams(1),
        name="sgu",
    )(uv, g, beta, w, bias)


def _step_mix_kernel(glu_ref, uv_ref, hist_ref, cw_ref, cb_ref, cg_ref, cbeta_ref, sg_ref, sbeta_ref,
                     sdiag_ref, sbias_ref, ob_ref, oc_ref, hx_ref, v_ref):
    x = glu_ref[...]
    hx = x[:, :CONV_CH] * _sigmoid(x[:, CONV_CH:])
    hx_ref[...] = hx
    y = cb_ref[...] + hx * cw_ref[CONV_WIDTH - 1:CONV_WIDTH, :]
    for k in range(CONV_WIDTH - 1):
        y = y + hist_ref[k] * cw_ref[k:k + 1, :]
    y = _layernorm(y, cg_ref[...], cbeta_ref[...])
    ob_ref[...] = (y * _sigmoid(y)).astype(ob_ref.dtype)
    a = _gelu(uv_ref[...])
    v = a[:, SGU_CH:]
    v_ref[...] = v
    vn = _layernorm(v, sg_ref[...], sbeta_ref[...])
    oc_ref[...] = (a[:, :SGU_CH] * (sdiag_ref[...] * vn + sbias_ref[...])).astype(oc_ref.dtype)


def _step_mix(glu, uv, hist, cw, cb, cg, cbeta, sg, sbeta, sdiag, sbias):
    n = glu.shape[0]
    sds = jax.ShapeDtypeStruct
    return pl.pallas_call(
        _step_mix_kernel,
        out_shape=[sds((n, CONV_CH), BF16), sds((n, SGU_CH), BF16), sds((n, CONV_CH), F32), sds((n, SGU_CH), F32)],
        compiler_params=pltpu.CompilerParams(vmem_limit_bytes=VMEM_LIMIT),
        name="step_mix",
    )(glu, uv, hist, cw, cb, cg, cbeta, sg, sbeta, sdiag, sbias)


def _merge_kernel(x_ref, g_ref, oa_ref, ob_ref, oc_ref, od_ref, wg_ref, wb_ref, wo_ref, y_ref):
    x = x_ref[...]
    d = x.shape[-1]
    hb = _rms(x, g_ref[...]).astype(BF16)
    acc = jnp.zeros(x.shape, F32)
    for k, o_ref in enumerate((oa_ref, ob_ref, oc_ref, od_ref)):
        gate = _sigmoid(_dot(hb, wg_ref[:, k * d:(k + 1) * d]))
        acc = acc + gate * _dot(o_ref[...], wb_ref[k])
    y_ref[...] = x + _dot(acc.astype(BF16), wo_ref[...])


def _merge(x, g, oa, ob, oc, od, wg, wb, wo, *, tm):
    n, d = x.shape
    row = pl.BlockSpec((tm, d), lambda i: (i, 0))
    br = pl.BlockSpec((tm, BRANCH_WIDTH), lambda i: (i, 0))
    once = pl.Buffered(1)
    return pl.pallas_call(
        _merge_kernel,
        out_shape=jax.ShapeDtypeStruct((n, d), F32),
        grid=(n // tm,),
        in_specs=[row, pl.BlockSpec((1, d), lambda i: (0, 0)), br, br, br, br,
                  pl.BlockSpec((d, N_BRANCH * d), lambda i: (0, 0), pipeline_mode=once),
                  pl.BlockSpec((N_BRANCH, BRANCH_WIDTH, d), lambda i: (0, 0, 0), pipeline_mode=once),
                  pl.BlockSpec((d, d), lambda i: (0, 0), pipeline_mode=once)],
        out_specs=row,
        compiler_params=_cparams(1),
        name="merge",
    )(x, g, oa, ob, oc, od, wg, wb, wo)


def _mlp_kernel(x_ref, g_ref, wu_ref, wd_ref, gf_ref, y_ref, *n_ref):
    x = x_ref[...]
    hb = _rms(x, g_ref[...]).astype(BF16)
    a = jnp.square(jnp.maximum(_dot(hb, wu_ref[...]), 0.0)).astype(BF16)
    y = x + _dot(a, wd_ref[...])
    y_ref[...] = y
    if n_ref:
        n_ref[0][...] = _rms(y, gf_ref[...])


def _mlp(x, g, wu, wd, gf, *, tm, final):
    n, d = x.shape
    row = pl.BlockSpec((tm, d), lambda i: (i, 0))
    vec = pl.BlockSpec((1, d), lambda i: (0, 0))
    once = pl.Buffered(1)
    out_shape = [jax.ShapeDtypeStruct((n, d), F32)] * (2 if final else 1)
    return pl.pallas_call(
        _mlp_kernel,
        out_shape=out_shape,
        grid=(n // tm,),
        in_specs=[row, vec, pl.BlockSpec(wu.shape, lambda i: (0, 0), pipeline_mode=once),
                  pl.BlockSpec(wd.shape, lambda i: (0, 0), pipeline_mode=once), vec],
        out_specs=[row] * (2 if final else 1),
        compiler_params=_cparams(1),
        name="mlp",
    )(x, g, wu, wd, gf)


def _assemble_kernel(pt_ref, *refs, n_pages, page):
    nsa_pages = refs[:n_pages]
    dsa_pages = refs[n_pages:2 * n_pages]
    win_ref, nsa_new_ref, dsa_new_ref = refs[2 * n_pages:2 * n_pages + 3]
    cmp_ref, kst_ref, vst_ref, kwt_ref, vwt_ref, kdt_ref, vdt_ref, kit_ref = refs[2 * n_pages + 3:]
    per_chunk = CK // page
    for p in range(n_pages):
        c, r = divmod(p, per_chunk)
        lanes = slice(r * page, (r + 1) * page)
        x = nsa_pages[p]
        cmp_ref[p * page:(p + 1) * page, :] = jnp.concatenate([x[0], x[1]], axis=0).T
        kst_ref[c, :, lanes] = x[2].astype(BF16)
        vst_ref[c, :, lanes] = x[3].astype(BF16)
        y = dsa_pages[p]
        kdt_ref[c, :, lanes] = y[0].astype(BF16)
        vdt_ref[c, :, lanes] = y[1].astype(BF16)
        kit_ref[c, :, lanes] = y[2].astype(BF16)

    first = lax.broadcasted_iota(I32, (HEAD_DIM, CK), 1) == 0

    def new_chunk(col):
        return jnp.where(first, jnp.broadcast_to(col, (HEAD_DIM, CK)), 0.0).astype(BF16)

    c_new = n_pages // per_chunk
    nsa_new = nsa_new_ref[...]
    dsa_new = dsa_new_ref[...]
    kst_ref[c_new] = new_chunk(nsa_new[2 * HEAD_DIM:3 * HEAD_DIM])
    vst_ref[c_new] = new_chunk(nsa_new[3 * HEAD_DIM:4 * HEAD_DIM])
    kdt_ref[c_new] = new_chunk(dsa_new[0:HEAD_DIM])
    vdt_ref[c_new] = new_chunk(dsa_new[HEAD_DIM:2 * HEAD_DIM])
    kit_ref[c_new] = new_chunk(dsa_new[2 * HEAD_DIM:3 * HEAD_DIM])
    kwt_ref[0] = win_ref[0].astype(BF16)
    vwt_ref[0] = win_ref[1].astype(BF16)
    kwt_ref[1] = new_chunk(nsa_new[4 * HEAD_DIM:5 * HEAD_DIM])
    vwt_ref[1] = new_chunk(nsa_new[5 * HEAD_DIM:6 * HEAD_DIM])


def _assemble(page_table, cache_nsa_t, cache_dsa_t, state_win_t, nsa_new, dsa_new, *, layer):
    nseq, n_pages = page_table.shape
    page = cache_nsa_t.shape[-1]
    n_chunks = n_pages * page // CK + 1
    sds = jax.ShapeDtypeStruct

    def page_spec(slots, p):
        return pl.BlockSpec((None, None, slots, HEAD_DIM, page), lambda b, pt: (pt[b, p], layer, 0, 0, 0))

    def per_seq(*shape):
        return pl.BlockSpec((None,) + shape, lambda b, pt: (b,) + (0,) * len(shape))

    chunks = per_seq(n_chunks, HEAD_DIM, CK)
    grid_spec = pltpu.PrefetchScalarGridSpec(
        num_scalar_prefetch=1,
        grid=(nseq,),
        in_specs=([page_spec(4, p) for p in range(n_pages)] + [page_spec(3, p) for p in range(n_pages)]
                  + [pl.BlockSpec((None, None, 2, HEAD_DIM, CK), lambda b, pt: (b, layer, 0, 0, 0)),
                     per_seq(6 * HEAD_DIM, 1), per_seq(3 * HEAD_DIM, 1)]),
        out_specs=[per_seq(n_pages * page, 2 * HEAD_DIM), chunks, chunks, per_seq(2, HEAD_DIM, CK),
                   per_seq(2, HEAD_DIM, CK), chunks, chunks, chunks],
    )
    chunk_shape = sds((nseq, n_chunks, HEAD_DIM, CK), BF16)
    win_shape = sds((nseq, 2, HEAD_DIM, CK), BF16)
    return pl.pallas_call(
        functools.partial(_assemble_kernel, n_pages=n_pages, page=page),
        out_shape=[sds((nseq, n_pages * page, 2 * HEAD_DIM), F32), chunk_shape, chunk_shape, win_shape, win_shape,
                   chunk_shape, chunk_shape, chunk_shape],
        grid_spec=grid_spec,
        compiler_params=_cparams(1),
        name="assemble",
    )(page_table, *([cache_nsa_t] * n_pages), *([cache_dsa_t] * n_pages), state_win_t, nsa_new, dsa_new)


def _prep_w_in(w_in):
    pts = np.cumsum(SPLIT_WIDTHS)[:-1].tolist()
    qa, kva, ga, glu, uv, qd, kvd, qi, ki, wi = jnp.split(w_in, pts, axis=-1)
    pad = jnp.zeros(w_in.shape[:2] + (128 - 16,), w_in.dtype)
    w = jnp.concatenate([qa, kva, glu, uv, qd, qi, wi, ga, pad, kvd, ki], axis=-1).astype(BF16)
    wt = jnp.concatenate([kva[..., 128:192], kva[..., 256:320], kvd[..., 0:64], ki], axis=-1)
    return w, jnp.swapaxes(wt, 1, 2).astype(BF16)


def _seq_spec(*shape):
    return pl.BlockSpec((None,) + shape, lambda b, i: (b,) + (0,) * len(shape))


def kernel(x_prompt, x_sample, cache_nsa, cache_dsa, state_win, state_conv, page_table, norm1_g, norm2_g,
           final_g, w_in, cmp_pos, cmp_w1, cmp_w2, conv_w, conv_b, conv_ln_g, conv_ln_b, sgu_ln_g, sgu_ln_b,
           sgu_w, sgu_b, w_branch, w_gate, w_out, w_up, w_down):
    nb_, seq, d = x_prompt.shape
    nseq, t_dec, _ = x_sample.shape
    depth = w_in.shape[0]
    n_pool, _, page, _, _ = cache_nsa.shape
    n_pages = page_table.shape[1]
    past = n_pages * page
    w_buf = state_win.shape[2]
    assert t_dec == 1 and seq % CK == 0 and past % CK == 0 and w_buf == CK == NSA_WINDOW
    nbp = seq // NSA_BLOCK
    assert nbp <= NB_PAD and past // NSA_BLOCK < NB_PAD
    n_tok = nb_ * seq
    sr = SAMPLE_ROWS

    w_proj, w_proj_t = _prep_w_in(w_in)
    wg, wb, wo = w_gate.astype(BF16), w_branch.astype(BF16), w_out.astype(BF16)
    wu, wd = w_up.astype(BF16), w_down.astype(BF16)
    def block_diag(a, b):
        z = jnp.zeros_like(a)
        return jnp.concatenate([jnp.concatenate([a, z], axis=-1), jnp.concatenate([z, b], axis=-1)], axis=-2)

    c_pos = jnp.concatenate([cmp_pos[:, 0], cmp_pos[:, 1]], axis=-1)
    c_w1 = cmp_w1.reshape(depth, 2, NSA_BLOCK, HEAD_DIM, HEAD_DIM).astype(BF16)
    c_w1 = block_diag(c_w1[:, 0], c_w1[:, 1])
    c_w2 = block_diag(cmp_w2[:, 0], cmp_w2[:, 1]).astype(BF16)
    sgu_bias = jnp.repeat(jnp.swapaxes(sgu_b, 1, 2), SGU_CH // SGU_GROUPS, axis=2)
    sgu_diag = jnp.repeat(sgu_w[:, :, 0, 0], SGU_CH // SGU_GROUPS, axis=1)[:, None, :]
    sgu_bias0 = sgu_bias[:, 0:1, :]
    vec = lambda a, l: a[l][None, :]
    cache_nsa_t = jnp.transpose(cache_nsa, (0, 1, 3, 4, 2))
    cache_dsa_t = jnp.transpose(cache_dsa, (0, 1, 3, 4, 2))
    state_win_t = jnp.transpose(state_win, (0, 1, 3, 4, 2))
    conv_hist = jnp.transpose(state_conv, (1, 2, 0, 3))

    n_chunks_p = seq // CK
    n_chunks_s = past // CK + 1
    nq = seq // Q_BLOCK

    def chunks_of(arr, slot, n_chunks):
        return arr, pl.BlockSpec((n_chunks, HEAD_DIM, CK), lambda b, i: (b, slot, 0))

    def rows_of(arr, slot, n_rows):
        return arr, pl.BlockSpec((None, n_rows, 2 * HEAD_DIM), lambda b, i: (slot, b, 0))

    xp = x_prompt.reshape(n_tok, d)
    xs = x_sample.reshape(nseq, d)
    outs = {k: [] for k in ("nsa_p", "nsa_s", "dsa_p", "dsa_s", "win_p", "win_s", "conv_p", "conv_s", "sgu_s")}
    yp = ys = None
    for l in range(depth):
        final = l == depth - 1
        qn, kva, glu, uv, qd, qi, misc, dsa_rows, kt4, v4 = _proj(
            xp, vec(norm1_g, l), w_proj[l], w_proj_t[l], tm=CK, emit_t=True)
        kva3 = kva.reshape(nb_, seq, 6, HEAD_DIM)
        kvc = _compress(kva, c_pos[l], c_w1[l], c_w2[l], rows_per_step=seq)
        kvc = jnp.pad(kvc.reshape(nb_, nbp, 2 * HEAD_DIM), ((0, 0), (0, NB_PAD - nbp), (0, 0)))
        o_a = _nsa(qn, misc, kvc, chunks_of(kt4, 0, n_chunks_p), rows_of(v4, 0, seq),
                   chunks_of(kt4, 1, n_chunks_p), rows_of(v4, 1, seq),
                   nbatch=nb_, nq=nq, tq=Q_BLOCK, pos0=0, pos_stride=1, win_base=0,
                   n_sel=min(NSA_N_SEL, seq // NSA_BLOCK), v_t=False)
        o_d = _dsa(qd, qi, misc, chunks_of(kt4, 2, n_chunks_p), rows_of(v4, 2, seq),
                   chunks_of(kt4, 3, n_chunks_p), nbatch=nb_, nq=nq, tq=Q_BLOCK,
                   n_keep=min(DSA_TOPK, seq // 4), n_chunks=n_chunks_p)
        o_b, conv_tail = _conv(glu, conv_w[l], vec(conv_b, l), vec(conv_ln_g, l), vec(conv_ln_b, l),
                               nbatch=nb_, seq=seq, tc=CK)
        o_c = _sgu(uv, vec(sgu_ln_g, l), vec(sgu_ln_b, l), sgu_w[l], sgu_bias[l], ts=CK)
        xp = _merge(xp, vec(norm1_g, l), o_a, o_b, o_c, o_d, wg[l], wb[l], wo[l], tm=256)
        res = _mlp(xp, vec(norm2_g, l), wu[l], wd[l], final_g[None, :], tm=256, final=final)
        xp = res[0]
        if final:
            yp = res[1]
        outs["nsa_p"].append(kva3[:, :, :4])
        outs["dsa_p"].append(dsa_rows.reshape(nb_, seq, 3, HEAD_DIM))
        outs["win_p"].append(kva3[:, seq - min(NSA_WINDOW, seq):, 4:])
        outs["conv_p"].append(conv_tail[:, CONV_HALO - (CONV_WIDTH - 1):])

        qn, kva, glu, uv, qd, qi, misc, dsa_rows = _proj(
            xs, vec(norm1_g, l), w_proj[l], w_proj_t[l], tm=nseq, emit_t=False)
        cmp_past, kst, vst, kwt, vwt, kdt, vdt, kit = _assemble(
            page_table, cache_nsa_t, cache_dsa_t, state_win_t, kva[:, :, None], dsa_rows[:, :, None], layer=l)
        nbs = past // NSA_BLOCK
        kvc = _compress(cmp_past.reshape(nseq * past, 2 * HEAD_DIM), c_pos[l], c_w1[l], c_w2[l],
                        rows_per_step=min(8, nseq) * past)
        kvc = jnp.pad(kvc.reshape(nseq, nbs, 2 * HEAD_DIM), ((0, 0), (0, NB_PAD - nbs), (0, 0)))

        def pad_rows(a):
            a = a[..., :, None, :]
            widths = [(0, 0)] * (a.ndim - 2) + [(0, sr - 1), (0, 0)]
            a = jnp.pad(a, widths)
            return a.reshape(a.shape[:-3] + (nseq * sr, a.shape[-1]))

        misc_r = pad_rows(misc)
        o_a = _nsa(pad_rows(qn), misc_r, kvc,
                   (kst, _seq_spec(n_chunks_s, HEAD_DIM, CK)), (vst, _seq_spec(n_chunks_s, HEAD_DIM, CK)),
                   (kwt, _seq_spec(2, HEAD_DIM, CK)), (vwt, _seq_spec(2, HEAD_DIM, CK)),
                   nbatch=nseq, nq=1, tq=sr, pos0=past, pos_stride=0, win_base=past // CK - 1,
                   n_sel=min(NSA_N_SEL, nbs + 1), v_t=True)
        o_d = _dsa_step(pad_rows(qd), pad_rows(qi), misc_r, kdt, vdt, kit, nseq=nseq, tq=sr, pos=past,
                        n_keep=min(DSA_TOPK, (past + 1) // 4), n_chunks=n_chunks_s)
        o_a = o_a.reshape(nseq, sr, -1)[:, 0]
        o_d = o_d.reshape(nseq, sr, -1)[:, 0]
        o_b, o_c, hx_new, v_rows = _step_mix(
            glu, uv, conv_hist[l], conv_w[l], vec(conv_b, l), vec(conv_ln_g, l), vec(conv_ln_b, l),
            vec(sgu_ln_g, l), vec(sgu_ln_b, l), sgu_diag[l], sgu_bias0[l])
        xs = _merge(xs, vec(norm1_g, l), o_a, o_b, o_c, o_d, wg[l], wb[l], wo[l], tm=nseq)
        res = _mlp(xs, vec(norm2_g, l), wu[l], wd[l], final_g[None, :], tm=nseq, final=final)
        xs = res[0]
        if final:
            ys = res[1]
        kva3 = kva.reshape(nseq, 1, 6, HEAD_DIM)
        outs["nsa_s"].append(kva3[:, :, :4])
        outs["dsa_s"].append(dsa_rows.reshape(nseq, 1, 3, HEAD_DIM))
        outs["win_s"].append(jnp.concatenate([state_win[:, l], kva3[:, :, 4:]], axis=1)[:, 1:])
        outs["conv_s"].append(jnp.concatenate([state_conv[:, l], hx_new[:, None, :]], axis=1)[:, 1:])
        outs["sgu_s"].append(v_rows[:, None, :])

    st = lambda k: jnp.stack(outs[k], axis=1)
    return (yp.reshape(nb_, seq, d), ys.reshape(nseq, 1, d), st("nsa_p"), st("nsa_s"), st("dsa_p"), st("dsa_s"),
            st("win_p"), st("win_s"), st("conv_p"), st("conv_s"), st("sgu_s"))
```
